```python
import math
import jax
import jax.numpy as jnp
from jax import lax
import numpy as np

D_MODEL = 1024
BATCH = 4
SEQ = 4096
DEPTH = 1

GRID_W = 64
CTX_LEN = 256
D_MIX = D_MODEL
D_RWKV = D_MIX // 2
RWKV_HEAD = 64
RWKV_HEADS = D_RWKV // RWKV_HEAD
D_S5 = D_MIX - D_RWKV
S5_GROUP = 16
S5_GROUPS = D_S5 // S5_GROUP
S5_STATE = 64
DECAY_LORA = 32
AAA_LORA = 32
GATE_LORA = 96
D_FF = ((8 * D_MODEL + 3 * 256 - 1) // (3 * 256)) * 256
IN_SPLITS = (D_RWKV, D_RWKV, D_RWKV, DECAY_LORA, DECAY_LORA, AAA_LORA, AAA_LORA, GATE_LORA, D_S5)
D_IN = sum(IN_SPLITS)
N_MOD = 6
NORM_EPS = 1e-6
RWKV_LN_EPS = 64e-5

kernel_name = "hybrid_rwkv7_s5_prefix_dit_block"


def rmsnorm(x, g):
    xf = x.astype(jnp.float32)
    y = xf * lax.rsqrt(jnp.mean(xf * xf, axis=-1, keepdims=True) + NORM_EPS)
    return (y * g.astype(jnp.float32)).astype(x.dtype)


def modulate(h, shift, scale):
    return h * (1 + scale) + shift


def split_proj(z):
    return jnp.split(z, np.cumsum(IN_SPLITS)[:-1].tolist(), axis=-1)


def swiglu(h, w1, w3, w2):
    return (jax.nn.silu(h @ w1) * (h @ w3)) @ w2


def dwconv3x3(x, w, rows):
    bsz, length, ch = x.shape
    xg = x.reshape(bsz, rows, length // rows, ch)
    y = lax.conv_general_dilated(xg, w[:, :, None, :].astype(x.dtype), (1, 1), "SAME",
                                 dimension_numbers=("NHWC", "HWIO", "NHWC"),
                                 feature_group_count=ch)
    return y.reshape(bsz, length, ch)


def _rwkv_step(state, inp):
    r_t, w_t, k_t, v_t, kk_t, a_t = inp
    sa = jnp.einsum("bhvk,bhk->bhv", state, -kk_t)
    state = (state * w_t[:, :, None, :]
             + sa[..., None] * (kk_t * a_t)[:, :, None, :]
             + v_t[..., None] * k_t[:, :, None, :])
    return state, jnp.einsum("bhvk,bhk->bhv", state, r_t)


def rwkv_scan(r, w, k, v, kk, a, s0, reverse):
    xs = tuple(jnp.moveaxis(t, 1, 0) for t in (r, w, k, v, kk, a))
    s_fin, y = lax.scan(_rwkv_step, s0, xs, reverse=reverse)
    return jnp.moveaxis(y, 0, 1), s_fin


def rwkv_mixer(zr, zk, zv, wd, ad, gd, rows, s0, conv_w, w0, w2, a0, a2, g2, k_k, k_a, r_k,
               ln_w, ln_b, readout):
    f32 = jnp.float32
    bsz, length, _ = zr.shape

    def heads(t):
        return t.reshape(bsz, length, RWKV_HEADS, RWKV_HEAD)

    rkv = dwconv3x3(jnp.concatenate([zr, zk, zv], axis=-1).astype(f32), conv_w.astype(f32), rows)
    r, k, v = jnp.split(rkv, 3, axis=-1)
    kk = heads(k * k_k.astype(f32))
    kk = kk * lax.rsqrt(jnp.maximum(jnp.sum(kk * kk, axis=-1, keepdims=True), 1e-12))
    rh, vh = heads(r), heads(v)
    ys, kds, finals = [], [], []
    for d, reverse in ((0, False), (1, True)):
        logw = -jax.nn.softplus(-(w0[d] + jnp.tanh(wd[d]) @ w2[d]).astype(f32)) - 0.5
        decay = jnp.exp(-jnp.exp(logw))
        a = jax.nn.sigmoid((a0[d] + ad[d] @ a2[d]).astype(f32))
        kd = heads(k * (1 + (a - 1) * k_a.astype(f32)))
        y, s_fin = rwkv_scan(rh, heads(decay), kd, vh, kk, heads(a), s0[d], reverse)
        ys.append(y)
        kds.append(kd)
        finals.append(s_fin)
    if not readout:
        return None, (finals[0], finals[1])
    y = ys[0] + ys[1]
    mu = jnp.mean(y, axis=-1, keepdims=True)
    var = jnp.mean(jnp.square(y - mu), axis=-1, keepdims=True)
    yn = ((y - mu) * lax.rsqrt(var + RWKV_LN_EPS)).reshape(bsz, length, D_RWKV)
    yn = yn * ln_w.astype(f32) + ln_b.astype(f32)
    rk = r_k.astype(f32)
    bonus = jnp.sum(rh * (kds[0] + kds[1]) * rk, axis=-1, keepdims=True)
    yn = yn + (bonus * vh).reshape(bsz, length, D_RWKV)
    g = jax.nn.sigmoid(gd.astype(f32)) @ g2.astype(f32)
    return (yn * g).astype(zr.dtype), (finals[0], finals[1])


def _complex_affine_combine(e1, e2):
    a1r, a1i, b1r, b1i = e1
    a2r, a2i, b2r, b2i = e2
    return (a2r * a1r - a2i * a1i,
            a2r * a1i + a2i * a1r,
            a2r * b1r - a2i * b1i + b2r,
            a2r * b1i + a2i * b1r + b2i)


def s5_discretize(lam_re, lam_im, log_step, b_re, b_im):
    step = jnp.exp(log_step)[:, None]
    mag = jnp.exp(lam_re * step)
    lb_re = mag * jnp.cos(lam_im * step)
    lb_im = mag * jnp.sin(lam_im * step)
    den = lam_re * lam_re + lam_im * lam_im
    nr = lb_re - 1
    q_re = (nr * lam_re + lb_im * lam_im) / den
    q_im = (lb_im * lam_re - nr * lam_im) / den
    bb_re = q_re[..., None] * b_re - q_im[..., None] * b_im
    bb_im = q_re[..., None] * b_im + q_im[..., None] * b_re
    return lb_re, lb_im, bb_re, bb_im


def s5_mixer(u, h0, lam_re, lam_im, log_step, b_re, b_im, c_re, c_im, d_skip, glu_w, glu_b, readout):
    f32 = jnp.float32
    bsz, length, _ = u.shape
    uf = u.astype(f32)
    ug = uf.reshape(bsz, length, S5_GROUPS, S5_GROUP)
    bre, bim = b_re.astype(f32), b_im.astype(f32)
    states, finals = [], []
    for d, reverse in ((0, False), (1, True)):
        lb_re, lb_im, bb_re, bb_im = s5_discretize(lam_re[d].astype(f32), lam_im[d].astype(f32),
                                                   log_step[d].astype(f32), bre, bim)
        bu_re = jnp.einsum("blgh,gph->blgp", ug, bb_re)
        bu_im = jnp.einsum("blgh,gph->blgp", ug, bb_im)
        first, final = (length - 1, 0) if reverse else (0, length - 1)
        h0_re, h0_im = h0[d]
        bu_re = bu_re.at[:, first].add(lb_re * h0_re - lb_im * h0_im)
        bu_im = bu_im.at[:, first].add(lb_re * h0_im + lb_im * h0_re)
        a_re = jnp.broadcast_to(lb_re, bu_re.shape)
        a_im = jnp.broadcast_to(lb_im, bu_im.shape)
        _, _, h_re, h_im = lax.associative_scan(_complex_affine_combine, (a_re, a_im, bu_re, bu_im),
                                                reverse=reverse, axis=1)
        states.append((h_re, h_im))
        finals.append((h_re[:, final], h_im[:, final]))
    if not readout:
        return None, (finals[0], finals[1])
    h_re = states[0][0] + states[1][0]
    h_im = states[0][1] + states[1][1]
    y = (jnp.einsum("blgp,ghp->blgh", h_re, c_re.astype(f32))
         - jnp.einsum("blgp,ghp->blgh", h_im, c_im.astype(f32)))
    y = y.reshape(bsz, length, D_S5) + d_skip.astype(f32) * uf
    z = jax.nn.gelu(y)
    out = z * jax.nn.sigmoid(z @ glu_w.astype(f32) + glu_b.astype(f32))
    return out.astype(u.dtype), (finals[0], finals[1])


def setup_inputs(seed: int = 0) -> dict:
    key = jax.random.key(seed)
    ks = jax.random.split(key, 40)
    f32 = jnp.float32

    def nrm(i, shape, scale):
        return scale * jax.random.normal(ks[i], shape, f32)

    conv_center = jnp.zeros((DEPTH, 3, 3, 3 * D_RWKV), f32).at[:, 1, 1].set(1.0)
    return {
        "x": nrm(0, (BATCH, SEQ, D_MODEL), 1.0),
        "c": nrm(1, (BATCH, D_MODEL), 1.0),
        "ctx": nrm(2, (BATCH, CTX_LEN, D_MODEL), 1.0),
        "c_ctx": nrm(3, (D_MODEL,), 1.0),
        "mod_w": nrm(4, (DEPTH, D_MODEL, N_MOD * D_MODEL), 0.5 * D_MODEL ** -0.5),
        "mod_b": nrm(5, (DEPTH, N_MOD * D_MODEL), 0.02),
        "norm1_g": 1.0 + nrm(6, (DEPTH, D_MODEL), 0.05),
        "norm2_g": 1.0 + nrm(7, (DEPTH, D_MODEL), 0.05),
        "w_in": nrm(8, (DEPTH, D_MODEL, D_IN), D_MODEL ** -0.5),
        "w_out": nrm(9, (DEPTH, D_MIX, D_MODEL), D_MIX ** -0.5),
        "rwkv_conv": conv_center + nrm(10, (DEPTH, 3, 3, 3 * D_RWKV), 0.1),
        "rwkv_w0": -6.0 + 5.0 * jnp.linspace(0.0, 1.0, D_RWKV, dtype=f32) + nrm(11, (DEPTH, 2, D_RWKV), 0.2),
        "rwkv_w2": nrm(12, (DEPTH, 2, DECAY_LORA, D_RWKV), 0.5 * DECAY_LORA ** -0.5),
        "rwkv_a0": nrm(13, (DEPTH, 2, D_RWKV), 0.3),
        "rwkv_a2": nrm(14, (DEPTH, 2, AAA_LORA, D_RWKV), 0.5 * AAA_LORA ** -0.5),
        "rwkv_g2": nrm(15, (DEPTH, GATE_LORA, D_RWKV), GATE_LORA ** -0.5),
        "rwkv_kk": 0.85 + nrm(16, (DEPTH, D_RWKV), 0.05),
        "rwkv_ka": 1.0 + nrm(17, (DEPTH, D_RWKV), 0.05),
        "rwkv_rk": nrm(18, (DEPTH, RWKV_HEADS, RWKV_HEAD), 0.1),
        "rwkv_ln_w": 1.0 + nrm(19, (DEPTH, D_RWKV), 0.05),
        "rwkv_ln_b": nrm(20, (DEPTH, D_RWKV), 0.01),
        "s5_lam_re": -0.5 + nrm(21, (DEPTH, 2, S5_GROUPS, S5_STATE), 0.01),
        "s5_lam_im": math.pi * jnp.arange(S5_STATE, dtype=f32) + nrm(22, (DEPTH, 2, S5_GROUPS, S5_STATE), 0.01),
        "s5_log_step": jax.random.uniform(ks[23], (DEPTH, 2, S5_GROUPS), f32,
                                          minval=math.log(1e-3), maxval=math.log(1e-1)),
        "s5_b_re": nrm(24, (DEPTH, S5_GROUPS, S5_STATE, S5_GROUP), (2 * S5_GROUP) ** -0.5),
        "s5_b_im": nrm(25, (DEPTH, S5_GROUPS, S5_STATE, S5_GROUP), (2 * S5_GROUP) ** -0.5),
        "s5_c_re": nrm(26, (DEPTH, S5_GROUPS, S5_GROUP, S5_STATE), S5_STATE ** -0.5),
        "s5_c_im": nrm(27, (DEPTH, S5_GROUPS, S5_GROUP, S5_STATE), S5_STATE ** -0.5),
        "s5_d": nrm(28, (DEPTH, D_S5), 1.0),
        "s5_glu_w": nrm(29, (DEPTH, D_S5, D_S5), D_S5 ** -0.5),
        "s5_glu_b": nrm(30, (DEPTH, D_S5), 0.01),
        "ffn_w1": nrm(31, (DEPTH, D_MODEL, D_FF), D_MODEL ** -0.5),
        "ffn_w3": nrm(32, (DEPTH, D_MODEL, D_FF), D_MODEL ** -0.5),
        "ffn_w2": nrm(33, (DEPTH, D_FF, D_MODEL), D_FF ** -0.5),
        "final_g": 1.0 + nrm(34, (D_MODEL,), 0.05),
    }


def reference(x, c, ctx, c_ctx, mod_w, mod_b, norm1_g, norm2_g, w_in, w_out, rwkv_conv, rwkv_w0,
              rwkv_w2, rwkv_a0, rwkv_a2, rwkv_g2, rwkv_kk, rwkv_ka, rwkv_rk, rwkv_ln_w, rwkv_ln_b,
              s5_lam_re, s5_lam_im, s5_log_step, s5_b_re, s5_b_im, s5_c_re, s5_c_im, s5_d, s5_glu_w,
              s5_glu_b, ffn_w1, ffn_w3, ffn_w2, final_g):
    f32 = jnp.float32
    bsz, length, _ = x.shape
    rows = length // GRID_W
    zero_rwkv = jnp.zeros((bsz, RWKV_HEADS, RWKV_HEAD, RWKV_HEAD), f32)
    zero_s5 = jnp.zeros((bsz, S5_GROUPS, S5_STATE), f32)
    for layer in range(DEPTH):
        last = layer == DEPTH - 1
        mod_x = jnp.split((jax.nn.silu(c) @ mod_w[layer] + mod_b[layer])[:, None, :], N_MOD, axis=-1)
        mod_c = jnp.split(jax.nn.silu(c_ctx) @ mod_w[layer] + mod_b[layer], N_MOD, axis=-1)
        z_x = split_proj(modulate(rmsnorm(x, norm1_g[layer]), mod_x[0], mod_x[1]) @ w_in[layer])
        z_c = split_proj(modulate(rmsnorm(ctx, norm1_g[layer]), mod_c[0], mod_c[1]) @ w_in[layer])

        def run_rwkv(z, grid_rows, s0, readout):
            return rwkv_mixer(z[0], z[1], z[2], (z[3], z[4]), (z[5], z[6]), z[7], grid_rows, s0,
                              rwkv_conv[layer], rwkv_w0[layer], rwkv_w2[layer], rwkv_a0[layer],
                              rwkv_a2[layer], rwkv_g2[layer], rwkv_kk[layer], rwkv_ka[layer],
                              rwkv_rk[layer], rwkv_ln_w[layer], rwkv_ln_b[layer], readout)

        def run_s5(z, h0, readout):
            return s5_mixer(z[8], h0, s5_lam_re[layer], s5_lam_im[layer], s5_log_step[layer],
                            s5_b_re[layer], s5_b_im[layer], s5_c_re[layer], s5_c_im[layer],
                            s5_d[layer], s5_glu_w[layer], s5_glu_b[layer], readout)

        rwkv_c, rwkv_state = run_rwkv(z_c, 1, (zero_rwkv, zero_rwkv), not last)
        s5_c, s5_state = run_s5(z_c, ((zero_s5, zero_s5), (zero_s5, zero_s5)), not last)
        rwkv_x, _ = run_rwkv(z_x, rows, rwkv_state, True)
        s5_x, _ = run_s5(z_x, s5_state, True)

        x = x + mod_x[2] * (jnp.concatenate([rwkv_x, s5_x], axis=-1) @ w_out[layer])
        x = x + mod_x[5] * swiglu(modulate(rmsnorm(x, norm2_g[layer]), mod_x[3], mod_x[4]),
                                  ffn_w1[layer], ffn_w3[layer], ffn_w2[layer])
        if not last:
            ctx = ctx + mod_c[2] * (jnp.concatenate([rwkv_c, s5_c], axis=-1) @ w_out[layer])
            ctx = ctx + mod_c[5] * swiglu(modulate(rmsnorm(ctx, norm2_g[layer]), mod_c[3], mod_c[4]),
                                          ffn_w1[layer], ffn_w3[layer], ffn_w2[layer])
    return rmsnorm(x, final_g)
```

```python
import functools
import math

import jax
import jax.numpy as jnp
from jax import lax
from jax.experimental import pallas as pl
from jax.experimental.pallas import tpu as pltpu

F32 = jnp.float32
BF16 = jnp.bfloat16

D_MODEL = 1024
GRID_W = 64
D_RWKV = 512
HEAD = 64
HEADS = D_RWKV // HEAD
D_S5 = 512
S5_GROUP = 16
S5_GROUPS = D_S5 // S5_GROUP
S5_STATE = 64
DECAY_LORA = 32
AAA_LORA = 32
GATE_LORA = 96
N_MOD = 6
NORM_EPS = 1e-6
RWKV_LN_EPS = 64e-5

LORA_W = 256
COL_U = 3 * D_RWKV
COL_LORA = COL_U + D_S5
D_Z = COL_LORA + LORA_W
LORA_AD = 2 * DECAY_LORA
LORA_GD = LORA_AD + 2 * AAA_LORA

CHUNK = 64
S5_T = 16
S5_ROWS = 8
S5_GS = 2
CONV_PAD = 72
CONV_TC = 256
CONV_PIECE = 128

NN = (((1,), (0,)), ((), ()))
NT = (((1,), (1,)), ((), ()))
TN = (((0,), (0,)), ((), ()))


def _dot(a, b, dims):
    return lax.dot_general(a, b, dims, preferred_element_type=F32)


def _split2(x):
    hi = x.astype(BF16)
    lo = (x - hi.astype(F32)).astype(BF16)
    return hi, lo


def _split3(x):
    hi = x.astype(BF16)
    r1 = x - hi.astype(F32)
    mid = r1.astype(BF16)
    lo = (r1 - mid.astype(F32)).astype(BF16)
    return hi, mid, lo


def _mm(a, b, mode="bf16", dims=NN):
    if mode == "bf16":
        return _dot(a.astype(BF16), b.astype(BF16), dims)
    if mode == "x3":
        ah, al = _split2(a)
        bh, bl = _split2(b)
        return _dot(ah, bh, dims) + (_dot(ah, bl, dims) + _dot(al, bh, dims))
    if mode == "xa":
        bb = b.astype(BF16)
        a1, a2, a3 = _split3(a)
        return _dot(a1, bb, dims) + (_dot(a2, bb, dims) + _dot(a3, bb, dims))
    if mode == "xb":
        ab = a.astype(BF16)
        b1, b2, b3 = _split3(b)
        return _dot(ab, b1, dims) + (_dot(ab, b2, dims) + _dot(ab, b3, dims))
    raise ValueError(mode)


def _sigmoid(x):
    return 1.0 / (1.0 + jnp.exp(-x))


def _softplus(x):
    return jnp.maximum(x, 0.0) + jnp.log(1.0 + jnp.exp(-jnp.abs(x)))


def _cparams(sem, vmem_mb):
    return pltpu.CompilerParams(dimension_semantics=sem, vmem_limit_bytes=vmem_mb * 1024 * 1024)


def _adaln_kernel(c_ref, w_ref, b_ref, o_ref):
    c = c_ref[...]
    s = c * _sigmoid(c)
    o_ref[...] = _mm(s, w_ref[...], "x3") + b_ref[...]


def _adaln(cc, mod_w, mod_b):
    rows, d = cc.shape
    n = mod_w.shape[1]
    tn = 1024
    return pl.pallas_call(
        _adaln_kernel,
        grid=(n // tn,),
        in_specs=[pl.BlockSpec((rows, d), lambda j: (0, 0)),
                  pl.BlockSpec((d, tn), lambda j: (0, j)),
                  pl.BlockSpec((1, tn), lambda j: (0, j))],
        out_specs=pl.BlockSpec((rows, tn), lambda j: (0, j)),
        out_shape=jax.ShapeDtypeStruct((rows, n), F32),
        compiler_params=_cparams(("parallel",), 40),
    )(cc, mod_w, mod_b.reshape(1, n))


def _inproj_kernel(x_ref, mod_ref, g_ref, w_ref, o_ref):
    x = x_ref[0]
    ms = jnp.mean(x * x, axis=-1, keepdims=True)
    xn = x * lax.rsqrt(ms + NORM_EPS) * g_ref[...]
    shift = mod_ref[0, 0:1, :]
    scale = mod_ref[0, 1:2, :]
    h = xn * (1.0 + scale) + shift
    o_ref[0] = _mm(h, w_ref[...], "bf16")


def _inproj(xcat, mod8, g1, w_z, ctx_blocks, tm):
    bsz, ltot, d = xcat.shape
    dz = w_z.shape[1]

    def mod_idx(b, j):
        return (jnp.where(j < ctx_blocks, bsz, b), 0, 0)

    return pl.pallas_call(
        _inproj_kernel,
        grid=(bsz, ltot // tm),
        in_specs=[pl.BlockSpec((1, tm, d), lambda b, j: (b, j, 0)),
                  pl.BlockSpec((1, 8, d), mod_idx),
                  pl.BlockSpec((1, d), lambda b, j: (0, 0)),
                  pl.BlockSpec((d, dz), lambda b, j: (0, 0))],
        out_specs=pl.BlockSpec((1, tm, dz), lambda b, j: (b, j, 0)),
        out_shape=jax.ShapeDtypeStruct((bsz, ltot, dz), F32),
        compiler_params=_cparams(("parallel", "parallel"), 48),
    )(xcat, mod8, g1.reshape(1, d), w_z)


def _conv_kernel(z_ref, w_ref, o_ref, xs_ref, *, ctx_len, lat_len):
    tc = z_ref.shape[2]
    pc = CONV_PIECE
    ctx_off = CONV_PAD
    lat_off = 2 * CONV_PAD + ctx_len
    zpad = jnp.zeros((CONV_PAD, tc), F32)
    for s in range(3):
        xs_ref[s, 0:CONV_PAD, :] = zpad
        xs_ref[s, ctx_off + ctx_len:lat_off, :] = zpad
        xs_ref[s, lat_off + lat_len:lat_off + lat_len + CONV_PAD, :] = zpad
    col = lax.broadcasted_iota(jnp.int32, (pc, tc), 0) % GRID_W
    keep_l = (col != GRID_W - 1).astype(F32)
    keep_r = (col != 0).astype(F32)
    for p in range(ctx_len // pc):
        xs_ref[0, ctx_off + p * pc:ctx_off + (p + 1) * pc, :] = z_ref[0, p * pc:(p + 1) * pc, :]
    for p in range(lat_len // pc):
        x = z_ref[0, ctx_len + p * pc:ctx_len + (p + 1) * pc, :]
        dst = slice(lat_off + p * pc, lat_off + (p + 1) * pc)
        xs_ref[0, dst, :] = x
        xs_ref[1, dst, :] = x * keep_l
        xs_ref[2, dst, :] = x * keep_r
    w = [w_ref[t:t + 1, :] for t in range(9)]
    for p in range(ctx_len // pc):
        base = ctx_off + p * pc
        acc = w[3] * xs_ref[0, base - 1:base - 1 + pc, :]
        acc = acc + w[4] * xs_ref[0, base:base + pc, :]
        acc = acc + w[5] * xs_ref[0, base + 1:base + 1 + pc, :]
        o_ref[0, p * pc:(p + 1) * pc, :] = acc
    for p in range(lat_len // pc):
        base = lat_off + p * pc
        acc = None
        for di in range(3):
            for dj in range(3):
                off = base + (di - 1) * GRID_W + (dj - 1)
                src = (1, 0, 2)[dj]
                term = w[3 * di + dj] * xs_ref[src, off:off + pc, :]
                acc = term if acc is None else acc + term
        o_ref[0, ctx_len + p * pc:ctx_len + (p + 1) * pc, :] = acc


def _conv(z, conv_w, ctx_len):
    bsz, ltot, _ = z.shape
    lat_len = ltot - ctx_len
    ch = 3 * D_RWKV
    rows = 3 * CONV_PAD + ltot
    kern = functools.partial(_conv_kernel, ctx_len=ctx_len, lat_len=lat_len)
    return pl.pallas_call(
        kern,
        grid=(bsz, ch // CONV_TC),
        in_specs=[pl.BlockSpec((1, ltot, CONV_TC), lambda b, j: (b, 0, j)),
                  pl.BlockSpec((9, CONV_TC), lambda b, j: (0, j))],
        out_specs=pl.BlockSpec((1, ltot, CONV_TC), lambda b, j: (b, 0, j)),
        out_shape=jax.ShapeDtypeStruct((bsz, ltot, ch), F32),
        scratch_shapes=[pltpu.VMEM((3, rows, CONV_TC), F32)],
        compiler_params=_cparams(("parallel", "parallel"), 48),
    )(z, conv_w.reshape(9, ch))


def _tri_inv(n_mat, size):
    ii = lax.broadcasted_iota(jnp.int32, (size, size), 0)
    jj = lax.broadcasted_iota(jnp.int32, (size, size), 1)
    eye = (ii == jj).astype(F32)
    t = eye + jnp.where((ii >> 1) == (jj >> 1), n_mat, 0.0)
    lg = 1
    while (1 << lg) < size:
        same = (ii >> (lg + 1)) == (jj >> (lg + 1))
        diff = (ii >> lg) != (jj >> lg)
        nl = jnp.where(same, jnp.where(diff, n_mat, 0.0), 0.0)
        t = t + _mm(_mm(t, nl), t)
        lg += 1
    return t


def _rwkv_kernel(r_ref, k_ref, v_ref, lo_ref, w0_ref, w2_ref, a0_ref, a2_ref, kkw_ref, kaw_ref,
                 y_ref, h_ref):
    d = pl.program_id(0)
    step = pl.program_id(2)

    @pl.when(step == 0)
    def _():
        h_ref[...] = jnp.zeros_like(h_ref)

    r = r_ref[0]
    k = k_ref[0]
    v = v_ref[0]
    lo = lo_ref[0]
    c = r.shape[0]

    logit_w = w0_ref[0] + _mm(jnp.tanh(lo), w2_ref[0])
    logw = -_softplus(-logit_w) - 0.5
    lw = -jnp.exp(logw)
    a = _sigmoid(a0_ref[0] + _mm(lo, a2_ref[0]))
    kd = k * (1.0 + (a - 1.0) * kaw_ref[...])
    kk = k * kkw_ref[...]

    ti = lax.broadcasted_iota(jnp.int32, (c, c), 0)
    tj = lax.broadcasted_iota(jnp.int32, (c, c), 1)
    sgn = 1 - 2 * d
    eye = (ti == tj).astype(F32)
    ti2 = lax.broadcasted_iota(jnp.int32, (c, 2 * c), 0)
    tj2 = lax.broadcasted_iota(jnp.int32, (c, 2 * c), 1) & (c - 1)
    dlt2 = (ti2 - tj2) * sgn
    incl2 = dlt2 >= 0
    strict2 = dlt2 > 0

    tri = jnp.where((ti - tj) * sgn >= 0, 1.0, 0.0)
    g = _mm(tri, lw, "xb")
    tot = jnp.sum(lw, axis=0, keepdims=True)
    e_prev = jnp.exp(g - lw)
    e_neg = jnp.exp(-g)
    e_pos = jnp.exp(g)
    e_end = jnp.exp(tot - g)
    g_end = jnp.exp(tot)
    zeros_cc = jnp.zeros((c, HEAD), F32)

    for h in range(HEADS):
        sl = slice(h * HEAD, (h + 1) * HEAD)
        kkh = kk[:, sl]
        n2 = jnp.sum(kkh * kkh, axis=-1, keepdims=True)
        kkh = kkh * lax.rsqrt(jnp.maximum(n2, 1e-12))
        ka = kkh * a[:, sl]
        a_t = -kkh * e_prev[:, sl]
        b_t = ka * e_neg[:, sl]
        k_t = kd[:, sl] * e_neg[:, sl]
        r_t = r[:, sl] * e_pos[:, sl]
        b_e = ka * e_end[:, sl]
        k_e = kd[:, sl] * e_end[:, sl]
        vh = v[:, sl]

        s1 = _mm(jnp.concatenate([a_t, r_t], axis=0), jnp.concatenate([b_t, k_t], axis=0), "bf16", NT)
        top = jnp.where(strict2, s1[:c], 0.0)
        bot = jnp.where(incl2, s1[c:], 0.0)
        t_inv = _tri_inv(top[:, :c], c)
        x1 = _mm(top[:, c:], vh)
        uw = _mm(t_inv, jnp.concatenate([x1, a_t], axis=1))
        m2 = jnp.concatenate([uw, jnp.concatenate([vh, zeros_cc], axis=1)], axis=0)
        yq = _mm(bot, m2)
        gp = _mm(jnp.concatenate([b_e, k_e], axis=0), m2, "bf16", TN)
        q = r_t + yq[:, HEAD:]
        p = gp[:, HEAD:] + eye * g_end[:, sl]
        qp = _mm(jnp.concatenate([q, p], axis=0), h_ref[h], "x3")
        y_ref[0, 0, :, sl] = qp[:c] + yq[:, :HEAD]
        h_ref[h] = qp[c:] + gp[:, :HEAD]


def _rwkv(rkvc, z, w0, w2p, a0, a2p, kkw, kaw, ctx_len):
    bsz, ltot, _ = rkvc.shape
    nchunk = ltot // CHUNK
    nctx = ctx_len // CHUNK
    lora_blk = COL_LORA // LORA_W

    def tok(d, b, i):
        bwd = jnp.where(i < nctx, nctx - 1 - i, nchunk + nctx - 1 - i)
        return jnp.where(d == 0, i, bwd)

    def par3(d, b, i):
        return (d, 0, 0)

    return pl.pallas_call(
        _rwkv_kernel,
        grid=(2, bsz, nchunk),
        in_specs=[pl.BlockSpec((1, CHUNK, D_RWKV), lambda d, b, i: (b, tok(d, b, i), 0)),
                  pl.BlockSpec((1, CHUNK, D_RWKV), lambda d, b, i: (b, tok(d, b, i), 1)),
                  pl.BlockSpec((1, CHUNK, D_RWKV), lambda d, b, i: (b, tok(d, b, i), 2)),
                  pl.BlockSpec((1, CHUNK, LORA_W), lambda d, b, i: (b, tok(d, b, i), lora_blk)),
                  pl.BlockSpec((1, 1, D_RWKV), par3),
                  pl.BlockSpec((1, LORA_W, D_RWKV), par3),
                  pl.BlockSpec((1, 1, D_RWKV), par3),
                  pl.BlockSpec((1, LORA_W, D_RWKV), par3),
                  pl.BlockSpec((1, D_RWKV), lambda d, b, i: (0, 0)),
                  pl.BlockSpec((1, D_RWKV), lambda d, b, i: (0, 0))],
        out_specs=pl.BlockSpec((1, 1, CHUNK, D_RWKV), lambda d, b, i: (d, b, tok(d, b, i), 0)),
        out_shape=jax.ShapeDtypeStruct((2, bsz, ltot, D_RWKV), F32),
        scratch_shapes=[pltpu.VMEM((HEADS, HEAD, HEAD), F32)],
        compiler_params=_cparams(("parallel", "parallel", "arbitrary"), 32),
    )(rkvc, rkvc, rkvc, z, w0, w2p, a0, a2p, kkw, kaw)


def _s5_consts(lam_re, lam_im, log_step, b_re, b_im, c_re, c_im):
    t = S5_T
    g, p, hg = b_re.shape
    step = jnp.exp(log_step)[..., None]
    mag = jnp.exp(lam_re * step)
    lb_re = mag * jnp.cos(lam_im * step)
    lb_im = mag * jnp.sin(lam_im * step)
    den = lam_re * lam_re + lam_im * lam_im
    nr = lb_re - 1
    q_re = (nr * lam_re + lb_im * lam_im) / den
    q_im = (lb_im * lam_re - nr * lam_im) / den
    bb_re = q_re[..., None] * b_re - q_im[..., None] * b_im
    bb_im = q_re[..., None] * b_im + q_im[..., None] * b_re
    n = jnp.arange(t + 1, dtype=F32)[:, None, None, None]
    pmag = jnp.exp(lam_re * step * n)
    pw_re = pmag * jnp.cos(lam_im * step * n)
    pw_im = pmag * jnp.sin(lam_im * step * n)
    cp_re = c_re[None, None] * pw_re[:, :, :, None, :] - c_im[None, None] * pw_im[:, :, :, None, :]
    cp_im = c_re[None, None] * pw_im[:, :, :, None, :] + c_im[None, None] * pw_re[:, :, :, None, :]
    hp = lax.Precision.HIGHEST
    kern = (jnp.einsum("ndgop,dgpi->ndgoi", cp_re[:t], bb_re, precision=hp)
            - jnp.einsum("ndgop,dgpi->ndgoi", cp_im[:t], bb_im, precision=hp))
    ss = jnp.arange(t)[:, None]
    tt = jnp.arange(t)[None, :]
    kf = kern[jnp.clip(tt - ss, 0, t - 1), 0] * (tt >= ss)[:, :, None, None, None]
    kb = kern[jnp.clip(ss - tt, 0, t - 1), 1] * (ss >= tt)[:, :, None, None, None]
    m_tot = jnp.transpose(kf + kb, (2, 0, 4, 1, 3)).reshape(g, t * hg, t * hg)

    def inject(d, powers):
        pr = pw_re[powers, d][..., None] * bb_re[d][None] - pw_im[powers, d][..., None] * bb_im[d][None]
        pi = pw_re[powers, d][..., None] * bb_im[d][None] + pw_im[powers, d][..., None] * bb_re[d][None]
        return (jnp.transpose(pr, (1, 0, 3, 2)).reshape(g, t * hg, p),
                jnp.transpose(pi, (1, 0, 3, 2)).reshape(g, t * hg, p))

    pf_re, pf_im = inject(0, t - 1 - jnp.arange(t))
    pb_re, pb_im = inject(1, jnp.arange(t))
    p_all = jnp.concatenate([pf_re, pb_re, pf_im, pb_im], axis=-1)

    def readout(d, powers):
        qr = jnp.transpose(cp_re[powers, d], (1, 3, 0, 2)).reshape(g, p, t * hg)
        qi = jnp.transpose(-cp_im[powers, d], (1, 3, 0, 2)).reshape(g, p, t * hg)
        return qr, qi

    qf_re, qf_im = readout(0, jnp.arange(t) + 1)
    qb_re, qb_im = readout(1, t - jnp.arange(t))
    q_re_all = jnp.concatenate([qf_re, qb_re], axis=1)
    q_im_all = jnp.concatenate([qf_im, qb_im], axis=1)
    l_re = jnp.concatenate([pw_re[t, 0], pw_re[t, 1]], axis=-1)
    l_im = jnp.concatenate([pw_im[t, 0], pw_im[t, 1]], axis=-1)
    l_all = jnp.stack([l_re, l_im], axis=1)
    l_all = jnp.concatenate([l_all, jnp.zeros((g, 6, 2 * p), F32)], axis=1)
    return m_tot, p_all, q_re_all, q_im_all, l_all


def _s5_kernel(u_ref, m_ref, p_ref, qre_ref, qim_ref, l_ref, y_ref, z_ref, hre_ref, him_ref,
               *, nchunk, nctx):
    gs = u_ref.shape[0]
    ps = S5_STATE
    for gi in range(gs):
        z_ref[gi] = _mm(u_ref[gi], p_ref[gi])
    lane = lax.broadcasted_iota(jnp.int32, (S5_ROWS, 2 * ps), 1)
    is_f = lane < ps

    def body(i, carry):
        cb = jnp.where(i < nctx, nctx - 1 - i, nchunk + nctx - 1 - i)
        rf = pl.multiple_of(i * S5_ROWS, S5_ROWS)
        rb = pl.multiple_of(cb * S5_ROWS, S5_ROWS)
        out = []
        for gi in range(gs):
            hre, him = carry[gi]
            hre_ref[gi, pl.ds(rf, S5_ROWS), 0:ps] = hre[:, 0:ps]
            hre_ref[gi, pl.ds(rb, S5_ROWS), ps:2 * ps] = hre[:, ps:2 * ps]
            him_ref[gi, pl.ds(rf, S5_ROWS), 0:ps] = him[:, 0:ps]
            him_ref[gi, pl.ds(rb, S5_ROWS), ps:2 * ps] = him[:, ps:2 * ps]
            zf = z_ref[gi, pl.ds(rf, S5_ROWS), :]
            zb = z_ref[gi, pl.ds(rb, S5_ROWS), :]
            zre = jnp.where(is_f, zf[:, 0:2 * ps], zb[:, 0:2 * ps])
            zim = jnp.where(is_f, zf[:, 2 * ps:4 * ps], zb[:, 2 * ps:4 * ps])
            lre = l_ref[gi, 0:1, :]
            lim = l_ref[gi, 1:2, :]
            out.append((lre * hre - lim * him + zre, lre * him + lim * hre + zim))
        return tuple(out)

    zero = jnp.zeros((S5_ROWS, 2 * ps), F32)
    lax.fori_loop(0, nchunk, body, tuple((zero, zero) for _ in range(gs)))
    for gi in range(gs):
        y = _mm(u_ref[gi], m_ref[gi])
        y = y + _mm(hre_ref[gi], qre_ref[gi])
        y = y + _mm(him_ref[gi], qim_ref[gi])
        y_ref[gi] = y


def _s5(ug, consts, nchunk, nctx):
    m_tot, p_all, q_re, q_im, l_all = consts
    g, rows, w = ug.shape
    ps2 = 2 * S5_STATE
    kern = functools.partial(_s5_kernel, nchunk=nchunk, nctx=nctx)
    blk = lambda a, b: pl.BlockSpec((S5_GS, a, b), lambda j: (j, 0, 0))
    return pl.pallas_call(
        kern,
        grid=(g // S5_GS,),
        in_specs=[blk(rows, w), blk(w, w), blk(w, 2 * ps2), blk(ps2, w), blk(ps2, w), blk(8, ps2)],
        out_specs=blk(rows, w),
        out_shape=jax.ShapeDtypeStruct((g, rows, w), F32),
        scratch_shapes=[pltpu.VMEM((S5_GS, rows, 2 * ps2), F32),
                        pltpu.VMEM((S5_GS, rows, ps2), F32),
                        pltpu.VMEM((S5_GS, rows, ps2), F32)],
        compiler_params=_cparams(("parallel",), 48),
    )(ug, m_tot, p_all, q_re, q_im, l_all)


def _mix_kernel(x_ref, yf_ref, yb_ref, r_ref, k_ref, v_ref, lo_ref, u_ref, ys_ref, mod_ref,
                seg_ref, a0_ref, a2_ref, g2_ref, kaw_ref, rk_ref, lnw_ref, lnb_ref,
                dsk_ref, gluw_ref, glub_ref, wo_ref, o_ref):
    lo = lo_ref[0]
    seg = seg_ref[...]
    y = yf_ref[0, 0] + yb_ref[0, 0]
    mu = _mm(y, seg, "xa") * (1.0 / HEAD)
    yc = y - mu
    var = _mm(yc * yc, seg, "xa") * (1.0 / HEAD)
    yn = yc * lax.rsqrt(var + RWKV_LN_EPS) * lnw_ref[...] + lnb_ref[...]
    a_f = _sigmoid(a0_ref[0] + _mm(lo, a2_ref[0]))
    a_b = _sigmoid(a0_ref[1] + _mm(lo, a2_ref[1]))
    kaw = kaw_ref[...]
    kd_sum = k_ref[0] * ((1.0 + (a_f - 1.0) * kaw) + (1.0 + (a_b - 1.0) * kaw))
    bonus = _mm(r_ref[0] * kd_sum * rk_ref[...], seg, "xa")
    yn = yn + bonus * v_ref[0]
    gate = _mm(_sigmoid(lo), g2_ref[...])
    rwkv_out = yn * gate

    u = u_ref[0]
    ys = ys_ref[0] + dsk_ref[...] * u
    zg = 0.5 * ys * (1.0 + jnp.tanh(math.sqrt(2.0 / math.pi) * (ys + 0.044715 * (ys * ys * ys))))
    s5_out = zg * _sigmoid(_mm(zg, gluw_ref[...]) + glub_ref[...])

    mix = _mm(rwkv_out, wo_ref[0:D_RWKV, :]) + _mm(s5_out, wo_ref[D_RWKV:, :])
    o_ref[0] = x_ref[0] + mod_ref[0, 2:3, :] * mix


def _mix(x, y_rwkv, rkvc, z, ys5, mod8, seg, a0, a2p, g2p, kaw, rk, lnw, lnb, dsk, gluw, glub, wo,
         ctx_len, tm):
    bsz, seq, d = x.shape
    off = ctx_len // tm
    lora_blk = COL_LORA // LORA_W
    u_blk = COL_U // D_S5
    tok = lambda c: pl.BlockSpec((1, tm, D_RWKV), lambda b, j: (b, j + off, c))
    full = lambda a: pl.BlockSpec(a.shape, lambda b, j: (0,) * a.ndim)
    return pl.pallas_call(
        _mix_kernel,
        grid=(bsz, seq // tm),
        in_specs=[pl.BlockSpec((1, tm, d), lambda b, j: (b, j, 0)),
                  pl.BlockSpec((1, 1, tm, D_RWKV), lambda b, j: (0, b, j + off, 0)),
                  pl.BlockSpec((1, 1, tm, D_RWKV), lambda b, j: (1, b, j + off, 0)),
                  tok(0), tok(1), tok(2),
                  pl.BlockSpec((1, tm, LORA_W), lambda b, j: (b, j + off, lora_blk)),
                  pl.BlockSpec((1, tm, D_S5), lambda b, j: (b, j + off, u_blk)),
                  pl.BlockSpec((1, tm, D_S5), lambda b, j: (b, j, 0)),
                  pl.BlockSpec((1, 8, d), lambda b, j: (b, 0, 0)),
                  full(seg), full(a0), full(a2p), full(g2p), full(kaw), full(rk), full(lnw), full(lnb),
                  full(dsk), full(gluw), full(glub), full(wo)],
        out_specs=pl.BlockSpec((1, tm, d), lambda b, j: (b, j, 0)),
        out_shape=jax.ShapeDtypeStruct((bsz, seq, d), F32),
        compiler_params=_cparams(("parallel", "parallel"), 48),
    )(x, y_rwkv, y_rwkv, rkvc, rkvc, rkvc, z, z, ys5, mod8, seg, a0, a2p, g2p, kaw, rk, lnw, lnb,
      dsk, gluw, glub, wo)


def _ffn_kernel(x_ref, mod_ref, g2_ref, w1_ref, w3_ref, w2_ref, gf_ref, o_ref):
    x = x_ref[0]
    ms = jnp.mean(x * x, axis=-1, keepdims=True)
    h = x * lax.rsqrt(ms + NORM_EPS) * g2_ref[...]
    h = h * (1.0 + mod_ref[0, 4:5, :]) + mod_ref[0, 3:4, :]
    hb = h.astype(BF16)
    h1 = _dot(hb, w1_ref[...], NN)
    h3 = _dot(hb, w3_ref[...], NN)
    act = (h1 * _sigmoid(h1)) * h3
    f = _dot(act.astype(BF16), w2_ref[...], NN)
    out = x + mod_ref[0, 5:6, :] * f
    ms2 = jnp.mean(out * out, axis=-1, keepdims=True)
    o_ref[0] = out * lax.rsqrt(ms2 + NORM_EPS) * gf_ref[...]


def _ffn(x1, mod8, g2, w1, w3, w2, gf, tm):
    bsz, seq, d = x1.shape
    dff = w1.shape[1]
    const = lambda shape: pl.BlockSpec(shape, lambda b, j: (0, 0), pipeline_mode=pl.Buffered(1))
    return pl.pallas_call(
        _ffn_kernel,
        grid=(bsz, seq // tm),
        in_specs=[pl.BlockSpec((1, tm, d), lambda b, j: (b, j, 0)),
                  pl.BlockSpec((1, 8, d), lambda b, j: (b, 0, 0)),
                  const((1, d)), const((d, dff)), const((d, dff)), const((dff, d)), const((1, d))],
        out_specs=pl.BlockSpec((1, tm, d), lambda b, j: (b, j, 0)),
        out_shape=jax.ShapeDtypeStruct((bsz, seq, d), F32),
        compiler_params=_cparams(("parallel", "parallel"), 56),
    )(x1, mod8, g2.reshape(1, d), w1, w3, w2, gf.reshape(1, d))


def _pad_rows(w, row0, rows_total):
    pad = [(0, 0)] * (w.ndim - 2) + [(row0, rows_total - row0 - w.shape[-2]), (0, 0)]
    return jnp.pad(w, pad)


def kernel(x, c, ctx, c_ctx, mod_w, mod_b, norm1_g, norm2_g, w_in, w_out, rwkv_conv, rwkv_w0, rwkv_w2, rwkv_a0, rwkv_a2, rwkv_g2, rwkv_kk, rwkv_ka, rwkv_rk, rwkv_ln_w, rwkv_ln_b, s5_lam_re, s5_lam_im, s5_log_step, s5_b_re, s5_b_im, s5_c_re, s5_c_im, s5_d, s5_glu_w, s5_glu_b, ffn_w1, ffn_w3, ffn_w2, final_g):
    bsz, seq, d = x.shape
    ctx_len = ctx.shape[1]
    ltot = ctx_len + seq
    layer = 0

    cc = jnp.concatenate([c, c_ctx[None, :], jnp.zeros((8 - bsz - 1, d), F32)], axis=0)
    mod = _adaln(cc, mod_w[layer], mod_b[layer])
    mod8 = jnp.pad(mod[:bsz + 1].reshape(bsz + 1, N_MOD, d), ((0, 0), (0, 8 - N_MOD), (0, 0)))

    wi = w_in[layer]
    w_z = jnp.concatenate([wi[:, :3 * D_RWKV], wi[:, -D_S5:], wi[:, 3 * D_RWKV:-D_S5],
                           jnp.zeros((d, D_Z - wi.shape[1]), F32)], axis=1).astype(BF16)
    xcat = jnp.concatenate([ctx, x], axis=1)
    tm_in = 256
    z = _inproj(xcat, mod8, norm1_g[layer], w_z, ctx_len // tm_in, tm_in)

    rkvc = _conv(z, rwkv_conv[layer], ctx_len)

    w2p = _pad_rows(rwkv_w2[layer], 0, LORA_W).at[1].set(_pad_rows(rwkv_w2[layer, 1], DECAY_LORA, LORA_W))
    a2p = jnp.stack([_pad_rows(rwkv_a2[layer, 0], LORA_AD, LORA_W),
                     _pad_rows(rwkv_a2[layer, 1], LORA_AD + AAA_LORA, LORA_W)])
    w0 = rwkv_w0[layer].reshape(2, 1, D_RWKV)
    a0 = rwkv_a0[layer].reshape(2, 1, D_RWKV)
    kkw = rwkv_kk[layer].reshape(1, D_RWKV)
    kaw = rwkv_ka[layer].reshape(1, D_RWKV)
    y_rwkv = _rwkv(rkvc, z, w0, w2p, a0, a2p, kkw, kaw, ctx_len)

    nchunk = ltot // S5_T
    u = z[:, :, COL_U:COL_U + D_S5].reshape(bsz, nchunk, S5_T, S5_GROUPS, S5_GROUP)
    ug = jnp.transpose(u, (3, 1, 0, 2, 4))
    ug = jnp.pad(ug, ((0, 0), (0, 0), (0, S5_ROWS - bsz), (0, 0), (0, 0)))
    ug = ug.reshape(S5_GROUPS, nchunk * S5_ROWS, S5_T * S5_GROUP)
    consts = _s5_consts(s5_lam_re[layer], s5_lam_im[layer], s5_log_step[layer], s5_b_re[layer],
                        s5_b_im[layer], s5_c_re[layer], s5_c_im[layer])
    yg = _s5(ug, consts, nchunk, ctx_len // S5_T)
    yg = yg.reshape(S5_GROUPS, nchunk, S5_ROWS, S5_T, S5_GROUP)[:, ctx_len // S5_T:, :bsz]
    ys5 = jnp.transpose(yg, (2, 1, 3, 0, 4)).reshape(bsz, seq, D_S5)

    hh = jnp.arange(D_RWKV) // HEAD
    seg = (hh[:, None] == hh[None, :]).astype(BF16)
    g2p = _pad_rows(rwkv_g2[layer], LORA_GD, LORA_W)
    x1 = _mix(x, y_rwkv, rkvc, z, ys5, mod8, seg, a0, a2p, g2p, kaw,
              rwkv_rk[layer].reshape(1, D_RWKV), rwkv_ln_w[layer].reshape(1, D_RWKV),
              rwkv_ln_b[layer].reshape(1, D_RWKV), s5_d[layer].reshape(1, D_S5),
              s5_glu_w[layer].astype(BF16), s5_glu_b[layer].reshape(1, D_S5),
              w_out[layer].astype(BF16), ctx_len, 256)

    return _ffn(x1, mod8, norm2_g[layer], ffn_w1[layer].astype(BF16), ffn_w3[layer].astype(BF16),
                ffn_w2[layer].astype(BF16), final_g, 256)
```

```python
import functools
import math

import jax
import jax.numpy as jnp
from jax import lax
from jax.experimental import pallas as pl
from jax.experimental.pallas import tpu as pltpu

F32 = jnp.float32
BF16 = jnp.bfloat16

D_MODEL = 1024
GRID_W = 64
D_RWKV = 512
HEAD = 64
HEADS = D_RWKV // HEAD
D_S5 = 512
S5_GROUP = 16
S5_GROUPS = D_S5 // S5_GROUP
S5_STATE = 64
DECAY_LORA = 32
AAA_LORA = 32
GATE_LORA = 96
N_MOD = 6
NORM_EPS = 1e-6
RWKV_LN_EPS = 64e-5

LORA_W = 256
COL_U = 3 * D_RWKV
COL_LORA = COL_U + D_S5
D_Z = COL_LORA + LORA_W
LORA_AD = 2 * DECAY_LORA
LORA_GD = LORA_AD + 2 * AAA_LORA

CHUNK = 64
RWKV_BLK = 256
S5_T = 16
S5_ROWS = 8
S5_GS = 2
CONV_PAD = 72
CONV_TC = 256
CONV_PIECE = 128

NN = (((1,), (0,)), ((), ()))
NT = (((1,), (1,)), ((), ()))
TN = (((0,), (0,)), ((), ()))


def _dot(a, b, dims):
    return lax.dot_general(a, b, dims, preferred_element_type=F32)


def _split2(x):
    hi = x.astype(BF16)
    lo = (x - hi.astype(F32)).astype(BF16)
    return hi, lo


def _split3(x):
    hi = x.astype(BF16)
    r1 = x - hi.astype(F32)
    mid = r1.astype(BF16)
    lo = (r1 - mid.astype(F32)).astype(BF16)
    return hi, mid, lo


def _mm(a, b, mode="bf16", dims=NN):
    if mode == "bf16":
        return _dot(a.astype(BF16), b.astype(BF16), dims)
    if mode == "x3":
        ah, al = _split2(a)
        bh, bl = _split2(b)
        return _dot(ah, bh, dims) + (_dot(ah, bl, dims) + _dot(al, bh, dims))
    if mode == "xa":
        bb = b.astype(BF16)
        a1, a2, a3 = _split3(a)
        return _dot(a1, bb, dims) + (_dot(a2, bb, dims) + _dot(a3, bb, dims))
    if mode == "xb":
        ab = a.astype(BF16)
        b1, b2, b3 = _split3(b)
        return _dot(ab, b1, dims) + (_dot(ab, b2, dims) + _dot(ab, b3, dims))
    raise ValueError(mode)


def _sigmoid(x):
    return 1.0 / (1.0 + jnp.exp(-x))


def _softplus(x):
    return jnp.maximum(x, 0.0) + jnp.log(1.0 + jnp.exp(-jnp.abs(x)))


def _cparams(sem, vmem_mb):
    return pltpu.CompilerParams(dimension_semantics=sem, vmem_limit_bytes=vmem_mb * 1024 * 1024)


def _adaln_kernel(c_ref, w_ref, b_ref, o_ref):
    c = c_ref[...]
    s = c * _sigmoid(c)
    o_ref[...] = _mm(s, w_ref[...], "x3") + b_ref[...]


def _adaln(cc, mod_w, mod_b):
    rows, d = cc.shape
    n = mod_w.shape[1]
    tn = 1024
    return pl.pallas_call(
        _adaln_kernel,
        grid=(n // tn,),
        in_specs=[pl.BlockSpec((rows, d), lambda j: (0, 0)),
                  pl.BlockSpec((d, tn), lambda j: (0, j)),
                  pl.BlockSpec((1, tn), lambda j: (0, j))],
        out_specs=pl.BlockSpec((rows, tn), lambda j: (0, j)),
        out_shape=jax.ShapeDtypeStruct((rows, n), F32),
        compiler_params=_cparams(("parallel",), 40),
    )(cc, mod_w, mod_b.reshape(1, n))


def _inproj_kernel(x_ref, mod_ref, g_ref, w_ref, o_ref):
    x = x_ref[0]
    ms = jnp.mean(x * x, axis=-1, keepdims=True)
    xn = x * lax.rsqrt(ms + NORM_EPS) * g_ref[...]
    shift = mod_ref[0, 0:1, :]
    scale = mod_ref[0, 1:2, :]
    h = xn * (1.0 + scale) + shift
    o_ref[0] = _mm(h, w_ref[...], "bf16")


def _inproj(xcat, mod8, g1, w_z, ctx_blocks, tm):
    bsz, ltot, d = xcat.shape
    dz = w_z.shape[1]

    def mod_idx(b, j):
        return (jnp.where(j < ctx_blocks, bsz, b), 0, 0)

    return pl.pallas_call(
        _inproj_kernel,
        grid=(bsz, ltot // tm),
        in_specs=[pl.BlockSpec((1, tm, d), lambda b, j: (b, j, 0)),
                  pl.BlockSpec((1, 8, d), mod_idx),
                  pl.BlockSpec((1, d), lambda b, j: (0, 0)),
                  pl.BlockSpec((d, dz), lambda b, j: (0, 0))],
        out_specs=pl.BlockSpec((1, tm, dz), lambda b, j: (b, j, 0)),
        out_shape=jax.ShapeDtypeStruct((bsz, ltot, dz), F32),
        compiler_params=_cparams(("parallel", "parallel"), 48),
    )(xcat, mod8, g1.reshape(1, d), w_z)


def _conv_kernel(z_ref, w_ref, o_ref, xs_ref, *, ctx_len, lat_len):
    tc = z_ref.shape[2]
    pc = CONV_PIECE
    ctx_off = CONV_PAD
    lat_off = 2 * CONV_PAD + ctx_len
    zpad = jnp.zeros((CONV_PAD, tc), F32)
    for s in range(3):
        xs_ref[s, 0:CONV_PAD, :] = zpad
        xs_ref[s, ctx_off + ctx_len:lat_off, :] = zpad
        xs_ref[s, lat_off + lat_len:lat_off + lat_len + CONV_PAD, :] = zpad
    col = lax.broadcasted_iota(jnp.int32, (pc, tc), 0) % GRID_W
    keep_l = (col != GRID_W - 1).astype(F32)
    keep_r = (col != 0).astype(F32)
    for p in range(ctx_len // pc):
        xs_ref[0, ctx_off + p * pc:ctx_off + (p + 1) * pc, :] = z_ref[0, p * pc:(p + 1) * pc, :]
    for p in range(lat_len // pc):
        x = z_ref[0, ctx_len + p * pc:ctx_len + (p + 1) * pc, :]
        dst = slice(lat_off + p * pc, lat_off + (p + 1) * pc)
        xs_ref[0, dst, :] = x
        xs_ref[1, dst, :] = x * keep_l
        xs_ref[2, dst, :] = x * keep_r
    w = [w_ref[t:t + 1, :] for t in range(9)]
    for p in range(ctx_len // pc):
        base = ctx_off + p * pc
        acc = w[3] * xs_ref[0, base - 1:base - 1 + pc, :]
        acc = acc + w[4] * xs_ref[0, base:base + pc, :]
        acc = acc + w[5] * xs_ref[0, base + 1:base + 1 + pc, :]
        o_ref[0, p * pc:(p + 1) * pc, :] = acc
    for p in range(lat_len // pc):
        base = lat_off + p * pc
        acc = None
        for di in range(3):
            for dj in range(3):
                off = base + (di - 1) * GRID_W + (dj - 1)
                src = (1, 0, 2)[dj]
                term = w[3 * di + dj] * xs_ref[src, off:off + pc, :]
                acc = term if acc is None else acc + term
        o_ref[0, ctx_len + p * pc:ctx_len + (p + 1) * pc, :] = acc


def _conv(z, conv_w, ctx_len):
    bsz, ltot, _ = z.shape
    lat_len = ltot - ctx_len
    ch = 3 * D_RWKV
    rows = 3 * CONV_PAD + ltot
    kern = functools.partial(_conv_kernel, ctx_len=ctx_len, lat_len=lat_len)
    return pl.pallas_call(
        kern,
        grid=(bsz, ch // CONV_TC),
        in_specs=[pl.BlockSpec((1, ltot, CONV_TC), lambda b, j: (b, 0, j)),
                  pl.BlockSpec((9, CONV_TC), lambda b, j: (0, j))],
        out_specs=pl.BlockSpec((1, ltot, CONV_TC), lambda b, j: (b, 0, j)),
        out_shape=jax.ShapeDtypeStruct((bsz, ltot, ch), F32),
        scratch_shapes=[pltpu.VMEM((3, rows, CONV_TC), F32)],
        compiler_params=_cparams(("parallel", "parallel"), 48),
    )(z, conv_w.reshape(9, ch))


def _tri_inv_many(n_mats, size):
    ii = lax.broadcasted_iota(jnp.int32, (size, size), 0)
    jj = lax.broadcasted_iota(jnp.int32, (size, size), 1)
    eye = (ii == jj).astype(F32)
    in2 = (ii >> 1) == (jj >> 1)
    ts = [eye + jnp.where(in2, n, 0.0) for n in n_mats]
    lg = 1
    while (1 << lg) < size:
        same = (ii >> (lg + 1)) == (jj >> (lg + 1))
        diff = (ii >> lg) != (jj >> lg)
        nls = [jnp.where(same, jnp.where(diff, n, 0.0), 0.0) for n in n_mats]
        tmp = [_mm(t, nl) for t, nl in zip(ts, nls)]
        ts = [t + _mm(x, t) for x, t in zip(tmp, ts)]
        lg += 1
    return ts


def _rwkv_kernel(r_ref, k_ref, v_ref, lo_ref, w0_ref, w2_ref, a0_ref, a2_ref, kkw_ref, kaw_ref,
                 y_ref, h_ref, *, reverse):
    step = pl.program_id(1)

    @pl.when(step == 0)
    def _():
        h_ref[...] = jnp.zeros_like(h_ref)

    r = r_ref[0]
    k = k_ref[0]
    v = v_ref[0]
    lo = lo_ref[0]
    n = r.shape[0]
    c = CHUNK
    nsub = n // c
    lgc = c.bit_length() - 1

    logit_w = w0_ref[0] + _mm(jnp.tanh(lo), w2_ref[0])
    logw = -_softplus(-logit_w) - 0.5
    lw = -jnp.exp(logw)
    a = _sigmoid(a0_ref[0] + _mm(lo, a2_ref[0]))
    kd = k * (1.0 + (a - 1.0) * kaw_ref[...])
    kk = k * kkw_ref[...]

    ti = lax.broadcasted_iota(jnp.int32, (n, n), 0)
    tj = lax.broadcasted_iota(jnp.int32, (n, n), 1)
    causal = (tj >= ti) if reverse else (tj <= ti)
    tri = jnp.where((ti >> lgc) == (tj >> lgc), jnp.where(causal, 1.0, 0.0), 0.0)
    g = _mm(tri, lw, "xb")

    ci = lax.broadcasted_iota(jnp.int32, (c, c), 0)
    cj = lax.broadcasted_iota(jnp.int32, (c, c), 1)
    eye = (ci == cj).astype(F32)
    ci2 = lax.broadcasted_iota(jnp.int32, (c, 2 * c), 0)
    cj2 = lax.broadcasted_iota(jnp.int32, (c, 2 * c), 1) & (c - 1)
    incl2 = (cj2 >= ci2) if reverse else (cj2 <= ci2)
    strict2 = (cj2 > ci2) if reverse else (cj2 < ci2)
    zeros_cc = jnp.zeros((c, HEAD), F32)

    units = [(j, h) for j in range(nsub) for h in range(HEADS)]
    a_t, r_t, b_t, k_t, b_e, k_e, vh, g_end = {}, {}, {}, {}, {}, {}, {}, {}
    for j in range(nsub):
        rows = slice(j * c, (j + 1) * c)
        lw_j = lw[rows]
        g_j = g[rows]
        tot = jnp.sum(lw_j, axis=0, keepdims=True)
        e_prev = jnp.exp(g_j - lw_j)
        e_neg = jnp.exp(-g_j)
        e_pos = jnp.exp(g_j)
        e_end = jnp.exp(tot - g_j)
        e_tot = jnp.exp(tot)
        for h in range(HEADS):
            sl = slice(h * HEAD, (h + 1) * HEAD)
            kkh = kk[rows, sl]
            n2 = jnp.sum(kkh * kkh, axis=-1, keepdims=True)
            kkh = kkh * lax.rsqrt(jnp.maximum(n2, 1e-12))
            ka = kkh * a[rows, sl]
            u = (j, h)
            a_t[u] = -kkh * e_prev[:, sl]
            b_t[u] = ka * e_neg[:, sl]
            k_t[u] = kd[rows, sl] * e_neg[:, sl]
            r_t[u] = r[rows, sl] * e_pos[:, sl]
            b_e[u] = ka * e_end[:, sl]
            k_e[u] = kd[rows, sl] * e_end[:, sl]
            vh[u] = v[rows, sl]
            g_end[u] = e_tot[:, sl]

    s1 = [_mm(jnp.concatenate([a_t[u], r_t[u]], axis=0), jnp.concatenate([b_t[u], k_t[u]], axis=0),
              "bf16", NT) for u in units]
    top = [jnp.where(strict2, s[:c], 0.0) for s in s1]
    bot = [jnp.where(incl2, s[c:], 0.0) for s in s1]
    t_inv = _tri_inv_many([t[:, :c] for t in top], c)
    x1 = [_mm(t[:, c:], vh[u]) for t, u in zip(top, units)]
    uw = [_mm(ti_, jnp.concatenate([x, a_t[u]], axis=1)) for ti_, x, u in zip(t_inv, x1, units)]
    m2 = [jnp.concatenate([w_, jnp.concatenate([vh[u], zeros_cc], axis=1)], axis=0)
          for w_, u in zip(uw, units)]
    yq = [_mm(b_, m_) for b_, m_ in zip(bot, m2)]
    gp = [_mm(jnp.concatenate([b_e[u], k_e[u]], axis=0), m_, "bf16", TN)
          for m_, u in zip(m2, units)]
    qp_l = {}
    for idx, u in enumerate(units):
        q = r_t[u] + yq[idx][:, HEAD:]
        p = gp[idx][:, HEAD:] + eye * g_end[u]
        qp_l[u] = (jnp.concatenate([q, p], axis=0), yq[idx][:, :HEAD], gp[idx][:, :HEAD])

    hs = [h_ref[h] for h in range(HEADS)]
    order = range(nsub - 1, -1, -1) if reverse else range(nsub)
    for j in order:
        res = [_mm(qp_l[(j, h)][0], hs[h], "x3") for h in range(HEADS)]
        for h in range(HEADS):
            y_ref[0, j * c:(j + 1) * c, h * HEAD:(h + 1) * HEAD] = res[h][:c] + qp_l[(j, h)][1]
            hs[h] = res[h][c:] + qp_l[(j, h)][2]
    for h in range(HEADS):
        h_ref[h] = hs[h]


def _rwkv(rkvc, z, w0, w2p, a0, a2p, kkw, kaw, ctx_len, reverse):
    bsz, ltot, _ = rkvc.shape
    nblk = ltot // RWKV_BLK
    nctx = ctx_len // RWKV_BLK
    lora_blk = COL_LORA // LORA_W
    d = 1 if reverse else 0

    def tok(i):
        return jnp.where(i < nctx, nctx - 1 - i, nblk + nctx - 1 - i) if reverse else i

    par3 = lambda b, i: (d, 0, 0)
    kern = functools.partial(_rwkv_kernel, reverse=reverse)
    return pl.pallas_call(
        kern,
        grid=(bsz, nblk),
        in_specs=[pl.BlockSpec((1, RWKV_BLK, D_RWKV), lambda b, i: (b, tok(i), 0)),
                  pl.BlockSpec((1, RWKV_BLK, D_RWKV), lambda b, i: (b, tok(i), 1)),
                  pl.BlockSpec((1, RWKV_BLK, D_RWKV), lambda b, i: (b, tok(i), 2)),
                  pl.BlockSpec((1, RWKV_BLK, LORA_W), lambda b, i: (b, tok(i), lora_blk)),
                  pl.BlockSpec((1, 1, D_RWKV), par3),
                  pl.BlockSpec((1, LORA_W, D_RWKV), par3),
                  pl.BlockSpec((1, 1, D_RWKV), par3),
                  pl.BlockSpec((1, LORA_W, D_RWKV), par3),
                  pl.BlockSpec((1, D_RWKV), lambda b, i: (0, 0)),
                  pl.BlockSpec((1, D_RWKV), lambda b, i: (0, 0))],
        out_specs=pl.BlockSpec((1, RWKV_BLK, D_RWKV), lambda b, i: (b, tok(i), 0)),
        out_shape=jax.ShapeDtypeStruct((bsz, ltot, D_RWKV), F32),
        scratch_shapes=[pltpu.VMEM((HEADS, HEAD, HEAD), F32)],
        compiler_params=_cparams(("parallel", "arbitrary"), 48),
    )(rkvc, rkvc, rkvc, z, w0, w2p, a0, a2p, kkw, kaw)


def _s5_consts(lam_re, lam_im, log_step, b_re, b_im, c_re, c_im):
    t = S5_T
    g, p, hg = b_re.shape
    step = jnp.exp(log_step)[..., None]
    mag = jnp.exp(lam_re * step)
    lb_re = mag * jnp.cos(lam_im * step)
    lb_im = mag * jnp.sin(lam_im * step)
    den = lam_re * lam_re + lam_im * lam_im
    nr = lb_re - 1
    q_re = (nr * lam_re + lb_im * lam_im) / den
    q_im = (lb_im * lam_re - nr * lam_im) / den
    bb_re = q_re[..., None] * b_re - q_im[..., None] * b_im
    bb_im = q_re[..., None] * b_im + q_im[..., None] * b_re
    n = jnp.arange(t + 1, dtype=F32)[:, None, None, None]
    pmag = jnp.exp(lam_re * step * n)
    pw_re = pmag * jnp.cos(lam_im * step * n)
    pw_im = pmag * jnp.sin(lam_im * step * n)
    cp_re = c_re[None, None] * pw_re[:, :, :, None, :] - c_im[None, None] * pw_im[:, :, :, None, :]
    cp_im = c_re[None, None] * pw_im[:, :, :, None, :] + c_im[None, None] * pw_re[:, :, :, None, :]
    hp = lax.Precision.HIGHEST
    kern = (jnp.einsum("ndgop,dgpi->ndgoi", cp_re[:t], bb_re, precision=hp)
            - jnp.einsum("ndgop,dgpi->ndgoi", cp_im[:t], bb_im, precision=hp))
    ss = jnp.arange(t)[:, None]
    tt = jnp.arange(t)[None, :]
    kf = kern[jnp.clip(tt - ss, 0, t - 1), 0] * (tt >= ss)[:, :, None, None, None]
    kb = kern[jnp.clip(ss - tt, 0, t - 1), 1] * (ss >= tt)[:, :, None, None, None]
    m_tot = jnp.transpose(kf + kb, (2, 0, 4, 1, 3)).reshape(g, t * hg, t * hg)

    def inject(d, powers):
        pr = pw_re[powers, d][..., None] * bb_re[d][None] - pw_im[powers, d][..., None] * bb_im[d][None]
        pi = pw_re[powers, d][..., None] * bb_im[d][None] + pw_im[powers, d][..., None] * bb_re[d][None]
        return (jnp.transpose(pr, (1, 0, 3, 2)).reshape(g, t * hg, p),
                jnp.transpose(pi, (1, 0, 3, 2)).reshape(g, t * hg, p))

    pf_re, pf_im = inject(0, t - 1 - jnp.arange(t))
    pb_re, pb_im = inject(1, jnp.arange(t))
    p_all = jnp.concatenate([pf_re, pb_re, pf_im, pb_im], axis=-1)

    def readout(d, powers):
        qr = jnp.transpose(cp_re[powers, d], (1, 3, 0, 2)).reshape(g, p, t * hg)
        qi = jnp.transpose(-cp_im[powers, d], (1, 3, 0, 2)).reshape(g, p, t * hg)
        return qr, qi

    qf_re, qf_im = readout(0, jnp.arange(t) + 1)
    qb_re, qb_im = readout(1, t - jnp.arange(t))
    q_re_all = jnp.concatenate([qf_re, qb_re], axis=1)
    q_im_all = jnp.concatenate([qf_im, qb_im], axis=1)
    l_re = jnp.concatenate([pw_re[t, 0], pw_re[t, 1]], axis=-1)
    l_im = jnp.concatenate([pw_im[t, 0], pw_im[t, 1]], axis=-1)
    l_all = jnp.stack([l_re, l_im], axis=1)
    l_all = jnp.concatenate([l_all, jnp.zeros((g, 6, 2 * p), F32)], axis=1)
    return m_tot, p_all, q_re_all, q_im_all, l_all


def _s5_kernel(u_ref, m_ref, p_ref, qre_ref, qim_ref, l_ref, y_ref, z_ref, hre_ref, him_ref,
               *, nchunk, nctx):
    gs = u_ref.shape[0]
    ps = S5_STATE
    for gi in range(gs):
        z_ref[gi] = _mm(u_ref[gi], p_ref[gi])
    lane = lax.broadcasted_iota(jnp.int32, (S5_ROWS, 2 * ps), 1)
    is_f = lane < ps

    def body(i, carry):
        cb = jnp.where(i < nctx, nctx - 1 - i, nchunk + nctx - 1 - i)
        rf = pl.multiple_of(i * S5_ROWS, S5_ROWS)
        rb = pl.multiple_of(cb * S5_ROWS, S5_ROWS)
        out = []
        for gi in range(gs):
            hre, him = carry[gi]
            hre_ref[gi, pl.ds(rf, S5_ROWS), 0:ps] = hre[:, 0:ps]
            hre_ref[gi, pl.ds(rb, S5_ROWS), ps:2 * ps] = hre[:, ps:2 * ps]
            him_ref[gi, pl.ds(rf, S5_ROWS), 0:ps] = him[:, 0:ps]
            him_ref[gi, pl.ds(rb, S5_ROWS), ps:2 * ps] = him[:, ps:2 * ps]
            zf = z_ref[gi, pl.ds(rf, S5_ROWS), :]
            zb = z_ref[gi, pl.ds(rb, S5_ROWS), :]
            zre = jnp.where(is_f, zf[:, 0:2 * ps], zb[:, 0:2 * ps])
            zim = jnp.where(is_f, zf[:, 2 * ps:4 * ps], zb[:, 2 * ps:4 * ps])
            lre = l_ref[gi, 0:1, :]
            lim = l_ref[gi, 1:2, :]
            out.append((lre * hre - lim * him + zre, lre * him + lim * hre + zim))
        return tuple(out)

    zero = jnp.zeros((S5_ROWS, 2 * ps), F32)
    lax.fori_loop(0, nchunk, body, tuple((zero, zero) for _ in range(gs)))
    for gi in range(gs):
        y = _mm(u_ref[gi], m_ref[gi])
        y = y + _mm(hre_ref[gi], qre_ref[gi])
        y = y + _mm(him_ref[gi], qim_ref[gi])
        y_ref[gi] = y


def _s5(ug, consts, nchunk, nctx):
    m_tot, p_all, q_re, q_im, l_all = consts
    g, rows, w = ug.shape
    ps2 = 2 * S5_STATE
    kern = functools.partial(_s5_kernel, nchunk=nchunk, nctx=nctx)
    blk = lambda a, b: pl.BlockSpec((S5_GS, a, b), lambda j: (j, 0, 0))
    return pl.pallas_call(
        kern,
        grid=(g // S5_GS,),
        in_specs=[blk(rows, w), blk(w, w), blk(w, 2 * ps2), blk(ps2, w), blk(ps2, w), blk(8, ps2)],
        out_specs=blk(rows, w),
        out_shape=jax.ShapeDtypeStruct((g, rows, w), F32),
        scratch_shapes=[pltpu.VMEM((S5_GS, rows, 2 * ps2), F32),
                        pltpu.VMEM((S5_GS, rows, ps2), F32),
                        pltpu.VMEM((S5_GS, rows, ps2), F32)],
        compiler_params=_cparams(("parallel",), 48),
    )(ug, m_tot, p_all, q_re, q_im, l_all)


def _mix_kernel(x_ref, yf_ref, yb_ref, r_ref, k_ref, v_ref, lo_ref, u_ref, ys_ref, mod_ref,
                seg_ref, a0_ref, a2_ref, g2_ref, kaw_ref, rk_ref, lnw_ref, lnb_ref,
                dsk_ref, gluw_ref, glub_ref, wo_ref, o_ref):
    lo = lo_ref[0]
    seg = seg_ref[...]
    y = yf_ref[0] + yb_ref[0]
    mu = _mm(y, seg, "xa") * (1.0 / HEAD)
    yc = y - mu
    var = _mm(yc * yc, seg, "xa") * (1.0 / HEAD)
    yn = yc * lax.rsqrt(var + RWKV_LN_EPS) * lnw_ref[...] + lnb_ref[...]
    a_f = _sigmoid(a0_ref[0] + _mm(lo, a2_ref[0]))
    a_b = _sigmoid(a0_ref[1] + _mm(lo, a2_ref[1]))
    kaw = kaw_ref[...]
    kd_sum = k_ref[0] * ((1.0 + (a_f - 1.0) * kaw) + (1.0 + (a_b - 1.0) * kaw))
    bonus = _mm(r_ref[0] * kd_sum * rk_ref[...], seg, "xa")
    yn = yn + bonus * v_ref[0]
    gate = _mm(_sigmoid(lo), g2_ref[...])
    rwkv_out = yn * gate

    u = u_ref[0]
    ys = ys_ref[0] + dsk_ref[...] * u
    zg = 0.5 * ys * (1.0 + jnp.tanh(math.sqrt(2.0 / math.pi) * (ys + 0.044715 * (ys * ys * ys))))
    s5_out = zg * _sigmoid(_mm(zg, gluw_ref[...]) + glub_ref[...])

    mix = _mm(rwkv_out, wo_ref[0:D_RWKV, :]) + _mm(s5_out, wo_ref[D_RWKV:, :])
    o_ref[0] = x_ref[0] + mod_ref[0, 2:3, :] * mix


def _mix(x, y_f, y_b, rkvc, z, ys5, mod8, seg, a0, a2p, g2p, kaw, rk, lnw, lnb, dsk, gluw, glub, wo,
         ctx_len, tm):
    bsz, seq, d = x.shape
    off = ctx_len // tm
    lora_blk = COL_LORA // LORA_W
    u_blk = COL_U // D_S5
    tok = lambda c: pl.BlockSpec((1, tm, D_RWKV), lambda b, j: (b, j + off, c))
    full = lambda a: pl.BlockSpec(a.shape, lambda b, j: (0,) * a.ndim)
    return pl.pallas_call(
        _mix_kernel,
        grid=(bsz, seq // tm),
        in_specs=[pl.BlockSpec((1, tm, d), lambda b, j: (b, j, 0)),
                  tok(0), tok(0), tok(0), tok(1), tok(2),
                  pl.BlockSpec((1, tm, LORA_W), lambda b, j: (b, j + off, lora_blk)),
                  pl.BlockSpec((1, tm, D_S5), lambda b, j: (b, j + off, u_blk)),
                  pl.BlockSpec((1, tm, D_S5), lambda b, j: (b, j, 0)),
                  pl.BlockSpec((1, 8, d), lambda b, j: (b, 0, 0)),
                  full(seg), full(a0), full(a2p), full(g2p), full(kaw), full(rk), full(lnw), full(lnb),
                  full(dsk), full(gluw), full(glub), full(wo)],
        out_specs=pl.BlockSpec((1, tm, d), lambda b, j: (b, j, 0)),
        out_shape=jax.ShapeDtypeStruct((bsz, seq, d), F32),
        compiler_params=_cparams(("parallel", "parallel"), 48),
    )(x, y_f, y_b, rkvc, rkvc, rkvc, z, z, ys5, mod8, seg, a0, a2p, g2p, kaw, rk, lnw, lnb,
      dsk, gluw, glub, wo)


def _ffn_kernel(x_ref, mod_ref, g2_ref, w1_ref, w3_ref, w2_ref, gf_ref, o_ref):
    x = x_ref[0]
    ms = jnp.mean(x * x, axis=-1, keepdims=True)
    h = x * lax.rsqrt(ms + NORM_EPS) * g2_ref[...]
    h = h * (1.0 + mod_ref[0, 4:5, :]) + mod_ref[0, 3:4, :]
    hb = h.astype(BF16)
    h1 = _dot(hb, w1_ref[...], NN)
    h3 = _dot(hb, w3_ref[...], NN)
    act = (h1 * _sigmoid(h1)) * h3
    f = _dot(act.astype(BF16), w2_ref[...], NN)
    out = x + mod_ref[0, 5:6, :] * f
    ms2 = jnp.mean(out * out, axis=-1, keepdims=True)
    o_ref[0] = out * lax.rsqrt(ms2 + NORM_EPS) * gf_ref[...]


def _ffn(x1, mod8, g2, w1, w3, w2, gf, tm):
    bsz, seq, d = x1.shape
    dff = w1.shape[1]
    const = lambda shape: pl.BlockSpec(shape, lambda b, j: (0, 0), pipeline_mode=pl.Buffered(1))
    return pl.pallas_call(
        _ffn_kernel,
        grid=(bsz, seq // tm),
        in_specs=[pl.BlockSpec((1, tm, d), lambda b, j: (b, j, 0)),
                  pl.BlockSpec((1, 8, d), lambda b, j: (b, 0, 0)),
                  const((1, d)), const((d, dff)), const((d, dff)), const((dff, d)), const((1, d))],
        out_specs=pl.BlockSpec((1, tm, d), lambda b, j: (b, j, 0)),
        out_shape=jax.ShapeDtypeStruct((bsz, seq, d), F32),
        compiler_params=_cparams(("parallel", "parallel"), 56),
    )(x1, mod8, g2.reshape(1, d), w1, w3, w2, gf.reshape(1, d))


def _pad_rows(w, row0, rows_total):
    pad = [(0, 0)] * (w.ndim - 2) + [(row0, rows_total - row0 - w.shape[-2]), (0, 0)]
    return jnp.pad(w, pad)


def kernel(x, c, ctx, c_ctx, mod_w, mod_b, norm1_g, norm2_g, w_in, w_out, rwkv_conv, rwkv_w0, rwkv_w2, rwkv_a0, rwkv_a2, rwkv_g2, rwkv_kk, rwkv_ka, rwkv_rk, rwkv_ln_w, rwkv_ln_b, s5_lam_re, s5_lam_im, s5_log_step, s5_b_re, s5_b_im, s5_c_re, s5_c_im, s5_d, s5_glu_w, s5_glu_b, ffn_w1, ffn_w3, ffn_w2, final_g):
    bsz, seq, d = x.shape
    ctx_len = ctx.shape[1]
    ltot = ctx_len + seq
    layer = 0

    cc = jnp.concatenate([c, c_ctx[None, :], jnp.zeros((8 - bsz - 1, d), F32)], axis=0)
    mod = _adaln(cc, mod_w[layer], mod_b[layer])
    mod8 = jnp.pad(mod[:bsz + 1].reshape(bsz + 1, N_MOD, d), ((0, 0), (0, 8 - N_MOD), (0, 0)))

    wi = w_in[layer]
    w_z = jnp.concatenate([wi[:, :3 * D_RWKV], wi[:, -D_S5:], wi[:, 3 * D_RWKV:-D_S5],
                           jnp.zeros((d, D_Z - wi.shape[1]), F32)], axis=1).astype(BF16)
    xcat = jnp.concatenate([ctx, x], axis=1)
    tm_in = 256
    z = _inproj(xcat, mod8, norm1_g[layer], w_z, ctx_len // tm_in, tm_in)

    rkvc = _conv(z, rwkv_conv[layer], ctx_len)

    w2p = _pad_rows(rwkv_w2[layer], 0, LORA_W).at[1].set(_pad_rows(rwkv_w2[layer, 1], DECAY_LORA, LORA_W))
    a2p = jnp.stack([_pad_rows(rwkv_a2[layer, 0], LORA_AD, LORA_W),
                     _pad_rows(rwkv_a2[layer, 1], LORA_AD + AAA_LORA, LORA_W)])
    w0 = rwkv_w0[layer].reshape(2, 1, D_RWKV)
    a0 = rwkv_a0[layer].reshape(2, 1, D_RWKV)
    kkw = rwkv_kk[layer].reshape(1, D_RWKV)
    kaw = rwkv_ka[layer].reshape(1, D_RWKV)
    y_f = _rwkv(rkvc, z, w0, w2p, a0, a2p, kkw, kaw, ctx_len, False)
    y_b = _rwkv(rkvc, z, w0, w2p, a0, a2p, kkw, kaw, ctx_len, True)

    nchunk = ltot // S5_T
    u = z[:, :, COL_U:COL_U + D_S5].reshape(bsz, nchunk, S5_T, S5_GROUPS, S5_GROUP)
    ug = jnp.transpose(u, (3, 1, 0, 2, 4))
    ug = jnp.pad(ug, ((0, 0), (0, 0), (0, S5_ROWS - bsz), (0, 0), (0, 0)))
    ug = ug.reshape(S5_GROUPS, nchunk * S5_ROWS, S5_T * S5_GROUP)
    consts = _s5_consts(s5_lam_re[layer], s5_lam_im[layer], s5_log_step[layer], s5_b_re[layer],
                        s5_b_im[layer], s5_c_re[layer], s5_c_im[layer])
    yg = _s5(ug, consts, nchunk, ctx_len // S5_T)
    yg = yg.reshape(S5_GROUPS, nchunk, S5_ROWS, S5_T, S5_GROUP)[:, ctx_len // S5_T:, :bsz]
    ys5 = jnp.transpose(yg, (2, 1, 3, 0, 4)).reshape(bsz, seq, D_S5)

    hh = jnp.arange(D_RWKV) // HEAD
    seg = (hh[:, None] == hh[None, :]).astype(BF16)
    g2p = _pad_rows(rwkv_g2[layer], LORA_GD, LORA_W)
    x1 = _mix(x, y_f, y_b, rkvc, z, ys5, mod8, seg, a0, a2p, g2p, kaw,
              rwkv_rk[layer].reshape(1, D_RWKV), rwkv_ln_w[layer].reshape(1, D_RWKV),
              rwkv_ln_b[layer].reshape(1, D_RWKV), s5_d[layer].reshape(1, D_S5),
              s5_glu_w[layer].astype(BF16), s5_glu_b[layer].reshape(1, D_S5),
              w_out[layer].astype(BF16), ctx_len, 256)

    return _ffn(x1, mod8, norm2_g[layer], ffn_w1[layer].astype(BF16), ffn_w3[layer].astype(BF16),
                ffn_w2[layer].astype(BF16), final_g, 256)
```

```python
import functools
import math

import jax
import jax.numpy as jnp
from jax import lax
from jax.experimental import pallas as pl
from jax.experimental.pallas import tpu as pltpu

F32 = jnp.float32
BF16 = jnp.bfloat16

D_MODEL = 1024
GRID_W = 64
D_RWKV = 512
HEAD = 64
HEADS = D_RWKV // HEAD
D_S5 = 512
S5_GROUP = 16
S5_GROUPS = D_S5 // S5_GROUP
S5_STATE = 64
DECAY_LORA = 32
AAA_LORA = 32
GATE_LORA = 96
N_MOD = 6
NORM_EPS = 1e-6
RWKV_LN_EPS = 64e-5

LORA_W = 256
COL_U = 3 * D_RWKV
COL_LORA = COL_U + D_S5
D_Z = COL_LORA + LORA_W
LORA_AD = 2 * DECAY_LORA
LORA_GD = LORA_AD + 2 * AAA_LORA

CHUNK = 64
RWKV_BLK = 256
S5_T = 16
S5_GH = 16
CONV_PAD = 72
CONV_TC = 256
CONV_PIECE = 128

NN = (((1,), (0,)), ((), ()))
NT = (((1,), (1,)), ((), ()))
TN = (((0,), (0,)), ((), ()))


def _dot(a, b, dims):
    return lax.dot_general(a, b, dims, preferred_element_type=F32)


def _split2(x):
    hi = x.astype(BF16)
    lo = (x - hi.astype(F32)).astype(BF16)
    return hi, lo


def _split3(x):
    hi = x.astype(BF16)
    r1 = x - hi.astype(F32)
    mid = r1.astype(BF16)
    lo = (r1 - mid.astype(F32)).astype(BF16)
    return hi, mid, lo


def _mm(a, b, mode="bf16", dims=NN):
    if mode == "bf16":
        return _dot(a.astype(BF16), b.astype(BF16), dims)
    if mode == "x3":
        ah, al = _split2(a)
        bh, bl = _split2(b)
        return _dot(ah, bh, dims) + (_dot(ah, bl, dims) + _dot(al, bh, dims))
    if mode == "xa":
        bb = b.astype(BF16)
        a1, a2, a3 = _split3(a)
        return _dot(a1, bb, dims) + (_dot(a2, bb, dims) + _dot(a3, bb, dims))
    if mode == "xb":
        ab = a.astype(BF16)
        b1, b2, b3 = _split3(b)
        return _dot(ab, b1, dims) + (_dot(ab, b2, dims) + _dot(ab, b3, dims))
    raise ValueError(mode)


def _sigmoid(x):
    return 1.0 / (1.0 + jnp.exp(-x))


def _softplus(x):
    return jnp.maximum(x, 0.0) + jnp.log(1.0 + jnp.exp(-jnp.abs(x)))


def _cparams(sem, vmem_mb):
    return pltpu.CompilerParams(dimension_semantics=sem, vmem_limit_bytes=vmem_mb * 1024 * 1024)


def _adaln_kernel(c_ref, w_ref, b_ref, o_ref):
    c = c_ref[...]
    s = c * _sigmoid(c)
    o_ref[...] = _mm(s, w_ref[...], "x3") + b_ref[...]


def _adaln(cc, mod_w, mod_b):
    rows, d = cc.shape
    n = mod_w.shape[1]
    tn = 1024
    return pl.pallas_call(
        _adaln_kernel,
        grid=(n // tn,),
        in_specs=[pl.BlockSpec((rows, d), lambda j: (0, 0)),
                  pl.BlockSpec((d, tn), lambda j: (0, j)),
                  pl.BlockSpec((1, tn), lambda j: (0, j))],
        out_specs=pl.BlockSpec((rows, tn), lambda j: (0, j)),
        out_shape=jax.ShapeDtypeStruct((rows, n), F32),
        compiler_params=_cparams(("parallel",), 40),
    )(cc, mod_w, mod_b.reshape(1, n))


def _inproj_kernel(x_ref, mod_ref, g_ref, w_ref, o_ref):
    x = x_ref[0]
    ms = jnp.mean(x * x, axis=-1, keepdims=True)
    xn = x * lax.rsqrt(ms + NORM_EPS) * g_ref[...]
    shift = mod_ref[0, 0:1, :]
    scale = mod_ref[0, 1:2, :]
    h = xn * (1.0 + scale) + shift
    o_ref[0] = _mm(h, w_ref[...], "bf16")


def _inproj(xcat, mod8, g1, w_z, ctx_blocks, tm):
    bsz, ltot, d = xcat.shape
    dz = w_z.shape[1]

    def mod_idx(b, j):
        return (jnp.where(j < ctx_blocks, bsz, b), 0, 0)

    return pl.pallas_call(
        _inproj_kernel,
        grid=(bsz, ltot // tm),
        in_specs=[pl.BlockSpec((1, tm, d), lambda b, j: (b, j, 0)),
                  pl.BlockSpec((1, 8, d), mod_idx),
                  pl.BlockSpec((1, d), lambda b, j: (0, 0)),
                  pl.BlockSpec((d, dz), lambda b, j: (0, 0))],
        out_specs=pl.BlockSpec((1, tm, dz), lambda b, j: (b, j, 0)),
        out_shape=jax.ShapeDtypeStruct((bsz, ltot, dz), F32),
        compiler_params=_cparams(("parallel", "parallel"), 48),
    )(xcat, mod8, g1.reshape(1, d), w_z)


def _conv_kernel(z_ref, w_ref, o_ref, xs_ref, *, ctx_len, lat_len):
    tc = z_ref.shape[2]
    pc = CONV_PIECE
    ctx_off = CONV_PAD
    lat_off = 2 * CONV_PAD + ctx_len
    zpad = jnp.zeros((CONV_PAD, tc), F32)
    for s in range(3):
        xs_ref[s, 0:CONV_PAD, :] = zpad
        xs_ref[s, ctx_off + ctx_len:lat_off, :] = zpad
        xs_ref[s, lat_off + lat_len:lat_off + lat_len + CONV_PAD, :] = zpad
    col = lax.broadcasted_iota(jnp.int32, (pc, tc), 0) % GRID_W
    keep_l = (col != GRID_W - 1).astype(F32)
    keep_r = (col != 0).astype(F32)
    for p in range(ctx_len // pc):
        xs_ref[0, ctx_off + p * pc:ctx_off + (p + 1) * pc, :] = z_ref[0, p * pc:(p + 1) * pc, :]
    for p in range(lat_len // pc):
        x = z_ref[0, ctx_len + p * pc:ctx_len + (p + 1) * pc, :]
        dst = slice(lat_off + p * pc, lat_off + (p + 1) * pc)
        xs_ref[0, dst, :] = x
        xs_ref[1, dst, :] = x * keep_l
        xs_ref[2, dst, :] = x * keep_r
    w = [w_ref[t:t + 1, :] for t in range(9)]
    for p in range(ctx_len // pc):
        base = ctx_off + p * pc
        acc = w[3] * xs_ref[0, base - 1:base - 1 + pc, :]
        acc = acc + w[4] * xs_ref[0, base:base + pc, :]
        acc = acc + w[5] * xs_ref[0, base + 1:base + 1 + pc, :]
        o_ref[0, p * pc:(p + 1) * pc, :] = acc
    for p in range(lat_len // pc):
        base = lat_off + p * pc
        acc = None
        for di in range(3):
            for dj in range(3):
                off = base + (di - 1) * GRID_W + (dj - 1)
                src = (1, 0, 2)[dj]
                term = w[3 * di + dj] * xs_ref[src, off:off + pc, :]
                acc = term if acc is None else acc + term
        o_ref[0, ctx_len + p * pc:ctx_len + (p + 1) * pc, :] = acc


def _conv(z, conv_w, ctx_len):
    bsz, ltot, _ = z.shape
    lat_len = ltot - ctx_len
    ch = 3 * D_RWKV
    rows = 3 * CONV_PAD + ltot
    kern = functools.partial(_conv_kernel, ctx_len=ctx_len, lat_len=lat_len)
    return pl.pallas_call(
        kern,
        grid=(bsz, ch // CONV_TC),
        in_specs=[pl.BlockSpec((1, ltot, CONV_TC), lambda b, j: (b, 0, j)),
                  pl.BlockSpec((9, CONV_TC), lambda b, j: (0, j))],
        out_specs=pl.BlockSpec((1, ltot, CONV_TC), lambda b, j: (b, 0, j)),
        out_shape=jax.ShapeDtypeStruct((bsz, ltot, ch), F32),
        scratch_shapes=[pltpu.VMEM((3, rows, CONV_TC), F32)],
        compiler_params=_cparams(("parallel", "parallel"), 48),
    )(z, conv_w.reshape(9, ch))


def _tri_inv_many(n_mats, size):
    ii = lax.broadcasted_iota(jnp.int32, (size, size), 0)
    jj = lax.broadcasted_iota(jnp.int32, (size, size), 1)
    eye = (ii == jj).astype(F32)
    in2 = (ii >> 1) == (jj >> 1)
    ts = [eye + jnp.where(in2, n, 0.0) for n in n_mats]
    lg = 1
    while (1 << lg) < size:
        same = (ii >> (lg + 1)) == (jj >> (lg + 1))
        diff = (ii >> lg) != (jj >> lg)
        nls = [jnp.where(same, jnp.where(diff, n, 0.0), 0.0) for n in n_mats]
        tmp = [_mm(t, nl) for t, nl in zip(ts, nls)]
        ts = [t + _mm(x, t) for x, t in zip(tmp, ts)]
        lg += 1
    return ts


def _rwkv_kernel(r_ref, k_ref, v_ref, lo_ref, w0_ref, w2_ref, a0_ref, a2_ref, kkw_ref, kaw_ref,
                 y_ref, h_ref, *, reverse):
    step = pl.program_id(1)

    @pl.when(step == 0)
    def _():
        h_ref[...] = jnp.zeros_like(h_ref)

    r = r_ref[0]
    k = k_ref[0]
    v = v_ref[0]
    lo = lo_ref[0]
    n = r.shape[0]
    c = CHUNK
    nsub = n // c
    lgc = c.bit_length() - 1

    logit_w = w0_ref[0] + _mm(jnp.tanh(lo), w2_ref[0])
    logw = -_softplus(-logit_w) - 0.5
    lw = -jnp.exp(logw)
    a = _sigmoid(a0_ref[0] + _mm(lo, a2_ref[0]))
    kd = k * (1.0 + (a - 1.0) * kaw_ref[...])
    kk = k * kkw_ref[...]

    ti = lax.broadcasted_iota(jnp.int32, (n, n), 0)
    tj = lax.broadcasted_iota(jnp.int32, (n, n), 1)
    causal = (tj >= ti) if reverse else (tj <= ti)
    tri = jnp.where((ti >> lgc) == (tj >> lgc), jnp.where(causal, 1.0, 0.0), 0.0)
    g = _mm(tri, lw, "xb")

    ci = lax.broadcasted_iota(jnp.int32, (c, c), 0)
    cj = lax.broadcasted_iota(jnp.int32, (c, c), 1)
    eye = (ci == cj).astype(F32)
    ci2 = lax.broadcasted_iota(jnp.int32, (c, 2 * c), 0)
    cj2 = lax.broadcasted_iota(jnp.int32, (c, 2 * c), 1) & (c - 1)
    incl2 = (cj2 >= ci2) if reverse else (cj2 <= ci2)
    strict2 = (cj2 > ci2) if reverse else (cj2 < ci2)
    zeros_cc = jnp.zeros((c, HEAD), F32)

    units = [(j, h) for j in range(nsub) for h in range(HEADS)]
    a_t, r_t, b_t, k_t, b_e, k_e, vh, g_end = {}, {}, {}, {}, {}, {}, {}, {}
    for j in range(nsub):
        rows = slice(j * c, (j + 1) * c)
        lw_j = lw[rows]
        g_j = g[rows]
        tot = jnp.sum(lw_j, axis=0, keepdims=True)
        e_prev = jnp.exp(g_j - lw_j)
        e_neg = jnp.exp(-g_j)
        e_pos = jnp.exp(g_j)
        e_end = jnp.exp(tot - g_j)
        e_tot = jnp.exp(tot)
        for h in range(HEADS):
            sl = slice(h * HEAD, (h + 1) * HEAD)
            kkh = kk[rows, sl]
            n2 = jnp.sum(kkh * kkh, axis=-1, keepdims=True)
            kkh = kkh * lax.rsqrt(jnp.maximum(n2, 1e-12))
            ka = kkh * a[rows, sl]
            u = (j, h)
            a_t[u] = -kkh * e_prev[:, sl]
            b_t[u] = ka * e_neg[:, sl]
            k_t[u] = kd[rows, sl] * e_neg[:, sl]
            r_t[u] = r[rows, sl] * e_pos[:, sl]
            b_e[u] = ka * e_end[:, sl]
            k_e[u] = kd[rows, sl] * e_end[:, sl]
            vh[u] = v[rows, sl]
            g_end[u] = e_tot[:, sl]

    s1 = [_mm(jnp.concatenate([a_t[u], r_t[u]], axis=0), jnp.concatenate([b_t[u], k_t[u]], axis=0),
              "bf16", NT) for u in units]
    top = [jnp.where(strict2, s[:c], 0.0) for s in s1]
    bot = [jnp.where(incl2, s[c:], 0.0) for s in s1]
    t_inv = _tri_inv_many([t[:, :c] for t in top], c)
    x1 = [_mm(t[:, c:], vh[u]) for t, u in zip(top, units)]
    uw = [_mm(ti_, jnp.concatenate([x, a_t[u]], axis=1)) for ti_, x, u in zip(t_inv, x1, units)]
    m2 = [jnp.concatenate([w_, jnp.concatenate([vh[u], zeros_cc], axis=1)], axis=0)
          for w_, u in zip(uw, units)]
    yq = [_mm(b_, m_) for b_, m_ in zip(bot, m2)]
    gp = [_mm(jnp.concatenate([b_e[u], k_e[u]], axis=0), m_, "bf16", TN)
          for m_, u in zip(m2, units)]
    qp_l = {}
    for idx, u in enumerate(units):
        q = r_t[u] + yq[idx][:, HEAD:]
        p = gp[idx][:, HEAD:] + eye * g_end[u]
        qp_l[u] = (jnp.concatenate([q, p], axis=0), yq[idx][:, :HEAD], gp[idx][:, :HEAD])

    hs = [h_ref[h] for h in range(HEADS)]
    order = range(nsub - 1, -1, -1) if reverse else range(nsub)
    for j in order:
        res = [_mm(qp_l[(j, h)][0], hs[h], "x3") for h in range(HEADS)]
        for h in range(HEADS):
            y_ref[0, j * c:(j + 1) * c, h * HEAD:(h + 1) * HEAD] = res[h][:c] + qp_l[(j, h)][1]
            hs[h] = res[h][c:] + qp_l[(j, h)][2]
    for h in range(HEADS):
        h_ref[h] = hs[h]


def _rwkv(rkvc, z, w0, w2p, a0, a2p, kkw, kaw, ctx_len, reverse):
    bsz, ltot, _ = rkvc.shape
    nblk = ltot // RWKV_BLK
    nctx = ctx_len // RWKV_BLK
    lora_blk = COL_LORA // LORA_W
    d = 1 if reverse else 0

    def tok(i):
        return jnp.where(i < nctx, nctx - 1 - i, nblk + nctx - 1 - i) if reverse else i

    par3 = lambda b, i: (d, 0, 0)
    kern = functools.partial(_rwkv_kernel, reverse=reverse)
    return pl.pallas_call(
        kern,
        grid=(bsz, nblk),
        in_specs=[pl.BlockSpec((1, RWKV_BLK, D_RWKV), lambda b, i: (b, tok(i), 0)),
                  pl.BlockSpec((1, RWKV_BLK, D_RWKV), lambda b, i: (b, tok(i), 1)),
                  pl.BlockSpec((1, RWKV_BLK, D_RWKV), lambda b, i: (b, tok(i), 2)),
                  pl.BlockSpec((1, RWKV_BLK, LORA_W), lambda b, i: (b, tok(i), lora_blk)),
                  pl.BlockSpec((1, 1, D_RWKV), par3),
                  pl.BlockSpec((1, LORA_W, D_RWKV), par3),
                  pl.BlockSpec((1, 1, D_RWKV), par3),
                  pl.BlockSpec((1, LORA_W, D_RWKV), par3),
                  pl.BlockSpec((1, D_RWKV), lambda b, i: (0, 0)),
                  pl.BlockSpec((1, D_RWKV), lambda b, i: (0, 0))],
        out_specs=pl.BlockSpec((1, RWKV_BLK, D_RWKV), lambda b, i: (b, tok(i), 0)),
        out_shape=jax.ShapeDtypeStruct((bsz, ltot, D_RWKV), F32),
        scratch_shapes=[pltpu.VMEM((HEADS, HEAD, HEAD), F32)],
        compiler_params=_cparams(("parallel", "arbitrary"), 48),
    )(rkvc, rkvc, rkvc, z, w0, w2p, a0, a2p, kkw, kaw)


def _s5_consts(lam_re, lam_im, log_step, b_re, b_im, c_re, c_im):
    t = S5_T
    g, p, hg = b_re.shape
    step = jnp.exp(log_step)[..., None]
    mag = jnp.exp(lam_re * step)
    lb_re = mag * jnp.cos(lam_im * step)
    lb_im = mag * jnp.sin(lam_im * step)
    den = lam_re * lam_re + lam_im * lam_im
    nr = lb_re - 1
    q_re = (nr * lam_re + lb_im * lam_im) / den
    q_im = (lb_im * lam_re - nr * lam_im) / den
    bb_re = q_re[..., None] * b_re - q_im[..., None] * b_im
    bb_im = q_re[..., None] * b_im + q_im[..., None] * b_re
    n = jnp.arange(t + 1, dtype=F32)[:, None, None, None]
    pmag = jnp.exp(lam_re * step * n)
    pw_re = pmag * jnp.cos(lam_im * step * n)
    pw_im = pmag * jnp.sin(lam_im * step * n)
    cp_re = c_re[None, None] * pw_re[:, :, :, None, :] - c_im[None, None] * pw_im[:, :, :, None, :]
    cp_im = c_re[None, None] * pw_im[:, :, :, None, :] + c_im[None, None] * pw_re[:, :, :, None, :]
    hp = lax.Precision.HIGHEST
    kern = (jnp.einsum("ndgop,dgpi->ndgoi", cp_re[:t], bb_re, precision=hp)
            - jnp.einsum("ndgop,dgpi->ndgoi", cp_im[:t], bb_im, precision=hp))
    zpad = jnp.zeros((t - 1,) + kern.shape[2:], F32)
    lag = (jnp.concatenate([zpad, kern[:, 0]], axis=0)
           + jnp.concatenate([kern[::-1, 1], zpad], axis=0))
    toe = jnp.stack([lag[t - 1 - s:2 * t - 1 - s] for s in range(t)], axis=0)
    m_tot = jnp.transpose(toe, (2, 0, 4, 1, 3)).reshape(g, t * hg, t * hg)

    def inject(d, pr_, pi_):
        pr = pr_[..., None] * bb_re[d][None] - pi_[..., None] * bb_im[d][None]
        pi = pr_[..., None] * bb_im[d][None] + pi_[..., None] * bb_re[d][None]
        return (jnp.transpose(pr, (1, 0, 3, 2)).reshape(g, t * hg, p),
                jnp.transpose(pi, (1, 0, 3, 2)).reshape(g, t * hg, p))

    pf_re, pf_im = inject(0, pw_re[:t, 0][::-1], pw_im[:t, 0][::-1])
    pb_re, pb_im = inject(1, pw_re[:t, 1], pw_im[:t, 1])
    inj = jnp.concatenate([pf_re, pb_re, pf_im, pb_im], axis=-1)

    def readout(cr, ci):
        return (jnp.transpose(cr, (1, 3, 0, 2)).reshape(g, p, t * hg),
                jnp.transpose(-ci, (1, 3, 0, 2)).reshape(g, p, t * hg))

    qf_re, qf_im = readout(cp_re[1:, 0], cp_im[1:, 0])
    qb_re, qb_im = readout(cp_re[1:, 1][::-1], cp_im[1:, 1][::-1])
    out = jnp.concatenate([m_tot, qf_re, qb_re, qf_im, qb_im], axis=1)
    lb16 = jnp.stack([jnp.concatenate([pw_re[t, 0], pw_re[t, 1]], axis=-1),
                      jnp.concatenate([pw_im[t, 0], pw_im[t, 1]], axis=-1)], axis=0)
    return inj.astype(BF16), out.astype(BF16), lb16


def _block_transpose8(xs):
    lane_blk = lax.broadcasted_iota(jnp.int32, xs[0].shape, 1) >> 4
    xs = list(xs)
    for d in (4, 2, 1):
        hi = (lane_blk & d) != 0
        for i in range(8):
            if i & d:
                continue
            a, b = xs[i], xs[i + d]
            xs[i] = jnp.where(hi, pltpu.roll(b, 16 * d, 1), a)
            xs[i + d] = jnp.where(hi, b, pltpu.roll(a, 128 - 16 * d, 1))
    return xs


def _s5_kernel(u0_ref, u1_ref, inj_ref, out_ref, l_ref, y_ref, ug_ref, zre_ref, zim_ref, hre_ref,
               him_ref, *, nchunk, nctx):
    gh = ug_ref.shape[0]
    ps = S5_STATE
    t = S5_T
    nlat = nchunk - nctx
    for jt, u_ref in enumerate((u0_ref, u1_ref)):
        for q in range(t // 8):
            cols = [u_ref[0, pl.ds(8 * q + tt, nchunk, stride=t), :] for tt in range(8)]
            grp = _block_transpose8(cols)
            for gl in range(8):
                ug_ref[8 * jt + gl, :, 128 * q:128 * (q + 1)] = grp[gl].astype(BF16)
    for gi in range(gh):
        zg = _dot(ug_ref[gi], inj_ref[gi], NN)
        zre_ref[pl.ds(gi, nchunk, stride=gh), :] = zg[:, 0:2 * ps]
        zim_ref[pl.ds(gi, nchunk, stride=gh), :] = zg[:, 2 * ps:4 * ps]

    lane = lax.broadcasted_iota(jnp.int32, (gh, 2 * ps), 1)
    is_f = lane < ps
    lre = l_ref[0, 0]
    lim = l_ref[0, 1]

    def body(i, carry):
        hre, him = carry
        cb = jnp.where(i < nctx, nctx - 1 - i, nchunk + nctx - 1 - i)
        rf = pl.multiple_of(i * gh, gh)
        rb = pl.multiple_of(cb * gh, gh)
        hre_ref[pl.ds(rf, gh), 0:ps] = hre[:, 0:ps]
        hre_ref[pl.ds(rb, gh), ps:2 * ps] = hre[:, ps:2 * ps]
        him_ref[pl.ds(rf, gh), 0:ps] = him[:, 0:ps]
        him_ref[pl.ds(rb, gh), ps:2 * ps] = him[:, ps:2 * ps]
        zre = jnp.where(is_f, zre_ref[pl.ds(rf, gh), :], zre_ref[pl.ds(rb, gh), :])
        zim = jnp.where(is_f, zim_ref[pl.ds(rf, gh), :], zim_ref[pl.ds(rb, gh), :])
        return lre * hre - lim * him + zre, lre * him + lim * hre + zim

    zero = jnp.zeros((gh, 2 * ps), F32)
    lax.fori_loop(0, nchunk, body, (zero, zero))

    for jt in range(gh // 8):
        ys = []
        for gl in range(8):
            gi = 8 * jt + gl
            hre = hre_ref[pl.ds(gi, nchunk, stride=gh), :].astype(BF16)
            him = him_ref[pl.ds(gi, nchunk, stride=gh), :].astype(BF16)
            ys.append(_dot(jnp.concatenate([ug_ref[gi], hre, him], axis=1), out_ref[gi], NN))
        for q in range(t // 8):
            tok = _block_transpose8([y[:, 128 * q:128 * (q + 1)] for y in ys])
            for tt in range(8):
                y_ref[0, jt, pl.ds(8 * q + tt, nlat, stride=t), :] = tok[tt][nctx:]


def _s5(z, consts, ctx_len):
    inj, out, lb16 = consts
    bsz, ltot, _ = z.shape
    nchunk = ltot // S5_T
    nctx = ctx_len // S5_T
    gh = S5_GH
    halves = S5_GROUPS // gh
    w = S5_T * S5_GROUP
    tiles = gh * S5_GROUP // 128
    ps2 = 2 * S5_STATE
    lb = lb16.reshape(2, halves, gh, ps2).transpose(1, 0, 2, 3)
    kern = functools.partial(_s5_kernel, nchunk=nchunk, nctx=nctx)
    u_spec = lambda k: pl.BlockSpec((1, ltot, 128), lambda b, j: (b, 0, COL_U // 128 + tiles * j + k))
    return pl.pallas_call(
        kern,
        grid=(bsz, halves),
        in_specs=[u_spec(0), u_spec(1),
                  pl.BlockSpec((gh, w, 2 * ps2), lambda b, j: (j, 0, 0)),
                  pl.BlockSpec((gh, w + 2 * ps2, w), lambda b, j: (j, 0, 0)),
                  pl.BlockSpec((1, 2, gh, ps2), lambda b, j: (j, 0, 0, 0))],
        out_specs=pl.BlockSpec((1, tiles, ltot - ctx_len, 128), lambda b, j: (b, j, 0, 0)),
        out_shape=jax.ShapeDtypeStruct((bsz, D_S5 // 128, ltot - ctx_len, 128), F32),
        scratch_shapes=[pltpu.VMEM((gh, nchunk, w), BF16),
                        pltpu.VMEM((nchunk * gh, ps2), F32),
                        pltpu.VMEM((nchunk * gh, ps2), F32),
                        pltpu.VMEM((nchunk * gh, ps2), F32),
                        pltpu.VMEM((nchunk * gh, ps2), F32)],
        compiler_params=_cparams(("parallel", "parallel"), 48),
    )(z, z, inj, out, lb)


def _mix_kernel(x_ref, yf_ref, yb_ref, r_ref, k_ref, v_ref, lo_ref, u_ref, ys_ref, mod_ref,
                seg_ref, a0_ref, a2_ref, g2_ref, kaw_ref, rk_ref, lnw_ref, lnb_ref,
                dsk_ref, gluw_ref, glub_ref, wo_ref, o_ref):
    lo = lo_ref[0]
    seg = seg_ref[...]
    y = yf_ref[0] + yb_ref[0]
    mu = _mm(y, seg, "xa") * (1.0 / HEAD)
    yc = y - mu
    var = _mm(yc * yc, seg, "xa") * (1.0 / HEAD)
    yn = yc * lax.rsqrt(var + RWKV_LN_EPS) * lnw_ref[...] + lnb_ref[...]
    a_f = _sigmoid(a0_ref[0] + _mm(lo, a2_ref[0]))
    a_b = _sigmoid(a0_ref[1] + _mm(lo, a2_ref[1]))
    kaw = kaw_ref[...]
    kd_sum = k_ref[0] * ((1.0 + (a_f - 1.0) * kaw) + (1.0 + (a_b - 1.0) * kaw))
    bonus = _mm(r_ref[0] * kd_sum * rk_ref[...], seg, "xa")
    yn = yn + bonus * v_ref[0]
    gate = _mm(_sigmoid(lo), g2_ref[...])
    rwkv_out = yn * gate

    u = u_ref[0]
    ys = jnp.concatenate([ys_ref[0, i] for i in range(ys_ref.shape[1])], axis=1) + dsk_ref[...] * u
    zg = 0.5 * ys * (1.0 + jnp.tanh(math.sqrt(2.0 / math.pi) * (ys + 0.044715 * (ys * ys * ys))))
    s5_out = zg * _sigmoid(_mm(zg, gluw_ref[...]) + glub_ref[...])

    mix = _mm(rwkv_out, wo_ref[0:D_RWKV, :]) + _mm(s5_out, wo_ref[D_RWKV:, :])
    o_ref[0] = x_ref[0] + mod_ref[0, 2:3, :] * mix


def _mix(x, y_f, y_b, rkvc, z, ys5, mod8, seg, a0, a2p, g2p, kaw, rk, lnw, lnb, dsk, gluw, glub, wo,
         ctx_len, tm):
    bsz, seq, d = x.shape
    off = ctx_len // tm
    lora_blk = COL_LORA // LORA_W
    u_blk = COL_U // D_S5
    tok = lambda c: pl.BlockSpec((1, tm, D_RWKV), lambda b, j: (b, j + off, c))
    full = lambda a: pl.BlockSpec(a.shape, lambda b, j: (0,) * a.ndim)
    return pl.pallas_call(
        _mix_kernel,
        grid=(bsz, seq // tm),
        in_specs=[pl.BlockSpec((1, tm, d), lambda b, j: (b, j, 0)),
                  tok(0), tok(0), tok(0), tok(1), tok(2),
                  pl.BlockSpec((1, tm, LORA_W), lambda b, j: (b, j + off, lora_blk)),
                  pl.BlockSpec((1, tm, D_S5), lambda b, j: (b, j + off, u_blk)),
                  pl.BlockSpec((1, D_S5 // 128, tm, 128), lambda b, j: (b, 0, j, 0)),
                  pl.BlockSpec((1, 8, d), lambda b, j: (b, 0, 0)),
                  full(seg), full(a0), full(a2p), full(g2p), full(kaw), full(rk), full(lnw), full(lnb),
                  full(dsk), full(gluw), full(glub), full(wo)],
        out_specs=pl.BlockSpec((1, tm, d), lambda b, j: (b, j, 0)),
        out_shape=jax.ShapeDtypeStruct((bsz, seq, d), F32),
        compiler_params=_cparams(("parallel", "parallel"), 48),
    )(x, y_f, y_b, rkvc, rkvc, rkvc, z, z, ys5, mod8, seg, a0, a2p, g2p, kaw, rk, lnw, lnb,
      dsk, gluw, glub, wo)


def _ffn_kernel(x_ref, mod_ref, g2_ref, w1_ref, w3_ref, w2_ref, gf_ref, o_ref):
    x = x_ref[0]
    ms = jnp.mean(x * x, axis=-1, keepdims=True)
    h = x * lax.rsqrt(ms + NORM_EPS) * g2_ref[...]
    h = h * (1.0 + mod_ref[0, 4:5, :]) + mod_ref[0, 3:4, :]
    hb = h.astype(BF16)
    h1 = _dot(hb, w1_ref[...], NN)
    h3 = _dot(hb, w3_ref[...], NN)
    act = (h1 * _sigmoid(h1)) * h3
    f = _dot(act.astype(BF16), w2_ref[...], NN)
    out = x + mod_ref[0, 5:6, :] * f
    ms2 = jnp.mean(out * out, axis=-1, keepdims=True)
    o_ref[0] = out * lax.rsqrt(ms2 + NORM_EPS) * gf_ref[...]


def _ffn(x1, mod8, g2, w1, w3, w2, gf, tm):
    bsz, seq, d = x1.shape
    dff = w1.shape[1]
    const = lambda shape: pl.BlockSpec(shape, lambda b, j: (0, 0), pipeline_mode=pl.Buffered(1))
    return pl.pallas_call(
        _ffn_kernel,
        grid=(bsz, seq // tm),
        in_specs=[pl.BlockSpec((1, tm, d), lambda b, j: (b, j, 0)),
                  pl.BlockSpec((1, 8, d), lambda b, j: (b, 0, 0)),
                  const((1, d)), const((d, dff)), const((d, dff)), const((dff, d)), const((1, d))],
        out_specs=pl.BlockSpec((1, tm, d), lambda b, j: (b, j, 0)),
        out_shape=jax.ShapeDtypeStruct((bsz, seq, d), F32),
        compiler_params=_cparams(("parallel", "parallel"), 56),
    )(x1, mod8, g2.reshape(1, d), w1, w3, w2, gf.reshape(1, d))


def _pad_rows(w, row0, rows_total):
    pad = [(0, 0)] * (w.ndim - 2) + [(row0, rows_total - row0 - w.shape[-2]), (0, 0)]
    return jnp.pad(w, pad)


def kernel(x, c, ctx, c_ctx, mod_w, mod_b, norm1_g, norm2_g, w_in, w_out, rwkv_conv, rwkv_w0, rwkv_w2, rwkv_a0, rwkv_a2, rwkv_g2, rwkv_kk, rwkv_ka, rwkv_rk, rwkv_ln_w, rwkv_ln_b, s5_lam_re, s5_lam_im, s5_log_step, s5_b_re, s5_b_im, s5_c_re, s5_c_im, s5_d, s5_glu_w, s5_glu_b, ffn_w1, ffn_w3, ffn_w2, final_g):
    bsz, seq, d = x.shape
    ctx_len = ctx.shape[1]
    ltot = ctx_len + seq
    layer = 0

    cc = jnp.concatenate([c, c_ctx[None, :], jnp.zeros((8 - bsz - 1, d), F32)], axis=0)
    mod = _adaln(cc, mod_w[layer], mod_b[layer])
    mod8 = jnp.pad(mod[:bsz + 1].reshape(bsz + 1, N_MOD, d), ((0, 0), (0, 8 - N_MOD), (0, 0)))

    wi = w_in[layer]
    w_z = jnp.concatenate([wi[:, :3 * D_RWKV], wi[:, -D_S5:], wi[:, 3 * D_RWKV:-D_S5],
                           jnp.zeros((d, D_Z - wi.shape[1]), F32)], axis=1).astype(BF16)
    xcat = jnp.concatenate([ctx, x], axis=1)
    tm_in = 256
    z = _inproj(xcat, mod8, norm1_g[layer], w_z, ctx_len // tm_in, tm_in)

    rkvc = _conv(z, rwkv_conv[layer], ctx_len)

    w2p = _pad_rows(rwkv_w2[layer], 0, LORA_W).at[1].set(_pad_rows(rwkv_w2[layer, 1], DECAY_LORA, LORA_W))
    a2p = jnp.stack([_pad_rows(rwkv_a2[layer, 0], LORA_AD, LORA_W),
                     _pad_rows(rwkv_a2[layer, 1], LORA_AD + AAA_LORA, LORA_W)])
    w0 = rwkv_w0[layer].reshape(2, 1, D_RWKV)
    a0 = rwkv_a0[layer].reshape(2, 1, D_RWKV)
    kkw = rwkv_kk[layer].reshape(1, D_RWKV)
    kaw = rwkv_ka[layer].reshape(1, D_RWKV)
    y_f = _rwkv(rkvc, z, w0, w2p, a0, a2p, kkw, kaw, ctx_len, False)
    y_b = _rwkv(rkvc, z, w0, w2p, a0, a2p, kkw, kaw, ctx_len, True)

    consts = _s5_consts(s5_lam_re[layer], s5_lam_im[layer], s5_log_step[layer], s5_b_re[layer],
                        s5_b_im[layer], s5_c_re[layer], s5_c_im[layer])
    ys5 = _s5(z, consts, ctx_len)

    hh = jnp.arange(D_RWKV) // HEAD
    seg = (hh[:, None] == hh[None, :]).astype(BF16)
    g2p = _pad_rows(rwkv_g2[layer], LORA_GD, LORA_W)
    x1 = _mix(x, y_f, y_b, rkvc, z, ys5, mod8, seg, a0, a2p, g2p, kaw,
              rwkv_rk[layer].reshape(1, D_RWKV), rwkv_ln_w[layer].reshape(1, D_RWKV),
              rwkv_ln_b[layer].reshape(1, D_RWKV), s5_d[layer].reshape(1, D_S5),
              s5_glu_w[layer].astype(BF16), s5_glu_b[layer].reshape(1, D_S5),
              w_out[layer].astype(BF16), ctx_len, 256)

    return _ffn(x1, mod8, norm2_g[layer], ffn_w1[layer].astype(BF16), ffn_w3[layer].astype(BF16),
                ffn_w2[layer].astype(BF16), final_g, 256)
```

```python
import functools
import math

import jax
import jax.numpy as jnp
from jax import lax
from jax.experimental import pallas as pl
from jax.experimental.pallas import tpu as pltpu

F32 = jnp.float32
BF16 = jnp.bfloat16

D_MODEL = 1024
GRID_W = 64
D_RWKV = 512
HEAD = 64
HEADS = D_RWKV // HEAD
D_S5 = 512
S5_GROUP = 16
S5_GROUPS = D_S5 // S5_GROUP
S5_STATE = 64
DECAY_LORA = 32
AAA_LORA = 32
GATE_LORA = 96
N_MOD = 6
NORM_EPS = 1e-6
RWKV_LN_EPS = 64e-5

LORA_W = 256
COL_U = 3 * D_RWKV
COL_LORA = COL_U + D_S5
D_Z = COL_LORA + LORA_W
LORA_AD = 2 * DECAY_LORA
LORA_GD = LORA_AD + 2 * AAA_LORA

CHUNK = 64
RWKV_BLK = 256
S5_T = 16
S5_GH = 16
CONV_PAD = 72
CONV_TC = 256
CONV_PIECE = 128

NN = (((1,), (0,)), ((), ()))
NT = (((1,), (1,)), ((), ()))
TN = (((0,), (0,)), ((), ()))


def _dot(a, b, dims):
    return lax.dot_general(a, b, dims, preferred_element_type=F32)


def _split2(x):
    hi = x.astype(BF16)
    lo = (x - hi.astype(F32)).astype(BF16)
    return hi, lo


def _split3(x):
    hi = x.astype(BF16)
    r1 = x - hi.astype(F32)
    mid = r1.astype(BF16)
    lo = (r1 - mid.astype(F32)).astype(BF16)
    return hi, mid, lo


def _mm(a, b, mode="bf16", dims=NN):
    if mode == "bf16":
        return _dot(a.astype(BF16), b.astype(BF16), dims)
    if mode == "x3":
        ah, al = _split2(a)
        bh, bl = _split2(b)
        return _dot(ah, bh, dims) + (_dot(ah, bl, dims) + _dot(al, bh, dims))
    if mode == "xa":
        bb = b.astype(BF16)
        a1, a2, a3 = _split3(a)
        return _dot(a1, bb, dims) + (_dot(a2, bb, dims) + _dot(a3, bb, dims))
    if mode == "xb":
        ab = a.astype(BF16)
        b1, b2, b3 = _split3(b)
        return _dot(ab, b1, dims) + (_dot(ab, b2, dims) + _dot(ab, b3, dims))
    raise ValueError(mode)


def _sigmoid(x):
    return 1.0 / (1.0 + jnp.exp(-x))


def _softplus(x):
    return jnp.maximum(x, 0.0) + jnp.log(1.0 + jnp.exp(-jnp.abs(x)))


def _cparams(sem, vmem_mb):
    return pltpu.CompilerParams(dimension_semantics=sem, vmem_limit_bytes=vmem_mb * 1024 * 1024)


def _adaln_kernel(c_ref, w_ref, b_ref, o_ref):
    c = c_ref[...]
    s = c * _sigmoid(c)
    o_ref[...] = _mm(s, w_ref[...], "x3") + b_ref[...]


def _adaln(cc, mod_w, mod_b):
    rows, d = cc.shape
    n = mod_w.shape[1]
    tn = 1024
    return pl.pallas_call(
        _adaln_kernel,
        grid=(n // tn,),
        in_specs=[pl.BlockSpec((rows, d), lambda j: (0, 0)),
                  pl.BlockSpec((d, tn), lambda j: (0, j)),
                  pl.BlockSpec((1, tn), lambda j: (0, j))],
        out_specs=pl.BlockSpec((rows, tn), lambda j: (0, j)),
        out_shape=jax.ShapeDtypeStruct((rows, n), F32),
        compiler_params=_cparams(("parallel",), 40),
    )(cc, mod_w, mod_b.reshape(1, n))


def _inproj_kernel(x_ref, mod_ref, g_ref, w_ref, o_ref):
    x = x_ref[0]
    ms = jnp.mean(x * x, axis=-1, keepdims=True)
    xn = x * lax.rsqrt(ms + NORM_EPS) * g_ref[...]
    shift = mod_ref[0, 0:1, :]
    scale = mod_ref[0, 1:2, :]
    h = xn * (1.0 + scale) + shift
    o_ref[0] = _mm(h, w_ref[...], "bf16")


def _inproj(xcat, mod8, g1, w_z, ctx_blocks, tm):
    bsz, ltot, d = xcat.shape
    dz = w_z.shape[1]

    def mod_idx(b, j):
        return (jnp.where(j < ctx_blocks, bsz, b), 0, 0)

    return pl.pallas_call(
        _inproj_kernel,
        grid=(bsz, ltot // tm),
        in_specs=[pl.BlockSpec((1, tm, d), lambda b, j: (b, j, 0)),
                  pl.BlockSpec((1, 8, d), mod_idx),
                  pl.BlockSpec((1, d), lambda b, j: (0, 0)),
                  pl.BlockSpec((d, dz), lambda b, j: (0, 0))],
        out_specs=pl.BlockSpec((1, tm, dz), lambda b, j: (b, j, 0)),
        out_shape=jax.ShapeDtypeStruct((bsz, ltot, dz), F32),
        compiler_params=_cparams(("parallel", "parallel"), 48),
    )(xcat, mod8, g1.reshape(1, d), w_z)


def _conv_kernel(z_ref, w_ref, o_ref, xs_ref, *, ctx_len, lat_len):
    tc = z_ref.shape[2]
    pc = CONV_PIECE
    ctx_off = CONV_PAD
    lat_off = 2 * CONV_PAD + ctx_len
    zpad = jnp.zeros((CONV_PAD, tc), F32)
    for s in range(3):
        xs_ref[s, 0:CONV_PAD, :] = zpad
        xs_ref[s, ctx_off + ctx_len:lat_off, :] = zpad
        xs_ref[s, lat_off + lat_len:lat_off + lat_len + CONV_PAD, :] = zpad
    col = lax.broadcasted_iota(jnp.int32, (pc, tc), 0) % GRID_W
    keep_l = (col != GRID_W - 1).astype(F32)
    keep_r = (col != 0).astype(F32)
    for p in range(ctx_len // pc):
        xs_ref[0, ctx_off + p * pc:ctx_off + (p + 1) * pc, :] = z_ref[0, p * pc:(p + 1) * pc, :]
    for p in range(lat_len // pc):
        x = z_ref[0, ctx_len + p * pc:ctx_len + (p + 1) * pc, :]
        dst = slice(lat_off + p * pc, lat_off + (p + 1) * pc)
        xs_ref[0, dst, :] = x
        xs_ref[1, dst, :] = x * keep_l
        xs_ref[2, dst, :] = x * keep_r
    w = [w_ref[t:t + 1, :] for t in range(9)]
    for p in range(ctx_len // pc):
        base = ctx_off + p * pc
        acc = w[3] * xs_ref[0, base - 1:base - 1 + pc, :]
        acc = acc + w[4] * xs_ref[0, base:base + pc, :]
        acc = acc + w[5] * xs_ref[0, base + 1:base + 1 + pc, :]
        o_ref[0, p * pc:(p + 1) * pc, :] = acc
    for p in range(lat_len // pc):
        base = lat_off + p * pc
        acc = None
        for di in range(3):
            for dj in range(3):
                off = base + (di - 1) * GRID_W + (dj - 1)
                src = (1, 0, 2)[dj]
                term = w[3 * di + dj] * xs_ref[src, off:off + pc, :]
                acc = term if acc is None else acc + term
        o_ref[0, ctx_len + p * pc:ctx_len + (p + 1) * pc, :] = acc


def _conv(z, conv_w, ctx_len):
    bsz, ltot, _ = z.shape
    lat_len = ltot - ctx_len
    ch = 3 * D_RWKV
    rows = 3 * CONV_PAD + ltot
    kern = functools.partial(_conv_kernel, ctx_len=ctx_len, lat_len=lat_len)
    return pl.pallas_call(
        kern,
        grid=(bsz, ch // CONV_TC),
        in_specs=[pl.BlockSpec((1, ltot, CONV_TC), lambda b, j: (b, 0, j)),
                  pl.BlockSpec((9, CONV_TC), lambda b, j: (0, j))],
        out_specs=pl.BlockSpec((1, ltot, CONV_TC), lambda b, j: (b, 0, j)),
        out_shape=jax.ShapeDtypeStruct((bsz, ltot, ch), F32),
        scratch_shapes=[pltpu.VMEM((3, rows, CONV_TC), F32)],
        compiler_params=_cparams(("parallel", "parallel"), 48),
    )(z, conv_w.reshape(9, ch))


def _bd(x, lo):
    return jnp.concatenate([jnp.where(lo, x, 0.0), jnp.where(lo, 0.0, x)], axis=0)


def _tri_inv_pairs(n_mats, size, lo):
    ii = lax.broadcasted_iota(jnp.int32, (size, 2 * size), 0)
    jj = lax.broadcasted_iota(jnp.int32, (size, 2 * size), 1) & (size - 1)
    eye = (ii == jj).astype(F32)
    in2 = (ii >> 1) == (jj >> 1)
    ts = [eye + jnp.where(in2, n, 0.0) for n in n_mats]
    lg = 1
    while (1 << lg) < size:
        same = (ii >> (lg + 1)) == (jj >> (lg + 1))
        diff = (ii >> lg) != (jj >> lg)
        nls = [jnp.where(same, jnp.where(diff, n, 0.0), 0.0) for n in n_mats]
        tmp = [_mm(t, _bd(nl, lo)) for t, nl in zip(ts, nls)]
        ts = [t + _mm(x, _bd(t, lo)) for x, t in zip(tmp, ts)]
        lg += 1
    return ts


def _rwkv_kernel(r_ref, k_ref, v_ref, lo_ref, w0_ref, w2_ref, a0_ref, a2_ref, kkw_ref, kaw_ref,
                 y_ref, h_ref, *, reverse):
    step = pl.program_id(1)

    @pl.when(step == 0)
    def _():
        h_ref[...] = jnp.zeros_like(h_ref)

    r = r_ref[0]
    k = k_ref[0]
    v = v_ref[0]
    lora = lo_ref[0]
    n = r.shape[0]
    c = CHUNK
    nsub = n // c
    lgc = c.bit_length() - 1
    pw = 2 * HEAD
    npair = HEADS // 2

    logit_w = w0_ref[0] + _mm(jnp.tanh(lora), w2_ref[0])
    logw = -_softplus(-logit_w) - 0.5
    lw = -jnp.exp(logw)
    a = _sigmoid(a0_ref[0] + _mm(lora, a2_ref[0]))
    kd = k * (1.0 + (a - 1.0) * kaw_ref[...])
    kk = k * kkw_ref[...]

    ti = lax.broadcasted_iota(jnp.int32, (n, n), 0)
    tj = lax.broadcasted_iota(jnp.int32, (n, n), 1)
    causal = (tj >= ti) if reverse else (tj <= ti)
    tri = jnp.where((ti >> lgc) == (tj >> lgc), jnp.where(causal, 1.0, 0.0), 0.0)
    g = _mm(tri, lw, "xb")

    ci2 = lax.broadcasted_iota(jnp.int32, (c, pw), 0)
    lane2 = lax.broadcasted_iota(jnp.int32, (c, pw), 1)
    cj2 = lane2 & (c - 1)
    lo = lane2 < HEAD
    incl2 = (cj2 >= ci2) if reverse else (cj2 <= ci2)
    strict2 = (cj2 > ci2) if reverse else (cj2 < ci2)
    ri = lax.broadcasted_iota(jnp.int32, (pw, pw), 0)
    rj = lax.broadcasted_iota(jnp.int32, (pw, pw), 1)
    blk = (ri < HEAD) == (rj < HEAD)
    eye_p = (ri == rj).astype(F32)
    zeros_c = jnp.zeros((c, pw), F32)
    zeros_p = jnp.zeros((pw, pw), F32)

    units = [(j, p) for j in range(nsub) for p in range(npair)]
    a_t, r_t, b_t, k_t, b_e, k_e, vh, g_end = {}, {}, {}, {}, {}, {}, {}, {}
    for j in range(nsub):
        rows = slice(j * c, (j + 1) * c)
        lw_j = lw[rows]
        g_j = g[rows]
        tot = jnp.sum(lw_j, axis=0, keepdims=True)
        e_prev = jnp.exp(g_j - lw_j)
        e_neg = jnp.exp(-g_j)
        e_pos = jnp.exp(g_j)
        e_end = jnp.exp(tot - g_j)
        e_tot = jnp.exp(tot)
        for p in range(npair):
            sl = slice(p * pw, (p + 1) * pw)
            kkp = kk[rows, sl]
            sq = kkp * kkp
            n2 = jnp.where(lo, jnp.sum(jnp.where(lo, sq, 0.0), axis=-1, keepdims=True),
                           jnp.sum(jnp.where(lo, 0.0, sq), axis=-1, keepdims=True))
            kkp = kkp * lax.rsqrt(jnp.maximum(n2, 1e-12))
            ka = kkp * a[rows, sl]
            u = (j, p)
            a_t[u] = -kkp * e_prev[:, sl]
            b_t[u] = ka * e_neg[:, sl]
            k_t[u] = kd[rows, sl] * e_neg[:, sl]
            r_t[u] = r[rows, sl] * e_pos[:, sl]
            b_e[u] = ka * e_end[:, sl]
            k_e[u] = kd[rows, sl] * e_end[:, sl]
            vh[u] = v[rows, sl]
            g_end[u] = e_tot[:, sl]

    ar = [jnp.concatenate([a_t[u], r_t[u]], axis=0) for u in units]
    s_b = [_mm(x, _bd(b_t[u], lo), "bf16", NT) for x, u in zip(ar, units)]
    s_k = [_mm(x, _bd(k_t[u], lo), "bf16", NT) for x, u in zip(ar, units)]
    nab = [jnp.where(strict2, s[:c], 0.0) for s in s_b]
    aak = [jnp.where(strict2, s[:c], 0.0) for s in s_k]
    arbk = [jnp.concatenate([jnp.where(incl2, sb[c:], 0.0), jnp.where(incl2, sk[c:], 0.0)], axis=1)
            for sb, sk in zip(s_b, s_k)]
    t_inv = _tri_inv_pairs(nab, c, lo)
    x1 = [_mm(m, _bd(vh[u], lo)) for m, u in zip(aak, units)]
    uw = [_mm(t, jnp.concatenate([_bd(x, lo), _bd(a_t[u], lo)], axis=1))
          for t, x, u in zip(t_inv, x1, units)]
    yq = [_mm(m, jnp.concatenate(
        [jnp.concatenate([_bd(w_[:, :pw], lo), _bd(w_[:, pw:], lo)], axis=1),
         jnp.concatenate([_bd(vh[u], lo), zeros_p], axis=1)], axis=0))
        for m, w_, u in zip(arbk, uw, units)]
    gp = [_mm(jnp.concatenate([b_e[u], k_e[u]], axis=0),
              jnp.concatenate([w_, jnp.concatenate([vh[u], zeros_c], axis=1)], axis=0), "bf16", TN)
          for w_, u in zip(uw, units)]
    qp_l = {}
    for idx, u in enumerate(units):
        q = r_t[u] + yq[idx][:, pw:]
        p_bd = jnp.where(blk, gp[idx][:, pw:], 0.0) + eye_p * g_end[u]
        g_bd = jnp.where(blk, gp[idx][:, :pw], 0.0)
        qp_l[u] = (jnp.concatenate([q, p_bd], axis=0), yq[idx][:, :pw], g_bd)

    hs = [h_ref[p] for p in range(npair)]
    order = range(nsub - 1, -1, -1) if reverse else range(nsub)
    for j in order:
        res = [_mm(qp_l[(j, p)][0], hs[p], "x3") for p in range(npair)]
        for p in range(npair):
            y_ref[0, j * c:(j + 1) * c, p * pw:(p + 1) * pw] = res[p][:c] + qp_l[(j, p)][1]
            hs[p] = res[p][c:] + qp_l[(j, p)][2]
    for p in range(npair):
        h_ref[p] = hs[p]


def _rwkv(rkvc, z, w0, w2p, a0, a2p, kkw, kaw, ctx_len, reverse):
    bsz, ltot, _ = rkvc.shape
    nblk = ltot // RWKV_BLK
    nctx = ctx_len // RWKV_BLK
    lora_blk = COL_LORA // LORA_W
    d = 1 if reverse else 0

    def tok(i):
        return jnp.where(i < nctx, nctx - 1 - i, nblk + nctx - 1 - i) if reverse else i

    par3 = lambda b, i: (d, 0, 0)
    kern = functools.partial(_rwkv_kernel, reverse=reverse)
    return pl.pallas_call(
        kern,
        grid=(bsz, nblk),
        in_specs=[pl.BlockSpec((1, RWKV_BLK, D_RWKV), lambda b, i: (b, tok(i), 0)),
                  pl.BlockSpec((1, RWKV_BLK, D_RWKV), lambda b, i: (b, tok(i), 1)),
                  pl.BlockSpec((1, RWKV_BLK, D_RWKV), lambda b, i: (b, tok(i), 2)),
                  pl.BlockSpec((1, RWKV_BLK, LORA_W), lambda b, i: (b, tok(i), lora_blk)),
                  pl.BlockSpec((1, 1, D_RWKV), par3),
                  pl.BlockSpec((1, LORA_W, D_RWKV), par3),
                  pl.BlockSpec((1, 1, D_RWKV), par3),
                  pl.BlockSpec((1, LORA_W, D_RWKV), par3),
                  pl.BlockSpec((1, D_RWKV), lambda b, i: (0, 0)),
                  pl.BlockSpec((1, D_RWKV), lambda b, i: (0, 0))],
        out_specs=pl.BlockSpec((1, RWKV_BLK, D_RWKV), lambda b, i: (b, tok(i), 0)),
        out_shape=jax.ShapeDtypeStruct((bsz, ltot, D_RWKV), F32),
        scratch_shapes=[pltpu.VMEM((HEADS // 2, 2 * HEAD, 2 * HEAD), F32)],
        compiler_params=_cparams(("parallel", "arbitrary"), 48),
    )(rkvc, rkvc, rkvc, z, w0, w2p, a0, a2p, kkw, kaw)


def _s5_consts(lam_re, lam_im, log_step, b_re, b_im, c_re, c_im):
    t = S5_T
    g, p, hg = b_re.shape
    step = jnp.exp(log_step)[..., None]
    mag = jnp.exp(lam_re * step)
    lb_re = mag * jnp.cos(lam_im * step)
    lb_im = mag * jnp.sin(lam_im * step)
    den = lam_re * lam_re + lam_im * lam_im
    nr = lb_re - 1
    q_re = (nr * lam_re + lb_im * lam_im) / den
    q_im = (lb_im * lam_re - nr * lam_im) / den
    bb_re = q_re[..., None] * b_re - q_im[..., None] * b_im
    bb_im = q_re[..., None] * b_im + q_im[..., None] * b_re
    n = jnp.arange(t + 1, dtype=F32)[:, None, None, None]
    pmag = jnp.exp(lam_re * step * n)
    pw_re = pmag * jnp.cos(lam_im * step * n)
    pw_im = pmag * jnp.sin(lam_im * step * n)
    cp_re = c_re[None, None] * pw_re[:, :, :, None, :] - c_im[None, None] * pw_im[:, :, :, None, :]
    cp_im = c_re[None, None] * pw_im[:, :, :, None, :] + c_im[None, None] * pw_re[:, :, :, None, :]
    hp = lax.Precision.HIGHEST
    kern = (jnp.einsum("ndgop,dgpi->ndgoi", cp_re[:t], bb_re, precision=hp)
            - jnp.einsum("ndgop,dgpi->ndgoi", cp_im[:t], bb_im, precision=hp))
    zpad = jnp.zeros((t - 1,) + kern.shape[2:], F32)
    lag = (jnp.concatenate([zpad, kern[:, 0]], axis=0)
           + jnp.concatenate([kern[::-1, 1], zpad], axis=0))
    toe = jnp.stack([lag[t - 1 - s:2 * t - 1 - s] for s in range(t)], axis=0)
    m_tot = jnp.transpose(toe, (2, 0, 4, 1, 3)).reshape(g, t * hg, t * hg)

    def inject(d, pr_, pi_):
        pr = pr_[..., None] * bb_re[d][None] - pi_[..., None] * bb_im[d][None]
        pi = pr_[..., None] * bb_im[d][None] + pi_[..., None] * bb_re[d][None]
        return (jnp.transpose(pr, (1, 0, 3, 2)).reshape(g, t * hg, p),
                jnp.transpose(pi, (1, 0, 3, 2)).reshape(g, t * hg, p))

    pf_re, pf_im = inject(0, pw_re[:t, 0][::-1], pw_im[:t, 0][::-1])
    pb_re, pb_im = inject(1, pw_re[:t, 1], pw_im[:t, 1])
    inj = jnp.concatenate([pf_re, pb_re, pf_im, pb_im], axis=-1)

    def readout(cr, ci):
        return (jnp.transpose(cr, (1, 3, 0, 2)).reshape(g, p, t * hg),
                jnp.transpose(-ci, (1, 3, 0, 2)).reshape(g, p, t * hg))

    qf_re, qf_im = readout(cp_re[1:, 0], cp_im[1:, 0])
    qb_re, qb_im = readout(cp_re[1:, 1][::-1], cp_im[1:, 1][::-1])
    out = jnp.concatenate([m_tot, qf_re, qb_re, qf_im, qb_im], axis=1)
    lb16 = jnp.stack([jnp.concatenate([pw_re[t, 0], pw_re[t, 1]], axis=-1),
                      jnp.concatenate([pw_im[t, 0], pw_im[t, 1]], axis=-1)], axis=0)
    return inj.astype(BF16), out.astype(BF16), lb16


def _block_transpose8(xs):
    lane_blk = lax.broadcasted_iota(jnp.int32, xs[0].shape, 1) >> 4
    xs = list(xs)
    for d in (4, 2, 1):
        hi = (lane_blk & d) != 0
        for i in range(8):
            if i & d:
                continue
            a, b = xs[i], xs[i + d]
            xs[i] = jnp.where(hi, pltpu.roll(b, 16 * d, 1), a)
            xs[i + d] = jnp.where(hi, b, pltpu.roll(a, 128 - 16 * d, 1))
    return xs


def _s5_kernel(u0_ref, u1_ref, inj_ref, out_ref, l_ref, y_ref, ug_ref, zre_ref, zim_ref, hre_ref,
               him_ref, *, nchunk, nctx):
    gh = ug_ref.shape[0]
    ps = S5_STATE
    t = S5_T
    nlat = nchunk - nctx
    for jt, u_ref in enumerate((u0_ref, u1_ref)):
        for q in range(t // 8):
            cols = [u_ref[0, pl.ds(8 * q + tt, nchunk, stride=t), :] for tt in range(8)]
            grp = _block_transpose8(cols)
            for gl in range(8):
                ug_ref[8 * jt + gl, :, 128 * q:128 * (q + 1)] = grp[gl].astype(BF16)
    for gi in range(gh):
        zg = _dot(ug_ref[gi], inj_ref[gi], NN)
        zre_ref[pl.ds(gi, nchunk, stride=gh), :] = zg[:, 0:2 * ps]
        zim_ref[pl.ds(gi, nchunk, stride=gh), :] = zg[:, 2 * ps:4 * ps]

    lane = lax.broadcasted_iota(jnp.int32, (gh, 2 * ps), 1)
    is_f = lane < ps
    lre = l_ref[0, 0]
    lim = l_ref[0, 1]

    def body(i, carry):
        hre, him = carry
        cb = jnp.where(i < nctx, nctx - 1 - i, nchunk + nctx - 1 - i)
        rf = pl.multiple_of(i * gh, gh)
        rb = pl.multiple_of(cb * gh, gh)
        hre_ref[pl.ds(rf, gh), 0:ps] = hre[:, 0:ps]
        hre_ref[pl.ds(rb, gh), ps:2 * ps] = hre[:, ps:2 * ps]
        him_ref[pl.ds(rf, gh), 0:ps] = him[:, 0:ps]
        him_ref[pl.ds(rb, gh), ps:2 * ps] = him[:, ps:2 * ps]
        zre = jnp.where(is_f, zre_ref[pl.ds(rf, gh), :], zre_ref[pl.ds(rb, gh), :])
        zim = jnp.where(is_f, zim_ref[pl.ds(rf, gh), :], zim_ref[pl.ds(rb, gh), :])
        return lre * hre - lim * him + zre, lre * him + lim * hre + zim

    zero = jnp.zeros((gh, 2 * ps), F32)
    lax.fori_loop(0, nchunk, body, (zero, zero))

    for jt in range(gh // 8):
        ys = []
        for gl in range(8):
            gi = 8 * jt + gl
            hre = hre_ref[pl.ds(gi, nchunk, stride=gh), :].astype(BF16)
            him = him_ref[pl.ds(gi, nchunk, stride=gh), :].astype(BF16)
            ys.append(_dot(jnp.concatenate([ug_ref[gi], hre, him], axis=1), out_ref[gi], NN))
        for q in range(t // 8):
            tok = _block_transpose8([y[:, 128 * q:128 * (q + 1)] for y in ys])
            for tt in range(8):
                y_ref[0, jt, pl.ds(8 * q + tt, nlat, stride=t), :] = tok[tt][nctx:]


def _s5(z, consts, ctx_len):
    inj, out, lb16 = consts
    bsz, ltot, _ = z.shape
    nchunk = ltot // S5_T
    nctx = ctx_len // S5_T
    gh = S5_GH
    halves = S5_GROUPS // gh
    w = S5_T * S5_GROUP
    tiles = gh * S5_GROUP // 128
    ps2 = 2 * S5_STATE
    lb = lb16.reshape(2, halves, gh, ps2).transpose(1, 0, 2, 3)
    kern = functools.partial(_s5_kernel, nchunk=nchunk, nctx=nctx)
    u_spec = lambda k: pl.BlockSpec((1, ltot, 128), lambda b, j: (b, 0, COL_U // 128 + tiles * j + k))
    return pl.pallas_call(
        kern,
        grid=(bsz, halves),
        in_specs=[u_spec(0), u_spec(1),
                  pl.BlockSpec((gh, w, 2 * ps2), lambda b, j: (j, 0, 0)),
                  pl.BlockSpec((gh, w + 2 * ps2, w), lambda b, j: (j, 0, 0)),
                  pl.BlockSpec((1, 2, gh, ps2), lambda b, j: (j, 0, 0, 0))],
        out_specs=pl.BlockSpec((1, tiles, ltot - ctx_len, 128), lambda b, j: (b, j, 0, 0)),
        out_shape=jax.ShapeDtypeStruct((bsz, D_S5 // 128, ltot - ctx_len, 128), F32),
        scratch_shapes=[pltpu.VMEM((gh, nchunk, w), BF16),
                        pltpu.VMEM((nchunk * gh, ps2), F32),
                        pltpu.VMEM((nchunk * gh, ps2), F32),
                        pltpu.VMEM((nchunk * gh, ps2), F32),
                        pltpu.VMEM((nchunk * gh, ps2), F32)],
        compiler_params=_cparams(("parallel", "parallel"), 48),
    )(z, z, inj, out, lb)


def _mix_kernel(x_ref, yf_ref, yb_ref, r_ref, k_ref, v_ref, lo_ref, u_ref, ys_ref, mod_ref,
                seg_ref, a0_ref, a2_ref, g2_ref, kaw_ref, rk_ref, lnw_ref, lnb_ref,
                dsk_ref, gluw_ref, glub_ref, wo_ref, o_ref):
    lo = lo_ref[0]
    seg = seg_ref[...]
    y = yf_ref[0] + yb_ref[0]
    mu = _mm(y, seg, "xa") * (1.0 / HEAD)
    yc = y - mu
    var = _mm(yc * yc, seg, "xa") * (1.0 / HEAD)
    yn = yc * lax.rsqrt(var + RWKV_LN_EPS) * lnw_ref[...] + lnb_ref[...]
    a_f = _sigmoid(a0_ref[0] + _mm(lo, a2_ref[0]))
    a_b = _sigmoid(a0_ref[1] + _mm(lo, a2_ref[1]))
    kaw = kaw_ref[...]
    kd_sum = k_ref[0] * ((1.0 + (a_f - 1.0) * kaw) + (1.0 + (a_b - 1.0) * kaw))
    bonus = _mm(r_ref[0] * kd_sum * rk_ref[...], seg, "xa")
    yn = yn + bonus * v_ref[0]
    gate = _mm(_sigmoid(lo), g2_ref[...])
    rwkv_out = yn * gate

    u = u_ref[0]
    ys = jnp.concatenate([ys_ref[0, i] for i in range(ys_ref.shape[1])], axis=1) + dsk_ref[...] * u
    zg = 0.5 * ys * (1.0 + jnp.tanh(math.sqrt(2.0 / math.pi) * (ys + 0.044715 * (ys * ys * ys))))
    s5_out = zg * _sigmoid(_mm(zg, gluw_ref[...]) + glub_ref[...])

    mix = _mm(rwkv_out, wo_ref[0:D_RWKV, :]) + _mm(s5_out, wo_ref[D_RWKV:, :])
    o_ref[0] = x_ref[0] + mod_ref[0, 2:3, :] * mix


def _mix(x, y_f, y_b, rkvc, z, ys5, mod8, seg, a0, a2p, g2p, kaw, rk, lnw, lnb, dsk, gluw, glub, wo,
         ctx_len, tm):
    bsz, seq, d = x.shape
    off = ctx_len // tm
    lora_blk = COL_LORA // LORA_W
    u_blk = COL_U // D_S5
    tok = lambda c: pl.BlockSpec((1, tm, D_RWKV), lambda b, j: (b, j + off, c))
    full = lambda a: pl.BlockSpec(a.shape, lambda b, j: (0,) * a.ndim)
    return pl.pallas_call(
        _mix_kernel,
        grid=(bsz, seq // tm),
        in_specs=[pl.BlockSpec((1, tm, d), lambda b, j: (b, j, 0)),
                  tok(0), tok(0), tok(0), tok(1), tok(2),
                  pl.BlockSpec((1, tm, LORA_W), lambda b, j: (b, j + off, lora_blk)),
                  pl.BlockSpec((1, tm, D_S5), lambda b, j: (b, j + off, u_blk)),
                  pl.BlockSpec((1, D_S5 // 128, tm, 128), lambda b, j: (b, 0, j, 0)),
                  pl.BlockSpec((1, 8, d), lambda b, j: (b, 0, 0)),
                  full(seg), full(a0), full(a2p), full(g2p), full(kaw), full(rk), full(lnw), full(lnb),
                  full(dsk), full(gluw), full(glub), full(wo)],
        out_specs=pl.BlockSpec((1, tm, d), lambda b, j: (b, j, 0)),
        out_shape=jax.ShapeDtypeStruct((bsz, seq, d), F32),
        compiler_params=_cparams(("parallel", "parallel"), 48),
    )(x, y_f, y_b, rkvc, rkvc, rkvc, z, z, ys5, mod8, seg, a0, a2p, g2p, kaw, rk, lnw, lnb,
      dsk, gluw, glub, wo)


def _ffn_kernel(x_ref, mod_ref, g2_ref, w1_ref, w3_ref, w2_ref, gf_ref, o_ref):
    x = x_ref[0]
    ms = jnp.mean(x * x, axis=-1, keepdims=True)
    h = x * lax.rsqrt(ms + NORM_EPS) * g2_ref[...]
    h = h * (1.0 + mod_ref[0, 4:5, :]) + mod_ref[0, 3:4, :]
    hb = h.astype(BF16)
    h1 = _dot(hb, w1_ref[...], NN)
    h3 = _dot(hb, w3_ref[...], NN)
    act = (h1 * _sigmoid(h1)) * h3
    f = _dot(act.astype(BF16), w2_ref[...], NN)
    out = x + mod_ref[0, 5:6, :] * f
    ms2 = jnp.mean(out * out, axis=-1, keepdims=True)
    o_ref[0] = out * lax.rsqrt(ms2 + NORM_EPS) * gf_ref[...]


def _ffn(x1, mod8, g2, w1, w3, w2, gf, tm):
    bsz, seq, d = x1.shape
    dff = w1.shape[1]
    const = lambda shape: pl.BlockSpec(shape, lambda b, j: (0, 0), pipeline_mode=pl.Buffered(1))
    return pl.pallas_call(
        _ffn_kernel,
        grid=(bsz, seq // tm),
        in_specs=[pl.BlockSpec((1, tm, d), lambda b, j: (b, j, 0)),
                  pl.BlockSpec((1, 8, d), lambda b, j: (b, 0, 0)),
                  const((1, d)), const((d, dff)), const((d, dff)), const((dff, d)), const((1, d))],
        out_specs=pl.BlockSpec((1, tm, d), lambda b, j: (b, j, 0)),
        out_shape=jax.ShapeDtypeStruct((bsz, seq, d), F32),
        compiler_params=_cparams(("parallel", "parallel"), 56),
    )(x1, mod8, g2.reshape(1, d), w1, w3, w2, gf.reshape(1, d))


def _pad_rows(w, row0, rows_total):
    pad = [(0, 0)] * (w.ndim - 2) + [(row0, rows_total - row0 - w.shape[-2]), (0, 0)]
    return jnp.pad(w, pad)


def kernel(x, c, ctx, c_ctx, mod_w, mod_b, norm1_g, norm2_g, w_in, w_out, rwkv_conv, rwkv_w0, rwkv_w2, rwkv_a0, rwkv_a2, rwkv_g2, rwkv_kk, rwkv_ka, rwkv_rk, rwkv_ln_w, rwkv_ln_b, s5_lam_re, s5_lam_im, s5_log_step, s5_b_re, s5_b_im, s5_c_re, s5_c_im, s5_d, s5_glu_w, s5_glu_b, ffn_w1, ffn_w3, ffn_w2, final_g):
    bsz, seq, d = x.shape
    ctx_len = ctx.shape[1]
    ltot = ctx_len + seq
    layer = 0

    cc = jnp.concatenate([c, c_ctx[None, :], jnp.zeros((8 - bsz - 1, d), F32)], axis=0)
    mod = _adaln(cc, mod_w[layer], mod_b[layer])
    mod8 = jnp.pad(mod[:bsz + 1].reshape(bsz + 1, N_MOD, d), ((0, 0), (0, 8 - N_MOD), (0, 0)))

    wi = w_in[layer]
    w_z = jnp.concatenate([wi[:, :3 * D_RWKV], wi[:, -D_S5:], wi[:, 3 * D_RWKV:-D_S5],
                           jnp.zeros((d, D_Z - wi.shape[1]), F32)], axis=1).astype(BF16)
    xcat = jnp.concatenate([ctx, x], axis=1)
    tm_in = 256
    z = _inproj(xcat, mod8, norm1_g[layer], w_z, ctx_len // tm_in, tm_in)

    rkvc = _conv(z, rwkv_conv[layer], ctx_len)

    w2p = _pad_rows(rwkv_w2[layer], 0, LORA_W).at[1].set(_pad_rows(rwkv_w2[layer, 1], DECAY_LORA, LORA_W))
    a2p = jnp.stack([_pad_rows(rwkv_a2[layer, 0], LORA_AD, LORA_W),
                     _pad_rows(rwkv_a2[layer, 1], LORA_AD + AAA_LORA, LORA_W)])
    w0 = rwkv_w0[layer].reshape(2, 1, D_RWKV)
    a0 = rwkv_a0[layer].reshape(2, 1, D_RWKV)
    kkw = rwkv_kk[layer].reshape(1, D_RWKV)
    kaw = rwkv_ka[layer].reshape(1, D_RWKV)
    y_f = _rwkv(rkvc, z, w0, w2p, a0, a2p, kkw, kaw, ctx_len, False)
    y_b = _rwkv(rkvc, z, w0, w2p, a0, a2p, kkw, kaw, ctx_len, True)

    consts = _s5_consts(s5_lam_re[layer], s5_lam_im[layer], s5_log_step[layer], s5_b_re[layer],
                        s5_b_im[layer], s5_c_re[layer], s5_c_im[layer])
    ys5 = _s5(z, consts, ctx_len)

    hh = jnp.arange(D_RWKV) // HEAD
    seg = (hh[:, None] == hh[None, :]).astype(BF16)
    g2p = _pad_rows(rwkv_g2[layer], LORA_GD, LORA_W)
    x1 = _mix(x, y_f, y_b, rkvc, z, ys5, mod8, seg, a0, a2p, g2p, kaw,
              rwkv_rk[layer].reshape(1, D_RWKV), rwkv_ln_w[layer].reshape(1, D_RWKV),
              rwkv_ln_b[layer].reshape(1, D_RWKV), s5_d[layer].reshape(1, D_S5),
              s5_glu_w[layer].astype(BF16), s5_glu_b[layer].reshape(1, D_S5),
              w_out[layer].astype(BF16), ctx_len, 256)

    return _ffn(x1, mod8, norm2_g[layer], ffn_w1[layer].astype(BF16), ffn_w3[layer].astype(BF16),
                ffn_w2[layer].astype(BF16), final_g, 256)
```

```python
import functools
import math

import jax
import jax.numpy as jnp
from jax import lax
from jax.experimental import pallas as pl
from jax.experimental.pallas import tpu as pltpu

F32 = jnp.float32
BF16 = jnp.bfloat16

D_MODEL = 1024
GRID_W = 64
D_RWKV = 512
HEAD = 64
HEADS = D_RWKV // HEAD
D_S5 = 512
S5_GROUP = 16
S5_GROUPS = D_S5 // S5_GROUP
S5_STATE = 64
DECAY_LORA = 32
AAA_LORA = 32
GATE_LORA = 96
N_MOD = 6
NORM_EPS = 1e-6
RWKV_LN_EPS = 64e-5

LORA_W = 256
COL_U = 3 * D_RWKV
COL_LORA = COL_U + D_S5
D_Z = COL_LORA + LORA_W
LORA_AD = 2 * DECAY_LORA
LORA_GD = LORA_AD + 2 * AAA_LORA

CHUNK = 64
RWKV_BLK = 256
S5_T = 16
S5_GH = 16
CONV_PAD = 72
CONV_TC = 256
CONV_PIECE = 128

NN = (((1,), (0,)), ((), ()))
NT = (((1,), (1,)), ((), ()))
TN = (((0,), (0,)), ((), ()))


def _dot(a, b, dims):
    return lax.dot_general(a, b, dims, preferred_element_type=F32)


def _split2(x):
    hi = x.astype(BF16)
    lo = (x - hi.astype(F32)).astype(BF16)
    return hi, lo


def _split3(x):
    hi = x.astype(BF16)
    r1 = x - hi.astype(F32)
    mid = r1.astype(BF16)
    lo = (r1 - mid.astype(F32)).astype(BF16)
    return hi, mid, lo


def _mm(a, b, mode="bf16", dims=NN):
    if mode == "bf16":
        return _dot(a.astype(BF16), b.astype(BF16), dims)
    if mode == "x3":
        ah, al = _split2(a)
        bh, bl = _split2(b)
        return _dot(ah, bh, dims) + (_dot(ah, bl, dims) + _dot(al, bh, dims))
    if mode == "xa":
        bb = b.astype(BF16)
        a1, a2, a3 = _split3(a)
        return _dot(a1, bb, dims) + (_dot(a2, bb, dims) + _dot(a3, bb, dims))
    if mode == "xb":
        ab = a.astype(BF16)
        b1, b2, b3 = _split3(b)
        return _dot(ab, b1, dims) + (_dot(ab, b2, dims) + _dot(ab, b3, dims))
    raise ValueError(mode)


def _sigmoid(x):
    return 1.0 / (1.0 + jnp.exp(-x))


def _softplus(x):
    return jnp.maximum(x, 0.0) + jnp.log(1.0 + jnp.exp(-jnp.abs(x)))


def _cparams(sem, vmem_mb):
    return pltpu.CompilerParams(dimension_semantics=sem, vmem_limit_bytes=vmem_mb * 1024 * 1024)


def _adaln_kernel(c_ref, w_ref, b_ref, o_ref):
    c = c_ref[...]
    s = c * _sigmoid(c)
    o_ref[...] = _mm(s, w_ref[...], "x3") + b_ref[...]


def _adaln(cc, mod_w, mod_b):
    rows, d = cc.shape
    n = mod_w.shape[1]
    tn = 1024
    return pl.pallas_call(
        _adaln_kernel,
        grid=(n // tn,),
        in_specs=[pl.BlockSpec((rows, d), lambda j: (0, 0)),
                  pl.BlockSpec((d, tn), lambda j: (0, j)),
                  pl.BlockSpec((1, tn), lambda j: (0, j))],
        out_specs=pl.BlockSpec((rows, tn), lambda j: (0, j)),
        out_shape=jax.ShapeDtypeStruct((rows, n), F32),
        compiler_params=_cparams(("parallel",), 40),
    )(cc, mod_w, mod_b.reshape(1, n))


def _inproj_kernel(c_ref, x_ref, mod_ref, g_ref, w_ref, o_ref, *, ctx_blocks):
    x = jnp.where(pl.program_id(1) < ctx_blocks, c_ref[0], x_ref[0])
    ms = jnp.mean(x * x, axis=-1, keepdims=True)
    xn = x * lax.rsqrt(ms + NORM_EPS) * g_ref[...]
    shift = mod_ref[0, 0:1, :]
    scale = mod_ref[0, 1:2, :]
    h = xn * (1.0 + scale) + shift
    o_ref[0] = _mm(h, w_ref[...], "bf16")


def _inproj(ctx, x, mod8, g1, w_z, tm):
    bsz, seq, d = x.shape
    ctx_blocks = ctx.shape[1] // tm
    dz = w_z.shape[1]
    kern = functools.partial(_inproj_kernel, ctx_blocks=ctx_blocks)
    return pl.pallas_call(
        kern,
        grid=(bsz, ctx_blocks + seq // tm),
        in_specs=[pl.BlockSpec((1, tm, d), lambda b, j: (b, jnp.minimum(j, ctx_blocks - 1), 0)),
                  pl.BlockSpec((1, tm, d), lambda b, j: (b, jnp.maximum(j - ctx_blocks, 0), 0)),
                  pl.BlockSpec((1, 8, d), lambda b, j: (jnp.where(j < ctx_blocks, bsz, b), 0, 0)),
                  pl.BlockSpec((1, d), lambda b, j: (0, 0)),
                  pl.BlockSpec((d, dz), lambda b, j: (0, 0))],
        out_specs=pl.BlockSpec((1, tm, dz), lambda b, j: (b, j, 0)),
        out_shape=jax.ShapeDtypeStruct((bsz, ctx.shape[1] + seq, dz), F32),
        compiler_params=_cparams(("parallel", "parallel"), 48),
    )(ctx, x, mod8, g1.reshape(1, d), w_z)


def _conv_kernel(z_ref, w_ref, o_ref, xs_ref, *, ctx_len, lat_len):
    tc = z_ref.shape[2]
    pc = CONV_PIECE
    ctx_off = CONV_PAD
    lat_off = 2 * CONV_PAD + ctx_len
    zpad = jnp.zeros((CONV_PAD, tc), F32)
    for s in range(3):
        xs_ref[s, 0:CONV_PAD, :] = zpad
        xs_ref[s, ctx_off + ctx_len:lat_off, :] = zpad
        xs_ref[s, lat_off + lat_len:lat_off + lat_len + CONV_PAD, :] = zpad
    col = lax.broadcasted_iota(jnp.int32, (pc, tc), 0) % GRID_W
    keep_l = (col != GRID_W - 1).astype(F32)
    keep_r = (col != 0).astype(F32)
    for p in range(ctx_len // pc):
        xs_ref[0, ctx_off + p * pc:ctx_off + (p + 1) * pc, :] = z_ref[0, p * pc:(p + 1) * pc, :]
    for p in range(lat_len // pc):
        x = z_ref[0, ctx_len + p * pc:ctx_len + (p + 1) * pc, :]
        dst = slice(lat_off + p * pc, lat_off + (p + 1) * pc)
        xs_ref[0, dst, :] = x
        xs_ref[1, dst, :] = x * keep_l
        xs_ref[2, dst, :] = x * keep_r
    w = [w_ref[t:t + 1, :] for t in range(9)]
    for p in range(ctx_len // pc):
        base = ctx_off + p * pc
        acc = w[3] * xs_ref[0, base - 1:base - 1 + pc, :]
        acc = acc + w[4] * xs_ref[0, base:base + pc, :]
        acc = acc + w[5] * xs_ref[0, base + 1:base + 1 + pc, :]
        o_ref[0, p * pc:(p + 1) * pc, :] = acc
    for p in range(lat_len // pc):
        base = lat_off + p * pc
        acc = None
        for di in range(3):
            for dj in range(3):
                off = base + (di - 1) * GRID_W + (dj - 1)
                src = (1, 0, 2)[dj]
                term = w[3 * di + dj] * xs_ref[src, off:off + pc, :]
                acc = term if acc is None else acc + term
        o_ref[0, ctx_len + p * pc:ctx_len + (p + 1) * pc, :] = acc


def _conv(z, conv_w, ctx_len):
    bsz, ltot, _ = z.shape
    lat_len = ltot - ctx_len
    ch = 3 * D_RWKV
    rows = 3 * CONV_PAD + ltot
    kern = functools.partial(_conv_kernel, ctx_len=ctx_len, lat_len=lat_len)
    return pl.pallas_call(
        kern,
        grid=(bsz, ch // CONV_TC),
        in_specs=[pl.BlockSpec((1, ltot, CONV_TC), lambda b, j: (b, 0, j)),
                  pl.BlockSpec((9, CONV_TC), lambda b, j: (0, j))],
        out_specs=pl.BlockSpec((1, ltot, CONV_TC), lambda b, j: (b, 0, j)),
        out_shape=jax.ShapeDtypeStruct((bsz, ltot, ch), F32),
        scratch_shapes=[pltpu.VMEM((3, rows, CONV_TC), F32)],
        compiler_params=_cparams(("parallel", "parallel"), 48),
    )(z, conv_w.reshape(9, ch))


def _bd(x, lo):
    return jnp.concatenate([jnp.where(lo, x, 0.0), jnp.where(lo, 0.0, x)], axis=0)


def _tri_inv_pairs(n_mats, size, lo):
    ii = lax.broadcasted_iota(jnp.int32, (size, 2 * size), 0)
    jj = lax.broadcasted_iota(jnp.int32, (size, 2 * size), 1) & (size - 1)
    eye = (ii == jj).astype(F32)
    in2 = (ii >> 1) == (jj >> 1)
    ts = [eye + jnp.where(in2, n, 0.0) for n in n_mats]
    lg = 1
    while (1 << lg) < size:
        same = (ii >> (lg + 1)) == (jj >> (lg + 1))
        diff = (ii >> lg) != (jj >> lg)
        nls = [jnp.where(same, jnp.where(diff, n, 0.0), 0.0) for n in n_mats]
        tmp = [_mm(t, _bd(nl, lo)) for t, nl in zip(ts, nls)]
        ts = [t + _mm(x, _bd(t, lo)) for x, t in zip(tmp, ts)]
        lg += 1
    return ts


def _rwkv_kernel(r_ref, k_ref, v_ref, lo_ref, w0_ref, w2_ref, a0_ref, a2_ref, kkw_ref, kaw_ref,
                 y_ref, h_ref, *, reverse):
    step = pl.program_id(1)

    @pl.when(step == 0)
    def _():
        h_ref[...] = jnp.zeros_like(h_ref)

    r = r_ref[0]
    k = k_ref[0]
    v = v_ref[0]
    lora = lo_ref[0]
    n = r.shape[0]
    c = CHUNK
    nsub = n // c
    lgc = c.bit_length() - 1
    pw = 2 * HEAD
    npair = HEADS // 2

    logit_w = w0_ref[0] + _mm(jnp.tanh(lora), w2_ref[0])
    logw = -_softplus(-logit_w) - 0.5
    lw = -jnp.exp(logw)
    a = _sigmoid(a0_ref[0] + _mm(lora, a2_ref[0]))
    kd = k * (1.0 + (a - 1.0) * kaw_ref[...])
    kk = k * kkw_ref[...]

    ti = lax.broadcasted_iota(jnp.int32, (n, n), 0)
    tj = lax.broadcasted_iota(jnp.int32, (n, n), 1)
    causal = (tj >= ti) if reverse else (tj <= ti)
    tri = jnp.where((ti >> lgc) == (tj >> lgc), jnp.where(causal, 1.0, 0.0), 0.0)
    g = _mm(tri, lw, "xb")

    ci2 = lax.broadcasted_iota(jnp.int32, (c, pw), 0)
    lane2 = lax.broadcasted_iota(jnp.int32, (c, pw), 1)
    cj2 = lane2 & (c - 1)
    lo = lane2 < HEAD
    incl2 = (cj2 >= ci2) if reverse else (cj2 <= ci2)
    strict2 = (cj2 > ci2) if reverse else (cj2 < ci2)
    ri = lax.broadcasted_iota(jnp.int32, (pw, pw), 0)
    rj = lax.broadcasted_iota(jnp.int32, (pw, pw), 1)
    blk = (ri < HEAD) == (rj < HEAD)
    eye_p = (ri == rj).astype(F32)
    zeros_c = jnp.zeros((c, pw), F32)
    zeros_p = jnp.zeros((pw, pw), F32)

    units = [(j, p) for j in range(nsub) for p in range(npair)]
    a_t, r_t, b_t, k_t, b_e, k_e, vh, g_end = {}, {}, {}, {}, {}, {}, {}, {}
    for j in range(nsub):
        rows = slice(j * c, (j + 1) * c)
        lw_j = lw[rows]
        g_j = g[rows]
        tot = jnp.sum(lw_j, axis=0, keepdims=True)
        e_prev = jnp.exp(g_j - lw_j)
        e_neg = jnp.exp(-g_j)
        e_pos = jnp.exp(g_j)
        e_end = jnp.exp(tot - g_j)
        e_tot = jnp.exp(tot)
        for p in range(npair):
            sl = slice(p * pw, (p + 1) * pw)
            kkp = kk[rows, sl]
            sq = kkp * kkp
            n2 = jnp.where(lo, jnp.sum(jnp.where(lo, sq, 0.0), axis=-1, keepdims=True),
                           jnp.sum(jnp.where(lo, 0.0, sq), axis=-1, keepdims=True))
            kkp = kkp * lax.rsqrt(jnp.maximum(n2, 1e-12))
            ka = kkp * a[rows, sl]
            u = (j, p)
            a_t[u] = -kkp * e_prev[:, sl]
            b_t[u] = ka * e_neg[:, sl]
            k_t[u] = kd[rows, sl] * e_neg[:, sl]
            r_t[u] = r[rows, sl] * e_pos[:, sl]
            b_e[u] = ka * e_end[:, sl]
            k_e[u] = kd[rows, sl] * e_end[:, sl]
            vh[u] = v[rows, sl]
            g_end[u] = e_tot[:, sl]

    ar = [jnp.concatenate([a_t[u], r_t[u]], axis=0) for u in units]
    s_b = [_mm(x, _bd(b_t[u], lo), "bf16", NT) for x, u in zip(ar, units)]
    s_k = [_mm(x, _bd(k_t[u], lo), "bf16", NT) for x, u in zip(ar, units)]
    nab = [jnp.where(strict2, s[:c], 0.0) for s in s_b]
    aak = [jnp.where(strict2, s[:c], 0.0) for s in s_k]
    arbk = [jnp.concatenate([jnp.where(incl2, sb[c:], 0.0), jnp.where(incl2, sk[c:], 0.0)], axis=1)
            for sb, sk in zip(s_b, s_k)]
    t_inv = _tri_inv_pairs(nab, c, lo)
    x1 = [_mm(m, _bd(vh[u], lo)) for m, u in zip(aak, units)]
    uw = [_mm(t, jnp.concatenate([_bd(x, lo), _bd(a_t[u], lo)], axis=1))
          for t, x, u in zip(t_inv, x1, units)]
    yq = [_mm(m, jnp.concatenate(
        [jnp.concatenate([_bd(w_[:, :pw], lo), _bd(w_[:, pw:], lo)], axis=1),
         jnp.concatenate([_bd(vh[u], lo), zeros_p], axis=1)], axis=0))
        for m, w_, u in zip(arbk, uw, units)]
    gp = [_mm(jnp.concatenate([b_e[u], k_e[u]], axis=0),
              jnp.concatenate([w_, jnp.concatenate([vh[u], zeros_c], axis=1)], axis=0), "bf16", TN)
          for w_, u in zip(uw, units)]
    qp_l = {}
    for idx, u in enumerate(units):
        q = r_t[u] + yq[idx][:, pw:]
        p_bd = jnp.where(blk, gp[idx][:, pw:], 0.0) + eye_p * g_end[u]
        g_bd = jnp.where(blk, gp[idx][:, :pw], 0.0)
        qp_l[u] = (jnp.concatenate([q, p_bd], axis=0), yq[idx][:, :pw], g_bd)

    hs = [h_ref[p] for p in range(npair)]
    order = range(nsub - 1, -1, -1) if reverse else range(nsub)
    for j in order:
        res = [_mm(qp_l[(j, p)][0], hs[p], "x3") for p in range(npair)]
        for p in range(npair):
            y_ref[0, j * c:(j + 1) * c, p * pw:(p + 1) * pw] = res[p][:c] + qp_l[(j, p)][1]
            hs[p] = res[p][c:] + qp_l[(j, p)][2]
    for p in range(npair):
        h_ref[p] = hs[p]


def _rwkv(rkvc, z, w0, w2p, a0, a2p, kkw, kaw, ctx_len, reverse):
    bsz, ltot, _ = rkvc.shape
    nblk = ltot // RWKV_BLK
    nctx = ctx_len // RWKV_BLK
    lora_blk = COL_LORA // LORA_W
    d = 1 if reverse else 0

    def tok(i):
        return jnp.where(i < nctx, nctx - 1 - i, nblk + nctx - 1 - i) if reverse else i

    par3 = lambda b, i: (d, 0, 0)
    kern = functools.partial(_rwkv_kernel, reverse=reverse)
    return pl.pallas_call(
        kern,
        grid=(bsz, nblk),
        in_specs=[pl.BlockSpec((1, RWKV_BLK, D_RWKV), lambda b, i: (b, tok(i), 0)),
                  pl.BlockSpec((1, RWKV_BLK, D_RWKV), lambda b, i: (b, tok(i), 1)),
                  pl.BlockSpec((1, RWKV_BLK, D_RWKV), lambda b, i: (b, tok(i), 2)),
                  pl.BlockSpec((1, RWKV_BLK, LORA_W), lambda b, i: (b, tok(i), lora_blk)),
                  pl.BlockSpec((1, 1, D_RWKV), par3),
                  pl.BlockSpec((1, LORA_W, D_RWKV), par3),
                  pl.BlockSpec((1, 1, D_RWKV), par3),
                  pl.BlockSpec((1, LORA_W, D_RWKV), par3),
                  pl.BlockSpec((1, D_RWKV), lambda b, i: (0, 0)),
                  pl.BlockSpec((1, D_RWKV), lambda b, i: (0, 0))],
        out_specs=pl.BlockSpec((1, RWKV_BLK, D_RWKV), lambda b, i: (b, tok(i), 0)),
        out_shape=jax.ShapeDtypeStruct((bsz, ltot, D_RWKV), F32),
        scratch_shapes=[pltpu.VMEM((HEADS // 2, 2 * HEAD, 2 * HEAD), F32)],
        compiler_params=_cparams(("parallel", "arbitrary"), 48),
    )(rkvc, rkvc, rkvc, z, w0, w2p, a0, a2p, kkw, kaw)


def _s5_consts(lam_re, lam_im, log_step, b_re, b_im, c_re, c_im):
    t = S5_T
    g, p, hg = b_re.shape
    step = jnp.exp(log_step)[..., None]
    mag = jnp.exp(lam_re * step)
    lb_re = mag * jnp.cos(lam_im * step)
    lb_im = mag * jnp.sin(lam_im * step)
    den = lam_re * lam_re + lam_im * lam_im
    nr = lb_re - 1
    q_re = (nr * lam_re + lb_im * lam_im) / den
    q_im = (lb_im * lam_re - nr * lam_im) / den
    bb_re = q_re[..., None] * b_re - q_im[..., None] * b_im
    bb_im = q_re[..., None] * b_im + q_im[..., None] * b_re
    n = jnp.arange(t + 1, dtype=F32)
    pmag = jnp.exp((lam_re * step)[..., None] * n)
    pw_re = pmag * jnp.cos((lam_im * step)[..., None] * n)
    pw_im = pmag * jnp.sin((lam_im * step)[..., None] * n)
    ct_re = jnp.transpose(c_re, (0, 2, 1))[None, :, :, None, :]
    ct_im = jnp.transpose(c_im, (0, 2, 1))[None, :, :, None, :]
    cp_re = ct_re * pw_re[..., None] - ct_im * pw_im[..., None]
    cp_im = ct_re * pw_im[..., None] + ct_im * pw_re[..., None]
    flat = lambda x: x.reshape(x.shape[:-2] + (x.shape[-2] * x.shape[-1],))
    cf_re, cf_im = flat(cp_re), flat(cp_im)
    hp = lax.Precision.HIGHEST
    kt = (jnp.einsum("dgpi,dgpm->dgim", bb_re, cf_re[..., :t * hg], precision=hp)
          - jnp.einsum("dgpi,dgpm->dgim", bb_im, cf_im[..., :t * hg], precision=hp))
    kb_rev = flat(kt[1].reshape(g, hg, t, hg)[:, :, ::-1])
    zpad = jnp.zeros((g, hg, (t - 1) * hg), F32)
    lag = (jnp.concatenate([zpad, kt[0]], axis=-1)
           + jnp.concatenate([kb_rev, zpad], axis=-1))
    m_tot = jnp.stack([lag[:, :, (t - 1 - s) * hg:(2 * t - 1 - s) * hg] for s in range(t)], axis=1)
    m_tot = m_tot.reshape(g, t * hg, t * hg)

    pwt_re = jnp.transpose(pw_re, (0, 1, 3, 2))
    pwt_im = jnp.transpose(pw_im, (0, 1, 3, 2))
    bbt_re = jnp.transpose(bb_re, (0, 1, 3, 2))
    bbt_im = jnp.transpose(bb_im, (0, 1, 3, 2))

    def inject(d, pr_, pi_):
        pr_, pi_ = pr_[:, :, None, :], pi_[:, :, None, :]
        br, bi = bbt_re[d][:, None], bbt_im[d][:, None]
        return pr_ * br - pi_ * bi, pr_ * bi + pi_ * br

    pf_re, pf_im = inject(0, pwt_re[0, :, :t][:, ::-1], pwt_im[0, :, :t][:, ::-1])
    pb_re, pb_im = inject(1, pwt_re[1, :, :t], pwt_im[1, :, :t])
    inj = jnp.concatenate([pf_re, pb_re, pf_im, pb_im], axis=-1).reshape(g, t * hg, 4 * p)

    cr_re, cr_im = flat(cp_re[1][:, :, ::-1]), flat(cp_im[1][:, :, ::-1])
    out = jnp.concatenate([m_tot, cf_re[0][..., hg:], cr_re[..., :t * hg],
                           -cf_im[0][..., hg:], -cr_im[..., :t * hg]], axis=1)
    lb16 = jnp.stack([jnp.concatenate([pw_re[0, :, :, t], pw_re[1, :, :, t]], axis=-1),
                      jnp.concatenate([pw_im[0, :, :, t], pw_im[1, :, :, t]], axis=-1)], axis=0)
    return inj.astype(BF16), out.astype(BF16), lb16


def _block_transpose8(xs):
    lane_blk = lax.broadcasted_iota(jnp.int32, xs[0].shape, 1) >> 4
    xs = list(xs)
    for d in (4, 2, 1):
        hi = (lane_blk & d) != 0
        for i in range(8):
            if i & d:
                continue
            a, b = xs[i], xs[i + d]
            xs[i] = jnp.where(hi, pltpu.roll(b, 16 * d, 1), a)
            xs[i + d] = jnp.where(hi, b, pltpu.roll(a, 128 - 16 * d, 1))
    return xs


def _s5_kernel(u0_ref, u1_ref, inj_ref, out_ref, l_ref, y_ref, ug_ref, zre_ref, zim_ref, hre_ref,
               him_ref, *, nchunk, nctx):
    gh = ug_ref.shape[0]
    ps = S5_STATE
    t = S5_T
    nlat = nchunk - nctx
    for jt, u_ref in enumerate((u0_ref, u1_ref)):
        for q in range(t // 8):
            cols = [u_ref[0, pl.ds(8 * q + tt, nchunk, stride=t), :] for tt in range(8)]
            grp = _block_transpose8(cols)
            for gl in range(8):
                ug_ref[8 * jt + gl, :, 128 * q:128 * (q + 1)] = grp[gl].astype(BF16)
    for gi in range(gh):
        zg = _dot(ug_ref[gi], inj_ref[gi], NN)
        zre_ref[pl.ds(gi, nchunk, stride=gh), :] = zg[:, 0:2 * ps]
        zim_ref[pl.ds(gi, nchunk, stride=gh), :] = zg[:, 2 * ps:4 * ps]

    lane = lax.broadcasted_iota(jnp.int32, (gh, 2 * ps), 1)
    is_f = lane < ps
    lre = l_ref[0, 0]
    lim = l_ref[0, 1]

    def body(i, carry):
        hre, him = carry
        cb = jnp.where(i < nctx, nctx - 1 - i, nchunk + nctx - 1 - i)
        rf = pl.multiple_of(i * gh, gh)
        rb = pl.multiple_of(cb * gh, gh)
        hre_ref[pl.ds(rf, gh), 0:ps] = hre[:, 0:ps]
        hre_ref[pl.ds(rb, gh), ps:2 * ps] = hre[:, ps:2 * ps]
        him_ref[pl.ds(rf, gh), 0:ps] = him[:, 0:ps]
        him_ref[pl.ds(rb, gh), ps:2 * ps] = him[:, ps:2 * ps]
        zre = jnp.where(is_f, zre_ref[pl.ds(rf, gh), :], zre_ref[pl.ds(rb, gh), :])
        zim = jnp.where(is_f, zim_ref[pl.ds(rf, gh), :], zim_ref[pl.ds(rb, gh), :])
        return lre * hre - lim * him + zre, lre * him + lim * hre + zim

    zero = jnp.zeros((gh, 2 * ps), F32)
    lax.fori_loop(0, nchunk, body, (zero, zero))

    for jt in range(gh // 8):
        ys = []
        for gl in range(8):
            gi = 8 * jt + gl
            hre = hre_ref[pl.ds(gi, nchunk, stride=gh), :].astype(BF16)
            him = him_ref[pl.ds(gi, nchunk, stride=gh), :].astype(BF16)
            ys.append(_dot(jnp.concatenate([ug_ref[gi], hre, him], axis=1), out_ref[gi], NN))
        for q in range(t // 8):
            tok = _block_transpose8([y[:, 128 * q:128 * (q + 1)] for y in ys])
            for tt in range(8):
                y_ref[0, jt, pl.ds(8 * q + tt, nlat, stride=t), :] = tok[tt][nctx:]


def _s5(z, consts, ctx_len):
    inj, out, lb16 = consts
    bsz, ltot, _ = z.shape
    nchunk = ltot // S5_T
    nctx = ctx_len // S5_T
    gh = S5_GH
    halves = S5_GROUPS // gh
    w = S5_T * S5_GROUP
    tiles = gh * S5_GROUP // 128
    ps2 = 2 * S5_STATE
    lb = lb16.reshape(2, halves, gh, ps2).transpose(1, 0, 2, 3)
    kern = functools.partial(_s5_kernel, nchunk=nchunk, nctx=nctx)
    u_spec = lambda k: pl.BlockSpec((1, ltot, 128), lambda b, j: (b, 0, COL_U // 128 + tiles * j + k))
    return pl.pallas_call(
        kern,
        grid=(bsz, halves),
        in_specs=[u_spec(0), u_spec(1),
                  pl.BlockSpec((gh, w, 2 * ps2), lambda b, j: (j, 0, 0)),
                  pl.BlockSpec((gh, w + 2 * ps2, w), lambda b, j: (j, 0, 0)),
                  pl.BlockSpec((1, 2, gh, ps2), lambda b, j: (j, 0, 0, 0))],
        out_specs=pl.BlockSpec((1, tiles, ltot - ctx_len, 128), lambda b, j: (b, j, 0, 0)),
        out_shape=jax.ShapeDtypeStruct((bsz, D_S5 // 128, ltot - ctx_len, 128), F32),
        scratch_shapes=[pltpu.VMEM((gh, nchunk, w), BF16),
                        pltpu.VMEM((nchunk * gh, ps2), F32),
                        pltpu.VMEM((nchunk * gh, ps2), F32),
                        pltpu.VMEM((nchunk * gh, ps2), F32),
                        pltpu.VMEM((nchunk * gh, ps2), F32)],
        compiler_params=_cparams(("parallel", "parallel"), 48),
    )(z, z, inj, out, lb)


def _mix_kernel(x_ref, yf_ref, yb_ref, r_ref, k_ref, v_ref, lo_ref, u_ref, ys_ref, mod_ref,
                seg_ref, a0_ref, a2_ref, g2_ref, kaw_ref, rk_ref, lnw_ref, lnb_ref,
                dsk_ref, gluw_ref, glub_ref, wo_ref, o_ref):
    lo = lo_ref[0]
    seg = seg_ref[...]
    y = yf_ref[0] + yb_ref[0]
    mu = _mm(y, seg, "xa") * (1.0 / HEAD)
    yc = y - mu
    var = _mm(yc * yc, seg, "xa") * (1.0 / HEAD)
    yn = yc * lax.rsqrt(var + RWKV_LN_EPS) * lnw_ref[...] + lnb_ref[...]
    a_f = _sigmoid(a0_ref[0] + _mm(lo, a2_ref[0]))
    a_b = _sigmoid(a0_ref[1] + _mm(lo, a2_ref[1]))
    kaw = kaw_ref[...]
    kd_sum = k_ref[0] * ((1.0 + (a_f - 1.0) * kaw) + (1.0 + (a_b - 1.0) * kaw))
    bonus = _mm(r_ref[0] * kd_sum * rk_ref[...], seg, "xa")
    yn = yn + bonus * v_ref[0]
    gate = _mm(_sigmoid(lo), g2_ref[...])
    rwkv_out = yn * gate

    u = u_ref[0]
    ys = jnp.concatenate([ys_ref[0, i] for i in range(ys_ref.shape[1])], axis=1) + dsk_ref[...] * u
    zg = 0.5 * ys * (1.0 + jnp.tanh(math.sqrt(2.0 / math.pi) * (ys + 0.044715 * (ys * ys * ys))))
    s5_out = zg * _sigmoid(_mm(zg, gluw_ref[...]) + glub_ref[...])

    mix = _mm(rwkv_out, wo_ref[0:D_RWKV, :]) + _mm(s5_out, wo_ref[D_RWKV:, :])
    o_ref[0] = x_ref[0] + mod_ref[0, 2:3, :] * mix


def _mix(x, y_f, y_b, rkvc, z, ys5, mod8, seg, a0, a2p, g2p, kaw, rk, lnw, lnb, dsk, gluw, glub, wo,
         ctx_len, tm):
    bsz, seq, d = x.shape
    off = ctx_len // tm
    lora_blk = COL_LORA // LORA_W
    u_blk = COL_U // D_S5
    tok = lambda c: pl.BlockSpec((1, tm, D_RWKV), lambda b, j: (b, j + off, c))
    full = lambda a: pl.BlockSpec(a.shape, lambda b, j: (0,) * a.ndim)
    return pl.pallas_call(
        _mix_kernel,
        grid=(bsz, seq // tm),
        in_specs=[pl.BlockSpec((1, tm, d), lambda b, j: (b, j, 0)),
                  tok(0), tok(0), tok(0), tok(1), tok(2),
                  pl.BlockSpec((1, tm, LORA_W), lambda b, j: (b, j + off, lora_blk)),
                  pl.BlockSpec((1, tm, D_S5), lambda b, j: (b, j + off, u_blk)),
                  pl.BlockSpec((1, D_S5 // 128, tm, 128), lambda b, j: (b, 0, j, 0)),
                  pl.BlockSpec((1, 8, d), lambda b, j: (b, 0, 0)),
                  full(seg), full(a0), full(a2p), full(g2p), full(kaw), full(rk), full(lnw), full(lnb),
                  full(dsk), full(gluw), full(glub), full(wo)],
        out_specs=pl.BlockSpec((1, tm, d), lambda b, j: (b, j, 0)),
        out_shape=jax.ShapeDtypeStruct((bsz, seq, d), F32),
        compiler_params=_cparams(("parallel", "parallel"), 48),
    )(x, y_f, y_b, rkvc, rkvc, rkvc, z, z, ys5, mod8, seg, a0, a2p, g2p, kaw, rk, lnw, lnb,
      dsk, gluw, glub, wo)


def _ffn_kernel(x_ref, mod_ref, g2_ref, w1_ref, w3_ref, w2_ref, gf_ref, o_ref):
    x = x_ref[0]
    ms = jnp.mean(x * x, axis=-1, keepdims=True)
    h = x * lax.rsqrt(ms + NORM_EPS) * g2_ref[...]
    h = h * (1.0 + mod_ref[0, 4:5, :]) + mod_ref[0, 3:4, :]
    hb = h.astype(BF16)
    h1 = _dot(hb, w1_ref[...], NN)
    h3 = _dot(hb, w3_ref[...], NN)
    act = (h1 * _sigmoid(h1)) * h3
    f = _dot(act.astype(BF16), w2_ref[...], NN)
    out = x + mod_ref[0, 5:6, :] * f
    ms2 = jnp.mean(out * out, axis=-1, keepdims=True)
    o_ref[0] = out * lax.rsqrt(ms2 + NORM_EPS) * gf_ref[...]


def _ffn(x1, mod8, g2, w1, w3, w2, gf, tm):
    bsz, seq, d = x1.shape
    dff = w1.shape[1]
    const = lambda shape: pl.BlockSpec(shape, lambda b, j: (0, 0), pipeline_mode=pl.Buffered(1))
    return pl.pallas_call(
        _ffn_kernel,
        grid=(bsz, seq // tm),
        in_specs=[pl.BlockSpec((1, tm, d), lambda b, j: (b, j, 0)),
                  pl.BlockSpec((1, 8, d), lambda b, j: (b, 0, 0)),
                  const((1, d)), const((d, dff)), const((d, dff)), const((dff, d)), const((1, d))],
        out_specs=pl.BlockSpec((1, tm, d), lambda b, j: (b, j, 0)),
        out_shape=jax.ShapeDtypeStruct((bsz, seq, d), F32),
        compiler_params=_cparams(("parallel", "parallel"), 56),
    )(x1, mod8, g2.reshape(1, d), w1, w3, w2, gf.reshape(1, d))


def _pad_rows(w, row0, rows_total):
    pad = [(0, 0)] * (w.ndim - 2) + [(row0, rows_total - row0 - w.shape[-2]), (0, 0)]
    return jnp.pad(w, pad)


def kernel(x, c, ctx, c_ctx, mod_w, mod_b, norm1_g, norm2_g, w_in, w_out, rwkv_conv, rwkv_w0, rwkv_w2, rwkv_a0, rwkv_a2, rwkv_g2, rwkv_kk, rwkv_ka, rwkv_rk, rwkv_ln_w, rwkv_ln_b, s5_lam_re, s5_lam_im, s5_log_step, s5_b_re, s5_b_im, s5_c_re, s5_c_im, s5_d, s5_glu_w, s5_glu_b, ffn_w1, ffn_w3, ffn_w2, final_g):
    bsz, seq, d = x.shape
    ctx_len = ctx.shape[1]
    ltot = ctx_len + seq
    layer = 0

    cc = jnp.concatenate([c, c_ctx[None, :], jnp.zeros((8 - bsz - 1, d), F32)], axis=0)
    mod = _adaln(cc, mod_w[layer], mod_b[layer])
    mod8 = jnp.pad(mod[:bsz + 1].reshape(bsz + 1, N_MOD, d), ((0, 0), (0, 8 - N_MOD), (0, 0)))

    wi = w_in[layer]
    w_z = jnp.concatenate([wi[:, :3 * D_RWKV], wi[:, -D_S5:], wi[:, 3 * D_RWKV:-D_S5],
                           jnp.zeros((d, D_Z - wi.shape[1]), F32)], axis=1).astype(BF16)
    z = _inproj(ctx, x, mod8, norm1_g[layer], w_z, 256)

    rkvc = _conv(z, rwkv_conv[layer], ctx_len)

    w2p = _pad_rows(rwkv_w2[layer], 0, LORA_W).at[1].set(_pad_rows(rwkv_w2[layer, 1], DECAY_LORA, LORA_W))
    a2p = jnp.stack([_pad_rows(rwkv_a2[layer, 0], LORA_AD, LORA_W),
                     _pad_rows(rwkv_a2[layer, 1], LORA_AD + AAA_LORA, LORA_W)])
    w0 = rwkv_w0[layer].reshape(2, 1, D_RWKV)
    a0 = rwkv_a0[layer].reshape(2, 1, D_RWKV)
    kkw = rwkv_kk[layer].reshape(1, D_RWKV)
    kaw = rwkv_ka[layer].reshape(1, D_RWKV)
    y_f = _rwkv(rkvc, z, w0, w2p, a0, a2p, kkw, kaw, ctx_len, False)
    y_b = _rwkv(rkvc, z, w0, w2p, a0, a2p, kkw, kaw, ctx_len, True)

    consts = _s5_consts(s5_lam_re[layer], s5_lam_im[layer], s5_log_step[layer], s5_b_re[layer],
                        s5_b_im[layer], s5_c_re[layer], s5_c_im[layer])
    ys5 = _s5(z, consts, ctx_len)

    hh = jnp.arange(D_RWKV) // HEAD
    seg = (hh[:, None] == hh[None, :]).astype(BF16)
    g2p = _pad_rows(rwkv_g2[layer], LORA_GD, LORA_W)
    x1 = _mix(x, y_f, y_b, rkvc, z, ys5, mod8, seg, a0, a2p, g2p, kaw,
              rwkv_rk[layer].reshape(1, D_RWKV), rwkv_ln_w[layer].reshape(1, D_RWKV),
              rwkv_ln_b[layer].reshape(1, D_RWKV), s5_d[layer].reshape(1, D_S5),
              s5_glu_w[layer].astype(BF16), s5_glu_b[layer].reshape(1, D_S5),
              w_out[layer].astype(BF16), ctx_len, 256)

    return _ffn(x1, mod8, norm2_g[layer], ffn_w1[layer].astype(BF16), ffn_w3[layer].astype(BF16),
                ffn_w2[layer].astype(BF16), final_g, 256)
```

```python
import functools
import math

import jax
import jax.numpy as jnp
from jax import lax
from jax.experimental import pallas as pl
from jax.experimental.pallas import tpu as pltpu

F32 = jnp.float32
BF16 = jnp.bfloat16

D_MODEL = 1024
GRID_W = 64
D_RWKV = 512
HEAD = 64
HEADS = D_RWKV // HEAD
D_S5 = 512
S5_GROUP = 16
S5_GROUPS = D_S5 // S5_GROUP
S5_STATE = 64
DECAY_LORA = 32
AAA_LORA = 32
GATE_LORA = 96
N_MOD = 6
NORM_EPS = 1e-6
RWKV_LN_EPS = 64e-5

LORA_W = 256
COL_U = 3 * D_RWKV
COL_LORA = COL_U + D_S5
D_Z = COL_LORA + LORA_W
LORA_AD = 2 * DECAY_LORA
LORA_GD = LORA_AD + 2 * AAA_LORA

CHUNK = 64
RWKV_BLK = 256
S5_T = 16
S5_GH = 16
CONV_PAD = 72
CONV_TC = 256
CONV_PIECE = 128

NN = (((1,), (0,)), ((), ()))
NT = (((1,), (1,)), ((), ()))
TN = (((0,), (0,)), ((), ()))


def _dot(a, b, dims):
    return lax.dot_general(a, b, dims, preferred_element_type=F32)


def _split2(x):
    hi = x.astype(BF16)
    lo = (x - hi.astype(F32)).astype(BF16)
    return hi, lo


def _split3(x):
    hi = x.astype(BF16)
    r1 = x - hi.astype(F32)
    mid = r1.astype(BF16)
    lo = (r1 - mid.astype(F32)).astype(BF16)
    return hi, mid, lo


def _mm(a, b, mode="bf16", dims=NN):
    if mode == "bf16":
        return _dot(a.astype(BF16), b.astype(BF16), dims)
    if mode == "x3":
        ah, al = _split2(a)
        bh, bl = _split2(b)
        return _dot(ah, bh, dims) + (_dot(ah, bl, dims) + _dot(al, bh, dims))
    if mode == "xa":
        bb = b.astype(BF16)
        a1, a2, a3 = _split3(a)
        return _dot(a1, bb, dims) + (_dot(a2, bb, dims) + _dot(a3, bb, dims))
    if mode == "xa2":
        bb = b.astype(BF16)
        a1, a2 = _split2(a)
        return _dot(a1, bb, dims) + _dot(a2, bb, dims)
    if mode == "xb":
        ab = a.astype(BF16)
        b1, b2, b3 = _split3(b)
        return _dot(ab, b1, dims) + (_dot(ab, b2, dims) + _dot(ab, b3, dims))
    raise ValueError(mode)


def _sigmoid(x):
    return 1.0 / (1.0 + jnp.exp(-x))


def _softplus(x):
    return jnp.maximum(x, 0.0) + jnp.log(1.0 + jnp.exp(-jnp.abs(x)))


def _cparams(sem, vmem_mb):
    return pltpu.CompilerParams(dimension_semantics=sem, vmem_limit_bytes=vmem_mb * 1024 * 1024)


def _adaln_kernel(c_ref, w_ref, b_ref, o_ref):
    c = c_ref[...]
    s = c * _sigmoid(c)
    o_ref[...] = _mm(s, w_ref[...], "x3") + b_ref[...]


def _adaln(cc, mod_w, mod_b):
    rows, d = cc.shape
    n = mod_w.shape[1]
    tn = 1024
    return pl.pallas_call(
        _adaln_kernel,
        grid=(n // tn,),
        in_specs=[pl.BlockSpec((rows, d), lambda j: (0, 0)),
                  pl.BlockSpec((d, tn), lambda j: (0, j)),
                  pl.BlockSpec((1, tn), lambda j: (0, j))],
        out_specs=pl.BlockSpec((rows, tn), lambda j: (0, j)),
        out_shape=jax.ShapeDtypeStruct((rows, n), F32),
        compiler_params=_cparams(("parallel",), 40),
    )(cc, mod_w, mod_b.reshape(1, n))


def _inproj_kernel(c_ref, x_ref, mod_ref, g_ref, w_ref, o_ref, *, ctx_blocks):
    x = jnp.where(pl.program_id(1) < ctx_blocks, c_ref[0], x_ref[0])
    ms = jnp.mean(x * x, axis=-1, keepdims=True)
    xn = x * lax.rsqrt(ms + NORM_EPS) * g_ref[...]
    shift = mod_ref[0, 0:1, :]
    scale = mod_ref[0, 1:2, :]
    h = xn * (1.0 + scale) + shift
    o_ref[0] = _mm(h, w_ref[...], "bf16")


def _inproj(ctx, x, mod8, g1, w_z, tm):
    bsz, seq, d = x.shape
    ctx_blocks = ctx.shape[1] // tm
    dz = w_z.shape[1]
    kern = functools.partial(_inproj_kernel, ctx_blocks=ctx_blocks)
    return pl.pallas_call(
        kern,
        grid=(bsz, ctx_blocks + seq // tm),
        in_specs=[pl.BlockSpec((1, tm, d), lambda b, j: (b, jnp.minimum(j, ctx_blocks - 1), 0)),
                  pl.BlockSpec((1, tm, d), lambda b, j: (b, jnp.maximum(j - ctx_blocks, 0), 0)),
                  pl.BlockSpec((1, 8, d), lambda b, j: (jnp.where(j < ctx_blocks, bsz, b), 0, 0)),
                  pl.BlockSpec((1, d), lambda b, j: (0, 0)),
                  pl.BlockSpec((d, dz), lambda b, j: (0, 0))],
        out_specs=pl.BlockSpec((1, tm, dz), lambda b, j: (b, j, 0)),
        out_shape=jax.ShapeDtypeStruct((bsz, ctx.shape[1] + seq, dz), F32),
        compiler_params=_cparams(("parallel", "parallel"), 48),
    )(ctx, x, mod8, g1.reshape(1, d), w_z)


def _conv_kernel(z_ref, w_ref, o_ref, xs_ref, *, ctx_len, lat_len):
    tc = z_ref.shape[2]
    pc = CONV_PIECE
    ctx_off = CONV_PAD
    lat_off = 2 * CONV_PAD + ctx_len
    zpad = jnp.zeros((CONV_PAD, tc), F32)
    for s in range(3):
        xs_ref[s, 0:CONV_PAD, :] = zpad
        xs_ref[s, ctx_off + ctx_len:lat_off, :] = zpad
        xs_ref[s, lat_off + lat_len:lat_off + lat_len + CONV_PAD, :] = zpad
    col = lax.broadcasted_iota(jnp.int32, (pc, tc), 0) % GRID_W
    keep_l = (col != GRID_W - 1).astype(F32)
    keep_r = (col != 0).astype(F32)
    for p in range(ctx_len // pc):
        xs_ref[0, ctx_off + p * pc:ctx_off + (p + 1) * pc, :] = z_ref[0, p * pc:(p + 1) * pc, :]
    for p in range(lat_len // pc):
        x = z_ref[0, ctx_len + p * pc:ctx_len + (p + 1) * pc, :]
        dst = slice(lat_off + p * pc, lat_off + (p + 1) * pc)
        xs_ref[0, dst, :] = x
        xs_ref[1, dst, :] = x * keep_l
        xs_ref[2, dst, :] = x * keep_r
    w = [w_ref[t:t + 1, :] for t in range(9)]
    for p in range(ctx_len // pc):
        base = ctx_off + p * pc
        acc = w[3] * xs_ref[0, base - 1:base - 1 + pc, :]
        acc = acc + w[4] * xs_ref[0, base:base + pc, :]
        acc = acc + w[5] * xs_ref[0, base + 1:base + 1 + pc, :]
        o_ref[0, p * pc:(p + 1) * pc, :] = acc
    for p in range(lat_len // pc):
        base = lat_off + p * pc
        acc = None
        for di in range(3):
            for dj in range(3):
                off = base + (di - 1) * GRID_W + (dj - 1)
                src = (1, 0, 2)[dj]
                term = w[3 * di + dj] * xs_ref[src, off:off + pc, :]
                acc = term if acc is None else acc + term
        o_ref[0, ctx_len + p * pc:ctx_len + (p + 1) * pc, :] = acc


def _conv(z, conv_w, ctx_len):
    bsz, ltot, _ = z.shape
    lat_len = ltot - ctx_len
    ch = 3 * D_RWKV
    rows = 3 * CONV_PAD + ltot
    kern = functools.partial(_conv_kernel, ctx_len=ctx_len, lat_len=lat_len)
    return pl.pallas_call(
        kern,
        grid=(bsz, ch // CONV_TC),
        in_specs=[pl.BlockSpec((1, ltot, CONV_TC), lambda b, j: (b, 0, j)),
                  pl.BlockSpec((9, CONV_TC), lambda b, j: (0, j))],
        out_specs=pl.BlockSpec((1, ltot, CONV_TC), lambda b, j: (b, 0, j)),
        out_shape=jax.ShapeDtypeStruct((bsz, ltot, ch), F32),
        scratch_shapes=[pltpu.VMEM((3, rows, CONV_TC), F32)],
        compiler_params=_cparams(("parallel", "parallel"), 48),
    )(z, conv_w.reshape(9, ch))


def _bd(x, lo):
    return jnp.concatenate([jnp.where(lo, x, 0.0), jnp.where(lo, 0.0, x)], axis=0)


def _tri_inv_pairs(n_mats, size, lo):
    ii = lax.broadcasted_iota(jnp.int32, (size, 2 * size), 0)
    jj = lax.broadcasted_iota(jnp.int32, (size, 2 * size), 1) & (size - 1)
    eye = (ii == jj).astype(F32)
    in2 = (ii >> 1) == (jj >> 1)
    ts = [eye + jnp.where(in2, n, 0.0) for n in n_mats]
    lg = 1
    while (1 << lg) < size:
        same = (ii >> (lg + 1)) == (jj >> (lg + 1))
        diff = (ii >> lg) != (jj >> lg)
        nls = [jnp.where(same, jnp.where(diff, n, 0.0), 0.0) for n in n_mats]
        tmp = [_mm(t, _bd(nl, lo)) for t, nl in zip(ts, nls)]
        ts = [t + _mm(x, _bd(t, lo)) for x, t in zip(tmp, ts)]
        lg += 1
    return ts


def _rwkv_kernel(r_ref, k_ref, v_ref, lo_ref, w0_ref, w2_ref, a0_ref, a2_ref, kkw_ref, kaw_ref,
                 y_ref, h_ref, *, reverse):
    step = pl.program_id(1)

    @pl.when(step == 0)
    def _():
        h_ref[...] = jnp.zeros_like(h_ref)

    r = r_ref[0]
    k = k_ref[0]
    v = v_ref[0]
    lora = lo_ref[0]
    n = r.shape[0]
    c = CHUNK
    nsub = n // c
    lgc = c.bit_length() - 1
    pw = 2 * HEAD
    npair = HEADS // 2

    logit_w = w0_ref[0] + _mm(jnp.tanh(lora), w2_ref[0])
    logw = -_softplus(-logit_w) - 0.5
    lw = -jnp.exp(logw)
    a = _sigmoid(a0_ref[0] + _mm(lora, a2_ref[0]))
    kd = k * (1.0 + (a - 1.0) * kaw_ref[...])
    kk = k * kkw_ref[...]

    ti = lax.broadcasted_iota(jnp.int32, (n, n), 0)
    tj = lax.broadcasted_iota(jnp.int32, (n, n), 1)
    causal = (tj >= ti) if reverse else (tj <= ti)
    tri = jnp.where((ti >> lgc) == (tj >> lgc), jnp.where(causal, 1.0, 0.0), 0.0)
    g = _mm(tri, lw, "xb")

    ci2 = lax.broadcasted_iota(jnp.int32, (c, pw), 0)
    lane2 = lax.broadcasted_iota(jnp.int32, (c, pw), 1)
    cj2 = lane2 & (c - 1)
    lo = lane2 < HEAD
    incl2 = (cj2 >= ci2) if reverse else (cj2 <= ci2)
    strict2 = (cj2 > ci2) if reverse else (cj2 < ci2)
    ri = lax.broadcasted_iota(jnp.int32, (pw, pw), 0)
    rj = lax.broadcasted_iota(jnp.int32, (pw, pw), 1)
    blk = (ri < HEAD) == (rj < HEAD)
    eye_p = (ri == rj).astype(F32)
    zeros_c = jnp.zeros((c, pw), F32)
    zeros_p = jnp.zeros((pw, pw), F32)

    units = [(j, p) for j in range(nsub) for p in range(npair)]
    a_t, r_t, b_t, k_t, b_e, k_e, vh, g_end = {}, {}, {}, {}, {}, {}, {}, {}
    for j in range(nsub):
        rows = slice(j * c, (j + 1) * c)
        lw_j = lw[rows]
        g_j = g[rows]
        tot = jnp.sum(lw_j, axis=0, keepdims=True)
        e_prev = jnp.exp(g_j - lw_j)
        e_neg = jnp.exp(-g_j)
        e_pos = jnp.exp(g_j)
        e_end = jnp.exp(tot - g_j)
        e_tot = jnp.exp(tot)
        for p in range(npair):
            sl = slice(p * pw, (p + 1) * pw)
            kkp = kk[rows, sl]
            sq = kkp * kkp
            n2 = jnp.where(lo, jnp.sum(jnp.where(lo, sq, 0.0), axis=-1, keepdims=True),
                           jnp.sum(jnp.where(lo, 0.0, sq), axis=-1, keepdims=True))
            kkp = kkp * lax.rsqrt(jnp.maximum(n2, 1e-12))
            ka = kkp * a[rows, sl]
            u = (j, p)
            a_t[u] = -kkp * e_prev[:, sl]
            b_t[u] = ka * e_neg[:, sl]
            k_t[u] = kd[rows, sl] * e_neg[:, sl]
            r_t[u] = r[rows, sl] * e_pos[:, sl]
            b_e[u] = ka * e_end[:, sl]
            k_e[u] = kd[rows, sl] * e_end[:, sl]
            vh[u] = v[rows, sl]
            g_end[u] = e_tot[:, sl]

    ar = [jnp.concatenate([a_t[u], r_t[u]], axis=0) for u in units]
    s_bk = [_mm(x, jnp.concatenate([_bd(b_t[u], lo), _bd(k_t[u], lo)], axis=0), "bf16", NT)
            for x, u in zip(ar, units)]
    nab = [jnp.where(strict2, s[:c, :pw], 0.0) for s in s_bk]
    aak = [jnp.where(strict2, s[:c, pw:], 0.0) for s in s_bk]
    arbk = [jnp.concatenate([jnp.where(incl2, s[c:, :pw], 0.0), jnp.where(incl2, s[c:, pw:], 0.0)], axis=1)
            for s in s_bk]
    t_inv = _tri_inv_pairs(nab, c, lo)
    x1 = [_mm(m, _bd(vh[u], lo)) for m, u in zip(aak, units)]
    uw = [_mm(t, jnp.concatenate([_bd(x, lo), _bd(a_t[u], lo)], axis=1))
          for t, x, u in zip(t_inv, x1, units)]
    yq = [_mm(m, jnp.concatenate(
        [jnp.concatenate([_bd(w_[:, :pw], lo), _bd(w_[:, pw:], lo)], axis=1),
         jnp.concatenate([_bd(vh[u], lo), zeros_p], axis=1)], axis=0))
        for m, w_, u in zip(arbk, uw, units)]
    gp = [_mm(jnp.concatenate([b_e[u], k_e[u]], axis=0),
              jnp.concatenate([w_, jnp.concatenate([vh[u], zeros_c], axis=1)], axis=0), "bf16", TN)
          for w_, u in zip(uw, units)]
    qp_l = {}
    for idx, u in enumerate(units):
        q = r_t[u] + yq[idx][:, pw:]
        p_bd = jnp.where(blk, gp[idx][:, pw:], 0.0) + eye_p * g_end[u]
        g_bd = jnp.where(blk, gp[idx][:, :pw], 0.0)
        qp_l[u] = (jnp.concatenate([q, p_bd], axis=0), yq[idx][:, :pw], g_bd)

    hs = [h_ref[p] for p in range(npair)]
    order = range(nsub - 1, -1, -1) if reverse else range(nsub)
    for j in order:
        res = [_mm(qp_l[(j, p)][0], hs[p]) for p in range(npair)]
        for p in range(npair):
            y_ref[0, j * c:(j + 1) * c, p * pw:(p + 1) * pw] = res[p][:c] + qp_l[(j, p)][1]
            hs[p] = res[p][c:] + qp_l[(j, p)][2]
    for p in range(npair):
        h_ref[p] = hs[p]


def _rwkv(rkvc, z, w0, w2p, a0, a2p, kkw, kaw, ctx_len, reverse):
    bsz, ltot, _ = rkvc.shape
    nblk = ltot // RWKV_BLK
    nctx = ctx_len // RWKV_BLK
    lora_blk = COL_LORA // LORA_W
    d = 1 if reverse else 0

    def tok(i):
        return jnp.where(i < nctx, nctx - 1 - i, nblk + nctx - 1 - i) if reverse else i

    par3 = lambda b, i: (d, 0, 0)
    kern = functools.partial(_rwkv_kernel, reverse=reverse)
    return pl.pallas_call(
        kern,
        grid=(bsz, nblk),
        in_specs=[pl.BlockSpec((1, RWKV_BLK, D_RWKV), lambda b, i: (b, tok(i), 0)),
                  pl.BlockSpec((1, RWKV_BLK, D_RWKV), lambda b, i: (b, tok(i), 1)),
                  pl.BlockSpec((1, RWKV_BLK, D_RWKV), lambda b, i: (b, tok(i), 2)),
                  pl.BlockSpec((1, RWKV_BLK, LORA_W), lambda b, i: (b, tok(i), lora_blk)),
                  pl.BlockSpec((1, 1, D_RWKV), par3),
                  pl.BlockSpec((1, LORA_W, D_RWKV), par3),
                  pl.BlockSpec((1, 1, D_RWKV), par3),
                  pl.BlockSpec((1, LORA_W, D_RWKV), par3),
                  pl.BlockSpec((1, D_RWKV), lambda b, i: (0, 0)),
                  pl.BlockSpec((1, D_RWKV), lambda b, i: (0, 0))],
        out_specs=pl.BlockSpec((1, RWKV_BLK, D_RWKV), lambda b, i: (b, tok(i), 0)),
        out_shape=jax.ShapeDtypeStruct((bsz, ltot, D_RWKV), F32),
        scratch_shapes=[pltpu.VMEM((HEADS // 2, 2 * HEAD, 2 * HEAD), F32)],
        compiler_params=_cparams(("parallel", "arbitrary"), 48),
    )(rkvc, rkvc, rkvc, z, w0, w2p, a0, a2p, kkw, kaw)


def _s5_consts(lam_re, lam_im, log_step, b_re, b_im, c_re, c_im):
    t = S5_T
    g, p, hg = b_re.shape
    step = jnp.exp(log_step)[..., None]
    mag = jnp.exp(lam_re * step)
    lb_re = mag * jnp.cos(lam_im * step)
    lb_im = mag * jnp.sin(lam_im * step)
    den = lam_re * lam_re + lam_im * lam_im
    nr = lb_re - 1
    q_re = (nr * lam_re + lb_im * lam_im) / den
    q_im = (lb_im * lam_re - nr * lam_im) / den
    bb_re = q_re[..., None] * b_re - q_im[..., None] * b_im
    bb_im = q_re[..., None] * b_im + q_im[..., None] * b_re
    n = jnp.arange(t + 1, dtype=F32)
    pmag = jnp.exp((lam_re * step)[..., None] * n)
    pw_re = pmag * jnp.cos((lam_im * step)[..., None] * n)
    pw_im = pmag * jnp.sin((lam_im * step)[..., None] * n)
    ct_re = jnp.transpose(c_re, (0, 2, 1))[None, :, :, None, :]
    ct_im = jnp.transpose(c_im, (0, 2, 1))[None, :, :, None, :]
    cp_re = ct_re * pw_re[..., None] - ct_im * pw_im[..., None]
    cp_im = ct_re * pw_im[..., None] + ct_im * pw_re[..., None]
    flat = lambda x: x.reshape(x.shape[:-2] + (x.shape[-2] * x.shape[-1],))
    cf_re, cf_im = flat(cp_re), flat(cp_im)
    hp = lax.Precision.HIGHEST
    kt = (jnp.einsum("dgpi,dgpm->dgim", bb_re, cf_re[..., :t * hg], precision=hp)
          - jnp.einsum("dgpi,dgpm->dgim", bb_im, cf_im[..., :t * hg], precision=hp))
    kb_rev = flat(kt[1].reshape(g, hg, t, hg)[:, :, ::-1])
    zpad = jnp.zeros((g, hg, (t - 1) * hg), F32)
    lag = (jnp.concatenate([zpad, kt[0]], axis=-1)
           + jnp.concatenate([kb_rev, zpad], axis=-1))
    m_tot = jnp.stack([lag[:, :, (t - 1 - s) * hg:(2 * t - 1 - s) * hg] for s in range(t)], axis=1)
    m_tot = m_tot.reshape(g, t * hg, t * hg)

    pwt_re = jnp.transpose(pw_re, (0, 1, 3, 2))
    pwt_im = jnp.transpose(pw_im, (0, 1, 3, 2))
    bbt_re = jnp.transpose(bb_re, (0, 1, 3, 2))
    bbt_im = jnp.transpose(bb_im, (0, 1, 3, 2))

    def inject(d, pr_, pi_):
        pr_, pi_ = pr_[:, :, None, :], pi_[:, :, None, :]
        br, bi = bbt_re[d][:, None], bbt_im[d][:, None]
        return pr_ * br - pi_ * bi, pr_ * bi + pi_ * br

    pf_re, pf_im = inject(0, pwt_re[0, :, :t][:, ::-1], pwt_im[0, :, :t][:, ::-1])
    pb_re, pb_im = inject(1, pwt_re[1, :, :t], pwt_im[1, :, :t])
    inj = jnp.concatenate([pf_re, pb_re, pf_im, pb_im], axis=-1).reshape(g, t * hg, 4 * p)

    cr_re, cr_im = flat(cp_re[1][:, :, ::-1]), flat(cp_im[1][:, :, ::-1])
    out = jnp.concatenate([m_tot, cf_re[0][..., hg:], cr_re[..., :t * hg],
                           -cf_im[0][..., hg:], -cr_im[..., :t * hg]], axis=1)
    lb16 = jnp.stack([jnp.concatenate([pw_re[0, :, :, t], pw_re[1, :, :, t]], axis=-1),
                      jnp.concatenate([pw_im[0, :, :, t], pw_im[1, :, :, t]], axis=-1)], axis=0)
    return inj.astype(BF16), out.astype(BF16), lb16


def _block_transpose8(xs):
    lane_blk = lax.broadcasted_iota(jnp.int32, xs[0].shape, 1) >> 4
    xs = list(xs)
    for d in (4, 2, 1):
        hi = (lane_blk & d) != 0
        for i in range(8):
            if i & d:
                continue
            a, b = xs[i], xs[i + d]
            xs[i] = jnp.where(hi, pltpu.roll(b, 16 * d, 1), a)
            xs[i + d] = jnp.where(hi, b, pltpu.roll(a, 128 - 16 * d, 1))
    return xs


def _s5_kernel(u0_ref, u1_ref, inj_ref, out_ref, l_ref, y_ref, ug_ref, zre_ref, zim_ref, hre_ref,
               him_ref, *, nchunk, nctx):
    gh = ug_ref.shape[0]
    ps = S5_STATE
    t = S5_T
    nlat = nchunk - nctx
    for jt, u_ref in enumerate((u0_ref, u1_ref)):
        for q in range(t // 8):
            cols = [u_ref[0, pl.ds(8 * q + tt, nchunk, stride=t), :] for tt in range(8)]
            grp = _block_transpose8(cols)
            for gl in range(8):
                ug_ref[8 * jt + gl, :, 128 * q:128 * (q + 1)] = grp[gl].astype(BF16)
    for gi in range(gh):
        zg = _dot(ug_ref[gi], inj_ref[gi], NN)
        zre_ref[pl.ds(gi, nchunk, stride=gh), :] = zg[:, 0:2 * ps]
        zim_ref[pl.ds(gi, nchunk, stride=gh), :] = zg[:, 2 * ps:4 * ps]

    lane = lax.broadcasted_iota(jnp.int32, (gh, 2 * ps), 1)
    is_f = lane < ps
    lre = l_ref[0, 0]
    lim = l_ref[0, 1]

    def body(i, carry):
        hre, him = carry
        cb = jnp.where(i < nctx, nctx - 1 - i, nchunk + nctx - 1 - i)
        rf = pl.multiple_of(i * gh, gh)
        rb = pl.multiple_of(cb * gh, gh)
        hre_ref[pl.ds(rf, gh), 0:ps] = hre[:, 0:ps]
        hre_ref[pl.ds(rb, gh), ps:2 * ps] = hre[:, ps:2 * ps]
        him_ref[pl.ds(rf, gh), 0:ps] = him[:, 0:ps]
        him_ref[pl.ds(rb, gh), ps:2 * ps] = him[:, ps:2 * ps]
        zre = jnp.where(is_f, zre_ref[pl.ds(rf, gh), :], zre_ref[pl.ds(rb, gh), :])
        zim = jnp.where(is_f, zim_ref[pl.ds(rf, gh), :], zim_ref[pl.ds(rb, gh), :])
        return lre * hre - lim * him + zre, lre * him + lim * hre + zim

    zero = jnp.zeros((gh, 2 * ps), F32)
    lax.fori_loop(0, nchunk, body, (zero, zero))

    for jt in range(gh // 8):
        ys = []
        for gl in range(8):
            gi = 8 * jt + gl
            hre = hre_ref[pl.ds(gi, nchunk, stride=gh), :].astype(BF16)
            him = him_ref[pl.ds(gi, nchunk, stride=gh), :].astype(BF16)
            ys.append(_dot(jnp.concatenate([ug_ref[gi], hre, him], axis=1), out_ref[gi], NN))
        for q in range(t // 8):
            tok = _block_transpose8([y[:, 128 * q:128 * (q + 1)] for y in ys])
            for tt in range(8):
                y_ref[0, jt, pl.ds(8 * q + tt, nlat, stride=t), :] = tok[tt][nctx:]


def _s5(z, consts, ctx_len):
    inj, out, lb16 = consts
    bsz, ltot, _ = z.shape
    nchunk = ltot // S5_T
    nctx = ctx_len // S5_T
    gh = S5_GH
    halves = S5_GROUPS // gh
    w = S5_T * S5_GROUP
    tiles = gh * S5_GROUP // 128
    ps2 = 2 * S5_STATE
    lb = lb16.reshape(2, halves, gh, ps2).transpose(1, 0, 2, 3)
    kern = functools.partial(_s5_kernel, nchunk=nchunk, nctx=nctx)
    u_spec = lambda k: pl.BlockSpec((1, ltot, 128), lambda b, j: (b, 0, COL_U // 128 + tiles * j + k))
    return pl.pallas_call(
        kern,
        grid=(bsz, halves),
        in_specs=[u_spec(0), u_spec(1),
                  pl.BlockSpec((gh, w, 2 * ps2), lambda b, j: (j, 0, 0)),
                  pl.BlockSpec((gh, w + 2 * ps2, w), lambda b, j: (j, 0, 0)),
                  pl.BlockSpec((1, 2, gh, ps2), lambda b, j: (j, 0, 0, 0))],
        out_specs=pl.BlockSpec((1, tiles, ltot - ctx_len, 128), lambda b, j: (b, j, 0, 0)),
        out_shape=jax.ShapeDtypeStruct((bsz, D_S5 // 128, ltot - ctx_len, 128), F32),
        scratch_shapes=[pltpu.VMEM((gh, nchunk, w), BF16),
                        pltpu.VMEM((nchunk * gh, ps2), F32),
                        pltpu.VMEM((nchunk * gh, ps2), F32),
                        pltpu.VMEM((nchunk * gh, ps2), F32),
                        pltpu.VMEM((nchunk * gh, ps2), F32)],
        compiler_params=_cparams(("parallel", "parallel"), 48),
    )(z, z, inj, out, lb)


def _mix_kernel(x_ref, yf_ref, yb_ref, r_ref, k_ref, v_ref, lo_ref, u_ref, ys_ref, mod_ref,
                seg_ref, a0_ref, a2_ref, g2_ref, kaw_ref, rk_ref, lnw_ref, lnb_ref,
                dsk_ref, gluw_ref, glub_ref, wo_ref, o_ref):
    lo = lo_ref[0]
    seg = seg_ref[...]
    y = yf_ref[0] + yb_ref[0]
    mu = _mm(y, seg, "xa2") * (1.0 / HEAD)
    yc = y - mu
    var = _mm(yc * yc, seg, "xa2") * (1.0 / HEAD)
    yn = yc * lax.rsqrt(var + RWKV_LN_EPS) * lnw_ref[...] + lnb_ref[...]
    a_f = _sigmoid(a0_ref[0] + _mm(lo, a2_ref[0]))
    a_b = _sigmoid(a0_ref[1] + _mm(lo, a2_ref[1]))
    kaw = kaw_ref[...]
    kd_sum = k_ref[0] * ((1.0 + (a_f - 1.0) * kaw) + (1.0 + (a_b - 1.0) * kaw))
    bonus = _mm(r_ref[0] * kd_sum * rk_ref[...], seg, "xa2")
    yn = yn + bonus * v_ref[0]
    gate = _mm(_sigmoid(lo), g2_ref[...])
    rwkv_out = yn * gate

    u = u_ref[0]
    ys = jnp.concatenate([ys_ref[0, i] for i in range(ys_ref.shape[1])], axis=1) + dsk_ref[...] * u
    zg = 0.5 * ys * (1.0 + jnp.tanh(math.sqrt(2.0 / math.pi) * (ys + 0.044715 * (ys * ys * ys))))
    s5_out = zg * _sigmoid(_mm(zg, gluw_ref[...]) + glub_ref[...])

    mix = _mm(rwkv_out, wo_ref[0:D_RWKV, :]) + _mm(s5_out, wo_ref[D_RWKV:, :])
    o_ref[0] = x_ref[0] + mod_ref[0, 2:3, :] * mix


def _mix(x, y_f, y_b, rkvc, z, ys5, mod8, seg, a0, a2p, g2p, kaw, rk, lnw, lnb, dsk, gluw, glub, wo,
         ctx_len, tm):
    bsz, seq, d = x.shape
    assert ctx_len % tm == 0 and seq % tm == 0
    off = ctx_len // tm
    lora_blk = COL_LORA // LORA_W
    u_blk = COL_U // D_S5
    tok = lambda c: pl.BlockSpec((1, tm, D_RWKV), lambda b, j: (b, j + off, c))
    full = lambda a: pl.BlockSpec(a.shape, lambda b, j: (0,) * a.ndim)
    return pl.pallas_call(
        _mix_kernel,
        grid=(bsz, seq // tm),
        in_specs=[pl.BlockSpec((1, tm, d), lambda b, j: (b, j, 0)),
                  tok(0), tok(0), tok(0), tok(1), tok(2),
                  pl.BlockSpec((1, tm, LORA_W), lambda b, j: (b, j + off, lora_blk)),
                  pl.BlockSpec((1, tm, D_S5), lambda b, j: (b, j + off, u_blk)),
                  pl.BlockSpec((1, D_S5 // 128, tm, 128), lambda b, j: (b, 0, j, 0)),
                  pl.BlockSpec((1, 8, d), lambda b, j: (b, 0, 0)),
                  full(seg), full(a0), full(a2p), full(g2p), full(kaw), full(rk), full(lnw), full(lnb),
                  full(dsk), full(gluw), full(glub), full(wo)],
        out_specs=pl.BlockSpec((1, tm, d), lambda b, j: (b, j, 0)),
        out_shape=jax.ShapeDtypeStruct((bsz, seq, d), F32),
        compiler_params=_cparams(("parallel", "parallel"), 48),
    )(x, y_f, y_b, rkvc, rkvc, rkvc, z, z, ys5, mod8, seg, a0, a2p, g2p, kaw, rk, lnw, lnb,
      dsk, gluw, glub, wo)


def _ffn_kernel(x_ref, mod_ref, g2_ref, w1_ref, w3_ref, w2_ref, gf_ref, o_ref):
    x = x_ref[0]
    ms = jnp.mean(x * x, axis=-1, keepdims=True)
    h = x * lax.rsqrt(ms + NORM_EPS) * g2_ref[...]
    h = h * (1.0 + mod_ref[0, 4:5, :]) + mod_ref[0, 3:4, :]
    hb = h.astype(BF16)
    h1 = _dot(hb, w1_ref[...], NN)
    h3 = _dot(hb, w3_ref[...], NN)
    act = (h1 * _sigmoid(h1)) * h3
    f = _dot(act.astype(BF16), w2_ref[...], NN)
    out = x + mod_ref[0, 5:6, :] * f
    ms2 = jnp.mean(out * out, axis=-1, keepdims=True)
    o_ref[0] = out * lax.rsqrt(ms2 + NORM_EPS) * gf_ref[...]


def _ffn(x1, mod8, g2, w1, w3, w2, gf, tm):
    bsz, seq, d = x1.shape
    dff = w1.shape[1]
    const = lambda shape: pl.BlockSpec(shape, lambda b, j: (0, 0), pipeline_mode=pl.Buffered(1))
    return pl.pallas_call(
        _ffn_kernel,
        grid=(bsz, seq // tm),
        in_specs=[pl.BlockSpec((1, tm, d), lambda b, j: (b, j, 0)),
                  pl.BlockSpec((1, 8, d), lambda b, j: (b, 0, 0)),
                  const((1, d)), const((d, dff)), const((d, dff)), const((dff, d)), const((1, d))],
        out_specs=pl.BlockSpec((1, tm, d), lambda b, j: (b, j, 0)),
        out_shape=jax.ShapeDtypeStruct((bsz, seq, d), F32),
        compiler_params=_cparams(("parallel", "parallel"), 56),
    )(x1, mod8, g2.reshape(1, d), w1, w3, w2, gf.reshape(1, d))


def _pad_rows(w, row0, rows_total):
    pad = [(0, 0)] * (w.ndim - 2) + [(row0, rows_total - row0 - w.shape[-2]), (0, 0)]
    return jnp.pad(w, pad)


def kernel(x, c, ctx, c_ctx, mod_w, mod_b, norm1_g, norm2_g, w_in, w_out, rwkv_conv, rwkv_w0, rwkv_w2, rwkv_a0, rwkv_a2, rwkv_g2, rwkv_kk, rwkv_ka, rwkv_rk, rwkv_ln_w, rwkv_ln_b, s5_lam_re, s5_lam_im, s5_log_step, s5_b_re, s5_b_im, s5_c_re, s5_c_im, s5_d, s5_glu_w, s5_glu_b, ffn_w1, ffn_w3, ffn_w2, final_g):
    bsz, seq, d = x.shape
    ctx_len = ctx.shape[1]
    ltot = ctx_len + seq
    layer = 0

    cc = jnp.concatenate([c, c_ctx[None, :], jnp.zeros((8 - bsz - 1, d), F32)], axis=0)
    mod = _adaln(cc, mod_w[layer], mod_b[layer])
    mod8 = jnp.pad(mod[:bsz + 1].reshape(bsz + 1, N_MOD, d), ((0, 0), (0, 8 - N_MOD), (0, 0)))

    wi = w_in[layer]
    w_z = jnp.concatenate([wi[:, :3 * D_RWKV], wi[:, -D_S5:], wi[:, 3 * D_RWKV:-D_S5],
                           jnp.zeros((d, D_Z - wi.shape[1]), F32)], axis=1).astype(BF16)
    z = _inproj(ctx, x, mod8, norm1_g[layer], w_z, 256)

    rkvc = _conv(z, rwkv_conv[layer], ctx_len)

    w2p = _pad_rows(rwkv_w2[layer], 0, LORA_W).at[1].set(_pad_rows(rwkv_w2[layer, 1], DECAY_LORA, LORA_W))
    a2p = jnp.stack([_pad_rows(rwkv_a2[layer, 0], LORA_AD, LORA_W),
                     _pad_rows(rwkv_a2[layer, 1], LORA_AD + AAA_LORA, LORA_W)])
    w0 = rwkv_w0[layer].reshape(2, 1, D_RWKV)
    a0 = rwkv_a0[layer].reshape(2, 1, D_RWKV)
    kkw = rwkv_kk[layer].reshape(1, D_RWKV)
    kaw = rwkv_ka[layer].reshape(1, D_RWKV)
    y_f = _rwkv(rkvc, z, w0, w2p, a0, a2p, kkw, kaw, ctx_len, False)
    y_b = _rwkv(rkvc, z, w0, w2p, a0, a2p, kkw, kaw, ctx_len, True)

    consts = _s5_consts(s5_lam_re[layer], s5_lam_im[layer], s5_log_step[layer], s5_b_re[layer],
                        s5_b_im[layer], s5_c_re[layer], s5_c_im[layer])
    ys5 = _s5(z, consts, ctx_len)

    hh = jnp.arange(D_RWKV) // HEAD
    seg = (hh[:, None] == hh[None, :]).astype(BF16)
    g2p = _pad_rows(rwkv_g2[layer], LORA_GD, LORA_W)
    x1 = _mix(x, y_f, y_b, rkvc, z, ys5, mod8, seg, a0, a2p, g2p, kaw,
              rwkv_rk[layer].reshape(1, D_RWKV), rwkv_ln_w[layer].reshape(1, D_RWKV),
              rwkv_ln_b[layer].reshape(1, D_RWKV), s5_d[layer].reshape(1, D_S5),
              s5_glu_w[layer].astype(BF16), s5_glu_b[layer].reshape(1, D_S5),
              w_out[layer].astype(BF16), ctx_len, 256)

    return _ffn(x1, mod8, norm2_g[layer], ffn_w1[layer].astype(BF16), ffn_w3[layer].astype(BF16),
                ffn_w2[layer].astype(BF16), final_g, 512)
```

```python
import functools
import math

import jax
import jax.numpy as jnp
from jax import lax
from jax.experimental import pallas as pl
from jax.experimental.pallas import tpu as pltpu

F32 = jnp.float32
BF16 = jnp.bfloat16

D_MODEL = 1024
GRID_W = 64
D_RWKV = 512
HEAD = 64
HEADS = D_RWKV // HEAD
D_S5 = 512
S5_GROUP = 16
S5_GROUPS = D_S5 // S5_GROUP
S5_STATE = 64
DECAY_LORA = 32
AAA_LORA = 32
GATE_LORA = 96
N_MOD = 6
NORM_EPS = 1e-6
RWKV_LN_EPS = 64e-5

LORA_W = 256
COL_U = 3 * D_RWKV
COL_LORA = COL_U + D_S5
D_Z = COL_LORA + LORA_W
LORA_AD = 2 * DECAY_LORA
LORA_GD = LORA_AD + 2 * AAA_LORA

CHUNK = 64
RWKV_BLK = 256
S5_T = 16
S5_GH = 16
CONV_PAD = 72
CONV_TC = 256
CONV_PIECE = 128

NN = (((1,), (0,)), ((), ()))
NT = (((1,), (1,)), ((), ()))
TN = (((0,), (0,)), ((), ()))


def _dot(a, b, dims):
    return lax.dot_general(a, b, dims, preferred_element_type=F32)


def _split2(x):
    hi = x.astype(BF16)
    lo = (x - hi.astype(F32)).astype(BF16)
    return hi, lo


def _split3(x):
    hi = x.astype(BF16)
    r1 = x - hi.astype(F32)
    mid = r1.astype(BF16)
    lo = (r1 - mid.astype(F32)).astype(BF16)
    return hi, mid, lo


def _mm(a, b, mode="bf16", dims=NN):
    if mode == "bf16":
        return _dot(a.astype(BF16), b.astype(BF16), dims)
    if mode == "x3":
        ah, al = _split2(a)
        bh, bl = _split2(b)
        return _dot(ah, bh, dims) + (_dot(ah, bl, dims) + _dot(al, bh, dims))
    if mode == "xa":
        bb = b.astype(BF16)
        a1, a2, a3 = _split3(a)
        return _dot(a1, bb, dims) + (_dot(a2, bb, dims) + _dot(a3, bb, dims))
    if mode == "xa2":
        bb = b.astype(BF16)
        a1, a2 = _split2(a)
        return _dot(a1, bb, dims) + _dot(a2, bb, dims)
    if mode == "xb":
        ab = a.astype(BF16)
        b1, b2, b3 = _split3(b)
        return _dot(ab, b1, dims) + (_dot(ab, b2, dims) + _dot(ab, b3, dims))
    raise ValueError(mode)


def _sigmoid(x):
    return 1.0 / (1.0 + jnp.exp(-x))


def _softplus(x):
    return jnp.maximum(x, 0.0) + jnp.log(1.0 + jnp.exp(-jnp.abs(x)))


def _cparams(sem, vmem_mb):
    return pltpu.CompilerParams(dimension_semantics=sem, vmem_limit_bytes=vmem_mb * 1024 * 1024)


def _adaln_kernel(c_ref, w_ref, b_ref, o_ref):
    c = c_ref[...]
    s = c * _sigmoid(c)
    o_ref[...] = _mm(s, w_ref[...], "x3") + b_ref[...]


def _adaln(cc, mod_w, mod_b):
    rows, d = cc.shape
    n = mod_w.shape[1]
    tn = 1024
    return pl.pallas_call(
        _adaln_kernel,
        grid=(n // tn,),
        in_specs=[pl.BlockSpec((rows, d), lambda j: (0, 0)),
                  pl.BlockSpec((d, tn), lambda j: (0, j)),
                  pl.BlockSpec((1, tn), lambda j: (0, j))],
        out_specs=pl.BlockSpec((rows, tn), lambda j: (0, j)),
        out_shape=jax.ShapeDtypeStruct((rows, n), F32),
        compiler_params=_cparams(("parallel",), 40),
    )(cc, mod_w, mod_b.reshape(1, n))


def _inproj_kernel(c_ref, x_ref, mod_ref, g_ref, w_ref, o_ref, *, ctx_blocks):
    x = jnp.where(pl.program_id(1) < ctx_blocks, c_ref[0], x_ref[0])
    ms = jnp.mean(x * x, axis=-1, keepdims=True)
    xn = x * lax.rsqrt(ms + NORM_EPS) * g_ref[...]
    shift = mod_ref[0, 0:1, :]
    scale = mod_ref[0, 1:2, :]
    h = xn * (1.0 + scale) + shift
    o_ref[0] = _mm(h, w_ref[...], "bf16")


def _inproj(ctx, x, mod8, g1, w_z, tm):
    bsz, seq, d = x.shape
    ctx_blocks = ctx.shape[1] // tm
    dz = w_z.shape[1]
    kern = functools.partial(_inproj_kernel, ctx_blocks=ctx_blocks)
    return pl.pallas_call(
        kern,
        grid=(bsz, ctx_blocks + seq // tm),
        in_specs=[pl.BlockSpec((1, tm, d), lambda b, j: (b, jnp.minimum(j, ctx_blocks - 1), 0)),
                  pl.BlockSpec((1, tm, d), lambda b, j: (b, jnp.maximum(j - ctx_blocks, 0), 0)),
                  pl.BlockSpec((1, 8, d), lambda b, j: (jnp.where(j < ctx_blocks, bsz, b), 0, 0)),
                  pl.BlockSpec((1, d), lambda b, j: (0, 0)),
                  pl.BlockSpec((d, dz), lambda b, j: (0, 0))],
        out_specs=pl.BlockSpec((1, tm, dz), lambda b, j: (b, j, 0)),
        out_shape=jax.ShapeDtypeStruct((bsz, ctx.shape[1] + seq, dz), F32),
        compiler_params=_cparams(("parallel", "parallel"), 48),
    )(ctx, x, mod8, g1.reshape(1, d), w_z)


def _conv_kernel(z_ref, w_ref, o_ref, xs_ref, *, ctx_len, lat_len):
    tc = z_ref.shape[2]
    pc = CONV_PIECE
    ctx_off = CONV_PAD
    lat_off = 2 * CONV_PAD + ctx_len
    zpad = jnp.zeros((CONV_PAD, tc), F32)
    for s in range(3):
        xs_ref[s, 0:CONV_PAD, :] = zpad
        xs_ref[s, ctx_off + ctx_len:lat_off, :] = zpad
        xs_ref[s, lat_off + lat_len:lat_off + lat_len + CONV_PAD, :] = zpad
    xs_ref[1, lat_off:lat_off + 8, :] = zpad[0:8]
    xs_ref[2, lat_off + lat_len - 8:lat_off + lat_len, :] = zpad[0:8]
    col = lax.broadcasted_iota(jnp.int32, (pc, tc), 0) % GRID_W
    keep_l = (col != GRID_W - 1).astype(F32)
    keep_r = (col != 0).astype(F32)
    for p in range(ctx_len // pc):
        xs_ref[0, ctx_off + p * pc:ctx_off + (p + 1) * pc, :] = z_ref[0, p * pc:(p + 1) * pc, :]
    for p in range(lat_len // pc):
        x = z_ref[0, ctx_len + p * pc:ctx_len + (p + 1) * pc, :]
        dst = lat_off + p * pc
        xs_ref[0, dst:dst + pc, :] = x
        xs_ref[1, dst + 1:dst + 1 + pc, :] = x * keep_l
        xs_ref[2, dst - 1:dst - 1 + pc, :] = x * keep_r
    w = [w_ref[t:t + 1, :] for t in range(9)]
    for p in range(ctx_len // pc):
        base = ctx_off + p * pc
        acc = w[3] * xs_ref[0, base - 1:base - 1 + pc, :]
        acc = acc + w[4] * xs_ref[0, base:base + pc, :]
        acc = acc + w[5] * xs_ref[0, base + 1:base + 1 + pc, :]
        o_ref[0, p * pc:(p + 1) * pc, :] = acc
    for p in range(lat_len // pc):
        base = lat_off + p * pc
        acc = None
        for di in range(3):
            for dj in range(3):
                off = base + (di - 1) * GRID_W
                src = (1, 0, 2)[dj]
                term = w[3 * di + dj] * xs_ref[src, off:off + pc, :]
                acc = term if acc is None else acc + term
        o_ref[0, ctx_len + p * pc:ctx_len + (p + 1) * pc, :] = acc


def _conv(z, conv_w, ctx_len):
    bsz, ltot, _ = z.shape
    lat_len = ltot - ctx_len
    ch = 3 * D_RWKV
    rows = 3 * CONV_PAD + ltot
    kern = functools.partial(_conv_kernel, ctx_len=ctx_len, lat_len=lat_len)
    return pl.pallas_call(
        kern,
        grid=(bsz, ch // CONV_TC),
        in_specs=[pl.BlockSpec((1, ltot, CONV_TC), lambda b, j: (b, 0, j)),
                  pl.BlockSpec((9, CONV_TC), lambda b, j: (0, j))],
        out_specs=pl.BlockSpec((1, ltot, CONV_TC), lambda b, j: (b, 0, j)),
        out_shape=jax.ShapeDtypeStruct((bsz, ltot, ch), F32),
        scratch_shapes=[pltpu.VMEM((3, rows, CONV_TC), F32)],
        compiler_params=_cparams(("parallel", "parallel"), 48),
    )(z, conv_w.reshape(9, ch))


def _bd(x, lo):
    return jnp.concatenate([jnp.where(lo, x, 0.0), jnp.where(lo, 0.0, x)], axis=0)


def _tri_inv_pairs(n_mats, size, lo):
    ii = lax.broadcasted_iota(jnp.int32, (size, 2 * size), 0)
    jj = lax.broadcasted_iota(jnp.int32, (size, 2 * size), 1) & (size - 1)
    eye = (ii == jj).astype(F32)
    in2 = (ii >> 1) == (jj >> 1)
    ts = [eye + jnp.where(in2, n, 0.0) for n in n_mats]
    lg = 1
    while (1 << lg) < size:
        same = (ii >> (lg + 1)) == (jj >> (lg + 1))
        diff = (ii >> lg) != (jj >> lg)
        nls = [jnp.where(same, jnp.where(diff, n, 0.0), 0.0) for n in n_mats]
        tmp = [_mm(t, _bd(nl, lo)) for t, nl in zip(ts, nls)]
        ts = [t + _mm(x, _bd(t, lo)) for x, t in zip(tmp, ts)]
        lg += 1
    return ts


def _rwkv_kernel(r_ref, k_ref, v_ref, lo_ref, w0_ref, w2_ref, a0_ref, a2_ref, kkw_ref, kaw_ref,
                 y_ref, h_ref, *, reverse):
    step = pl.program_id(1)

    @pl.when(step == 0)
    def _():
        h_ref[...] = jnp.zeros_like(h_ref)

    r = r_ref[0]
    k = k_ref[0]
    v = v_ref[0]
    lora = lo_ref[0]
    n = r.shape[0]
    c = CHUNK
    nsub = n // c
    lgc = c.bit_length() - 1
    pw = 2 * HEAD
    npair = HEADS // 2

    logit_w = w0_ref[0] + _mm(jnp.tanh(lora), w2_ref[0])
    logw = -_softplus(-logit_w) - 0.5
    lw = -jnp.exp(logw)
    a = _sigmoid(a0_ref[0] + _mm(lora, a2_ref[0]))
    kd = k * (1.0 + (a - 1.0) * kaw_ref[...])
    kk = k * kkw_ref[...]

    ti = lax.broadcasted_iota(jnp.int32, (n, n), 0)
    tj = lax.broadcasted_iota(jnp.int32, (n, n), 1)
    causal = (tj >= ti) if reverse else (tj <= ti)
    tri = jnp.where((ti >> lgc) == (tj >> lgc), jnp.where(causal, 1.0, 0.0), 0.0)
    g = _mm(tri, lw, "xb")

    ci2 = lax.broadcasted_iota(jnp.int32, (c, pw), 0)
    lane2 = lax.broadcasted_iota(jnp.int32, (c, pw), 1)
    cj2 = lane2 & (c - 1)
    lo = lane2 < HEAD
    incl2 = (cj2 >= ci2) if reverse else (cj2 <= ci2)
    strict2 = (cj2 > ci2) if reverse else (cj2 < ci2)
    ri = lax.broadcasted_iota(jnp.int32, (pw, pw), 0)
    rj = lax.broadcasted_iota(jnp.int32, (pw, pw), 1)
    blk = (ri < HEAD) == (rj < HEAD)
    eye_p = (ri == rj).astype(F32)
    zeros_c = jnp.zeros((c, pw), F32)
    zeros_p = jnp.zeros((pw, pw), F32)

    def prep(j):
        rows = slice(j * c, (j + 1) * c)
        lw_j = lw[rows]
        g_j = g[rows]
        tot = jnp.sum(lw_j, axis=0, keepdims=True)
        e_prev = jnp.exp(g_j - lw_j)
        e_neg = jnp.exp(-g_j)
        e_pos = jnp.exp(g_j)
        e_tot = jnp.exp(tot)
        e_end = e_tot * e_neg
        out = []
        for p in range(npair):
            sl = slice(p * pw, (p + 1) * pw)
            kkp = kk[rows, sl]
            sq = kkp * kkp
            n2 = jnp.where(lo, jnp.sum(jnp.where(lo, sq, 0.0), axis=-1, keepdims=True),
                           jnp.sum(jnp.where(lo, 0.0, sq), axis=-1, keepdims=True))
            kkp = kkp * lax.rsqrt(jnp.maximum(n2, 1e-12))
            ka = kkp * a[rows, sl]
            out.append(dict(a_t=-kkp * e_prev[:, sl], b_t=ka * e_neg[:, sl], k_t=kd[rows, sl] * e_neg[:, sl],
                            r_t=r[rows, sl] * e_pos[:, sl], b_e=ka * e_end[:, sl],
                            k_e=kd[rows, sl] * e_end[:, sl], vh=v[rows, sl], g_end=e_tot[:, sl]))
        return out

    def local(us):
        s_bk = [_mm(jnp.concatenate([u["a_t"], u["r_t"]], axis=0),
                    jnp.concatenate([_bd(u["b_t"], lo), _bd(u["k_t"], lo)], axis=0), "bf16", NT)
                for u in us]
        nab = [jnp.where(strict2, s[:c, :pw], 0.0) for s in s_bk]
        aak = [jnp.where(strict2, s[:c, pw:], 0.0) for s in s_bk]
        arbk = [jnp.concatenate([jnp.where(incl2, s[c:, :pw], 0.0), jnp.where(incl2, s[c:, pw:], 0.0)],
                                axis=1) for s in s_bk]
        t_inv = _tri_inv_pairs(nab, c, lo)
        x1 = [_mm(m, _bd(u["vh"], lo)) for m, u in zip(aak, us)]
        uw = [_mm(t, jnp.concatenate([_bd(x, lo), _bd(u["a_t"], lo)], axis=1))
              for t, x, u in zip(t_inv, x1, us)]
        yq = [_mm(m, jnp.concatenate(
            [jnp.concatenate([_bd(w_[:, :pw], lo), _bd(w_[:, pw:], lo)], axis=1),
             jnp.concatenate([_bd(u["vh"], lo), zeros_p], axis=1)], axis=0))
            for m, w_, u in zip(arbk, uw, us)]
        gp = [_mm(jnp.concatenate([u["b_e"], u["k_e"]], axis=0),
                  jnp.concatenate([w_, jnp.concatenate([u["vh"], zeros_c], axis=1)], axis=0), "bf16", TN)
              for w_, u in zip(uw, us)]
        res = []
        for u, yq_, gp_ in zip(us, yq, gp):
            q = u["r_t"] + yq_[:, pw:]
            p_bd = jnp.where(blk, gp_[:, pw:], 0.0) + eye_p * u["g_end"]
            res.append((jnp.concatenate([q, p_bd], axis=0), yq_[:, :pw], jnp.where(blk, gp_[:, :pw], 0.0)))
        return res

    hs = [h_ref[p] for p in range(npair)]
    order = list(range(nsub - 1, -1, -1) if reverse else range(nsub))
    for grp in (order,):
        loc = local([u for j in grp for u in prep(j)])
        for gi, j in enumerate(grp):
            res = [_mm(loc[gi * npair + p][0], hs[p]) for p in range(npair)]
            for p in range(npair):
                y_ref[0, j * c:(j + 1) * c, p * pw:(p + 1) * pw] = res[p][:c] + loc[gi * npair + p][1]
                hs[p] = res[p][c:] + loc[gi * npair + p][2]
    for p in range(npair):
        h_ref[p] = hs[p]


def _rwkv(rkvc, z, w0, w2p, a0, a2p, kkw, kaw, ctx_len, reverse):
    bsz, ltot, _ = rkvc.shape
    nblk = ltot // RWKV_BLK
    nctx = ctx_len // RWKV_BLK
    lora_blk = COL_LORA // LORA_W
    d = 1 if reverse else 0

    def tok(i):
        return jnp.where(i < nctx, nctx - 1 - i, nblk + nctx - 1 - i) if reverse else i

    par3 = lambda b, i: (d, 0, 0)
    kern = functools.partial(_rwkv_kernel, reverse=reverse)
    return pl.pallas_call(
        kern,
        grid=(bsz, nblk),
        in_specs=[pl.BlockSpec((1, RWKV_BLK, D_RWKV), lambda b, i: (b, tok(i), 0)),
                  pl.BlockSpec((1, RWKV_BLK, D_RWKV), lambda b, i: (b, tok(i), 1)),
                  pl.BlockSpec((1, RWKV_BLK, D_RWKV), lambda b, i: (b, tok(i), 2)),
                  pl.BlockSpec((1, RWKV_BLK, LORA_W), lambda b, i: (b, tok(i), lora_blk)),
                  pl.BlockSpec((1, 1, D_RWKV), par3),
                  pl.BlockSpec((1, LORA_W, D_RWKV), par3),
                  pl.BlockSpec((1, 1, D_RWKV), par3),
                  pl.BlockSpec((1, LORA_W, D_RWKV), par3),
                  pl.BlockSpec((1, D_RWKV), lambda b, i: (0, 0)),
                  pl.BlockSpec((1, D_RWKV), lambda b, i: (0, 0))],
        out_specs=pl.BlockSpec((1, RWKV_BLK, D_RWKV), lambda b, i: (b, tok(i), 0)),
        out_shape=jax.ShapeDtypeStruct((bsz, ltot, D_RWKV), F32),
        scratch_shapes=[pltpu.VMEM((HEADS // 2, 2 * HEAD, 2 * HEAD), F32)],
        compiler_params=_cparams(("parallel", "arbitrary"), 48),
    )(rkvc, rkvc, rkvc, z, w0, w2p, a0, a2p, kkw, kaw)


def _s5_consts(lam_re, lam_im, log_step, b_re, b_im, c_re, c_im):
    t = S5_T
    g, p, hg = b_re.shape
    step = jnp.exp(log_step)[..., None]
    mag = jnp.exp(lam_re * step)
    lb_re = mag * jnp.cos(lam_im * step)
    lb_im = mag * jnp.sin(lam_im * step)
    den = lam_re * lam_re + lam_im * lam_im
    nr = lb_re - 1
    q_re = (nr * lam_re + lb_im * lam_im) / den
    q_im = (lb_im * lam_re - nr * lam_im) / den
    bb_re = q_re[..., None] * b_re - q_im[..., None] * b_im
    bb_im = q_re[..., None] * b_im + q_im[..., None] * b_re
    n = jnp.arange(t + 1, dtype=F32)
    pmag = jnp.exp((lam_re * step)[..., None] * n)
    pw_re = pmag * jnp.cos((lam_im * step)[..., None] * n)
    pw_im = pmag * jnp.sin((lam_im * step)[..., None] * n)
    ct_re = jnp.transpose(c_re, (0, 2, 1))[None, :, :, None, :]
    ct_im = jnp.transpose(c_im, (0, 2, 1))[None, :, :, None, :]
    cp_re = ct_re * pw_re[..., None] - ct_im * pw_im[..., None]
    cp_im = ct_re * pw_im[..., None] + ct_im * pw_re[..., None]
    flat = lambda x: x.reshape(x.shape[:-2] + (x.shape[-2] * x.shape[-1],))
    cf_re, cf_im = flat(cp_re), flat(cp_im)
    hp = lax.Precision.HIGHEST
    kt = (jnp.einsum("dgpi,dgpm->dgim", bb_re, cf_re[..., :t * hg], precision=hp)
          - jnp.einsum("dgpi,dgpm->dgim", bb_im, cf_im[..., :t * hg], precision=hp))
    kb_rev = flat(kt[1].reshape(g, hg, t, hg)[:, :, ::-1])
    zpad = jnp.zeros((g, hg, (t - 1) * hg), F32)
    lag = (jnp.concatenate([zpad, kt[0]], axis=-1)
           + jnp.concatenate([kb_rev, zpad], axis=-1))
    m_tot = jnp.stack([lag[:, :, (t - 1 - s) * hg:(2 * t - 1 - s) * hg] for s in range(t)], axis=1)
    m_tot = m_tot.reshape(g, t * hg, t * hg)

    pwt_re = jnp.transpose(pw_re, (0, 1, 3, 2))
    pwt_im = jnp.transpose(pw_im, (0, 1, 3, 2))
    bbt_re = jnp.transpose(bb_re, (0, 1, 3, 2))
    bbt_im = jnp.transpose(bb_im, (0, 1, 3, 2))

    def inject(d, pr_, pi_):
        pr_, pi_ = pr_[:, :, None, :], pi_[:, :, None, :]
        br, bi = bbt_re[d][:, None], bbt_im[d][:, None]
        return pr_ * br - pi_ * bi, pr_ * bi + pi_ * br

    pf_re, pf_im = inject(0, pwt_re[0, :, :t][:, ::-1], pwt_im[0, :, :t][:, ::-1])
    pb_re, pb_im = inject(1, pwt_re[1, :, :t], pwt_im[1, :, :t])
    inj = jnp.concatenate([pf_re, pb_re, pf_im, pb_im], axis=-1).reshape(g, t * hg, 4 * p)

    cr_re, cr_im = flat(cp_re[1][:, :, ::-1]), flat(cp_im[1][:, :, ::-1])
    out = jnp.concatenate([m_tot, cf_re[0][..., hg:], cr_re[..., :t * hg],
                           -cf_im[0][..., hg:], -cr_im[..., :t * hg]], axis=1)
    lb16 = jnp.stack([jnp.concatenate([pw_re[0, :, :, t], pw_re[1, :, :, t]], axis=-1),
                      jnp.concatenate([pw_im[0, :, :, t], pw_im[1, :, :, t]], axis=-1)], axis=0)
    return inj.astype(BF16), out.astype(BF16), lb16


def _block_transpose8(xs):
    lane_blk = lax.broadcasted_iota(jnp.int32, xs[0].shape, 1) >> 4
    xs = list(xs)
    for d in (4, 2, 1):
        hi = (lane_blk & d) != 0
        for i in range(8):
            if i & d:
                continue
            a, b = xs[i], xs[i + d]
            xs[i] = jnp.where(hi, pltpu.roll(b, 16 * d, 1), a)
            xs[i + d] = jnp.where(hi, b, pltpu.roll(a, 128 - 16 * d, 1))
    return xs


def _s5_kernel(u0_ref, u1_ref, inj_ref, out_ref, l_ref, y_ref, ug_ref, zre_ref, zim_ref, hre_ref,
               him_ref, *, nchunk, nctx):
    gh = ug_ref.shape[0]
    ps = S5_STATE
    t = S5_T
    nlat = nchunk - nctx
    for jt, u_ref in enumerate((u0_ref, u1_ref)):
        for q in range(t // 8):
            cols = [u_ref[0, pl.ds(8 * q + tt, nchunk, stride=t), :] for tt in range(8)]
            grp = _block_transpose8(cols)
            for gl in range(8):
                ug_ref[8 * jt + gl, :, 128 * q:128 * (q + 1)] = grp[gl].astype(BF16)
    for gi in range(gh):
        zg = _dot(ug_ref[gi], inj_ref[gi], NN)
        zre_ref[pl.ds(gi, nchunk, stride=gh), :] = zg[:, 0:2 * ps]
        zim_ref[pl.ds(gi, nchunk, stride=gh), :] = zg[:, 2 * ps:4 * ps]

    lane = lax.broadcasted_iota(jnp.int32, (gh, 2 * ps), 1)
    is_f = lane < ps
    lre = l_ref[0, 0]
    lim = l_ref[0, 1]

    def body(i, carry):
        hre, him = carry
        cb = jnp.where(i < nctx, nctx - 1 - i, nchunk + nctx - 1 - i)
        rf = pl.multiple_of(i * gh, gh)
        rb = pl.multiple_of(cb * gh, gh)
        hre_ref[pl.ds(rf, gh), 0:ps] = hre[:, 0:ps]
        hre_ref[pl.ds(rb, gh), ps:2 * ps] = hre[:, ps:2 * ps]
        him_ref[pl.ds(rf, gh), 0:ps] = him[:, 0:ps]
        him_ref[pl.ds(rb, gh), ps:2 * ps] = him[:, ps:2 * ps]
        zre = jnp.where(is_f, zre_ref[pl.ds(rf, gh), :], zre_ref[pl.ds(rb, gh), :])
        zim = jnp.where(is_f, zim_ref[pl.ds(rf, gh), :], zim_ref[pl.ds(rb, gh), :])
        return lre * hre - lim * him + zre, lre * him + lim * hre + zim

    zero = jnp.zeros((gh, 2 * ps), F32)
    lax.fori_loop(0, nchunk, body, (zero, zero))

    for jt in range(gh // 8):
        ys = []
        for gl in range(8):
            gi = 8 * jt + gl
            hre = hre_ref[pl.ds(gi, nchunk, stride=gh), :].astype(BF16)
            him = him_ref[pl.ds(gi, nchunk, stride=gh), :].astype(BF16)
            ys.append(_dot(jnp.concatenate([ug_ref[gi], hre, him], axis=1), out_ref[gi], NN))
        for q in range(t // 8):
            tok = _block_transpose8([y[:, 128 * q:128 * (q + 1)] for y in ys])
            for tt in range(8):
                y_ref[0, jt, pl.ds(8 * q + tt, nlat, stride=t), :] = tok[tt][nctx:]


def _s5(z, consts, ctx_len):
    inj, out, lb16 = consts
    bsz, ltot, _ = z.shape
    nchunk = ltot // S5_T
    nctx = ctx_len // S5_T
    gh = S5_GH
    halves = S5_GROUPS // gh
    w = S5_T * S5_GROUP
    tiles = gh * S5_GROUP // 128
    ps2 = 2 * S5_STATE
    lb = lb16.reshape(2, halves, gh, ps2).transpose(1, 0, 2, 3)
    kern = functools.partial(_s5_kernel, nchunk=nchunk, nctx=nctx)
    u_spec = lambda k: pl.BlockSpec((1, ltot, 128), lambda b, j: (b, 0, COL_U // 128 + tiles * j + k))
    return pl.pallas_call(
        kern,
        grid=(bsz, halves),
        in_specs=[u_spec(0), u_spec(1),
                  pl.BlockSpec((gh, w, 2 * ps2), lambda b, j: (j, 0, 0)),
                  pl.BlockSpec((gh, w + 2 * ps2, w), lambda b, j: (j, 0, 0)),
                  pl.BlockSpec((1, 2, gh, ps2), lambda b, j: (j, 0, 0, 0))],
        out_specs=pl.BlockSpec((1, tiles, ltot - ctx_len, 128), lambda b, j: (b, j, 0, 0)),
        out_shape=jax.ShapeDtypeStruct((bsz, D_S5 // 128, ltot - ctx_len, 128), F32),
        scratch_shapes=[pltpu.VMEM((gh, nchunk, w), BF16),
                        pltpu.VMEM((nchunk * gh, ps2), F32),
                        pltpu.VMEM((nchunk * gh, ps2), F32),
                        pltpu.VMEM((nchunk * gh, ps2), F32),
                        pltpu.VMEM((nchunk * gh, ps2), F32)],
        compiler_params=_cparams(("parallel", "parallel"), 48),
    )(z, z, inj, out, lb)


def _mix_kernel(x_ref, yf_ref, yb_ref, r_ref, k_ref, v_ref, lo_ref, u_ref, ys_ref, mod_ref,
                seg_ref, a0_ref, a2_ref, g2_ref, kaw_ref, rk_ref, lnw_ref, lnb_ref,
                dsk_ref, gluw_ref, glub_ref, wo_ref, o_ref):
    lo = lo_ref[0]
    seg = seg_ref[...]

    def head_sum(t):
        pw = seg.shape[0]
        return jnp.concatenate([_mm(t[:, i:i + pw], seg, "xa2") for i in range(0, t.shape[1], pw)], axis=1)

    y = yf_ref[0] + yb_ref[0]
    mu = head_sum(y) * (1.0 / HEAD)
    yc = y - mu
    var = head_sum(yc * yc) * (1.0 / HEAD)
    yn = yc * lax.rsqrt(var + RWKV_LN_EPS) * lnw_ref[...] + lnb_ref[...]
    a_f = _sigmoid(a0_ref[0] + _mm(lo, a2_ref[0]))
    a_b = _sigmoid(a0_ref[1] + _mm(lo, a2_ref[1]))
    kaw = kaw_ref[...]
    kd_sum = k_ref[0] * ((1.0 + (a_f - 1.0) * kaw) + (1.0 + (a_b - 1.0) * kaw))
    bonus = head_sum(r_ref[0] * kd_sum * rk_ref[...])
    yn = yn + bonus * v_ref[0]
    gate = _mm(_sigmoid(lo), g2_ref[...])
    rwkv_out = yn * gate

    u = u_ref[0]
    ys = jnp.concatenate([ys_ref[0, i] for i in range(ys_ref.shape[1])], axis=1) + dsk_ref[...] * u
    zg = 0.5 * ys * (1.0 + jnp.tanh(math.sqrt(2.0 / math.pi) * (ys + 0.044715 * (ys * ys * ys))))
    s5_out = zg * _sigmoid(_mm(zg, gluw_ref[...]) + glub_ref[...])

    mix = _mm(rwkv_out, wo_ref[0:D_RWKV, :]) + _mm(s5_out, wo_ref[D_RWKV:, :])
    o_ref[0] = x_ref[0] + mod_ref[0, 2:3, :] * mix


def _mix(x, y_f, y_b, rkvc, z, ys5, mod8, seg, a0, a2p, g2p, kaw, rk, lnw, lnb, dsk, gluw, glub, wo,
         ctx_len, tm):
    bsz, seq, d = x.shape
    assert ctx_len % tm == 0 and seq % tm == 0
    off = ctx_len // tm
    lora_blk = COL_LORA // LORA_W
    u_blk = COL_U // D_S5
    tok = lambda c: pl.BlockSpec((1, tm, D_RWKV), lambda b, j: (b, j + off, c))
    full = lambda a: pl.BlockSpec(a.shape, lambda b, j: (0,) * a.ndim)
    return pl.pallas_call(
        _mix_kernel,
        grid=(bsz, seq // tm),
        in_specs=[pl.BlockSpec((1, tm, d), lambda b, j: (b, j, 0)),
                  tok(0), tok(0), tok(0), tok(1), tok(2),
                  pl.BlockSpec((1, tm, LORA_W), lambda b, j: (b, j + off, lora_blk)),
                  pl.BlockSpec((1, tm, D_S5), lambda b, j: (b, j + off, u_blk)),
                  pl.BlockSpec((1, D_S5 // 128, tm, 128), lambda b, j: (b, 0, j, 0)),
                  pl.BlockSpec((1, 8, d), lambda b, j: (b, 0, 0)),
                  full(seg), full(a0), full(a2p), full(g2p), full(kaw), full(rk), full(lnw), full(lnb),
                  full(dsk), full(gluw), full(glub), full(wo)],
        out_specs=pl.BlockSpec((1, tm, d), lambda b, j: (b, j, 0)),
        out_shape=jax.ShapeDtypeStruct((bsz, seq, d), F32),
        compiler_params=_cparams(("parallel", "parallel"), 48),
    )(x, y_f, y_b, rkvc, rkvc, rkvc, z, z, ys5, mod8, seg, a0, a2p, g2p, kaw, rk, lnw, lnb,
      dsk, gluw, glub, wo)


def _ffn_kernel(x_ref, mod_ref, g2_ref, w1_ref, w3_ref, w2_ref, gf_ref, o_ref):
    x = x_ref[0]
    ms = jnp.mean(x * x, axis=-1, keepdims=True)
    h = x * lax.rsqrt(ms + NORM_EPS) * g2_ref[...]
    h = h * (1.0 + mod_ref[0, 4:5, :]) + mod_ref[0, 3:4, :]
    hb = h.astype(BF16)
    h1 = _dot(hb, w1_ref[...], NN)
    h3 = _dot(hb, w3_ref[...], NN)
    act = (h1 * _sigmoid(h1)) * h3
    f = _dot(act.astype(BF16), w2_ref[...], NN)
    out = x + mod_ref[0, 5:6, :] * f
    ms2 = jnp.mean(out * out, axis=-1, keepdims=True)
    o_ref[0] = out * lax.rsqrt(ms2 + NORM_EPS) * gf_ref[...]


def _ffn(x1, mod8, g2, w1, w3, w2, gf, tm):
    bsz, seq, d = x1.shape
    dff = w1.shape[1]
    const = lambda shape: pl.BlockSpec(shape, lambda b, j: (0, 0), pipeline_mode=pl.Buffered(1))
    return pl.pallas_call(
        _ffn_kernel,
        grid=(bsz, seq // tm),
        in_specs=[pl.BlockSpec((1, tm, d), lambda b, j: (b, j, 0)),
                  pl.BlockSpec((1, 8, d), lambda b, j: (b, 0, 0)),
                  const((1, d)), const((d, dff)), const((d, dff)), const((dff, d)), const((1, d))],
        out_specs=pl.BlockSpec((1, tm, d), lambda b, j: (b, j, 0)),
        out_shape=jax.ShapeDtypeStruct((bsz, seq, d), F32),
        compiler_params=_cparams(("parallel", "parallel"), 56),
    )(x1, mod8, g2.reshape(1, d), w1, w3, w2, gf.reshape(1, d))


def _pad_rows(w, row0, rows_total):
    pad = [(0, 0)] * (w.ndim - 2) + [(row0, rows_total - row0 - w.shape[-2]), (0, 0)]
    return jnp.pad(w, pad)


def kernel(x, c, ctx, c_ctx, mod_w, mod_b, norm1_g, norm2_g, w_in, w_out, rwkv_conv, rwkv_w0, rwkv_w2, rwkv_a0, rwkv_a2, rwkv_g2, rwkv_kk, rwkv_ka, rwkv_rk, rwkv_ln_w, rwkv_ln_b, s5_lam_re, s5_lam_im, s5_log_step, s5_b_re, s5_b_im, s5_c_re, s5_c_im, s5_d, s5_glu_w, s5_glu_b, ffn_w1, ffn_w3, ffn_w2, final_g):
    bsz, seq, d = x.shape
    ctx_len = ctx.shape[1]
    ltot = ctx_len + seq
    layer = 0

    cc = jnp.concatenate([c, c_ctx[None, :], jnp.zeros((8 - bsz - 1, d), F32)], axis=0)
    mod = _adaln(cc, mod_w[layer], mod_b[layer])
    mod8 = jnp.pad(mod[:bsz + 1].reshape(bsz + 1, N_MOD, d), ((0, 0), (0, 8 - N_MOD), (0, 0)))

    wi = w_in[layer]
    w_z = jnp.concatenate([wi[:, :3 * D_RWKV], wi[:, -D_S5:], wi[:, 3 * D_RWKV:-D_S5],
                           jnp.zeros((d, D_Z - wi.shape[1]), F32)], axis=1).astype(BF16)
    z = _inproj(ctx, x, mod8, norm1_g[layer], w_z, 256)

    rkvc = _conv(z, rwkv_conv[layer], ctx_len)

    w2p = _pad_rows(rwkv_w2[layer], 0, LORA_W).at[1].set(_pad_rows(rwkv_w2[layer, 1], DECAY_LORA, LORA_W))
    a2p = jnp.stack([_pad_rows(rwkv_a2[layer, 0], LORA_AD, LORA_W),
                     _pad_rows(rwkv_a2[layer, 1], LORA_AD + AAA_LORA, LORA_W)])
    w0 = rwkv_w0[layer].reshape(2, 1, D_RWKV)
    a0 = rwkv_a0[layer].reshape(2, 1, D_RWKV)
    kkw = rwkv_kk[layer].reshape(1, D_RWKV)
    kaw = rwkv_ka[layer].reshape(1, D_RWKV)
    y_f = _rwkv(rkvc, z, w0, w2p, a0, a2p, kkw, kaw, ctx_len, False)
    y_b = _rwkv(rkvc, z, w0, w2p, a0, a2p, kkw, kaw, ctx_len, True)

    consts = _s5_consts(s5_lam_re[layer], s5_lam_im[layer], s5_log_step[layer], s5_b_re[layer],
                        s5_b_im[layer], s5_c_re[layer], s5_c_im[layer])
    ys5 = _s5(z, consts, ctx_len)

    hh = jnp.arange(2 * HEAD) // HEAD
    seg = (hh[:, None] == hh[None, :]).astype(BF16)
    g2p = _pad_rows(rwkv_g2[layer], LORA_GD, LORA_W)
    x1 = _mix(x, y_f, y_b, rkvc, z, ys5, mod8, seg, a0, a2p, g2p, kaw,
              rwkv_rk[layer].reshape(1, D_RWKV), rwkv_ln_w[layer].reshape(1, D_RWKV),
              rwkv_ln_b[layer].reshape(1, D_RWKV), s5_d[layer].reshape(1, D_S5),
              s5_glu_w[layer].astype(BF16), s5_glu_b[layer].reshape(1, D_S5),
              w_out[layer].astype(BF16), ctx_len, 256)

    return _ffn(x1, mod8, norm2_g[layer], ffn_w1[layer].astype(BF16), ffn_w3[layer].astype(BF16),
                ffn_w2[layer].astype(BF16), final_g, 512)
```

```python
import functools
import math

import jax
import jax.numpy as jnp
from jax import lax
from jax.experimental import pallas as pl
from jax.experimental.pallas import tpu as pltpu

F32 = jnp.float32
BF16 = jnp.bfloat16

D_MODEL = 1024
GRID_W = 64
D_RWKV = 512
HEAD = 64
HEADS = D_RWKV // HEAD
D_S5 = 512
S5_GROUP = 16
S5_GROUPS = D_S5 // S5_GROUP
S5_STATE = 64
DECAY_LORA = 32
AAA_LORA = 32
GATE_LORA = 96
N_MOD = 6
NORM_EPS = 1e-6
RWKV_LN_EPS = 64e-5

LORA_W = 256
COL_U = 3 * D_RWKV
COL_LORA = COL_U + D_S5
D_Z = COL_LORA + LORA_W
LORA_AD = 2 * DECAY_LORA
LORA_GD = LORA_AD + 2 * AAA_LORA

CHUNK = 64
RWKV_BLK = 256
S5_T = 16
S5_GH = 16
CONV_PAD = 72
CONV_TC = 256
CONV_PIECE = 128

NN = (((1,), (0,)), ((), ()))
NT = (((1,), (1,)), ((), ()))
TN = (((0,), (0,)), ((), ()))


def _dot(a, b, dims):
    return lax.dot_general(a, b, dims, preferred_element_type=F32)


def _split2(x):
    hi = x.astype(BF16)
    lo = (x - hi.astype(F32)).astype(BF16)
    return hi, lo


def _split3(x):
    hi = x.astype(BF16)
    r1 = x - hi.astype(F32)
    mid = r1.astype(BF16)
    lo = (r1 - mid.astype(F32)).astype(BF16)
    return hi, mid, lo


def _mm(a, b, mode="bf16", dims=NN):
    if mode == "bf16":
        return _dot(a.astype(BF16), b.astype(BF16), dims)
    if mode == "x3":
        ah, al = _split2(a)
        bh, bl = _split2(b)
        return _dot(ah, bh, dims) + (_dot(ah, bl, dims) + _dot(al, bh, dims))
    if mode == "xa":
        bb = b.astype(BF16)
        a1, a2, a3 = _split3(a)
        return _dot(a1, bb, dims) + (_dot(a2, bb, dims) + _dot(a3, bb, dims))
    if mode == "xa2":
        bb = b.astype(BF16)
        a1, a2 = _split2(a)
        return _dot(a1, bb, dims) + _dot(a2, bb, dims)
    if mode == "xb":
        ab = a.astype(BF16)
        b1, b2, b3 = _split3(b)
        return _dot(ab, b1, dims) + (_dot(ab, b2, dims) + _dot(ab, b3, dims))
    raise ValueError(mode)


def _sigmoid(x):
    return 1.0 / (1.0 + jnp.exp(-x))


def _softplus(x):
    return jnp.maximum(x, 0.0) + jnp.log(1.0 + jnp.exp(-jnp.abs(x)))


def _cparams(sem, vmem_mb):
    return pltpu.CompilerParams(dimension_semantics=sem, vmem_limit_bytes=vmem_mb * 1024 * 1024)


def _adaln_kernel(c_ref, w_ref, b_ref, o_ref):
    c = c_ref[...]
    s = c * _sigmoid(c)
    o_ref[...] = _mm(s, w_ref[...], "x3") + b_ref[...]


def _adaln(cc, mod_w, mod_b):
    rows, d = cc.shape
    n = mod_w.shape[1]
    tn = 1024
    return pl.pallas_call(
        _adaln_kernel,
        grid=(n // tn,),
        in_specs=[pl.BlockSpec((rows, d), lambda j: (0, 0)),
                  pl.BlockSpec((d, tn), lambda j: (0, j)),
                  pl.BlockSpec((1, tn), lambda j: (0, j))],
        out_specs=pl.BlockSpec((rows, tn), lambda j: (0, j)),
        out_shape=jax.ShapeDtypeStruct((rows, n), F32),
        compiler_params=_cparams(("parallel",), 40),
    )(cc, mod_w, mod_b.reshape(1, n))


def _inproj_kernel(c_ref, x_ref, mod_ref, g_ref, w_ref, o_ref, *, ctx_blocks):
    x = jnp.where(pl.program_id(1) < ctx_blocks, c_ref[0], x_ref[0])
    ms = jnp.mean(x * x, axis=-1, keepdims=True)
    xn = x * lax.rsqrt(ms + NORM_EPS) * g_ref[...]
    shift = mod_ref[0, 0:1, :]
    scale = mod_ref[0, 1:2, :]
    h = xn * (1.0 + scale) + shift
    o_ref[0] = _mm(h, w_ref[...], "bf16")


def _inproj(ctx, x, mod8, g1, w_z, tm):
    bsz, seq, d = x.shape
    ctx_blocks = ctx.shape[1] // tm
    dz = w_z.shape[1]
    kern = functools.partial(_inproj_kernel, ctx_blocks=ctx_blocks)
    return pl.pallas_call(
        kern,
        grid=(bsz, ctx_blocks + seq // tm),
        in_specs=[pl.BlockSpec((1, tm, d), lambda b, j: (b, jnp.minimum(j, ctx_blocks - 1), 0)),
                  pl.BlockSpec((1, tm, d), lambda b, j: (b, jnp.maximum(j - ctx_blocks, 0), 0)),
                  pl.BlockSpec((1, 8, d), lambda b, j: (jnp.where(j < ctx_blocks, bsz, b), 0, 0)),
                  pl.BlockSpec((1, d), lambda b, j: (0, 0)),
                  pl.BlockSpec((d, dz), lambda b, j: (0, 0))],
        out_specs=pl.BlockSpec((1, tm, dz), lambda b, j: (b, j, 0)),
        out_shape=jax.ShapeDtypeStruct((bsz, ctx.shape[1] + seq, dz), F32),
        compiler_params=_cparams(("parallel", "parallel"), 48),
    )(ctx, x, mod8, g1.reshape(1, d), w_z)


def _conv_kernel(z_ref, w_ref, o_ref, xs_ref, *, ctx_len, lat_len):
    tc = z_ref.shape[2]
    pc = CONV_PIECE
    ctx_off = CONV_PAD
    lat_off = 2 * CONV_PAD + ctx_len
    zpad = jnp.zeros((CONV_PAD, tc), F32)
    for s in range(3):
        xs_ref[s, 0:CONV_PAD, :] = zpad
        xs_ref[s, ctx_off + ctx_len:lat_off, :] = zpad
        xs_ref[s, lat_off + lat_len:lat_off + lat_len + CONV_PAD, :] = zpad
    xs_ref[1, lat_off:lat_off + 8, :] = zpad[0:8]
    xs_ref[2, lat_off + lat_len - 8:lat_off + lat_len, :] = zpad[0:8]
    col = lax.broadcasted_iota(jnp.int32, (pc, tc), 0) % GRID_W
    keep_l = (col != GRID_W - 1).astype(F32)
    keep_r = (col != 0).astype(F32)
    for p in range(ctx_len // pc):
        xs_ref[0, ctx_off + p * pc:ctx_off + (p + 1) * pc, :] = z_ref[0, p * pc:(p + 1) * pc, :]
    for p in range(lat_len // pc):
        x = z_ref[0, ctx_len + p * pc:ctx_len + (p + 1) * pc, :]
        dst = lat_off + p * pc
        xs_ref[0, dst:dst + pc, :] = x
        xs_ref[1, dst + 1:dst + 1 + pc, :] = x * keep_l
        xs_ref[2, dst - 1:dst - 1 + pc, :] = x * keep_r
    w = [w_ref[t:t + 1, :] for t in range(9)]
    for p in range(ctx_len // pc):
        base = ctx_off + p * pc
        acc = w[3] * xs_ref[0, base - 1:base - 1 + pc, :]
        acc = acc + w[4] * xs_ref[0, base:base + pc, :]
        acc = acc + w[5] * xs_ref[0, base + 1:base + 1 + pc, :]
        o_ref[0, p * pc:(p + 1) * pc, :] = acc
    for p in range(lat_len // pc):
        base = lat_off + p * pc
        acc = None
        for di in range(3):
            for dj in range(3):
                off = base + (di - 1) * GRID_W
                src = (1, 0, 2)[dj]
                term = w[3 * di + dj] * xs_ref[src, off:off + pc, :]
                acc = term if acc is None else acc + term
        o_ref[0, ctx_len + p * pc:ctx_len + (p + 1) * pc, :] = acc


def _conv(z, conv_w, ctx_len):
    bsz, ltot, _ = z.shape
    lat_len = ltot - ctx_len
    ch = 3 * D_RWKV
    rows = 3 * CONV_PAD + ltot
    kern = functools.partial(_conv_kernel, ctx_len=ctx_len, lat_len=lat_len)
    return pl.pallas_call(
        kern,
        grid=(bsz, ch // CONV_TC),
        in_specs=[pl.BlockSpec((1, ltot, CONV_TC), lambda b, j: (b, 0, j)),
                  pl.BlockSpec((9, CONV_TC), lambda b, j: (0, j))],
        out_specs=pl.BlockSpec((1, ltot, CONV_TC), lambda b, j: (b, 0, j)),
        out_shape=jax.ShapeDtypeStruct((bsz, ltot, ch), F32),
        scratch_shapes=[pltpu.VMEM((3, rows, CONV_TC), F32)],
        compiler_params=_cparams(("parallel", "parallel"), 48),
    )(z, conv_w.reshape(9, ch))


def _bd(x, lo):
    return jnp.concatenate([jnp.where(lo, x, 0.0), jnp.where(lo, 0.0, x)], axis=0)


def _tri_inv_pairs(n_mats, size, lo):
    ii = lax.broadcasted_iota(jnp.int32, (size, 2 * size), 0)
    jj = lax.broadcasted_iota(jnp.int32, (size, 2 * size), 1) & (size - 1)
    eye = (ii == jj).astype(F32)
    in2 = (ii >> 1) == (jj >> 1)
    ts = [eye + jnp.where(in2, n, 0.0) for n in n_mats]
    lg = 1
    while (1 << lg) < size:
        same = (ii >> (lg + 1)) == (jj >> (lg + 1))
        diff = (ii >> lg) != (jj >> lg)
        nls = [jnp.where(same, jnp.where(diff, n, 0.0), 0.0) for n in n_mats]
        tmp = [_mm(t, _bd(nl, lo)) for t, nl in zip(ts, nls)]
        ts = [t + _mm(x, _bd(t, lo)) for x, t in zip(tmp, ts)]
        lg += 1
    return ts


def _rwkv_kernel(r_ref, k_ref, v_ref, lo_ref, w0_ref, w2_ref, a0_ref, a2_ref, kkw_ref, kaw_ref,
                 y_ref, h_ref, *, reverse):
    step = pl.program_id(1)

    @pl.when(step == 0)
    def _():
        h_ref[...] = jnp.zeros_like(h_ref)

    r = r_ref[0]
    k = k_ref[0]
    v = v_ref[0]
    lora = lo_ref[0]
    n = r.shape[0]
    c = CHUNK
    nsub = n // c
    lgc = c.bit_length() - 1
    pw = 2 * HEAD
    npair = HEADS // 2

    logit_w = w0_ref[0] + _mm(jnp.tanh(lora), w2_ref[0])
    logw = -_softplus(-logit_w) - 0.5
    lw = -jnp.exp(logw)
    a = _sigmoid(a0_ref[0] + _mm(lora, a2_ref[0]))
    kd = k * (1.0 + (a - 1.0) * kaw_ref[...])
    kk = k * kkw_ref[...]

    ti = lax.broadcasted_iota(jnp.int32, (n, n), 0)
    tj = lax.broadcasted_iota(jnp.int32, (n, n), 1)
    causal = (tj >= ti) if reverse else (tj <= ti)
    tri = jnp.where((ti >> lgc) == (tj >> lgc), jnp.where(causal, 1.0, 0.0), 0.0)
    g = _mm(tri, lw, "xb")

    ci2 = lax.broadcasted_iota(jnp.int32, (c, pw), 0)
    lane2 = lax.broadcasted_iota(jnp.int32, (c, pw), 1)
    cj2 = lane2 & (c - 1)
    lo = lane2 < HEAD
    incl2 = (cj2 >= ci2) if reverse else (cj2 <= ci2)
    strict2 = (cj2 > ci2) if reverse else (cj2 < ci2)
    ri = lax.broadcasted_iota(jnp.int32, (pw, pw), 0)
    rj = lax.broadcasted_iota(jnp.int32, (pw, pw), 1)
    blk = (ri < HEAD) == (rj < HEAD)
    eye_p = (ri == rj).astype(F32)
    zeros_c = jnp.zeros((c, pw), F32)
    zeros_p = jnp.zeros((pw, pw), F32)

    def prep(j):
        rows = slice(j * c, (j + 1) * c)
        lw_j = lw[rows]
        g_j = g[rows]
        tot = jnp.sum(lw_j, axis=0, keepdims=True)
        e_prev = jnp.exp(g_j - lw_j)
        e_neg = jnp.exp(-g_j)
        e_pos = jnp.exp(g_j)
        e_tot = jnp.exp(tot)
        e_end = e_tot * e_neg
        out = []
        for p in range(npair):
            sl = slice(p * pw, (p + 1) * pw)
            kkp = kk[rows, sl]
            sq = kkp * kkp
            n2 = jnp.where(lo, jnp.sum(jnp.where(lo, sq, 0.0), axis=-1, keepdims=True),
                           jnp.sum(jnp.where(lo, 0.0, sq), axis=-1, keepdims=True))
            kkp = kkp * lax.rsqrt(jnp.maximum(n2, 1e-12))
            ka = kkp * a[rows, sl]
            out.append(dict(a_t=-kkp * e_prev[:, sl], b_t=ka * e_neg[:, sl], k_t=kd[rows, sl] * e_neg[:, sl],
                            r_t=r[rows, sl] * e_pos[:, sl], b_e=ka * e_end[:, sl],
                            k_e=kd[rows, sl] * e_end[:, sl], vh=v[rows, sl], g_end=e_tot[:, sl]))
        return out

    def local(us):
        s_bk = [_mm(jnp.concatenate([u["a_t"], u["r_t"]], axis=0),
                    jnp.concatenate([_bd(u["b_t"], lo), _bd(u["k_t"], lo)], axis=0), "bf16", NT)
                for u in us]
        nab = [jnp.where(strict2, s[:c, :pw], 0.0) for s in s_bk]
        aak = [jnp.where(strict2, s[:c, pw:], 0.0) for s in s_bk]
        arbk = [jnp.concatenate([jnp.where(incl2, s[c:, :pw], 0.0), jnp.where(incl2, s[c:, pw:], 0.0)],
                                axis=1) for s in s_bk]
        t_inv = _tri_inv_pairs(nab, c, lo)
        x1 = [_mm(m, _bd(u["vh"], lo)) for m, u in zip(aak, us)]
        uw = [_mm(t, jnp.concatenate([_bd(x, lo), _bd(u["a_t"], lo)], axis=1))
              for t, x, u in zip(t_inv, x1, us)]
        yq = [_mm(m, jnp.concatenate(
            [jnp.concatenate([_bd(w_[:, :pw], lo), _bd(w_[:, pw:], lo)], axis=1),
             jnp.concatenate([_bd(u["vh"], lo), zeros_p], axis=1)], axis=0))
            for m, w_, u in zip(arbk, uw, us)]
        gp = [_mm(jnp.concatenate([u["b_e"], u["k_e"]], axis=0),
                  jnp.concatenate([w_, jnp.concatenate([u["vh"], zeros_c], axis=1)], axis=0), "bf16", TN)
              for w_, u in zip(uw, us)]
        res = []
        for u, yq_, gp_ in zip(us, yq, gp):
            q = u["r_t"] + yq_[:, pw:]
            p_bd = jnp.where(blk, gp_[:, pw:], 0.0) + eye_p * u["g_end"]
            res.append((jnp.concatenate([q, p_bd], axis=0), yq_[:, :pw], jnp.where(blk, gp_[:, :pw], 0.0)))
        return res

    hs = [h_ref[p] for p in range(npair)]
    order = list(range(nsub - 1, -1, -1) if reverse else range(nsub))
    for grp in (order,):
        loc = local([u for j in grp for u in prep(j)])
        for gi, j in enumerate(grp):
            res = [_mm(loc[gi * npair + p][0], hs[p]) for p in range(npair)]
            for p in range(npair):
                y_ref[0, j * c:(j + 1) * c, p * pw:(p + 1) * pw] = res[p][:c] + loc[gi * npair + p][1]
                hs[p] = res[p][c:] + loc[gi * npair + p][2]
    for p in range(npair):
        h_ref[p] = hs[p]


def _rwkv(rkvc, z, w0, w2p, a0, a2p, kkw, kaw, ctx_len, reverse):
    bsz, ltot, _ = rkvc.shape
    nblk = ltot // RWKV_BLK
    nctx = ctx_len // RWKV_BLK
    lora_blk = COL_LORA // LORA_W
    d = 1 if reverse else 0

    def tok(i):
        return jnp.where(i < nctx, nctx - 1 - i, nblk + nctx - 1 - i) if reverse else i

    par3 = lambda b, i: (d, 0, 0)
    kern = functools.partial(_rwkv_kernel, reverse=reverse)
    return pl.pallas_call(
        kern,
        grid=(bsz, nblk),
        in_specs=[pl.BlockSpec((1, RWKV_BLK, D_RWKV), lambda b, i: (b, tok(i), 0)),
                  pl.BlockSpec((1, RWKV_BLK, D_RWKV), lambda b, i: (b, tok(i), 1)),
                  pl.BlockSpec((1, RWKV_BLK, D_RWKV), lambda b, i: (b, tok(i), 2)),
                  pl.BlockSpec((1, RWKV_BLK, LORA_W), lambda b, i: (b, tok(i), lora_blk)),
                  pl.BlockSpec((1, 1, D_RWKV), par3),
                  pl.BlockSpec((1, LORA_W, D_RWKV), par3),
                  pl.BlockSpec((1, 1, D_RWKV), par3),
                  pl.BlockSpec((1, LORA_W, D_RWKV), par3),
                  pl.BlockSpec((1, D_RWKV), lambda b, i: (0, 0)),
                  pl.BlockSpec((1, D_RWKV), lambda b, i: (0, 0))],
        out_specs=pl.BlockSpec((1, RWKV_BLK, D_RWKV), lambda b, i: (b, tok(i), 0)),
        out_shape=jax.ShapeDtypeStruct((bsz, ltot, D_RWKV), F32),
        scratch_shapes=[pltpu.VMEM((HEADS // 2, 2 * HEAD, 2 * HEAD), F32)],
        compiler_params=_cparams(("parallel", "arbitrary"), 48),
    )(rkvc, rkvc, rkvc, z, w0, w2p, a0, a2p, kkw, kaw)


def _s5_consts(lam_re, lam_im, log_step, b_re, b_im, c_re, c_im):
    t = S5_T
    g, p, hg = b_re.shape
    step = jnp.exp(log_step)[..., None]
    mag = jnp.exp(lam_re * step)
    lb_re = mag * jnp.cos(lam_im * step)
    lb_im = mag * jnp.sin(lam_im * step)
    den = lam_re * lam_re + lam_im * lam_im
    nr = lb_re - 1
    q_re = (nr * lam_re + lb_im * lam_im) / den
    q_im = (lb_im * lam_re - nr * lam_im) / den
    bb_re = q_re[..., None] * b_re - q_im[..., None] * b_im
    bb_im = q_re[..., None] * b_im + q_im[..., None] * b_re
    n = jnp.arange(t + 1, dtype=F32)
    pmag = jnp.exp((lam_re * step)[..., None] * n)
    pw_re = pmag * jnp.cos((lam_im * step)[..., None] * n)
    pw_im = pmag * jnp.sin((lam_im * step)[..., None] * n)
    ct_re = jnp.transpose(c_re, (0, 2, 1))[None, :, :, None, :]
    ct_im = jnp.transpose(c_im, (0, 2, 1))[None, :, :, None, :]
    cp_re = ct_re * pw_re[..., None] - ct_im * pw_im[..., None]
    cp_im = ct_re * pw_im[..., None] + ct_im * pw_re[..., None]
    flat = lambda x: x.reshape(x.shape[:-2] + (x.shape[-2] * x.shape[-1],))
    cf_re, cf_im = flat(cp_re), flat(cp_im)
    hp = lax.Precision.HIGHEST
    kt = (jnp.einsum("dgpi,dgpm->dgim", bb_re, cf_re[..., :t * hg], precision=hp)
          - jnp.einsum("dgpi,dgpm->dgim", bb_im, cf_im[..., :t * hg], precision=hp))
    kb_rev = flat(kt[1].reshape(g, hg, t, hg)[:, :, ::-1])
    zpad = jnp.zeros((g, hg, (t - 1) * hg), F32)
    lag = (jnp.concatenate([zpad, kt[0]], axis=-1)
           + jnp.concatenate([kb_rev, zpad], axis=-1))
    m_tot = jnp.stack([lag[:, :, (t - 1 - s) * hg:(2 * t - 1 - s) * hg] for s in range(t)], axis=1)
    m_tot = m_tot.reshape(g, t * hg, t * hg)

    pwt_re = jnp.transpose(pw_re, (0, 1, 3, 2))
    pwt_im = jnp.transpose(pw_im, (0, 1, 3, 2))
    bbt_re = jnp.transpose(bb_re, (0, 1, 3, 2))
    bbt_im = jnp.transpose(bb_im, (0, 1, 3, 2))

    def inject(d, pr_, pi_):
        pr_, pi_ = pr_[:, :, None, :], pi_[:, :, None, :]
        br, bi = bbt_re[d][:, None], bbt_im[d][:, None]
        return pr_ * br - pi_ * bi, pr_ * bi + pi_ * br

    pf_re, pf_im = inject(0, pwt_re[0, :, :t][:, ::-1], pwt_im[0, :, :t][:, ::-1])
    pb_re, pb_im = inject(1, pwt_re[1, :, :t], pwt_im[1, :, :t])
    inj = jnp.concatenate([pf_re, pb_re, pf_im, pb_im], axis=-1).reshape(g, t * hg, 4 * p)

    cr_re, cr_im = flat(cp_re[1][:, :, ::-1]), flat(cp_im[1][:, :, ::-1])
    out = jnp.concatenate([m_tot, cf_re[0][..., hg:], cr_re[..., :t * hg],
                           -cf_im[0][..., hg:], -cr_im[..., :t * hg]], axis=1)
    lb16 = jnp.stack([jnp.concatenate([pw_re[0, :, :, t], pw_re[1, :, :, t]], axis=-1),
                      jnp.concatenate([pw_im[0, :, :, t], pw_im[1, :, :, t]], axis=-1)], axis=0)
    return inj.astype(BF16), out.astype(BF16), lb16


def _block_transpose8(xs):
    lane_blk = lax.broadcasted_iota(jnp.int32, xs[0].shape, 1) >> 4
    xs = list(xs)
    for d in (4, 2, 1):
        hi = (lane_blk & d) != 0
        for i in range(8):
            if i & d:
                continue
            a, b = xs[i], xs[i + d]
            xs[i] = jnp.where(hi, pltpu.roll(b, 16 * d, 1), a)
            xs[i + d] = jnp.where(hi, b, pltpu.roll(a, 128 - 16 * d, 1))
    return xs


def _s5_kernel(u0_ref, u1_ref, inj_ref, out_ref, l_ref, y_ref, ug_ref, zre_ref, zim_ref, hre_ref,
               him_ref, *, nchunk, nctx):
    gh = ug_ref.shape[0]
    ps = S5_STATE
    t = S5_T
    nlat = nchunk - nctx
    for jt, u_ref in enumerate((u0_ref, u1_ref)):
        for q in range(t // 8):
            cols = [u_ref[0, pl.ds(8 * q + tt, nchunk, stride=t), :] for tt in range(8)]
            grp = _block_transpose8(cols)
            for gl in range(8):
                ug_ref[8 * jt + gl, :, 128 * q:128 * (q + 1)] = grp[gl].astype(BF16)
    for gi in range(gh):
        zg = _dot(ug_ref[gi], inj_ref[gi], NN)
        zre_ref[pl.ds(gi, nchunk, stride=gh), :] = zg[:, 0:2 * ps]
        zim_ref[pl.ds(gi, nchunk, stride=gh), :] = zg[:, 2 * ps:4 * ps]

    lane = lax.broadcasted_iota(jnp.int32, (gh, 2 * ps), 1)
    is_f = lane < ps
    lre = l_ref[0, 0]
    lim = l_ref[0, 1]

    def body(i, carry):
        hre, him = carry
        cb = jnp.where(i < nctx, nctx - 1 - i, nchunk + nctx - 1 - i)
        rf = pl.multiple_of(i * gh, gh)
        rb = pl.multiple_of(cb * gh, gh)
        hre_ref[pl.ds(rf, gh), 0:ps] = hre[:, 0:ps]
        hre_ref[pl.ds(rb, gh), ps:2 * ps] = hre[:, ps:2 * ps]
        him_ref[pl.ds(rf, gh), 0:ps] = him[:, 0:ps]
        him_ref[pl.ds(rb, gh), ps:2 * ps] = him[:, ps:2 * ps]
        zre = jnp.where(is_f, zre_ref[pl.ds(rf, gh), :], zre_ref[pl.ds(rb, gh), :])
        zim = jnp.where(is_f, zim_ref[pl.ds(rf, gh), :], zim_ref[pl.ds(rb, gh), :])
        return lre * hre - lim * him + zre, lre * him + lim * hre + zim

    zero = jnp.zeros((gh, 2 * ps), F32)
    lax.fori_loop(0, nchunk, body, (zero, zero))

    for jt in range(gh // 8):
        ys = []
        for gl in range(8):
            gi = 8 * jt + gl
            hre = hre_ref[pl.ds(gi, nchunk, stride=gh), :].astype(BF16)
            him = him_ref[pl.ds(gi, nchunk, stride=gh), :].astype(BF16)
            ys.append(_dot(jnp.concatenate([ug_ref[gi], hre, him], axis=1), out_ref[gi], NN))
        for q in range(t // 8):
            tok = _block_transpose8([y[:, 128 * q:128 * (q + 1)] for y in ys])
            for tt in range(8):
                y_ref[0, jt, pl.ds(8 * q + tt, nlat, stride=t), :] = tok[tt][nctx:]


def _s5(z, consts, ctx_len):
    inj, out, lb16 = consts
    bsz, ltot, _ = z.shape
    nchunk = ltot // S5_T
    nctx = ctx_len // S5_T
    gh = S5_GH
    halves = S5_GROUPS // gh
    w = S5_T * S5_GROUP
    tiles = gh * S5_GROUP // 128
    ps2 = 2 * S5_STATE
    lb = lb16.reshape(2, halves, gh, ps2).transpose(1, 0, 2, 3)
    kern = functools.partial(_s5_kernel, nchunk=nchunk, nctx=nctx)
    u_spec = lambda k: pl.BlockSpec((1, ltot, 128), lambda b, j: (b, 0, COL_U // 128 + tiles * j + k))
    return pl.pallas_call(
        kern,
        grid=(bsz, halves),
        in_specs=[u_spec(0), u_spec(1),
                  pl.BlockSpec((gh, w, 2 * ps2), lambda b, j: (j, 0, 0)),
                  pl.BlockSpec((gh, w + 2 * ps2, w), lambda b, j: (j, 0, 0)),
                  pl.BlockSpec((1, 2, gh, ps2), lambda b, j: (j, 0, 0, 0))],
        out_specs=pl.BlockSpec((1, tiles, ltot - ctx_len, 128), lambda b, j: (b, j, 0, 0)),
        out_shape=jax.ShapeDtypeStruct((bsz, D_S5 // 128, ltot - ctx_len, 128), F32),
        scratch_shapes=[pltpu.VMEM((gh, nchunk, w), BF16),
                        pltpu.VMEM((nchunk * gh, ps2), F32),
                        pltpu.VMEM((nchunk * gh, ps2), F32),
                        pltpu.VMEM((nchunk * gh, ps2), F32),
                        pltpu.VMEM((nchunk * gh, ps2), F32)],
        compiler_params=_cparams(("parallel", "parallel"), 48),
    )(z, z, inj, out, lb)


def _mix_kernel(x_ref, yf_ref, yb_ref, r_ref, k_ref, v_ref, lo_ref, u_ref, ys_ref, mod_ref,
                seg_ref, a0_ref, a2_ref, g2_ref, kaw_ref, rk_ref, lnw_ref, lnb_ref,
                dsk_ref, gluw_ref, glub_ref, wo_ref, g2n_ref, w1_ref, w3_ref, w2_ref, gf_ref, o_ref):
    lo = lo_ref[0]
    seg = seg_ref[...]

    def head_sum(t):
        pw = seg.shape[0]
        return jnp.concatenate([_mm(t[:, i:i + pw], seg, "xa2") for i in range(0, t.shape[1], pw)], axis=1)

    y = yf_ref[0] + yb_ref[0]
    mu = head_sum(y) * (1.0 / HEAD)
    yc = y - mu
    var = head_sum(yc * yc) * (1.0 / HEAD)
    yn = yc * lax.rsqrt(var + RWKV_LN_EPS) * lnw_ref[...] + lnb_ref[...]
    a_f = _sigmoid(a0_ref[0] + _mm(lo, a2_ref[0]))
    a_b = _sigmoid(a0_ref[1] + _mm(lo, a2_ref[1]))
    kaw = kaw_ref[...]
    kd_sum = k_ref[0] * ((1.0 + (a_f - 1.0) * kaw) + (1.0 + (a_b - 1.0) * kaw))
    bonus = head_sum(r_ref[0] * kd_sum * rk_ref[...])
    yn = yn + bonus * v_ref[0]
    gate = _mm(_sigmoid(lo), g2_ref[...])
    rwkv_out = yn * gate

    u = u_ref[0]
    ys = jnp.concatenate([ys_ref[0, i] for i in range(ys_ref.shape[1])], axis=1) + dsk_ref[...] * u
    zg = 0.5 * ys * (1.0 + jnp.tanh(math.sqrt(2.0 / math.pi) * (ys + 0.044715 * (ys * ys * ys))))
    s5_out = zg * _sigmoid(_mm(zg, gluw_ref[...]) + glub_ref[...])

    mix = _mm(rwkv_out, wo_ref[0:D_RWKV, :]) + _mm(s5_out, wo_ref[D_RWKV:, :])
    x1 = x_ref[0] + mod_ref[0, 2:3, :] * mix

    ms = jnp.mean(x1 * x1, axis=-1, keepdims=True)
    h = x1 * lax.rsqrt(ms + NORM_EPS) * g2n_ref[...]
    h = h * (1.0 + mod_ref[0, 4:5, :]) + mod_ref[0, 3:4, :]
    hb = h.astype(BF16)
    h1 = _dot(hb, w1_ref[...], NN)
    h3 = _dot(hb, w3_ref[...], NN)
    act = (h1 * _sigmoid(h1)) * h3
    out = x1 + mod_ref[0, 5:6, :] * _dot(act.astype(BF16), w2_ref[...], NN)
    ms2 = jnp.mean(out * out, axis=-1, keepdims=True)
    o_ref[0] = out * lax.rsqrt(ms2 + NORM_EPS) * gf_ref[...]


def _mix(x, y_f, y_b, rkvc, z, ys5, mod8, seg, a0, a2p, g2p, kaw, rk, lnw, lnb, dsk, gluw, glub, wo,
         g2n, w1, w3, w2, gf, ctx_len, tm):
    bsz, seq, d = x.shape
    assert ctx_len % tm == 0 and seq % tm == 0
    off = ctx_len // tm
    lora_blk = COL_LORA // LORA_W
    u_blk = COL_U // D_S5
    tok = lambda c: pl.BlockSpec((1, tm, D_RWKV), lambda b, j: (b, j + off, c))
    full = lambda a: pl.BlockSpec(a.shape, lambda b, j: (0,) * a.ndim, pipeline_mode=pl.Buffered(1))
    return pl.pallas_call(
        _mix_kernel,
        grid=(bsz, seq // tm),
        in_specs=[pl.BlockSpec((1, tm, d), lambda b, j: (b, j, 0)),
                  tok(0), tok(0), tok(0), tok(1), tok(2),
                  pl.BlockSpec((1, tm, LORA_W), lambda b, j: (b, j + off, lora_blk)),
                  pl.BlockSpec((1, tm, D_S5), lambda b, j: (b, j + off, u_blk)),
                  pl.BlockSpec((1, D_S5 // 128, tm, 128), lambda b, j: (b, 0, j, 0)),
                  pl.BlockSpec((1, 8, d), lambda b, j: (b, 0, 0)),
                  full(seg), full(a0), full(a2p), full(g2p), full(kaw), full(rk), full(lnw), full(lnb),
                  full(dsk), full(gluw), full(glub), full(wo),
                  full(g2n), full(w1), full(w3), full(w2), full(gf)],
        out_specs=pl.BlockSpec((1, tm, d), lambda b, j: (b, j, 0)),
        out_shape=jax.ShapeDtypeStruct((bsz, seq, d), F32),
        compiler_params=_cparams(("parallel", "parallel"), 56),
    )(x, y_f, y_b, rkvc, rkvc, rkvc, z, z, ys5, mod8, seg, a0, a2p, g2p, kaw, rk, lnw, lnb,
      dsk, gluw, glub, wo, g2n, w1, w3, w2, gf)


def _pad_rows(w, row0, rows_total):
    pad = [(0, 0)] * (w.ndim - 2) + [(row0, rows_total - row0 - w.shape[-2]), (0, 0)]
    return jnp.pad(w, pad)


def kernel(x, c, ctx, c_ctx, mod_w, mod_b, norm1_g, norm2_g, w_in, w_out, rwkv_conv, rwkv_w0, rwkv_w2, rwkv_a0, rwkv_a2, rwkv_g2, rwkv_kk, rwkv_ka, rwkv_rk, rwkv_ln_w, rwkv_ln_b, s5_lam_re, s5_lam_im, s5_log_step, s5_b_re, s5_b_im, s5_c_re, s5_c_im, s5_d, s5_glu_w, s5_glu_b, ffn_w1, ffn_w3, ffn_w2, final_g):
    bsz, seq, d = x.shape
    ctx_len = ctx.shape[1]
    ltot = ctx_len + seq
    layer = 0

    cc = jnp.concatenate([c, c_ctx[None, :], jnp.zeros((8 - bsz - 1, d), F32)], axis=0)
    mod = _adaln(cc, mod_w[layer], mod_b[layer])
    mod8 = jnp.pad(mod[:bsz + 1].reshape(bsz + 1, N_MOD, d), ((0, 0), (0, 8 - N_MOD), (0, 0)))

    wi = w_in[layer]
    w_z = jnp.concatenate([wi[:, :3 * D_RWKV], wi[:, -D_S5:], wi[:, 3 * D_RWKV:-D_S5],
                           jnp.zeros((d, D_Z - wi.shape[1]), F32)], axis=1).astype(BF16)
    z = _inproj(ctx, x, mod8, norm1_g[layer], w_z, 256)

    rkvc = _conv(z, rwkv_conv[layer], ctx_len)

    w2p = _pad_rows(rwkv_w2[layer], 0, LORA_W).at[1].set(_pad_rows(rwkv_w2[layer, 1], DECAY_LORA, LORA_W))
    a2p = jnp.stack([_pad_rows(rwkv_a2[layer, 0], LORA_AD, LORA_W),
                     _pad_rows(rwkv_a2[layer, 1], LORA_AD + AAA_LORA, LORA_W)])
    w0 = rwkv_w0[layer].reshape(2, 1, D_RWKV)
    a0 = rwkv_a0[layer].reshape(2, 1, D_RWKV)
    kkw = rwkv_kk[layer].reshape(1, D_RWKV)
    kaw = rwkv_ka[layer].reshape(1, D_RWKV)
    y_f = _rwkv(rkvc, z, w0, w2p, a0, a2p, kkw, kaw, ctx_len, False)
    y_b = _rwkv(rkvc, z, w0, w2p, a0, a2p, kkw, kaw, ctx_len, True)

    consts = _s5_consts(s5_lam_re[layer], s5_lam_im[layer], s5_log_step[layer], s5_b_re[layer],
                        s5_b_im[layer], s5_c_re[layer], s5_c_im[layer])
    ys5 = _s5(z, consts, ctx_len)

    hh = jnp.arange(2 * HEAD) // HEAD
    seg = (hh[:, None] == hh[None, :]).astype(BF16)
    g2p = _pad_rows(rwkv_g2[layer], LORA_GD, LORA_W)
    return _mix(x, y_f, y_b, rkvc, z, ys5, mod8, seg, a0, a2p, g2p, kaw,
                rwkv_rk[layer].reshape(1, D_RWKV), rwkv_ln_w[layer].reshape(1, D_RWKV),
                rwkv_ln_b[layer].reshape(1, D_RWKV), s5_d[layer].reshape(1, D_S5),
                s5_glu_w[layer].astype(BF16), s5_glu_b[layer].reshape(1, D_S5),
                w_out[layer].astype(BF16), norm2_g[layer].reshape(1, d), ffn_w1[layer].astype(BF16),
                ffn_w3[layer].astype(BF16), ffn_w2[layer].astype(BF16), final_g.reshape(1, d), ctx_len, 256)
```

```python
import functools
import math

import jax
import jax.numpy as jnp
from jax import lax
from jax.experimental import pallas as pl
from jax.experimental.pallas import tpu as pltpu

F32 = jnp.float32
BF16 = jnp.bfloat16

D_MODEL = 1024
GRID_W = 64
D_RWKV = 512
HEAD = 64
HEADS = D_RWKV // HEAD
D_S5 = 512
S5_GROUP = 16
S5_GROUPS = D_S5 // S5_GROUP
S5_STATE = 64
DECAY_LORA = 32
AAA_LORA = 32
GATE_LORA = 96
N_MOD = 6
NORM_EPS = 1e-6
RWKV_LN_EPS = 64e-5

LORA_W = 256
COL_U = 3 * D_RWKV
COL_LORA = COL_U + D_S5
D_Z = COL_LORA + LORA_W
LORA_AD = 2 * DECAY_LORA
LORA_GD = LORA_AD + 2 * AAA_LORA

CHUNK = 64
RWKV_BLK = 256
S5_T = 16
S5_GH = 16
CONV_PAD = 72
CONV_TC = 256
CONV_PIECE = 128

NN = (((1,), (0,)), ((), ()))
NT = (((1,), (1,)), ((), ()))
TN = (((0,), (0,)), ((), ()))


def _dot(a, b, dims):
    return lax.dot_general(a, b, dims, preferred_element_type=F32)


def _split2(x):
    hi = x.astype(BF16)
    lo = (x - hi.astype(F32)).astype(BF16)
    return hi, lo


def _split3(x):
    hi = x.astype(BF16)
    r1 = x - hi.astype(F32)
    mid = r1.astype(BF16)
    lo = (r1 - mid.astype(F32)).astype(BF16)
    return hi, mid, lo


def _mm(a, b, mode="bf16", dims=NN):
    if mode == "bf16":
        return _dot(a.astype(BF16), b.astype(BF16), dims)
    if mode == "x3":
        ah, al = _split2(a)
        bh, bl = _split2(b)
        return _dot(ah, bh, dims) + (_dot(ah, bl, dims) + _dot(al, bh, dims))
    if mode == "xa":
        bb = b.astype(BF16)
        a1, a2, a3 = _split3(a)
        return _dot(a1, bb, dims) + (_dot(a2, bb, dims) + _dot(a3, bb, dims))
    if mode == "xa2":
        bb = b.astype(BF16)
        a1, a2 = _split2(a)
        return _dot(a1, bb, dims) + _dot(a2, bb, dims)
    if mode == "xb":
        ab = a.astype(BF16)
        b1, b2, b3 = _split3(b)
        return _dot(ab, b1, dims) + (_dot(ab, b2, dims) + _dot(ab, b3, dims))
    raise ValueError(mode)


def _sigmoid(x):
    return 1.0 / (1.0 + jnp.exp(-x))


def _softplus(x):
    return jnp.maximum(x, 0.0) + jnp.log(1.0 + jnp.exp(-jnp.abs(x)))


def _cparams(sem, vmem_mb):
    return pltpu.CompilerParams(dimension_semantics=sem, vmem_limit_bytes=vmem_mb * 1024 * 1024)


def _adaln_kernel(c_ref, w_ref, b_ref, o_ref):
    c = c_ref[...]
    s = c * _sigmoid(c)
    o_ref[...] = _mm(s, w_ref[...], "x3") + b_ref[...]


def _adaln(cc, mod_w, mod_b):
    rows, d = cc.shape
    n = mod_w.shape[1]
    tn = 1024
    return pl.pallas_call(
        _adaln_kernel,
        grid=(n // tn,),
        in_specs=[pl.BlockSpec((rows, d), lambda j: (0, 0)),
                  pl.BlockSpec((d, tn), lambda j: (0, j)),
                  pl.BlockSpec((1, tn), lambda j: (0, j))],
        out_specs=pl.BlockSpec((rows, tn), lambda j: (0, j)),
        out_shape=jax.ShapeDtypeStruct((rows, n), F32),
        compiler_params=_cparams(("parallel",), 40),
    )(cc, mod_w, mod_b.reshape(1, n))


def _inproj_kernel(c_ref, x_ref, mod_ref, g_ref, w_ref, o_ref, *, ctx_blocks):
    x = jnp.where(pl.program_id(1) < ctx_blocks, c_ref[0], x_ref[0])
    ms = jnp.mean(x * x, axis=-1, keepdims=True)
    xn = x * lax.rsqrt(ms + NORM_EPS) * g_ref[...]
    shift = mod_ref[0, 0:1, :]
    scale = mod_ref[0, 1:2, :]
    h = xn * (1.0 + scale) + shift
    o_ref[0] = _mm(h, w_ref[...], "bf16")


def _inproj(ctx, x, mod8, g1, w_z, tm):
    bsz, seq, d = x.shape
    ctx_blocks = ctx.shape[1] // tm
    dz = w_z.shape[1]
    kern = functools.partial(_inproj_kernel, ctx_blocks=ctx_blocks)
    return pl.pallas_call(
        kern,
        grid=(bsz, ctx_blocks + seq // tm),
        in_specs=[pl.BlockSpec((1, tm, d), lambda b, j: (b, jnp.minimum(j, ctx_blocks - 1), 0)),
                  pl.BlockSpec((1, tm, d), lambda b, j: (b, jnp.maximum(j - ctx_blocks, 0), 0)),
                  pl.BlockSpec((1, 8, d), lambda b, j: (jnp.where(j < ctx_blocks, bsz, b), 0, 0)),
                  pl.BlockSpec((1, d), lambda b, j: (0, 0)),
                  pl.BlockSpec((d, dz), lambda b, j: (0, 0))],
        out_specs=pl.BlockSpec((1, tm, dz), lambda b, j: (b, j, 0)),
        out_shape=jax.ShapeDtypeStruct((bsz, ctx.shape[1] + seq, dz), F32),
        compiler_params=_cparams(("parallel", "parallel"), 48),
    )(ctx, x, mod8, g1.reshape(1, d), w_z)


def _conv_kernel(z_ref, w_ref, o_ref, xs_ref, *, ctx_len, lat_len):
    tc = z_ref.shape[2]
    pc = CONV_PIECE
    ctx_off = CONV_PAD
    lat_off = 2 * CONV_PAD + ctx_len
    zpad = jnp.zeros((CONV_PAD, tc), F32)
    for s in range(3):
        xs_ref[s, 0:CONV_PAD, :] = zpad
        xs_ref[s, ctx_off + ctx_len:lat_off, :] = zpad
        xs_ref[s, lat_off + lat_len:lat_off + lat_len + CONV_PAD, :] = zpad
    xs_ref[1, lat_off:lat_off + 8, :] = zpad[0:8]
    xs_ref[2, lat_off + lat_len - 8:lat_off + lat_len, :] = zpad[0:8]
    col = lax.broadcasted_iota(jnp.int32, (pc, tc), 0) % GRID_W
    keep_l = (col != GRID_W - 1).astype(F32)
    keep_r = (col != 0).astype(F32)
    for p in range(ctx_len // pc):
        xs_ref[0, ctx_off + p * pc:ctx_off + (p + 1) * pc, :] = z_ref[0, p * pc:(p + 1) * pc, :]
    for p in range(lat_len // pc):
        x = z_ref[0, ctx_len + p * pc:ctx_len + (p + 1) * pc, :]
        dst = lat_off + p * pc
        xs_ref[0, dst:dst + pc, :] = x
        xs_ref[1, dst + 1:dst + 1 + pc, :] = x * keep_l
        xs_ref[2, dst - 1:dst - 1 + pc, :] = x * keep_r
    w = [w_ref[t:t + 1, :] for t in range(9)]
    for p in range(ctx_len // pc):
        base = ctx_off + p * pc
        acc = w[3] * xs_ref[0, base - 1:base - 1 + pc, :]
        acc = acc + w[4] * xs_ref[0, base:base + pc, :]
        acc = acc + w[5] * xs_ref[0, base + 1:base + 1 + pc, :]
        o_ref[0, p * pc:(p + 1) * pc, :] = acc
    for p in range(lat_len // pc):
        base = lat_off + p * pc
        acc = None
        for di in range(3):
            for dj in range(3):
                off = base + (di - 1) * GRID_W
                src = (1, 0, 2)[dj]
                term = w[3 * di + dj] * xs_ref[src, off:off + pc, :]
                acc = term if acc is None else acc + term
        o_ref[0, ctx_len + p * pc:ctx_len + (p + 1) * pc, :] = acc


def _conv(z, conv_w, ctx_len):
    bsz, ltot, _ = z.shape
    lat_len = ltot - ctx_len
    ch = 3 * D_RWKV
    rows = 3 * CONV_PAD + ltot
    kern = functools.partial(_conv_kernel, ctx_len=ctx_len, lat_len=lat_len)
    return pl.pallas_call(
        kern,
        grid=(bsz, ch // CONV_TC),
        in_specs=[pl.BlockSpec((1, ltot, CONV_TC), lambda b, j: (b, 0, j)),
                  pl.BlockSpec((9, CONV_TC), lambda b, j: (0, j))],
        out_specs=pl.BlockSpec((1, ltot, CONV_TC), lambda b, j: (b, 0, j)),
        out_shape=jax.ShapeDtypeStruct((bsz, ltot, ch), F32),
        scratch_shapes=[pltpu.VMEM((3, rows, CONV_TC), F32)],
        compiler_params=_cparams(("parallel", "parallel"), 48),
    )(z, conv_w.reshape(9, ch))


def _bd(x, lo):
    return jnp.concatenate([jnp.where(lo, x, 0.0), jnp.where(lo, 0.0, x)], axis=0)


def _tri_inv_pairs(n_mats, size, lo):
    ii = lax.broadcasted_iota(jnp.int32, (size, 2 * size), 0)
    jj = lax.broadcasted_iota(jnp.int32, (size, 2 * size), 1) & (size - 1)
    eye = (ii == jj).astype(F32)
    in2 = (ii >> 1) == (jj >> 1)
    ts = [eye + jnp.where(in2, n, 0.0) for n in n_mats]
    lg = 1
    while (1 << lg) < size:
        same = (ii >> (lg + 1)) == (jj >> (lg + 1))
        diff = (ii >> lg) != (jj >> lg)
        nls = [jnp.where(same, jnp.where(diff, n, 0.0), 0.0) for n in n_mats]
        tmp = [_mm(t, _bd(nl, lo)) for t, nl in zip(ts, nls)]
        ts = [t + _mm(x, _bd(t, lo)) for x, t in zip(tmp, ts)]
        lg += 1
    return ts


def _rwkv_kernel(r_ref, k_ref, v_ref, lo_ref, w0_ref, w2_ref, a0_ref, a2_ref, kkw_ref, kaw_ref,
                 y_ref, h_ref, qp_ref, y0_ref, g_ref, *, reverse, nblk):
    step = pl.program_id(1)
    c = CHUNK
    nsub = RWKV_BLK // c
    pw = 2 * HEAD
    npair = HEADS // 2
    order = list(range(nsub - 1, -1, -1) if reverse else range(nsub))
    cur = step % 2

    @pl.when(step == 0)
    def _():
        h_ref[...] = jnp.zeros_like(h_ref)
        qp_ref[1] = jnp.zeros(qp_ref.shape[1:], qp_ref.dtype)
        y0_ref[1] = jnp.zeros(y0_ref.shape[1:], y0_ref.dtype)
        g_ref[1] = jnp.zeros(g_ref.shape[1:], g_ref.dtype)

    def recur():
        slot = 1 - cur
        hs = [h_ref[p] for p in range(npair)]
        for gi, j in enumerate(order):
            res = [_dot(qp_ref[slot, gi * npair + p], hs[p].astype(BF16), NN) for p in range(npair)]
            for p in range(npair):
                y_ref[0, j * c:(j + 1) * c, p * pw:(p + 1) * pw] = res[p][:c] + y0_ref[slot, gi * npair + p]
                hs[p] = res[p][c:] + g_ref[slot, gi * npair + p]
        for p in range(npair):
            h_ref[p] = hs[p]

    @pl.when(step == nblk)
    def _():
        recur()

    @pl.when(step < nblk)
    def _():
        recur()
        _rwkv_local(r_ref, k_ref, v_ref, lo_ref, w0_ref, w2_ref, a0_ref, a2_ref, kkw_ref, kaw_ref,
                    qp_ref, y0_ref, g_ref, cur, order, reverse)


def _rwkv_local(r_ref, k_ref, v_ref, lo_ref, w0_ref, w2_ref, a0_ref, a2_ref, kkw_ref, kaw_ref,
                qp_ref, y0_ref, g_ref, slot, order, reverse):
    r = r_ref[0]
    k = k_ref[0]
    v = v_ref[0]
    lora = lo_ref[0]
    n = r.shape[0]
    c = CHUNK
    nsub = n // c
    lgc = c.bit_length() - 1
    pw = 2 * HEAD
    npair = HEADS // 2

    logit_w = w0_ref[0] + _mm(jnp.tanh(lora), w2_ref[0])
    logw = -_softplus(-logit_w) - 0.5
    lw = -jnp.exp(logw)
    a = _sigmoid(a0_ref[0] + _mm(lora, a2_ref[0]))
    kd = k * (1.0 + (a - 1.0) * kaw_ref[...])
    kk = k * kkw_ref[...]

    ti = lax.broadcasted_iota(jnp.int32, (n, n), 0)
    tj = lax.broadcasted_iota(jnp.int32, (n, n), 1)
    causal = (tj >= ti) if reverse else (tj <= ti)
    tri = jnp.where((ti >> lgc) == (tj >> lgc), jnp.where(causal, 1.0, 0.0), 0.0)
    g = _mm(tri, lw, "xb")

    ci2 = lax.broadcasted_iota(jnp.int32, (c, pw), 0)
    lane2 = lax.broadcasted_iota(jnp.int32, (c, pw), 1)
    cj2 = lane2 & (c - 1)
    lo = lane2 < HEAD
    incl2 = (cj2 >= ci2) if reverse else (cj2 <= ci2)
    strict2 = (cj2 > ci2) if reverse else (cj2 < ci2)
    ri = lax.broadcasted_iota(jnp.int32, (pw, pw), 0)
    rj = lax.broadcasted_iota(jnp.int32, (pw, pw), 1)
    blk = (ri < HEAD) == (rj < HEAD)
    eye_p = (ri == rj).astype(F32)
    zeros_c = jnp.zeros((c, pw), F32)
    zeros_p = jnp.zeros((pw, pw), F32)

    def prep(j):
        rows = slice(j * c, (j + 1) * c)
        lw_j = lw[rows]
        g_j = g[rows]
        tot = jnp.sum(lw_j, axis=0, keepdims=True)
        e_prev = jnp.exp(g_j - lw_j)
        e_neg = jnp.exp(-g_j)
        e_pos = jnp.exp(g_j)
        e_tot = jnp.exp(tot)
        e_end = e_tot * e_neg
        out = []
        for p in range(npair):
            sl = slice(p * pw, (p + 1) * pw)
            kkp = kk[rows, sl]
            sq = kkp * kkp
            n2 = jnp.where(lo, jnp.sum(jnp.where(lo, sq, 0.0), axis=-1, keepdims=True),
                           jnp.sum(jnp.where(lo, 0.0, sq), axis=-1, keepdims=True))
            kkp = kkp * lax.rsqrt(jnp.maximum(n2, 1e-12))
            ka = kkp * a[rows, sl]
            out.append(dict(a_t=-kkp * e_prev[:, sl], b_t=ka * e_neg[:, sl], k_t=kd[rows, sl] * e_neg[:, sl],
                            r_t=r[rows, sl] * e_pos[:, sl], b_e=ka * e_end[:, sl],
                            k_e=kd[rows, sl] * e_end[:, sl], vh=v[rows, sl], g_end=e_tot[:, sl]))
        return out

    def local(us):
        s_bk = [_mm(jnp.concatenate([u["a_t"], u["r_t"]], axis=0),
                    jnp.concatenate([_bd(u["b_t"], lo), _bd(u["k_t"], lo)], axis=0), "bf16", NT)
                for u in us]
        nab = [jnp.where(strict2, s[:c, :pw], 0.0) for s in s_bk]
        aak = [jnp.where(strict2, s[:c, pw:], 0.0) for s in s_bk]
        arbk = [jnp.concatenate([jnp.where(incl2, s[c:, :pw], 0.0), jnp.where(incl2, s[c:, pw:], 0.0)],
                                axis=1) for s in s_bk]
        t_inv = _tri_inv_pairs(nab, c, lo)
        x1 = [_mm(m, _bd(u["vh"], lo)) for m, u in zip(aak, us)]
        uw = [_mm(t, jnp.concatenate([_bd(x, lo), _bd(u["a_t"], lo)], axis=1))
              for t, x, u in zip(t_inv, x1, us)]
        yq = [_mm(m, jnp.concatenate(
            [jnp.concatenate([_bd(w_[:, :pw], lo), _bd(w_[:, pw:], lo)], axis=1),
             jnp.concatenate([_bd(u["vh"], lo), zeros_p], axis=1)], axis=0))
            for m, w_, u in zip(arbk, uw, us)]
        gp = [_mm(jnp.concatenate([u["b_e"], u["k_e"]], axis=0),
                  jnp.concatenate([w_, jnp.concatenate([u["vh"], zeros_c], axis=1)], axis=0), "bf16", TN)
              for w_, u in zip(uw, us)]
        res = []
        for u, yq_, gp_ in zip(us, yq, gp):
            q = u["r_t"] + yq_[:, pw:]
            p_bd = jnp.where(blk, gp_[:, pw:], 0.0) + eye_p * u["g_end"]
            res.append((jnp.concatenate([q, p_bd], axis=0), yq_[:, :pw], jnp.where(blk, gp_[:, :pw], 0.0)))
        return res

    for idx, (qp, y0, gg) in enumerate(local([u for j in order for u in prep(j)])):
        qp_ref[slot, idx] = qp.astype(BF16)
        y0_ref[slot, idx] = y0
        g_ref[slot, idx] = gg


def _rwkv(rkvc, z, w0, w2p, a0, a2p, kkw, kaw, ctx_len, reverse):
    bsz, ltot, _ = rkvc.shape
    nblk = ltot // RWKV_BLK
    nctx = ctx_len // RWKV_BLK
    lora_blk = COL_LORA // LORA_W
    d = 1 if reverse else 0

    def tok(i):
        return jnp.where(i < nctx, nctx - 1 - i, nblk + nctx - 1 - i) if reverse else i

    par3 = lambda b, i: (d, 0, 0)
    kern = functools.partial(_rwkv_kernel, reverse=reverse, nblk=nblk)
    tin = lambda i: tok(jnp.minimum(i, nblk - 1))
    tout = lambda i: tok(jnp.maximum(i - 1, 0))
    units = (RWKV_BLK // CHUNK) * (HEADS // 2)
    return pl.pallas_call(
        kern,
        grid=(bsz, nblk + 1),
        in_specs=[pl.BlockSpec((1, RWKV_BLK, D_RWKV), lambda b, i: (b, tin(i), 0)),
                  pl.BlockSpec((1, RWKV_BLK, D_RWKV), lambda b, i: (b, tin(i), 1)),
                  pl.BlockSpec((1, RWKV_BLK, D_RWKV), lambda b, i: (b, tin(i), 2)),
                  pl.BlockSpec((1, RWKV_BLK, LORA_W), lambda b, i: (b, tin(i), lora_blk)),
                  pl.BlockSpec((1, 1, D_RWKV), par3),
                  pl.BlockSpec((1, LORA_W, D_RWKV), par3),
                  pl.BlockSpec((1, 1, D_RWKV), par3),
                  pl.BlockSpec((1, LORA_W, D_RWKV), par3),
                  pl.BlockSpec((1, D_RWKV), lambda b, i: (0, 0)),
                  pl.BlockSpec((1, D_RWKV), lambda b, i: (0, 0))],
        out_specs=pl.BlockSpec((1, RWKV_BLK, D_RWKV), lambda b, i: (b, tout(i), 0)),
        out_shape=jax.ShapeDtypeStruct((bsz, ltot, D_RWKV), F32),
        scratch_shapes=[pltpu.VMEM((HEADS // 2, 2 * HEAD, 2 * HEAD), F32),
                        pltpu.VMEM((2, units, CHUNK + 2 * HEAD, 2 * HEAD), BF16),
                        pltpu.VMEM((2, units, CHUNK, 2 * HEAD), F32),
                        pltpu.VMEM((2, units, 2 * HEAD, 2 * HEAD), F32)],
        compiler_params=_cparams(("parallel", "arbitrary"), 48),
    )(rkvc, rkvc, rkvc, z, w0, w2p, a0, a2p, kkw, kaw)


def _s5_consts(lam_re, lam_im, log_step, b_re, b_im, c_re, c_im):
    t = S5_T
    g, p, hg = b_re.shape
    step = jnp.exp(log_step)[..., None]
    mag = jnp.exp(lam_re * step)
    lb_re = mag * jnp.cos(lam_im * step)
    lb_im = mag * jnp.sin(lam_im * step)
    den = lam_re * lam_re + lam_im * lam_im
    nr = lb_re - 1
    q_re = (nr * lam_re + lb_im * lam_im) / den
    q_im = (lb_im * lam_re - nr * lam_im) / den
    bb_re = q_re[..., None] * b_re - q_im[..., None] * b_im
    bb_im = q_re[..., None] * b_im + q_im[..., None] * b_re
    n = jnp.arange(t + 1, dtype=F32)
    pmag = jnp.exp((lam_re * step)[..., None] * n)
    pw_re = pmag * jnp.cos((lam_im * step)[..., None] * n)
    pw_im = pmag * jnp.sin((lam_im * step)[..., None] * n)
    ct_re = jnp.transpose(c_re, (0, 2, 1))[None, :, :, None, :]
    ct_im = jnp.transpose(c_im, (0, 2, 1))[None, :, :, None, :]
    cp_re = ct_re * pw_re[..., None] - ct_im * pw_im[..., None]
    cp_im = ct_re * pw_im[..., None] + ct_im * pw_re[..., None]
    flat = lambda x: x.reshape(x.shape[:-2] + (x.shape[-2] * x.shape[-1],))
    cf_re, cf_im = flat(cp_re), flat(cp_im)
    hp = lax.Precision.HIGHEST
    kt = (jnp.einsum("dgpi,dgpm->dgim", bb_re, cf_re[..., :t * hg], precision=hp)
          - jnp.einsum("dgpi,dgpm->dgim", bb_im, cf_im[..., :t * hg], precision=hp))
    kb_rev = flat(kt[1].reshape(g, hg, t, hg)[:, :, ::-1])
    zpad = jnp.zeros((g, hg, (t - 1) * hg), F32)
    lag = (jnp.concatenate([zpad, kt[0]], axis=-1)
           + jnp.concatenate([kb_rev, zpad], axis=-1))
    m_tot = jnp.stack([lag[:, :, (t - 1 - s) * hg:(2 * t - 1 - s) * hg] for s in range(t)], axis=1)
    m_tot = m_tot.reshape(g, t * hg, t * hg)

    pwt_re = jnp.transpose(pw_re, (0, 1, 3, 2))
    pwt_im = jnp.transpose(pw_im, (0, 1, 3, 2))
    bbt_re = jnp.transpose(bb_re, (0, 1, 3, 2))
    bbt_im = jnp.transpose(bb_im, (0, 1, 3, 2))

    def inject(d, pr_, pi_):
        pr_, pi_ = pr_[:, :, None, :], pi_[:, :, None, :]
        br, bi = bbt_re[d][:, None], bbt_im[d][:, None]
        return pr_ * br - pi_ * bi, pr_ * bi + pi_ * br

    pf_re, pf_im = inject(0, pwt_re[0, :, :t][:, ::-1], pwt_im[0, :, :t][:, ::-1])
    pb_re, pb_im = inject(1, pwt_re[1, :, :t], pwt_im[1, :, :t])
    inj = jnp.concatenate([pf_re, pb_re, pf_im, pb_im], axis=-1).reshape(g, t * hg, 4 * p)

    cr_re, cr_im = flat(cp_re[1][:, :, ::-1]), flat(cp_im[1][:, :, ::-1])
    out = jnp.concatenate([m_tot, cf_re[0][..., hg:], cr_re[..., :t * hg],
                           -cf_im[0][..., hg:], -cr_im[..., :t * hg]], axis=1)
    lb16 = jnp.stack([jnp.concatenate([pw_re[0, :, :, t], pw_re[1, :, :, t]], axis=-1),
                      jnp.concatenate([pw_im[0, :, :, t], pw_im[1, :, :, t]], axis=-1)], axis=0)
    return inj.astype(BF16), out.astype(BF16), lb16


def _block_transpose8(xs):
    lane_blk = lax.broadcasted_iota(jnp.int32, xs[0].shape, 1) >> 4
    xs = list(xs)
    for d in (4, 2, 1):
        hi = (lane_blk & d) != 0
        for i in range(8):
            if i & d:
                continue
            a, b = xs[i], xs[i + d]
            xs[i] = jnp.where(hi, pltpu.roll(b, 16 * d, 1), a)
            xs[i + d] = jnp.where(hi, b, pltpu.roll(a, 128 - 16 * d, 1))
    return xs


def _s5_kernel(u0_ref, u1_ref, inj_ref, out_ref, l_ref, y_ref, ug_ref, zre_ref, zim_ref, hre_ref,
               him_ref, *, nchunk, nctx):
    gh = ug_ref.shape[0]
    ps = S5_STATE
    t = S5_T
    nlat = nchunk - nctx
    for jt, u_ref in enumerate((u0_ref, u1_ref)):
        for q in range(t // 8):
            cols = [u_ref[0, pl.ds(8 * q + tt, nchunk, stride=t), :] for tt in range(8)]
            grp = _block_transpose8(cols)
            for gl in range(8):
                ug_ref[8 * jt + gl, :, 128 * q:128 * (q + 1)] = grp[gl].astype(BF16)
    for gi in range(gh):
        zg = _dot(ug_ref[gi], inj_ref[gi], NN)
        zre_ref[pl.ds(gi, nchunk, stride=gh), :] = zg[:, 0:2 * ps]
        zim_ref[pl.ds(gi, nchunk, stride=gh), :] = zg[:, 2 * ps:4 * ps]

    lane = lax.broadcasted_iota(jnp.int32, (gh, 2 * ps), 1)
    is_f = lane < ps
    lre = l_ref[0, 0]
    lim = l_ref[0, 1]

    def body(i, carry):
        hre, him = carry
        cb = jnp.where(i < nctx, nctx - 1 - i, nchunk + nctx - 1 - i)
        rf = pl.multiple_of(i * gh, gh)
        rb = pl.multiple_of(cb * gh, gh)
        hre_ref[pl.ds(rf, gh), 0:ps] = hre[:, 0:ps]
        hre_ref[pl.ds(rb, gh), ps:2 * ps] = hre[:, ps:2 * ps]
        him_ref[pl.ds(rf, gh), 0:ps] = him[:, 0:ps]
        him_ref[pl.ds(rb, gh), ps:2 * ps] = him[:, ps:2 * ps]
        zre = jnp.where(is_f, zre_ref[pl.ds(rf, gh), :], zre_ref[pl.ds(rb, gh), :])
        zim = jnp.where(is_f, zim_ref[pl.ds(rf, gh), :], zim_ref[pl.ds(rb, gh), :])
        return lre * hre - lim * him + zre, lre * him + lim * hre + zim

    zero = jnp.zeros((gh, 2 * ps), F32)
    lax.fori_loop(0, nchunk, body, (zero, zero))

    for jt in range(gh // 8):
        ys = []
        for gl in range(8):
            gi = 8 * jt + gl
            hre = hre_ref[pl.ds(gi, nchunk, stride=gh), :].astype(BF16)
            him = him_ref[pl.ds(gi, nchunk, stride=gh), :].astype(BF16)
            ys.append(_dot(jnp.concatenate([ug_ref[gi], hre, him], axis=1), out_ref[gi], NN))
        for q in range(t // 8):
            tok = _block_transpose8([y[:, 128 * q:128 * (q + 1)] for y in ys])
            for tt in range(8):
                y_ref[0, jt, pl.ds(8 * q + tt, nlat, stride=t), :] = tok[tt][nctx:]


def _s5(z, consts, ctx_len):
    inj, out, lb16 = consts
    bsz, ltot, _ = z.shape
    nchunk = ltot // S5_T
    nctx = ctx_len // S5_T
    gh = S5_GH
    halves = S5_GROUPS // gh
    w = S5_T * S5_GROUP
    tiles = gh * S5_GROUP // 128
    ps2 = 2 * S5_STATE
    lb = lb16.reshape(2, halves, gh, ps2).transpose(1, 0, 2, 3)
    kern = functools.partial(_s5_kernel, nchunk=nchunk, nctx=nctx)
    u_spec = lambda k: pl.BlockSpec((1, ltot, 128), lambda b, j: (b, 0, COL_U // 128 + tiles * j + k))
    return pl.pallas_call(
        kern,
        grid=(bsz, halves),
        in_specs=[u_spec(0), u_spec(1),
                  pl.BlockSpec((gh, w, 2 * ps2), lambda b, j: (j, 0, 0)),
                  pl.BlockSpec((gh, w + 2 * ps2, w), lambda b, j: (j, 0, 0)),
                  pl.BlockSpec((1, 2, gh, ps2), lambda b, j: (j, 0, 0, 0))],
        out_specs=pl.BlockSpec((1, tiles, ltot - ctx_len, 128), lambda b, j: (b, j, 0, 0)),
        out_shape=jax.ShapeDtypeStruct((bsz, D_S5 // 128, ltot - ctx_len, 128), F32),
        scratch_shapes=[pltpu.VMEM((gh, nchunk, w), BF16),
                        pltpu.VMEM((nchunk * gh, ps2), F32),
                        pltpu.VMEM((nchunk * gh, ps2), F32),
                        pltpu.VMEM((nchunk * gh, ps2), F32),
                        pltpu.VMEM((nchunk * gh, ps2), F32)],
        compiler_params=_cparams(("parallel", "parallel"), 48),
    )(z, z, inj, out, lb)


def _mix_kernel(x_ref, yf_ref, yb_ref, r_ref, k_ref, v_ref, lo_ref, u_ref, ys_ref, mod_ref,
                seg_ref, a0_ref, a2_ref, g2_ref, kaw_ref, rk_ref, lnw_ref, lnb_ref,
                dsk_ref, gluw_ref, glub_ref, wo_ref, g2n_ref, w1_ref, w3_ref, w2_ref, gf_ref, o_ref):
    lo = lo_ref[0]
    seg = seg_ref[...]

    def head_sum(t):
        pw = seg.shape[0]
        return jnp.concatenate([_mm(t[:, i:i + pw], seg, "xa2") for i in range(0, t.shape[1], pw)], axis=1)

    y = yf_ref[0] + yb_ref[0]
    mu = head_sum(y) * (1.0 / HEAD)
    yc = y - mu
    var = head_sum(yc * yc) * (1.0 / HEAD)
    yn = yc * lax.rsqrt(var + RWKV_LN_EPS) * lnw_ref[...] + lnb_ref[...]
    a_f = _sigmoid(a0_ref[0] + _mm(lo, a2_ref[0]))
    a_b = _sigmoid(a0_ref[1] + _mm(lo, a2_ref[1]))
    kaw = kaw_ref[...]
    kd_sum = k_ref[0] * ((1.0 + (a_f - 1.0) * kaw) + (1.0 + (a_b - 1.0) * kaw))
    bonus = head_sum(r_ref[0] * kd_sum * rk_ref[...])
    yn = yn + bonus * v_ref[0]
    gate = _mm(_sigmoid(lo), g2_ref[...])
    rwkv_out = yn * gate

    u = u_ref[0]
    ys = jnp.concatenate([ys_ref[0, i] for i in range(ys_ref.shape[1])], axis=1) + dsk_ref[...] * u
    zg = 0.5 * ys * (1.0 + jnp.tanh(math.sqrt(2.0 / math.pi) * (ys + 0.044715 * (ys * ys * ys))))
    s5_out = zg * _sigmoid(_mm(zg, gluw_ref[...]) + glub_ref[...])

    mix = _mm(rwkv_out, wo_ref[0:D_RWKV, :]) + _mm(s5_out, wo_ref[D_RWKV:, :])
    x1 = x_ref[0] + mod_ref[0, 2:3, :] * mix

    ms = jnp.mean(x1 * x1, axis=-1, keepdims=True)
    h = x1 * lax.rsqrt(ms + NORM_EPS) * g2n_ref[...]
    h = h * (1.0 + mod_ref[0, 4:5, :]) + mod_ref[0, 3:4, :]
    hb = h.astype(BF16)
    h1 = _dot(hb, w1_ref[...], NN)
    h3 = _dot(hb, w3_ref[...], NN)
    act = (h1 * _sigmoid(h1)) * h3
    out = x1 + mod_ref[0, 5:6, :] * _dot(act.astype(BF16), w2_ref[...], NN)
    ms2 = jnp.mean(out * out, axis=-1, keepdims=True)
    o_ref[0] = out * lax.rsqrt(ms2 + NORM_EPS) * gf_ref[...]


def _mix(x, y_f, y_b, rkvc, z, ys5, mod8, seg, a0, a2p, g2p, kaw, rk, lnw, lnb, dsk, gluw, glub, wo,
         g2n, w1, w3, w2, gf, ctx_len, tm):
    bsz, seq, d = x.shape
    assert ctx_len % tm == 0 and seq % tm == 0
    off = ctx_len // tm
    lora_blk = COL_LORA // LORA_W
    u_blk = COL_U // D_S5
    tok = lambda c: pl.BlockSpec((1, tm, D_RWKV), lambda b, j: (b, j + off, c))
    full = lambda a: pl.BlockSpec(a.shape, lambda b, j: (0,) * a.ndim, pipeline_mode=pl.Buffered(1))
    return pl.pallas_call(
        _mix_kernel,
        grid=(bsz, seq // tm),
        in_specs=[pl.BlockSpec((1, tm, d), lambda b, j: (b, j, 0)),
                  tok(0), tok(0), tok(0), tok(1), tok(2),
                  pl.BlockSpec((1, tm, LORA_W), lambda b, j: (b, j + off, lora_blk)),
                  pl.BlockSpec((1, tm, D_S5), lambda b, j: (b, j + off, u_blk)),
                  pl.BlockSpec((1, D_S5 // 128, tm, 128), lambda b, j: (b, 0, j, 0)),
                  pl.BlockSpec((1, 8, d), lambda b, j: (b, 0, 0)),
                  full(seg), full(a0), full(a2p), full(g2p), full(kaw), full(rk), full(lnw), full(lnb),
                  full(dsk), full(gluw), full(glub), full(wo),
                  full(g2n), full(w1), full(w3), full(w2), full(gf)],
        out_specs=pl.BlockSpec((1, tm, d), lambda b, j: (b, j, 0)),
        out_shape=jax.ShapeDtypeStruct((bsz, seq, d), F32),
        compiler_params=_cparams(("parallel", "parallel"), 56),
    )(x, y_f, y_b, rkvc, rkvc, rkvc, z, z, ys5, mod8, seg, a0, a2p, g2p, kaw, rk, lnw, lnb,
      dsk, gluw, glub, wo, g2n, w1, w3, w2, gf)


def _pad_rows(w, row0, rows_total):
    pad = [(0, 0)] * (w.ndim - 2) + [(row0, rows_total - row0 - w.shape[-2]), (0, 0)]
    return jnp.pad(w, pad)


def kernel(x, c, ctx, c_ctx, mod_w, mod_b, norm1_g, norm2_g, w_in, w_out, rwkv_conv, rwkv_w0, rwkv_w2, rwkv_a0, rwkv_a2, rwkv_g2, rwkv_kk, rwkv_ka, rwkv_rk, rwkv_ln_w, rwkv_ln_b, s5_lam_re, s5_lam_im, s5_log_step, s5_b_re, s5_b_im, s5_c_re, s5_c_im, s5_d, s5_glu_w, s5_glu_b, ffn_w1, ffn_w3, ffn_w2, final_g):
    bsz, seq, d = x.shape
    ctx_len = ctx.shape[1]
    ltot = ctx_len + seq
    layer = 0

    cc = jnp.concatenate([c, c_ctx[None, :], jnp.zeros((8 - bsz - 1, d), F32)], axis=0)
    mod = _adaln(cc, mod_w[layer], mod_b[layer])
    mod8 = jnp.pad(mod[:bsz + 1].reshape(bsz + 1, N_MOD, d), ((0, 0), (0, 8 - N_MOD), (0, 0)))

    wi = w_in[layer]
    w_z = jnp.concatenate([wi[:, :3 * D_RWKV], wi[:, -D_S5:], wi[:, 3 * D_RWKV:-D_S5],
                           jnp.zeros((d, D_Z - wi.shape[1]), F32)], axis=1).astype(BF16)
    z = _inproj(ctx, x, mod8, norm1_g[layer], w_z, 256)

    rkvc = _conv(z, rwkv_conv[layer], ctx_len)

    w2p = _pad_rows(rwkv_w2[layer], 0, LORA_W).at[1].set(_pad_rows(rwkv_w2[layer, 1], DECAY_LORA, LORA_W))
    a2p = jnp.stack([_pad_rows(rwkv_a2[layer, 0], LORA_AD, LORA_W),
                     _pad_rows(rwkv_a2[layer, 1], LORA_AD + AAA_LORA, LORA_W)])
    w0 = rwkv_w0[layer].reshape(2, 1, D_RWKV)
    a0 = rwkv_a0[layer].reshape(2, 1, D_RWKV)
    kkw = rwkv_kk[layer].reshape(1, D_RWKV)
    kaw = rwkv_ka[layer].reshape(1, D_RWKV)
    y_f = _rwkv(rkvc, z, w0, w2p, a0, a2p, kkw, kaw, ctx_len, False)
    y_b = _rwkv(rkvc, z, w0, w2p, a0, a2p, kkw, kaw, ctx_len, True)

    consts = _s5_consts(s5_lam_re[layer], s5_lam_im[layer], s5_log_step[layer], s5_b_re[layer],
                        s5_b_im[layer], s5_c_re[layer], s5_c_im[layer])
    ys5 = _s5(z, consts, ctx_len)

    hh = jnp.arange(2 * HEAD) // HEAD
    seg = (hh[:, None] == hh[None, :]).astype(BF16)
    g2p = _pad_rows(rwkv_g2[layer], LORA_GD, LORA_W)
    return _mix(x, y_f, y_b, rkvc, z, ys5, mod8, seg, a0, a2p, g2p, kaw,
                rwkv_rk[layer].reshape(1, D_RWKV), rwkv_ln_w[layer].reshape(1, D_RWKV),
                rwkv_ln_b[layer].reshape(1, D_RWKV), s5_d[layer].reshape(1, D_S5),
                s5_glu_w[layer].astype(BF16), s5_glu_b[layer].reshape(1, D_S5),
                w_out[layer].astype(BF16), norm2_g[layer].reshape(1, d), ffn_w1[layer].astype(BF16),
                ffn_w3[layer].astype(BF16), ffn_w2[layer].astype(BF16), final_g.reshape(1, d), ctx_len, 256)
```

```python
import functools
import math

import jax
import jax.numpy as jnp
from jax import lax
from jax.experimental import pallas as pl
from jax.experimental.pallas import tpu as pltpu

F32 = jnp.float32
BF16 = jnp.bfloat16

D_MODEL = 1024
GRID_W = 64
D_RWKV = 512
HEAD = 64
HEADS = D_RWKV // HEAD
D_S5 = 512
S5_GROUP = 16
S5_GROUPS = D_S5 // S5_GROUP
S5_STATE = 64
DECAY_LORA = 32
AAA_LORA = 32
GATE_LORA = 96
N_MOD = 6
NORM_EPS = 1e-6
RWKV_LN_EPS = 64e-5

LORA_W = 256
COL_U = 3 * D_RWKV
COL_LORA = COL_U + D_S5
D_Z = COL_LORA + LORA_W
LORA_AD = 2 * DECAY_LORA
LORA_GD = LORA_AD + 2 * AAA_LORA

CHUNK = 64
RWKV_BLK = 256
S5_T = 16
S5_GH = 16
S5_TG = 8
CONV_PAD = 72
CONV_TC = 256
CONV_PIECE = 128

NN = (((1,), (0,)), ((), ()))
NT = (((1,), (1,)), ((), ()))
TN = (((0,), (0,)), ((), ()))


def _dot(a, b, dims):
    return lax.dot_general(a, b, dims, preferred_element_type=F32)


def _split2(x):
    hi = x.astype(BF16)
    lo = (x - hi.astype(F32)).astype(BF16)
    return hi, lo


def _split3(x):
    hi = x.astype(BF16)
    r1 = x - hi.astype(F32)
    mid = r1.astype(BF16)
    lo = (r1 - mid.astype(F32)).astype(BF16)
    return hi, mid, lo


def _mm(a, b, mode="bf16", dims=NN):
    if mode == "bf16":
        return _dot(a.astype(BF16), b.astype(BF16), dims)
    if mode == "x3":
        ah, al = _split2(a)
        bh, bl = _split2(b)
        return _dot(ah, bh, dims) + (_dot(ah, bl, dims) + _dot(al, bh, dims))
    if mode == "xa":
        bb = b.astype(BF16)
        a1, a2, a3 = _split3(a)
        return _dot(a1, bb, dims) + (_dot(a2, bb, dims) + _dot(a3, bb, dims))
    if mode == "xa2":
        bb = b.astype(BF16)
        a1, a2 = _split2(a)
        return _dot(a1, bb, dims) + _dot(a2, bb, dims)
    if mode == "xb":
        ab = a.astype(BF16)
        b1, b2, b3 = _split3(b)
        return _dot(ab, b1, dims) + (_dot(ab, b2, dims) + _dot(ab, b3, dims))
    raise ValueError(mode)


def _sigmoid(x):
    return 1.0 / (1.0 + jnp.exp(-x))


def _softplus(x):
    return jnp.maximum(x, 0.0) + jnp.log(1.0 + jnp.exp(-jnp.abs(x)))


def _cparams(sem, vmem_mb):
    return pltpu.CompilerParams(dimension_semantics=sem, vmem_limit_bytes=vmem_mb * 1024 * 1024)


def _adaln_kernel(c_ref, w_ref, b_ref, o_ref):
    c = c_ref[...]
    s = c * _sigmoid(c)
    o_ref[...] = _mm(s, w_ref[...], "x3") + b_ref[...]


def _adaln(cc, mod_w, mod_b):
    rows, d = cc.shape
    n = mod_w.shape[1]
    tn = 1024
    return pl.pallas_call(
        _adaln_kernel,
        grid=(n // tn,),
        in_specs=[pl.BlockSpec((rows, d), lambda j: (0, 0)),
                  pl.BlockSpec((d, tn), lambda j: (0, j)),
                  pl.BlockSpec((1, tn), lambda j: (0, j))],
        out_specs=pl.BlockSpec((rows, tn), lambda j: (0, j)),
        out_shape=jax.ShapeDtypeStruct((rows, n), F32),
        compiler_params=_cparams(("parallel",), 40),
    )(cc, mod_w, mod_b.reshape(1, n))


def _inproj_kernel(c_ref, x_ref, mod_ref, g_ref, w_ref, o_ref, *, ctx_blocks):
    x = jnp.where(pl.program_id(1) < ctx_blocks, c_ref[0], x_ref[0])
    ms = jnp.mean(x * x, axis=-1, keepdims=True)
    xn = x * lax.rsqrt(ms + NORM_EPS) * g_ref[...]
    shift = mod_ref[0, 0:1, :]
    scale = mod_ref[0, 1:2, :]
    h = xn * (1.0 + scale) + shift
    o_ref[0] = _mm(h, w_ref[...], "bf16")


def _inproj(ctx, x, mod8, g1, w_z, tm):
    bsz, seq, d = x.shape
    ctx_blocks = ctx.shape[1] // tm
    dz = w_z.shape[1]
    kern = functools.partial(_inproj_kernel, ctx_blocks=ctx_blocks)
    return pl.pallas_call(
        kern,
        grid=(bsz, ctx_blocks + seq // tm),
        in_specs=[pl.BlockSpec((1, tm, d), lambda b, j: (b, jnp.minimum(j, ctx_blocks - 1), 0)),
                  pl.BlockSpec((1, tm, d), lambda b, j: (b, jnp.maximum(j - ctx_blocks, 0), 0)),
                  pl.BlockSpec((1, 8, d), lambda b, j: (jnp.where(j < ctx_blocks, bsz, b), 0, 0)),
                  pl.BlockSpec((1, d), lambda b, j: (0, 0)),
                  pl.BlockSpec((d, dz), lambda b, j: (0, 0))],
        out_specs=pl.BlockSpec((1, tm, dz), lambda b, j: (b, j, 0)),
        out_shape=jax.ShapeDtypeStruct((bsz, ctx.shape[1] + seq, dz), F32),
        compiler_params=_cparams(("parallel", "parallel"), 48),
    )(ctx, x, mod8, g1.reshape(1, d), w_z)


def _conv_kernel(z_ref, w_ref, o_ref, xs_ref, *, ctx_len, lat_len):
    tc = z_ref.shape[2]
    pc = CONV_PIECE
    ctx_off = CONV_PAD
    lat_off = 2 * CONV_PAD + ctx_len
    zpad = jnp.zeros((CONV_PAD, tc), F32)
    for s in range(3):
        xs_ref[s, 0:CONV_PAD, :] = zpad
        xs_ref[s, ctx_off + ctx_len:lat_off, :] = zpad
        xs_ref[s, lat_off + lat_len:lat_off + lat_len + CONV_PAD, :] = zpad
    xs_ref[1, lat_off:lat_off + 8, :] = zpad[0:8]
    xs_ref[2, lat_off + lat_len - 8:lat_off + lat_len, :] = zpad[0:8]
    col = lax.broadcasted_iota(jnp.int32, (pc, tc), 0) % GRID_W
    keep_l = (col != GRID_W - 1).astype(F32)
    keep_r = (col != 0).astype(F32)
    for p in range(ctx_len // pc):
        xs_ref[0, ctx_off + p * pc:ctx_off + (p + 1) * pc, :] = z_ref[0, p * pc:(p + 1) * pc, :]
    for p in range(lat_len // pc):
        x = z_ref[0, ctx_len + p * pc:ctx_len + (p + 1) * pc, :]
        dst = lat_off + p * pc
        xs_ref[0, dst:dst + pc, :] = x
        xs_ref[1, dst + 1:dst + 1 + pc, :] = x * keep_l
        xs_ref[2, dst - 1:dst - 1 + pc, :] = x * keep_r
    w = [w_ref[t:t + 1, :] for t in range(9)]
    for p in range(ctx_len // pc):
        base = ctx_off + p * pc
        acc = w[3] * xs_ref[0, base - 1:base - 1 + pc, :]
        acc = acc + w[4] * xs_ref[0, base:base + pc, :]
        acc = acc + w[5] * xs_ref[0, base + 1:base + 1 + pc, :]
        o_ref[0, p * pc:(p + 1) * pc, :] = acc
    for p in range(lat_len // pc):
        base = lat_off + p * pc
        acc = None
        for di in range(3):
            for dj in range(3):
                off = base + (di - 1) * GRID_W
                src = (1, 0, 2)[dj]
                term = w[3 * di + dj] * xs_ref[src, off:off + pc, :]
                acc = term if acc is None else acc + term
        o_ref[0, ctx_len + p * pc:ctx_len + (p + 1) * pc, :] = acc


def _conv(z, conv_w, ctx_len):
    bsz, ltot, _ = z.shape
    lat_len = ltot - ctx_len
    ch = 3 * D_RWKV
    rows = 3 * CONV_PAD + ltot
    kern = functools.partial(_conv_kernel, ctx_len=ctx_len, lat_len=lat_len)
    return pl.pallas_call(
        kern,
        grid=(bsz, ch // CONV_TC),
        in_specs=[pl.BlockSpec((1, ltot, CONV_TC), lambda b, j: (b, 0, j)),
                  pl.BlockSpec((9, CONV_TC), lambda b, j: (0, j))],
        out_specs=pl.BlockSpec((1, ltot, CONV_TC), lambda b, j: (b, 0, j)),
        out_shape=jax.ShapeDtypeStruct((bsz, ltot, ch), F32),
        scratch_shapes=[pltpu.VMEM((3, rows, CONV_TC), F32)],
        compiler_params=_cparams(("parallel", "parallel"), 48),
    )(z, conv_w.reshape(9, ch))


def _bd(x, lo):
    return jnp.concatenate([jnp.where(lo, x, 0.0), jnp.where(lo, 0.0, x)], axis=0)


def _tri_inv_pairs(n_mats, size, lo):
    ii = lax.broadcasted_iota(jnp.int32, (size, 2 * size), 0)
    jj = lax.broadcasted_iota(jnp.int32, (size, 2 * size), 1) & (size - 1)
    eye = (ii == jj).astype(F32)
    in2 = (ii >> 1) == (jj >> 1)
    ts = [eye + jnp.where(in2, n, 0.0) for n in n_mats]
    lg = 1
    while (1 << lg) < size:
        same = (ii >> (lg + 1)) == (jj >> (lg + 1))
        diff = (ii >> lg) != (jj >> lg)
        nls = [jnp.where(same, jnp.where(diff, n, 0.0), 0.0) for n in n_mats]
        tmp = [_mm(t, _bd(nl, lo)) for t, nl in zip(ts, nls)]
        ts = [t + _mm(x, _bd(t, lo)) for x, t in zip(tmp, ts)]
        lg += 1
    return ts


def _rwkv_kernel(r_ref, k_ref, v_ref, lo_ref, w0_ref, w2_ref, a0_ref, a2_ref, kkw_ref, kaw_ref,
                 y_ref, h_ref, *, reverse):
    step = pl.program_id(1)

    @pl.when(step == 0)
    def _():
        h_ref[...] = jnp.zeros_like(h_ref)

    r = r_ref[0]
    k = k_ref[0]
    v = v_ref[0]
    lora = lo_ref[0]
    n = r.shape[0]
    c = CHUNK
    nsub = n // c
    lgc = c.bit_length() - 1
    pw = 2 * HEAD
    npair = HEADS // 2

    logit_w = w0_ref[0] + _mm(jnp.tanh(lora), w2_ref[0])
    logw = -_softplus(-logit_w) - 0.5
    lw = -jnp.exp(logw)
    a = _sigmoid(a0_ref[0] + _mm(lora, a2_ref[0]))
    kd = k * (1.0 + (a - 1.0) * kaw_ref[...])
    kk = k * kkw_ref[...]

    ti = lax.broadcasted_iota(jnp.int32, (n, n), 0)
    tj = lax.broadcasted_iota(jnp.int32, (n, n), 1)
    causal = (tj >= ti) if reverse else (tj <= ti)
    tri = jnp.where((ti >> lgc) == (tj >> lgc), jnp.where(causal, 1.0, 0.0), 0.0)
    g = _mm(tri, lw, "xb")

    ci2 = lax.broadcasted_iota(jnp.int32, (c, pw), 0)
    lane2 = lax.broadcasted_iota(jnp.int32, (c, pw), 1)
    cj2 = lane2 & (c - 1)
    lo = lane2 < HEAD
    incl2 = (cj2 >= ci2) if reverse else (cj2 <= ci2)
    strict2 = (cj2 > ci2) if reverse else (cj2 < ci2)
    ri = lax.broadcasted_iota(jnp.int32, (pw, pw), 0)
    rj = lax.broadcasted_iota(jnp.int32, (pw, pw), 1)
    blk = (ri < HEAD) == (rj < HEAD)
    eye_p = (ri == rj).astype(F32)
    zeros_c = jnp.zeros((c, pw), F32)
    zeros_p = jnp.zeros((pw, pw), F32)

    def prep(j):
        rows = slice(j * c, (j + 1) * c)
        lw_j = lw[rows]
        g_j = g[rows]
        tot = jnp.sum(lw_j, axis=0, keepdims=True)
        e_prev = jnp.exp(g_j - lw_j)
        e_neg = jnp.exp(-g_j)
        e_pos = jnp.exp(g_j)
        e_tot = jnp.exp(tot)
        e_end = e_tot * e_neg
        out = []
        for p in range(npair):
            sl = slice(p * pw, (p + 1) * pw)
            kkp = kk[rows, sl]
            sq = kkp * kkp
            n2 = jnp.where(lo, jnp.sum(jnp.where(lo, sq, 0.0), axis=-1, keepdims=True),
                           jnp.sum(jnp.where(lo, 0.0, sq), axis=-1, keepdims=True))
            kkp = kkp * lax.rsqrt(jnp.maximum(n2, 1e-12))
            ka = kkp * a[rows, sl]
            out.append(dict(a_t=-kkp * e_prev[:, sl], b_t=ka * e_neg[:, sl], k_t=kd[rows, sl] * e_neg[:, sl],
                            r_t=r[rows, sl] * e_pos[:, sl], b_e=ka * e_end[:, sl],
                            k_e=kd[rows, sl] * e_end[:, sl], vh=v[rows, sl], g_end=e_tot[:, sl]))
        return out

    def local(us):
        s_bk = [_mm(jnp.concatenate([u["a_t"], u["r_t"]], axis=0),
                    jnp.concatenate([_bd(u["b_t"], lo), _bd(u["k_t"], lo)], axis=0), "bf16", NT)
                for u in us]
        nab = [jnp.where(strict2, s[:c, :pw], 0.0) for s in s_bk]
        aak = [jnp.where(strict2, s[:c, pw:], 0.0) for s in s_bk]
        arbk = [jnp.concatenate([jnp.where(incl2, s[c:, :pw], 0.0), jnp.where(incl2, s[c:, pw:], 0.0)],
                                axis=1) for s in s_bk]
        t_inv = _tri_inv_pairs(nab, c, lo)
        x1 = [_mm(m, _bd(u["vh"], lo)) for m, u in zip(aak, us)]
        uw = [_mm(t, jnp.concatenate([_bd(x, lo), _bd(u["a_t"], lo)], axis=1))
              for t, x, u in zip(t_inv, x1, us)]
        yq = [_mm(m, jnp.concatenate(
            [jnp.concatenate([_bd(w_[:, :pw], lo), _bd(w_[:, pw:], lo)], axis=1),
             jnp.concatenate([_bd(u["vh"], lo), zeros_p], axis=1)], axis=0))
            for m, w_, u in zip(arbk, uw, us)]
        gp = [_mm(jnp.concatenate([u["b_e"], u["k_e"]], axis=0),
                  jnp.concatenate([w_, jnp.concatenate([u["vh"], zeros_c], axis=1)], axis=0), "bf16", TN)
              for w_, u in zip(uw, us)]
        res = []
        for u, yq_, gp_ in zip(us, yq, gp):
            q = u["r_t"] + yq_[:, pw:]
            p_bd = jnp.where(blk, gp_[:, pw:], 0.0) + eye_p * u["g_end"]
            res.append((jnp.concatenate([q, p_bd], axis=0), yq_[:, :pw], jnp.where(blk, gp_[:, :pw], 0.0)))
        return res

    hs = [h_ref[p] for p in range(npair)]
    order = list(range(nsub - 1, -1, -1) if reverse else range(nsub))
    for grp in (order,):
        loc = local([u for j in grp for u in prep(j)])
        for gi, j in enumerate(grp):
            res = [_mm(loc[gi * npair + p][0], hs[p]) for p in range(npair)]
            for p in range(npair):
                y_ref[0, j * c:(j + 1) * c, p * pw:(p + 1) * pw] = res[p][:c] + loc[gi * npair + p][1]
                hs[p] = res[p][c:] + loc[gi * npair + p][2]
    for p in range(npair):
        h_ref[p] = hs[p]


def _rwkv(rkvc, z, w0, w2p, a0, a2p, kkw, kaw, ctx_len, reverse):
    bsz, ltot, _ = rkvc.shape
    nblk = ltot // RWKV_BLK
    nctx = ctx_len // RWKV_BLK
    lora_blk = COL_LORA // LORA_W
    d = 1 if reverse else 0

    def tok(i):
        return jnp.where(i < nctx, nctx - 1 - i, nblk + nctx - 1 - i) if reverse else i

    par3 = lambda b, i: (d, 0, 0)
    kern = functools.partial(_rwkv_kernel, reverse=reverse)
    return pl.pallas_call(
        kern,
        grid=(bsz, nblk),
        in_specs=[pl.BlockSpec((1, RWKV_BLK, D_RWKV), lambda b, i: (b, tok(i), 0)),
                  pl.BlockSpec((1, RWKV_BLK, D_RWKV), lambda b, i: (b, tok(i), 1)),
                  pl.BlockSpec((1, RWKV_BLK, D_RWKV), lambda b, i: (b, tok(i), 2)),
                  pl.BlockSpec((1, RWKV_BLK, LORA_W), lambda b, i: (b, tok(i), lora_blk)),
                  pl.BlockSpec((1, 1, D_RWKV), par3),
                  pl.BlockSpec((1, LORA_W, D_RWKV), par3),
                  pl.BlockSpec((1, 1, D_RWKV), par3),
                  pl.BlockSpec((1, LORA_W, D_RWKV), par3),
                  pl.BlockSpec((1, D_RWKV), lambda b, i: (0, 0)),
                  pl.BlockSpec((1, D_RWKV), lambda b, i: (0, 0))],
        out_specs=pl.BlockSpec((1, RWKV_BLK, D_RWKV), lambda b, i: (b, tok(i), 0)),
        out_shape=jax.ShapeDtypeStruct((bsz, ltot, D_RWKV), F32),
        scratch_shapes=[pltpu.VMEM((HEADS // 2, 2 * HEAD, 2 * HEAD), F32)],
        compiler_params=_cparams(("parallel", "arbitrary"), 48),
    )(rkvc, rkvc, rkvc, z, w0, w2p, a0, a2p, kkw, kaw)


def _s5_tables_kernel(lam_re_ref, lam_im_ref, dt_ref, bt_re_ref, bt_im_ref, c_re_ref, c_im_ref,
                      inj_ref, mt_ref, qt_ref, lre_ref, lim_ref):
    t = S5_T
    hg = S5_GROUP
    ps = S5_STATE
    row = lax.broadcasted_iota(jnp.int32, (t, 2 * ps), 0).astype(F32)
    is_f = lax.broadcasted_iota(jnp.int32, (t, 2 * ps), 1) < ps
    is_f1 = is_f[0:1]
    n_inj = jnp.where(is_f, t - 1.0 - row, row)
    n_out = jnp.where(is_f, row + 1.0, t - row)
    n_lag = jnp.where(is_f, row, t - 1.0 - row)
    for gl in range(lam_re_ref.shape[0]):
        lam_re = lam_re_ref[gl:gl + 1, :]
        lam_im = lam_im_ref[gl:gl + 1, :]
        dt = dt_ref[gl:gl + 1, :]

        def power(n):
            mag = jnp.exp(lam_re * dt * n)
            return mag * jnp.cos(lam_im * dt * n), mag * jnp.sin(lam_im * dt * n)

        lb_re, lb_im = power(1.0)
        den = lam_re * lam_re + lam_im * lam_im
        q_re = ((lb_re - 1.0) * lam_re + lb_im * lam_im) / den
        q_im = (lb_im * lam_re - (lb_re - 1.0) * lam_im) / den
        bt_re = jnp.concatenate([bt_re_ref[gl], bt_re_ref[gl]], axis=1)
        bt_im = jnp.concatenate([bt_im_ref[gl], bt_im_ref[gl]], axis=1)
        bb_re = q_re * bt_re - q_im * bt_im
        bb_im = q_re * bt_im + q_im * bt_re
        c_re = jnp.concatenate([c_re_ref[gl], c_re_ref[gl]], axis=1)
        c_im = jnp.concatenate([c_im_ref[gl], c_im_ref[gl]], axis=1)

        pi_re, pi_im = power(n_inj)
        po_re, po_im = power(n_out)
        pl_re, pl_im = power(n_lag)
        ck_re, ck_im = [], []
        for s in range(t):
            wr, wi = pi_re[s:s + 1], pi_im[s:s + 1]
            rows = slice(s * hg, (s + 1) * hg)
            inj_ref[gl, rows, 0:2 * ps] = (wr * bb_re - wi * bb_im).astype(BF16)
            inj_ref[gl, rows, 2 * ps:4 * ps] = (wr * bb_im + wi * bb_re).astype(BF16)
            wr, wi = po_re[s:s + 1], po_im[s:s + 1]
            qt_ref[gl, rows, 0:2 * ps] = (wr * c_re - wi * c_im).astype(BF16)
            qt_ref[gl, rows, 2 * ps:4 * ps] = (-(wr * c_im + wi * c_re)).astype(BF16)
            wr, wi = pl_re[s:s + 1], pl_im[s:s + 1]
            ck_re.append(wr * c_re - wi * c_im)
            ck_im.append(wr * c_im + wi * c_re)
        ck = jnp.concatenate([jnp.concatenate(ck_re, axis=0), jnp.concatenate(ck_im, axis=0)], axis=1)
        bsel = jnp.concatenate(
            [jnp.concatenate([jnp.where(is_f1, bb_re, 0.0), jnp.where(is_f1, -bb_im, 0.0)], axis=1),
             jnp.concatenate([jnp.where(is_f1, 0.0, bb_re), jnp.where(is_f1, 0.0, -bb_im)], axis=1)], axis=0)
        kt = _mm(bsel, ck, "x3", NT)
        w = t * hg
        zero = jnp.zeros((hg, w), F32)
        lag = (pltpu.roll(jnp.concatenate([kt[0:hg], zero], axis=1), w - hg, 1)
               + jnp.concatenate([kt[hg:2 * hg], zero], axis=1))
        for s in range(t):
            sh = (t - 1 - s) * hg
            blk = lag if sh == 0 else pltpu.roll(lag, 2 * w - sh, 1)
            mt_ref[gl, s * hg:(s + 1) * hg, :] = blk[:, 0:w].astype(BF16)
        l16_re, l16_im = power(float(t))
        lre_ref[gl:gl + 1, :] = l16_re
        lim_ref[gl:gl + 1, :] = l16_im


def _s5_tables(lam_re, lam_im, log_step, b_re, b_im, c_re, c_im):
    g, p, hg = b_re.shape
    w = S5_T * hg
    lanes = lambda x: jnp.concatenate([x[0], x[1]], axis=-1)
    dt = lanes(jnp.broadcast_to(jnp.exp(log_step)[..., None], lam_re.shape))
    bt_re = jnp.transpose(b_re, (0, 2, 1))
    bt_im = jnp.transpose(b_im, (0, 2, 1))
    vec = pl.BlockSpec((S5_TG, 2 * p), lambda j: (j, 0))
    mat = pl.BlockSpec((S5_TG, hg, p), lambda j: (j, 0, 0))
    tab = pl.BlockSpec((S5_TG, w, w), lambda j: (j, 0, 0))
    return pl.pallas_call(
        _s5_tables_kernel,
        grid=(g // S5_TG,),
        in_specs=[vec, vec, vec, mat, mat, mat, mat],
        out_specs=[tab, tab, tab, vec, vec],
        out_shape=[jax.ShapeDtypeStruct((g, w, w), BF16)] * 3 + [jax.ShapeDtypeStruct((g, 2 * p), F32)] * 2,
        compiler_params=_cparams(("parallel",), 32),
    )(lanes(lam_re), lanes(lam_im), dt, bt_re, bt_im, c_re, c_im)


def _block_transpose8(xs):
    lane_blk = lax.broadcasted_iota(jnp.int32, xs[0].shape, 1) >> 4
    xs = list(xs)
    for d in (4, 2, 1):
        hi = (lane_blk & d) != 0
        for i in range(8):
            if i & d:
                continue
            a, b = xs[i], xs[i + d]
            xs[i] = jnp.where(hi, pltpu.roll(b, 16 * d, 1), a)
            xs[i + d] = jnp.where(hi, b, pltpu.roll(a, 128 - 16 * d, 1))
    return xs


def _s5_kernel(u0_ref, u1_ref, inj_ref, mt_ref, qt_ref, lre_ref, lim_ref, y_ref, ug_ref, zre_ref, zim_ref,
               hre_ref, him_ref, *, nchunk, nctx):
    gh = ug_ref.shape[0]
    ps = S5_STATE
    t = S5_T
    nlat = nchunk - nctx
    for jt, u_ref in enumerate((u0_ref, u1_ref)):
        for q in range(t // 8):
            cols = [u_ref[0, pl.ds(8 * q + tt, nchunk, stride=t), :] for tt in range(8)]
            grp = _block_transpose8(cols)
            for gl in range(8):
                ug_ref[8 * jt + gl, :, 128 * q:128 * (q + 1)] = grp[gl].astype(BF16)
    for gi in range(gh):
        zg = _dot(ug_ref[gi], inj_ref[gi], NN)
        zre_ref[pl.ds(gi, nchunk, stride=gh), :] = zg[:, 0:2 * ps]
        zim_ref[pl.ds(gi, nchunk, stride=gh), :] = zg[:, 2 * ps:4 * ps]

    lane = lax.broadcasted_iota(jnp.int32, (gh, 2 * ps), 1)
    is_f = lane < ps
    lre = lre_ref[...]
    lim = lim_ref[...]

    def body(i, carry):
        hre, him = carry
        cb = jnp.where(i < nctx, nctx - 1 - i, nchunk + nctx - 1 - i)
        rf = pl.multiple_of(i * gh, gh)
        rb = pl.multiple_of(cb * gh, gh)
        hre_ref[pl.ds(rf, gh), 0:ps] = hre[:, 0:ps]
        hre_ref[pl.ds(rb, gh), ps:2 * ps] = hre[:, ps:2 * ps]
        him_ref[pl.ds(rf, gh), 0:ps] = him[:, 0:ps]
        him_ref[pl.ds(rb, gh), ps:2 * ps] = him[:, ps:2 * ps]
        zre = jnp.where(is_f, zre_ref[pl.ds(rf, gh), :], zre_ref[pl.ds(rb, gh), :])
        zim = jnp.where(is_f, zim_ref[pl.ds(rf, gh), :], zim_ref[pl.ds(rb, gh), :])
        return lre * hre - lim * him + zre, lre * him + lim * hre + zim

    zero = jnp.zeros((gh, 2 * ps), F32)
    lax.fori_loop(0, nchunk, body, (zero, zero))

    for jt in range(gh // 8):
        ys = []
        for gl in range(8):
            gi = 8 * jt + gl
            hre = hre_ref[pl.ds(gi, nchunk, stride=gh), :].astype(BF16)
            him = him_ref[pl.ds(gi, nchunk, stride=gh), :].astype(BF16)
            ys.append(_dot(ug_ref[gi], mt_ref[gi], NN)
                      + _dot(jnp.concatenate([hre, him], axis=1), qt_ref[gi], NT))
        for q in range(t // 8):
            tok = _block_transpose8([y[:, 128 * q:128 * (q + 1)] for y in ys])
            for tt in range(8):
                y_ref[0, jt, pl.ds(8 * q + tt, nlat, stride=t), :] = tok[tt][nctx:]


def _s5(z, tables, ctx_len):
    inj, mt, qt, lre, lim = tables
    bsz, ltot, _ = z.shape
    nchunk = ltot // S5_T
    nctx = ctx_len // S5_T
    gh = S5_GH
    halves = S5_GROUPS // gh
    w = S5_T * S5_GROUP
    tiles = gh * S5_GROUP // 128
    ps2 = 2 * S5_STATE
    kern = functools.partial(_s5_kernel, nchunk=nchunk, nctx=nctx)
    u_spec = lambda k: pl.BlockSpec((1, ltot, 128), lambda b, j: (b, 0, COL_U // 128 + tiles * j + k))
    tab = pl.BlockSpec((gh, w, w), lambda b, j: (j, 0, 0))
    vec = pl.BlockSpec((gh, ps2), lambda b, j: (j, 0))
    return pl.pallas_call(
        kern,
        grid=(bsz, halves),
        in_specs=[u_spec(0), u_spec(1), tab, tab, tab, vec, vec],
        out_specs=pl.BlockSpec((1, tiles, ltot - ctx_len, 128), lambda b, j: (b, j, 0, 0)),
        out_shape=jax.ShapeDtypeStruct((bsz, D_S5 // 128, ltot - ctx_len, 128), F32),
        scratch_shapes=[pltpu.VMEM((gh, nchunk, w), BF16),
                        pltpu.VMEM((nchunk * gh, ps2), F32),
                        pltpu.VMEM((nchunk * gh, ps2), F32),
                        pltpu.VMEM((nchunk * gh, ps2), F32),
                        pltpu.VMEM((nchunk * gh, ps2), F32)],
        compiler_params=_cparams(("parallel", "parallel"), 48),
    )(z, z, inj, mt, qt, lre, lim)


def _mix_kernel(x_ref, yf_ref, yb_ref, r_ref, k_ref, v_ref, lo_ref, u_ref, ys_ref, mod_ref,
                seg_ref, a0_ref, a2_ref, g2_ref, kaw_ref, rk_ref, lnw_ref, lnb_ref,
                dsk_ref, gluw_ref, glub_ref, wo_ref, g2n_ref, w1_ref, w3_ref, w2_ref, gf_ref, o_ref):
    lo = lo_ref[0]
    seg = seg_ref[...]

    def head_sum(t):
        pw = seg.shape[0]
        return jnp.concatenate([_mm(t[:, i:i + pw], seg, "xa2") for i in range(0, t.shape[1], pw)], axis=1)

    y = yf_ref[0] + yb_ref[0]
    mu = head_sum(y) * (1.0 / HEAD)
    yc = y - mu
    var = head_sum(yc * yc) * (1.0 / HEAD)
    yn = yc * lax.rsqrt(var + RWKV_LN_EPS) * lnw_ref[...] + lnb_ref[...]
    a_f = _sigmoid(a0_ref[0] + _mm(lo, a2_ref[0]))
    a_b = _sigmoid(a0_ref[1] + _mm(lo, a2_ref[1]))
    kaw = kaw_ref[...]
    kd_sum = k_ref[0] * ((1.0 + (a_f - 1.0) * kaw) + (1.0 + (a_b - 1.0) * kaw))
    bonus = head_sum(r_ref[0] * kd_sum * rk_ref[...])
    yn = yn + bonus * v_ref[0]
    gate = _mm(_sigmoid(lo), g2_ref[...])
    rwkv_out = yn * gate

    u = u_ref[0]
    ys = jnp.concatenate([ys_ref[0, i] for i in range(ys_ref.shape[1])], axis=1) + dsk_ref[...] * u
    zg = 0.5 * ys * (1.0 + jnp.tanh(math.sqrt(2.0 / math.pi) * (ys + 0.044715 * (ys * ys * ys))))
    s5_out = zg * _sigmoid(_mm(zg, gluw_ref[...]) + glub_ref[...])

    mix = _mm(rwkv_out, wo_ref[0:D_RWKV, :]) + _mm(s5_out, wo_ref[D_RWKV:, :])
    x1 = x_ref[0] + mod_ref[0, 2:3, :] * mix

    ms = jnp.mean(x1 * x1, axis=-1, keepdims=True)
    h = x1 * lax.rsqrt(ms + NORM_EPS) * g2n_ref[...]
    h = h * (1.0 + mod_ref[0, 4:5, :]) + mod_ref[0, 3:4, :]
    hb = h.astype(BF16)
    h1 = _dot(hb, w1_ref[...], NN)
    h3 = _dot(hb, w3_ref[...], NN)
    act = (h1 * _sigmoid(h1)) * h3
    out = x1 + mod_ref[0, 5:6, :] * _dot(act.astype(BF16), w2_ref[...], NN)
    ms2 = jnp.mean(out * out, axis=-1, keepdims=True)
    o_ref[0] = out * lax.rsqrt(ms2 + NORM_EPS) * gf_ref[...]


def _mix(x, y_f, y_b, rkvc, z, ys5, mod8, seg, a0, a2p, g2p, kaw, rk, lnw, lnb, dsk, gluw, glub, wo,
         g2n, w1, w3, w2, gf, ctx_len, tm):
    bsz, seq, d = x.shape
    assert ctx_len % tm == 0 and seq % tm == 0
    off = ctx_len // tm
    lora_blk = COL_LORA // LORA_W
    u_blk = COL_U // D_S5
    tok = lambda c: pl.BlockSpec((1, tm, D_RWKV), lambda b, j: (b, j + off, c))
    full = lambda a: pl.BlockSpec(a.shape, lambda b, j: (0,) * a.ndim, pipeline_mode=pl.Buffered(1))
    return pl.pallas_call(
        _mix_kernel,
        grid=(bsz, seq // tm),
        in_specs=[pl.BlockSpec((1, tm, d), lambda b, j: (b, j, 0)),
                  tok(0), tok(0), tok(0), tok(1), tok(2),
                  pl.BlockSpec((1, tm, LORA_W), lambda b, j: (b, j + off, lora_blk)),
                  pl.BlockSpec((1, tm, D_S5), lambda b, j: (b, j + off, u_blk)),
                  pl.BlockSpec((1, D_S5 // 128, tm, 128), lambda b, j: (b, 0, j, 0)),
                  pl.BlockSpec((1, 8, d), lambda b, j: (b, 0, 0)),
                  full(seg), full(a0), full(a2p), full(g2p), full(kaw), full(rk), full(lnw), full(lnb),
                  full(dsk), full(gluw), full(glub), full(wo),
                  full(g2n), full(w1), full(w3), full(w2), full(gf)],
        out_specs=pl.BlockSpec((1, tm, d), lambda b, j: (b, j, 0)),
        out_shape=jax.ShapeDtypeStruct((bsz, seq, d), F32),
        compiler_params=_cparams(("parallel", "parallel"), 56),
    )(x, y_f, y_b, rkvc, rkvc, rkvc, z, z, ys5, mod8, seg, a0, a2p, g2p, kaw, rk, lnw, lnb,
      dsk, gluw, glub, wo, g2n, w1, w3, w2, gf)


def _pad_rows(w, row0, rows_total):
    pad = [(0, 0)] * (w.ndim - 2) + [(row0, rows_total - row0 - w.shape[-2]), (0, 0)]
    return jnp.pad(w, pad)


def kernel(x, c, ctx, c_ctx, mod_w, mod_b, norm1_g, norm2_g, w_in, w_out, rwkv_conv, rwkv_w0, rwkv_w2, rwkv_a0, rwkv_a2, rwkv_g2, rwkv_kk, rwkv_ka, rwkv_rk, rwkv_ln_w, rwkv_ln_b, s5_lam_re, s5_lam_im, s5_log_step, s5_b_re, s5_b_im, s5_c_re, s5_c_im, s5_d, s5_glu_w, s5_glu_b, ffn_w1, ffn_w3, ffn_w2, final_g):
    bsz, seq, d = x.shape
    ctx_len = ctx.shape[1]
    ltot = ctx_len + seq
    layer = 0

    cc = jnp.concatenate([c, c_ctx[None, :], jnp.zeros((8 - bsz - 1, d), F32)], axis=0)
    mod = _adaln(cc, mod_w[layer], mod_b[layer])
    mod8 = jnp.pad(mod[:bsz + 1].reshape(bsz + 1, N_MOD, d), ((0, 0), (0, 8 - N_MOD), (0, 0)))

    wi = w_in[layer]
    w_z = jnp.concatenate([wi[:, :3 * D_RWKV], wi[:, -D_S5:], wi[:, 3 * D_RWKV:-D_S5],
                           jnp.zeros((d, D_Z - wi.shape[1]), F32)], axis=1).astype(BF16)
    z = _inproj(ctx, x, mod8, norm1_g[layer], w_z, 256)

    rkvc = _conv(z, rwkv_conv[layer], ctx_len)

    w2p = _pad_rows(rwkv_w2[layer], 0, LORA_W).at[1].set(_pad_rows(rwkv_w2[layer, 1], DECAY_LORA, LORA_W))
    a2p = jnp.stack([_pad_rows(rwkv_a2[layer, 0], LORA_AD, LORA_W),
                     _pad_rows(rwkv_a2[layer, 1], LORA_AD + AAA_LORA, LORA_W)])
    w0 = rwkv_w0[layer].reshape(2, 1, D_RWKV)
    a0 = rwkv_a0[layer].reshape(2, 1, D_RWKV)
    kkw = rwkv_kk[layer].reshape(1, D_RWKV)
    kaw = rwkv_ka[layer].reshape(1, D_RWKV)
    y_f = _rwkv(rkvc, z, w0, w2p, a0, a2p, kkw, kaw, ctx_len, False)
    y_b = _rwkv(rkvc, z, w0, w2p, a0, a2p, kkw, kaw, ctx_len, True)

    tables = _s5_tables(s5_lam_re[layer], s5_lam_im[layer], s5_log_step[layer], s5_b_re[layer],
                        s5_b_im[layer], s5_c_re[layer], s5_c_im[layer])
    ys5 = _s5(z, tables, ctx_len)

    hh = jnp.arange(2 * HEAD) // HEAD
    seg = (hh[:, None] == hh[None, :]).astype(BF16)
    g2p = _pad_rows(rwkv_g2[layer], LORA_GD, LORA_W)
    return _mix(x, y_f, y_b, rkvc, z, ys5, mod8, seg, a0, a2p, g2p, kaw,
                rwkv_rk[layer].reshape(1, D_RWKV), rwkv_ln_w[layer].reshape(1, D_RWKV),
                rwkv_ln_b[layer].reshape(1, D_RWKV), s5_d[layer].reshape(1, D_S5),
                s5_glu_w[layer].astype(BF16), s5_glu_b[layer].reshape(1, D_S5),
                w_out[layer].astype(BF16), norm2_g[layer].reshape(1, d), ffn_w1[layer].astype(BF16),
                ffn_w3[layer].astype(BF16), ffn_w2[layer].astype(BF16), final_g.reshape(1, d), ctx_len, 256)
```

```python
import functools
import math

import jax
import jax.numpy as jnp
from jax import lax
from jax.experimental import pallas as pl
from jax.experimental.pallas import tpu as pltpu

F32 = jnp.float32
BF16 = jnp.bfloat16

D_MODEL = 1024
GRID_W = 64
D_RWKV = 512
HEAD = 64
HEADS = D_RWKV // HEAD
D_S5 = 512
S5_GROUP = 16
S5_GROUPS = D_S5 // S5_GROUP
S5_STATE = 64
DECAY_LORA = 32
AAA_LORA = 32
GATE_LORA = 96
N_MOD = 6
NORM_EPS = 1e-6
RWKV_LN_EPS = 64e-5

LORA_W = 256
COL_LORA = 3 * D_RWKV
D_Z16 = COL_LORA + LORA_W
LORA_AD = 2 * DECAY_LORA
LORA_GD = LORA_AD + 2 * AAA_LORA

CHUNK = 64
RWKV_BLK = 256
S5_T = 16
S5_GH = 16
S5_TG = 8
CONV_PAD = 72
CONV_TC = 256
CONV_PIECE = 128

NN = (((1,), (0,)), ((), ()))
NT = (((1,), (1,)), ((), ()))
TN = (((0,), (0,)), ((), ()))


def _dot(a, b, dims):
    return lax.dot_general(a, b, dims, preferred_element_type=F32)


def _split2(x):
    hi = x.astype(BF16)
    lo = (x - hi.astype(F32)).astype(BF16)
    return hi, lo


def _split3(x):
    hi = x.astype(BF16)
    r1 = x - hi.astype(F32)
    mid = r1.astype(BF16)
    lo = (r1 - mid.astype(F32)).astype(BF16)
    return hi, mid, lo


def _mm(a, b, mode="bf16", dims=NN):
    if mode == "bf16":
        return _dot(a.astype(BF16), b.astype(BF16), dims)
    if mode == "x3":
        ah, al = _split2(a)
        bh, bl = _split2(b)
        return _dot(ah, bh, dims) + (_dot(ah, bl, dims) + _dot(al, bh, dims))
    if mode == "xa":
        bb = b.astype(BF16)
        a1, a2, a3 = _split3(a)
        return _dot(a1, bb, dims) + (_dot(a2, bb, dims) + _dot(a3, bb, dims))
    if mode == "xa2":
        bb = b.astype(BF16)
        a1, a2 = _split2(a)
        return _dot(a1, bb, dims) + _dot(a2, bb, dims)
    if mode == "xb":
        ab = a.astype(BF16)
        b1, b2, b3 = _split3(b)
        return _dot(ab, b1, dims) + (_dot(ab, b2, dims) + _dot(ab, b3, dims))
    raise ValueError(mode)


def _sigmoid(x):
    return 1.0 / (1.0 + jnp.exp(-x))


def _softplus(x):
    return jnp.maximum(x, 0.0) + jnp.log(1.0 + jnp.exp(-jnp.abs(x)))


def _cparams(sem, vmem_mb):
    return pltpu.CompilerParams(dimension_semantics=sem, vmem_limit_bytes=vmem_mb * 1024 * 1024)


def _adaln_kernel(c_ref, w_ref, b_ref, o_ref):
    c = c_ref[...]
    s = c * _sigmoid(c)
    o_ref[...] = _mm(s, w_ref[...], "x3") + b_ref[...]


def _adaln(cc, mod_w, mod_b):
    rows, d = cc.shape
    n = mod_w.shape[1]
    tn = 1024
    return pl.pallas_call(
        _adaln_kernel,
        grid=(n // tn,),
        in_specs=[pl.BlockSpec((rows, d), lambda j: (0, 0)),
                  pl.BlockSpec((d, tn), lambda j: (0, j)),
                  pl.BlockSpec((1, tn), lambda j: (0, j))],
        out_specs=pl.BlockSpec((rows, tn), lambda j: (0, j)),
        out_shape=jax.ShapeDtypeStruct((rows, n), F32),
        compiler_params=_cparams(("parallel",), 40),
    )(cc, mod_w, mod_b.reshape(1, n))


def _inproj_kernel(c_ref, x_ref, mod_ref, g_ref, w_ref, o_ref, u_ref, *, ctx_blocks):
    x = jnp.where(pl.program_id(1) < ctx_blocks, c_ref[0], x_ref[0])
    ms = jnp.mean(x * x, axis=-1, keepdims=True)
    xn = x * lax.rsqrt(ms + NORM_EPS) * g_ref[...]
    shift = mod_ref[0, 0:1, :]
    scale = mod_ref[0, 1:2, :]
    h = xn * (1.0 + scale) + shift
    res = _mm(h, w_ref[...], "bf16")
    o_ref[0] = res[:, :D_Z16].astype(BF16)
    u_ref[0] = res[:, D_Z16:]


def _inproj(ctx, x, mod8, g1, w_z, tm):
    bsz, seq, d = x.shape
    ctx_blocks = ctx.shape[1] // tm
    dz = w_z.shape[1]
    ltot = ctx.shape[1] + seq
    kern = functools.partial(_inproj_kernel, ctx_blocks=ctx_blocks)
    return pl.pallas_call(
        kern,
        grid=(bsz, ctx_blocks + seq // tm),
        in_specs=[pl.BlockSpec((1, tm, d), lambda b, j: (b, jnp.minimum(j, ctx_blocks - 1), 0)),
                  pl.BlockSpec((1, tm, d), lambda b, j: (b, jnp.maximum(j - ctx_blocks, 0), 0)),
                  pl.BlockSpec((1, 8, d), lambda b, j: (jnp.where(j < ctx_blocks, bsz, b), 0, 0)),
                  pl.BlockSpec((1, d), lambda b, j: (0, 0)),
                  pl.BlockSpec((d, dz), lambda b, j: (0, 0))],
        out_specs=[pl.BlockSpec((1, tm, D_Z16), lambda b, j: (b, j, 0)),
                   pl.BlockSpec((1, tm, dz - D_Z16), lambda b, j: (b, j, 0))],
        out_shape=[jax.ShapeDtypeStruct((bsz, ltot, D_Z16), BF16),
                   jax.ShapeDtypeStruct((bsz, ltot, dz - D_Z16), F32)],
        compiler_params=_cparams(("parallel", "parallel"), 48),
    )(ctx, x, mod8, g1.reshape(1, d), w_z)


def _conv_kernel(z_ref, w_ref, o_ref, xs_ref, *, ctx_len, lat_len):
    tc = z_ref.shape[2]
    pc = CONV_PIECE
    ctx_off = CONV_PAD
    lat_off = 2 * CONV_PAD + ctx_len
    zpad = jnp.zeros((CONV_PAD, tc), F32)
    for s in range(3):
        xs_ref[s, 0:CONV_PAD, :] = zpad
        xs_ref[s, ctx_off + ctx_len:lat_off, :] = zpad
        xs_ref[s, lat_off + lat_len:lat_off + lat_len + CONV_PAD, :] = zpad
    xs_ref[1, lat_off:lat_off + 8, :] = zpad[0:8]
    xs_ref[2, lat_off + lat_len - 8:lat_off + lat_len, :] = zpad[0:8]
    col = lax.broadcasted_iota(jnp.int32, (pc, tc), 0) % GRID_W
    keep_l = (col != GRID_W - 1).astype(F32)
    keep_r = (col != 0).astype(F32)
    for p in range(ctx_len // pc):
        xs_ref[0, ctx_off + p * pc:ctx_off + (p + 1) * pc, :] = z_ref[0, p * pc:(p + 1) * pc, :].astype(F32)
    for p in range(lat_len // pc):
        x = z_ref[0, ctx_len + p * pc:ctx_len + (p + 1) * pc, :].astype(F32)
        dst = lat_off + p * pc
        xs_ref[0, dst:dst + pc, :] = x
        xs_ref[1, dst + 1:dst + 1 + pc, :] = x * keep_l
        xs_ref[2, dst - 1:dst - 1 + pc, :] = x * keep_r
    w = [w_ref[t:t + 1, :] for t in range(9)]
    for p in range(ctx_len // pc):
        base = ctx_off + p * pc
        acc = w[3] * xs_ref[0, base - 1:base - 1 + pc, :]
        acc = acc + w[4] * xs_ref[0, base:base + pc, :]
        acc = acc + w[5] * xs_ref[0, base + 1:base + 1 + pc, :]
        o_ref[0, p * pc:(p + 1) * pc, :] = acc
    for p in range(lat_len // pc):
        base = lat_off + p * pc
        acc = None
        for di in range(3):
            for dj in range(3):
                off = base + (di - 1) * GRID_W
                src = (1, 0, 2)[dj]
                term = w[3 * di + dj] * xs_ref[src, off:off + pc, :]
                acc = term if acc is None else acc + term
        o_ref[0, ctx_len + p * pc:ctx_len + (p + 1) * pc, :] = acc


def _conv(z, conv_w, ctx_len):
    bsz, ltot, _ = z.shape
    lat_len = ltot - ctx_len
    ch = 3 * D_RWKV
    rows = 3 * CONV_PAD + ltot
    kern = functools.partial(_conv_kernel, ctx_len=ctx_len, lat_len=lat_len)
    return pl.pallas_call(
        kern,
        grid=(bsz, ch // CONV_TC),
        in_specs=[pl.BlockSpec((1, ltot, CONV_TC), lambda b, j: (b, 0, j)),
                  pl.BlockSpec((9, CONV_TC), lambda b, j: (0, j))],
        out_specs=pl.BlockSpec((1, ltot, CONV_TC), lambda b, j: (b, 0, j)),
        out_shape=jax.ShapeDtypeStruct((bsz, ltot, ch), F32),
        scratch_shapes=[pltpu.VMEM((3, rows, CONV_TC), F32)],
        compiler_params=_cparams(("parallel", "parallel"), 48),
    )(z, conv_w.reshape(9, ch))


def _bd(x, lo):
    return jnp.concatenate([jnp.where(lo, x, 0.0), jnp.where(lo, 0.0, x)], axis=0)


def _tri_inv_pairs(n_mats, size, lo):
    ii = lax.broadcasted_iota(jnp.int32, (size, 2 * size), 0)
    jj = lax.broadcasted_iota(jnp.int32, (size, 2 * size), 1) & (size - 1)
    eye = (ii == jj).astype(F32)
    in2 = (ii >> 1) == (jj >> 1)
    ts = [eye + jnp.where(in2, n, 0.0) for n in n_mats]
    lg = 1
    while (1 << lg) < size:
        same = (ii >> (lg + 1)) == (jj >> (lg + 1))
        diff = (ii >> lg) != (jj >> lg)
        nls = [jnp.where(same, jnp.where(diff, n, 0.0), 0.0) for n in n_mats]
        tmp = [_mm(t, _bd(nl, lo)) for t, nl in zip(ts, nls)]
        ts = [t + _mm(x, _bd(t, lo)) for x, t in zip(tmp, ts)]
        lg += 1
    return ts


def _rwkv_kernel(r_ref, k_ref, v_ref, lo_ref, w0_ref, w2_ref, a0_ref, a2_ref, kkw_ref, kaw_ref,
                 y_ref, h_ref, *, reverse):
    step = pl.program_id(1)

    @pl.when(step == 0)
    def _():
        h_ref[...] = jnp.zeros_like(h_ref)

    r = r_ref[0]
    k = k_ref[0]
    v = v_ref[0]
    lora = lo_ref[0].astype(F32)
    n = r.shape[0]
    c = CHUNK
    nsub = n // c
    lgc = c.bit_length() - 1
    pw = 2 * HEAD
    npair = HEADS // 2

    logit_w = w0_ref[0] + _mm(jnp.tanh(lora), w2_ref[0])
    logw = -_softplus(-logit_w) - 0.5
    lw = -jnp.exp(logw)
    a = _sigmoid(a0_ref[0] + _mm(lora, a2_ref[0]))
    kd = k * (1.0 + (a - 1.0) * kaw_ref[...])
    kk = k * kkw_ref[...]

    ti = lax.broadcasted_iota(jnp.int32, (n, n), 0)
    tj = lax.broadcasted_iota(jnp.int32, (n, n), 1)
    causal = (tj >= ti) if reverse else (tj <= ti)
    tri = jnp.where((ti >> lgc) == (tj >> lgc), jnp.where(causal, 1.0, 0.0), 0.0)
    g = _mm(tri, lw, "xb")

    ci2 = lax.broadcasted_iota(jnp.int32, (c, pw), 0)
    lane2 = lax.broadcasted_iota(jnp.int32, (c, pw), 1)
    cj2 = lane2 & (c - 1)
    lo = lane2 < HEAD
    incl2 = (cj2 >= ci2) if reverse else (cj2 <= ci2)
    strict2 = (cj2 > ci2) if reverse else (cj2 < ci2)
    ri = lax.broadcasted_iota(jnp.int32, (pw, pw), 0)
    rj = lax.broadcasted_iota(jnp.int32, (pw, pw), 1)
    blk = (ri < HEAD) == (rj < HEAD)
    eye_p = (ri == rj).astype(F32)
    zeros_c = jnp.zeros((c, pw), F32)
    zeros_p = jnp.zeros((pw, pw), F32)

    def prep(j):
        rows = slice(j * c, (j + 1) * c)
        lw_j = lw[rows]
        g_j = g[rows]
        tot = jnp.sum(lw_j, axis=0, keepdims=True)
        e_prev = jnp.exp(g_j - lw_j)
        e_neg = jnp.exp(-g_j)
        e_pos = jnp.exp(g_j)
        e_tot = jnp.exp(tot)
        e_end = e_tot * e_neg
        out = []
        for p in range(npair):
            sl = slice(p * pw, (p + 1) * pw)
            kkp = kk[rows, sl]
            sq = kkp * kkp
            n2 = jnp.where(lo, jnp.sum(jnp.where(lo, sq, 0.0), axis=-1, keepdims=True),
                           jnp.sum(jnp.where(lo, 0.0, sq), axis=-1, keepdims=True))
            kkp = kkp * lax.rsqrt(jnp.maximum(n2, 1e-12))
            ka = kkp * a[rows, sl]
            out.append(dict(a_t=-kkp * e_prev[:, sl], b_t=ka * e_neg[:, sl], k_t=kd[rows, sl] * e_neg[:, sl],
                            r_t=r[rows, sl] * e_pos[:, sl], b_e=ka * e_end[:, sl],
                            k_e=kd[rows, sl] * e_end[:, sl], vh=v[rows, sl], g_end=e_tot[:, sl]))
        return out

    def local(us):
        s_bk = [_mm(jnp.concatenate([u["a_t"], u["r_t"]], axis=0),
                    jnp.concatenate([_bd(u["b_t"], lo), _bd(u["k_t"], lo)], axis=0), "bf16", NT)
                for u in us]
        nab = [jnp.where(strict2, s[:c, :pw], 0.0) for s in s_bk]
        aak = [jnp.where(strict2, s[:c, pw:], 0.0) for s in s_bk]
        arbk = [jnp.concatenate([jnp.where(incl2, s[c:, :pw], 0.0), jnp.where(incl2, s[c:, pw:], 0.0)],
                                axis=1) for s in s_bk]
        t_inv = _tri_inv_pairs(nab, c, lo)
        x1 = [_mm(m, _bd(u["vh"], lo)) for m, u in zip(aak, us)]
        uw = [_mm(t, jnp.concatenate([_bd(x, lo), _bd(u["a_t"], lo)], axis=1))
              for t, x, u in zip(t_inv, x1, us)]
        yq = [_mm(m, jnp.concatenate(
            [jnp.concatenate([_bd(w_[:, :pw], lo), _bd(w_[:, pw:], lo)], axis=1),
             jnp.concatenate([_bd(u["vh"], lo), zeros_p], axis=1)], axis=0))
            for m, w_, u in zip(arbk, uw, us)]
        gp = [_mm(jnp.concatenate([u["b_e"], u["k_e"]], axis=0),
                  jnp.concatenate([w_, jnp.concatenate([u["vh"], zeros_c], axis=1)], axis=0), "bf16", TN)
              for w_, u in zip(uw, us)]
        res = []
        for u, yq_, gp_ in zip(us, yq, gp):
            q = u["r_t"] + yq_[:, pw:]
            p_bd = jnp.where(blk, gp_[:, pw:], 0.0) + eye_p * u["g_end"]
            res.append((jnp.concatenate([q, p_bd], axis=0), yq_[:, :pw], jnp.where(blk, gp_[:, :pw], 0.0)))
        return res

    hs = [h_ref[p] for p in range(npair)]
    order = list(range(nsub - 1, -1, -1) if reverse else range(nsub))
    for grp in (order,):
        loc = local([u for j in grp for u in prep(j)])
        for gi, j in enumerate(grp):
            res = [_mm(loc[gi * npair + p][0], hs[p]) for p in range(npair)]
            for p in range(npair):
                y_ref[0, j * c:(j + 1) * c, p * pw:(p + 1) * pw] = res[p][:c] + loc[gi * npair + p][1]
                hs[p] = res[p][c:] + loc[gi * npair + p][2]
    for p in range(npair):
        h_ref[p] = hs[p]


def _rwkv(rkvc, z, w0, w2p, a0, a2p, kkw, kaw, ctx_len, reverse):
    bsz, ltot, _ = rkvc.shape
    nblk = ltot // RWKV_BLK
    nctx = ctx_len // RWKV_BLK
    lora_blk = COL_LORA // LORA_W
    d = 1 if reverse else 0

    def tok(i):
        return jnp.where(i < nctx, nctx - 1 - i, nblk + nctx - 1 - i) if reverse else i

    par3 = lambda b, i: (d, 0, 0)
    kern = functools.partial(_rwkv_kernel, reverse=reverse)
    return pl.pallas_call(
        kern,
        grid=(bsz, nblk),
        in_specs=[pl.BlockSpec((1, RWKV_BLK, D_RWKV), lambda b, i: (b, tok(i), 0)),
                  pl.BlockSpec((1, RWKV_BLK, D_RWKV), lambda b, i: (b, tok(i), 1)),
                  pl.BlockSpec((1, RWKV_BLK, D_RWKV), lambda b, i: (b, tok(i), 2)),
                  pl.BlockSpec((1, RWKV_BLK, LORA_W), lambda b, i: (b, tok(i), lora_blk)),
                  pl.BlockSpec((1, 1, D_RWKV), par3),
                  pl.BlockSpec((1, LORA_W, D_RWKV), par3),
                  pl.BlockSpec((1, 1, D_RWKV), par3),
                  pl.BlockSpec((1, LORA_W, D_RWKV), par3),
                  pl.BlockSpec((1, D_RWKV), lambda b, i: (0, 0)),
                  pl.BlockSpec((1, D_RWKV), lambda b, i: (0, 0))],
        out_specs=pl.BlockSpec((1, RWKV_BLK, D_RWKV), lambda b, i: (b, tok(i), 0)),
        out_shape=jax.ShapeDtypeStruct((bsz, ltot, D_RWKV), F32),
        scratch_shapes=[pltpu.VMEM((HEADS // 2, 2 * HEAD, 2 * HEAD), F32)],
        compiler_params=_cparams(("parallel", "arbitrary"), 48),
    )(rkvc, rkvc, rkvc, z, w0, w2p, a0, a2p, kkw, kaw)


def _s5_tables_kernel(lam_re_ref, lam_im_ref, dt_ref, bt_re_ref, bt_im_ref, c_re_ref, c_im_ref,
                      inj_ref, mt_ref, qt_ref, lre_ref, lim_ref):
    t = S5_T
    hg = S5_GROUP
    ps = S5_STATE
    row = lax.broadcasted_iota(jnp.int32, (t, 2 * ps), 0).astype(F32)
    is_f = lax.broadcasted_iota(jnp.int32, (t, 2 * ps), 1) < ps
    is_f1 = is_f[0:1]
    n_inj = jnp.where(is_f, t - 1.0 - row, row)
    n_out = jnp.where(is_f, row + 1.0, t - row)
    n_lag = jnp.where(is_f, row, t - 1.0 - row)
    for gl in range(lam_re_ref.shape[0]):
        lam_re = lam_re_ref[gl:gl + 1, :]
        lam_im = lam_im_ref[gl:gl + 1, :]
        dt = dt_ref[gl:gl + 1, :]

        def power(n):
            mag = jnp.exp(lam_re * dt * n)
            return mag * jnp.cos(lam_im * dt * n), mag * jnp.sin(lam_im * dt * n)

        lb_re, lb_im = power(1.0)
        den = lam_re * lam_re + lam_im * lam_im
        q_re = ((lb_re - 1.0) * lam_re + lb_im * lam_im) / den
        q_im = (lb_im * lam_re - (lb_re - 1.0) * lam_im) / den
        bt_re = jnp.concatenate([bt_re_ref[gl], bt_re_ref[gl]], axis=1)
        bt_im = jnp.concatenate([bt_im_ref[gl], bt_im_ref[gl]], axis=1)
        bb_re = q_re * bt_re - q_im * bt_im
        bb_im = q_re * bt_im + q_im * bt_re
        c_re = jnp.concatenate([c_re_ref[gl], c_re_ref[gl]], axis=1)
        c_im = jnp.concatenate([c_im_ref[gl], c_im_ref[gl]], axis=1)

        pi_re, pi_im = power(n_inj)
        po_re, po_im = power(n_out)
        pl_re, pl_im = power(n_lag)
        ck_re, ck_im = [], []
        for s in range(t):
            wr, wi = pi_re[s:s + 1], pi_im[s:s + 1]
            rows = slice(s * hg, (s + 1) * hg)
            inj_ref[gl, rows, 0:2 * ps] = (wr * bb_re - wi * bb_im).astype(BF16)
            inj_ref[gl, rows, 2 * ps:4 * ps] = (wr * bb_im + wi * bb_re).astype(BF16)
            wr, wi = po_re[s:s + 1], po_im[s:s + 1]
            qt_ref[gl, rows, 0:2 * ps] = (wr * c_re - wi * c_im).astype(BF16)
            qt_ref[gl, rows, 2 * ps:4 * ps] = (-(wr * c_im + wi * c_re)).astype(BF16)
            wr, wi = pl_re[s:s + 1], pl_im[s:s + 1]
            ck_re.append(wr * c_re - wi * c_im)
            ck_im.append(wr * c_im + wi * c_re)
        ck = jnp.concatenate([jnp.concatenate(ck_re, axis=0), jnp.concatenate(ck_im, axis=0)], axis=1)
        bsel = jnp.concatenate(
            [jnp.concatenate([jnp.where(is_f1, bb_re, 0.0), jnp.where(is_f1, -bb_im, 0.0)], axis=1),
             jnp.concatenate([jnp.where(is_f1, 0.0, bb_re), jnp.where(is_f1, 0.0, -bb_im)], axis=1)], axis=0)
        kt = _mm(bsel, ck, "x3", NT)
        w = t * hg
        zero = jnp.zeros((hg, w), F32)
        lag = (pltpu.roll(jnp.concatenate([kt[0:hg], zero], axis=1), w - hg, 1)
               + jnp.concatenate([kt[hg:2 * hg], zero], axis=1))
        for s in range(t):
            sh = (t - 1 - s) * hg
            blk = lag if sh == 0 else pltpu.roll(lag, 2 * w - sh, 1)
            mt_ref[gl, s * hg:(s + 1) * hg, :] = blk[:, 0:w].astype(BF16)
        l16_re, l16_im = power(float(t))
        lre_ref[gl:gl + 1, :] = l16_re
        lim_ref[gl:gl + 1, :] = l16_im


def _s5_tables(lam_re, lam_im, log_step, b_re, b_im, c_re, c_im):
    g, p, hg = b_re.shape
    w = S5_T * hg
    lanes = lambda x: jnp.concatenate([x[0], x[1]], axis=-1)
    dt = lanes(jnp.broadcast_to(jnp.exp(log_step)[..., None], lam_re.shape))
    bt_re = jnp.transpose(b_re, (0, 2, 1))
    bt_im = jnp.transpose(b_im, (0, 2, 1))
    vec = pl.BlockSpec((S5_TG, 2 * p), lambda j: (j, 0))
    mat = pl.BlockSpec((S5_TG, hg, p), lambda j: (j, 0, 0))
    tab = pl.BlockSpec((S5_TG, w, w), lambda j: (j, 0, 0))
    return pl.pallas_call(
        _s5_tables_kernel,
        grid=(g // S5_TG,),
        in_specs=[vec, vec, vec, mat, mat, mat, mat],
        out_specs=[tab, tab, tab, vec, vec],
        out_shape=[jax.ShapeDtypeStruct((g, w, w), BF16)] * 3 + [jax.ShapeDtypeStruct((g, 2 * p), F32)] * 2,
        compiler_params=_cparams(("parallel",), 32),
    )(lanes(lam_re), lanes(lam_im), dt, bt_re, bt_im, c_re, c_im)


def _block_transpose8(xs):
    lane_blk = lax.broadcasted_iota(jnp.int32, xs[0].shape, 1) >> 4
    xs = list(xs)
    for d in (4, 2, 1):
        hi = (lane_blk & d) != 0
        for i in range(8):
            if i & d:
                continue
            a, b = xs[i], xs[i + d]
            xs[i] = jnp.where(hi, pltpu.roll(b, 16 * d, 1), a)
            xs[i + d] = jnp.where(hi, b, pltpu.roll(a, 128 - 16 * d, 1))
    return xs


def _s5_kernel(u0_ref, u1_ref, inj_ref, mt_ref, qt_ref, lre_ref, lim_ref, y_ref, ug_ref, zre_ref, zim_ref,
               hre_ref, him_ref, *, nchunk, nctx):
    gh = ug_ref.shape[0]
    ps = S5_STATE
    t = S5_T
    nlat = nchunk - nctx
    for jt, u_ref in enumerate((u0_ref, u1_ref)):
        for q in range(t // 8):
            cols = [u_ref[0, pl.ds(8 * q + tt, nchunk, stride=t), :] for tt in range(8)]
            grp = _block_transpose8(cols)
            for gl in range(8):
                ug_ref[8 * jt + gl, :, 128 * q:128 * (q + 1)] = grp[gl].astype(BF16)
    for gi in range(gh):
        zg = _dot(ug_ref[gi], inj_ref[gi], NN)
        zre_ref[pl.ds(gi, nchunk, stride=gh), :] = zg[:, 0:2 * ps]
        zim_ref[pl.ds(gi, nchunk, stride=gh), :] = zg[:, 2 * ps:4 * ps]

    lane = lax.broadcasted_iota(jnp.int32, (gh, 2 * ps), 1)
    is_f = lane < ps
    lre = lre_ref[...]
    lim = lim_ref[...]

    def body(i, carry):
        hre, him = carry
        cb = jnp.where(i < nctx, nctx - 1 - i, nchunk + nctx - 1 - i)
        rf = pl.multiple_of(i * gh, gh)
        rb = pl.multiple_of(cb * gh, gh)
        hre_ref[pl.ds(rf, gh), 0:ps] = hre[:, 0:ps]
        hre_ref[pl.ds(rb, gh), ps:2 * ps] = hre[:, ps:2 * ps]
        him_ref[pl.ds(rf, gh), 0:ps] = him[:, 0:ps]
        him_ref[pl.ds(rb, gh), ps:2 * ps] = him[:, ps:2 * ps]
        zre = jnp.where(is_f, zre_ref[pl.ds(rf, gh), :], zre_ref[pl.ds(rb, gh), :])
        zim = jnp.where(is_f, zim_ref[pl.ds(rf, gh), :], zim_ref[pl.ds(rb, gh), :])
        return lre * hre - lim * him + zre, lre * him + lim * hre + zim

    zero = jnp.zeros((gh, 2 * ps), F32)
    lax.fori_loop(0, nchunk, body, (zero, zero))

    for jt in range(gh // 8):
        ys = []
        for gl in range(8):
            gi = 8 * jt + gl
            hre = hre_ref[pl.ds(gi, nchunk, stride=gh), :].astype(BF16)
            him = him_ref[pl.ds(gi, nchunk, stride=gh), :].astype(BF16)
            ys.append(_dot(ug_ref[gi], mt_ref[gi], NN)
                      + _dot(jnp.concatenate([hre, him], axis=1), qt_ref[gi], NT))
        for q in range(t // 8):
            tok = _block_transpose8([y[:, 128 * q:128 * (q + 1)] for y in ys])
            for tt in range(8):
                y_ref[0, jt, pl.ds(8 * q + tt, nlat, stride=t), :] = tok[tt][nctx:]


def _s5(u, tables, ctx_len):
    inj, mt, qt, lre, lim = tables
    bsz, ltot, _ = u.shape
    nchunk = ltot // S5_T
    nctx = ctx_len // S5_T
    gh = S5_GH
    halves = S5_GROUPS // gh
    w = S5_T * S5_GROUP
    tiles = gh * S5_GROUP // 128
    ps2 = 2 * S5_STATE
    kern = functools.partial(_s5_kernel, nchunk=nchunk, nctx=nctx)
    u_spec = lambda k: pl.BlockSpec((1, ltot, 128), lambda b, j: (b, 0, tiles * j + k))
    tab = pl.BlockSpec((gh, w, w), lambda b, j: (j, 0, 0))
    vec = pl.BlockSpec((gh, ps2), lambda b, j: (j, 0))
    return pl.pallas_call(
        kern,
        grid=(bsz, halves),
        in_specs=[u_spec(0), u_spec(1), tab, tab, tab, vec, vec],
        out_specs=pl.BlockSpec((1, tiles, ltot - ctx_len, 128), lambda b, j: (b, j, 0, 0)),
        out_shape=jax.ShapeDtypeStruct((bsz, D_S5 // 128, ltot - ctx_len, 128), F32),
        scratch_shapes=[pltpu.VMEM((gh, nchunk, w), BF16),
                        pltpu.VMEM((nchunk * gh, ps2), F32),
                        pltpu.VMEM((nchunk * gh, ps2), F32),
                        pltpu.VMEM((nchunk * gh, ps2), F32),
                        pltpu.VMEM((nchunk * gh, ps2), F32)],
        compiler_params=_cparams(("parallel", "parallel"), 48),
    )(u, u, inj, mt, qt, lre, lim)


def _mix_kernel(x_ref, yf_ref, yb_ref, r_ref, k_ref, v_ref, lo_ref, u_ref, ys_ref, mod_ref,
                seg_ref, a0_ref, a2_ref, g2_ref, kaw_ref, rk_ref, lnw_ref, lnb_ref,
                dsk_ref, gluw_ref, glub_ref, wo_ref, g2n_ref, w1_ref, w3_ref, w2_ref, gf_ref, o_ref):
    lo = lo_ref[0].astype(F32)
    seg = seg_ref[...]

    def head_sum(t):
        pw = seg.shape[0]
        return jnp.concatenate([_mm(t[:, i:i + pw], seg, "xa2") for i in range(0, t.shape[1], pw)], axis=1)

    y = yf_ref[0] + yb_ref[0]
    mu = head_sum(y) * (1.0 / HEAD)
    yc = y - mu
    var = head_sum(yc * yc) * (1.0 / HEAD)
    yn = yc * lax.rsqrt(var + RWKV_LN_EPS) * lnw_ref[...] + lnb_ref[...]
    a_f = _sigmoid(a0_ref[0] + _mm(lo, a2_ref[0]))
    a_b = _sigmoid(a0_ref[1] + _mm(lo, a2_ref[1]))
    kaw = kaw_ref[...]
    kd_sum = k_ref[0] * ((1.0 + (a_f - 1.0) * kaw) + (1.0 + (a_b - 1.0) * kaw))
    bonus = head_sum(r_ref[0] * kd_sum * rk_ref[...])
    yn = yn + bonus * v_ref[0]
    gate = _mm(_sigmoid(lo), g2_ref[...])
    rwkv_out = yn * gate

    u = u_ref[0]
    ys = jnp.concatenate([ys_ref[0, i] for i in range(ys_ref.shape[1])], axis=1) + dsk_ref[...] * u
    zg = 0.5 * ys * (1.0 + jnp.tanh(math.sqrt(2.0 / math.pi) * (ys + 0.044715 * (ys * ys * ys))))
    s5_out = zg * _sigmoid(_mm(zg, gluw_ref[...]) + glub_ref[...])

    mix = _mm(rwkv_out, wo_ref[0:D_RWKV, :]) + _mm(s5_out, wo_ref[D_RWKV:, :])
    x1 = x_ref[0] + mod_ref[0, 2:3, :] * mix

    ms = jnp.mean(x1 * x1, axis=-1, keepdims=True)
    h = x1 * lax.rsqrt(ms + NORM_EPS) * g2n_ref[...]
    h = h * (1.0 + mod_ref[0, 4:5, :]) + mod_ref[0, 3:4, :]
    hb = h.astype(BF16)
    h1 = _dot(hb, w1_ref[...], NN)
    h3 = _dot(hb, w3_ref[...], NN)
    act = (h1 * _sigmoid(h1)) * h3
    out = x1 + mod_ref[0, 5:6, :] * _dot(act.astype(BF16), w2_ref[...], NN)
    ms2 = jnp.mean(out * out, axis=-1, keepdims=True)
    o_ref[0] = out * lax.rsqrt(ms2 + NORM_EPS) * gf_ref[...]


def _mix(x, y_f, y_b, rkvc, z, u, ys5, mod8, seg, a0, a2p, g2p, kaw, rk, lnw, lnb, dsk, gluw, glub, wo,
         g2n, w1, w3, w2, gf, ctx_len, tm):
    bsz, seq, d = x.shape
    assert ctx_len % tm == 0 and seq % tm == 0
    off = ctx_len // tm
    lora_blk = COL_LORA // LORA_W
    tok = lambda c: pl.BlockSpec((1, tm, D_RWKV), lambda b, j: (b, j + off, c))
    full = lambda a: pl.BlockSpec(a.shape, lambda b, j: (0,) * a.ndim, pipeline_mode=pl.Buffered(1))
    return pl.pallas_call(
        _mix_kernel,
        grid=(bsz, seq // tm),
        in_specs=[pl.BlockSpec((1, tm, d), lambda b, j: (b, j, 0)),
                  tok(0), tok(0), tok(0), tok(1), tok(2),
                  pl.BlockSpec((1, tm, LORA_W), lambda b, j: (b, j + off, lora_blk)),
                  pl.BlockSpec((1, tm, D_S5), lambda b, j: (b, j + off, 0)),
                  pl.BlockSpec((1, D_S5 // 128, tm, 128), lambda b, j: (b, 0, j, 0)),
                  pl.BlockSpec((1, 8, d), lambda b, j: (b, 0, 0)),
                  full(seg), full(a0), full(a2p), full(g2p), full(kaw), full(rk), full(lnw), full(lnb),
                  full(dsk), full(gluw), full(glub), full(wo),
                  full(g2n), full(w1), full(w3), full(w2), full(gf)],
        out_specs=pl.BlockSpec((1, tm, d), lambda b, j: (b, j, 0)),
        out_shape=jax.ShapeDtypeStruct((bsz, seq, d), F32),
        compiler_params=_cparams(("parallel", "parallel"), 56),
    )(x, y_f, y_b, rkvc, rkvc, rkvc, z, u, ys5, mod8, seg, a0, a2p, g2p, kaw, rk, lnw, lnb,
      dsk, gluw, glub, wo, g2n, w1, w3, w2, gf)


def _pad_rows(w, row0, rows_total):
    pad = [(0, 0)] * (w.ndim - 2) + [(row0, rows_total - row0 - w.shape[-2]), (0, 0)]
    return jnp.pad(w, pad)


def kernel(x, c, ctx, c_ctx, mod_w, mod_b, norm1_g, norm2_g, w_in, w_out, rwkv_conv, rwkv_w0, rwkv_w2, rwkv_a0, rwkv_a2, rwkv_g2, rwkv_kk, rwkv_ka, rwkv_rk, rwkv_ln_w, rwkv_ln_b, s5_lam_re, s5_lam_im, s5_log_step, s5_b_re, s5_b_im, s5_c_re, s5_c_im, s5_d, s5_glu_w, s5_glu_b, ffn_w1, ffn_w3, ffn_w2, final_g):
    bsz, seq, d = x.shape
    ctx_len = ctx.shape[1]
    ltot = ctx_len + seq
    layer = 0

    cc = jnp.concatenate([c, c_ctx[None, :], jnp.zeros((8 - bsz - 1, d), F32)], axis=0)
    mod = _adaln(cc, mod_w[layer], mod_b[layer])
    mod8 = jnp.pad(mod[:bsz + 1].reshape(bsz + 1, N_MOD, d), ((0, 0), (0, 8 - N_MOD), (0, 0)))

    wi = w_in[layer]
    w_z = jnp.concatenate([wi[:, :-D_S5], jnp.zeros((d, D_Z16 + D_S5 - wi.shape[1]), F32),
                           wi[:, -D_S5:]], axis=1).astype(BF16)
    z, u = _inproj(ctx, x, mod8, norm1_g[layer], w_z, 256)

    rkvc = _conv(z, rwkv_conv[layer], ctx_len)

    w2p = _pad_rows(rwkv_w2[layer], 0, LORA_W).at[1].set(_pad_rows(rwkv_w2[layer, 1], DECAY_LORA, LORA_W))
    a2p = jnp.stack([_pad_rows(rwkv_a2[layer, 0], LORA_AD, LORA_W),
                     _pad_rows(rwkv_a2[layer, 1], LORA_AD + AAA_LORA, LORA_W)])
    w0 = rwkv_w0[layer].reshape(2, 1, D_RWKV)
    a0 = rwkv_a0[layer].reshape(2, 1, D_RWKV)
    kkw = rwkv_kk[layer].reshape(1, D_RWKV)
    kaw = rwkv_ka[layer].reshape(1, D_RWKV)
    y_f = _rwkv(rkvc, z, w0, w2p, a0, a2p, kkw, kaw, ctx_len, False)
    y_b = _rwkv(rkvc, z, w0, w2p, a0, a2p, kkw, kaw, ctx_len, True)

    tables = _s5_tables(s5_lam_re[layer], s5_lam_im[layer], s5_log_step[layer], s5_b_re[layer],
                        s5_b_im[layer], s5_c_re[layer], s5_c_im[layer])
    ys5 = _s5(u, tables, ctx_len)

    hh = jnp.arange(2 * HEAD) // HEAD
    seg = (hh[:, None] == hh[None, :]).astype(BF16)
    g2p = _pad_rows(rwkv_g2[layer], LORA_GD, LORA_W)
    return _mix(x, y_f, y_b, rkvc, z, u, ys5, mod8, seg, a0, a2p, g2p, kaw,
                rwkv_rk[layer].reshape(1, D_RWKV), rwkv_ln_w[layer].reshape(1, D_RWKV),
                rwkv_ln_b[layer].reshape(1, D_RWKV), s5_d[layer].reshape(1, D_S5),
                s5_glu_w[layer].astype(BF16), s5_glu_b[layer].reshape(1, D_S5),
                w_out[layer].astype(BF16), norm2_g[layer].reshape(1, d), ffn_w1[layer].astype(BF16),
                ffn_w3[layer].astype(BF16), ffn_w2[layer].astype(BF16), final_g.reshape(1, d), ctx_len, 256)
```

```python
import functools
import math

import jax
import jax.numpy as jnp
from jax import lax
from jax.experimental import pallas as pl
from jax.experimental.pallas import tpu as pltpu

F32 = jnp.float32
BF16 = jnp.bfloat16

D_MODEL = 1024
GRID_W = 64
D_RWKV = 512
HEAD = 64
HEADS = D_RWKV // HEAD
D_S5 = 512
S5_GROUP = 16
S5_GROUPS = D_S5 // S5_GROUP
S5_STATE = 64
DECAY_LORA = 32
AAA_LORA = 32
GATE_LORA = 96
N_MOD = 6
NORM_EPS = 1e-6
RWKV_LN_EPS = 64e-5

LORA_W = 256
COL_LORA = 3 * D_RWKV
D_Z16 = COL_LORA + LORA_W
LORA_AD = 2 * DECAY_LORA
LORA_GD = LORA_AD + 2 * AAA_LORA

CHUNK = 64
RWKV_BLK = 256
S5_T = 16
S5_GH = 16
S5_TG = 8
CONV_PAD = 72
CONV_TC = 256
CONV_PIECE = 128

NN = (((1,), (0,)), ((), ()))
NT = (((1,), (1,)), ((), ()))
TN = (((0,), (0,)), ((), ()))


def _dot(a, b, dims):
    return lax.dot_general(a, b, dims, preferred_element_type=F32)


def _split2(x):
    hi = x.astype(BF16)
    lo = (x - hi.astype(F32)).astype(BF16)
    return hi, lo


def _mm(a, b, mode="bf16", dims=NN):
    if mode == "bf16":
        return _dot(a.astype(BF16), b.astype(BF16), dims)
    if mode == "x3":
        ah, al = _split2(a)
        bh, bl = _split2(b)
        return _dot(ah, bh, dims) + (_dot(ah, bl, dims) + _dot(al, bh, dims))
    if mode == "xa2":
        bb = b.astype(BF16)
        a1, a2 = _split2(a)
        return _dot(a1, bb, dims) + _dot(a2, bb, dims)
    if mode == "xb2":
        ab = a.astype(BF16)
        b1, b2 = _split2(b)
        return _dot(ab, b1, dims) + _dot(ab, b2, dims)
    raise ValueError(mode)


def _sigmoid(x):
    return 1.0 / (1.0 + jnp.exp(-x))


def _cparams(sem, vmem_mb):
    return pltpu.CompilerParams(dimension_semantics=sem, vmem_limit_bytes=vmem_mb * 1024 * 1024)


def _adaln_kernel(c_ref, w_ref, b_ref, o_ref):
    c = c_ref[...]
    s = c * _sigmoid(c)
    o_ref[...] = _mm(s, w_ref[...], "x3") + b_ref[...]


def _adaln(cc, mod_w, mod_b):
    rows, d = cc.shape
    n = mod_w.shape[1]
    tn = 1024
    return pl.pallas_call(
        _adaln_kernel,
        grid=(n // tn,),
        in_specs=[pl.BlockSpec((rows, d), lambda j: (0, 0)),
                  pl.BlockSpec((d, tn), lambda j: (0, j)),
                  pl.BlockSpec((1, tn), lambda j: (0, j))],
        out_specs=pl.BlockSpec((rows, tn), lambda j: (0, j)),
        out_shape=jax.ShapeDtypeStruct((rows, n), F32),
        compiler_params=_cparams(("parallel",), 40),
    )(cc, mod_w, mod_b.reshape(1, n))


def _inproj_kernel(c_ref, x_ref, mod_ref, g_ref, w_ref, o_ref, u_ref, *, ctx_blocks):
    x = jnp.where(pl.program_id(1) < ctx_blocks, c_ref[0], x_ref[0])
    ms = jnp.mean(x * x, axis=-1, keepdims=True)
    xn = x * lax.rsqrt(ms + NORM_EPS) * g_ref[...]
    shift = mod_ref[0, 0:1, :]
    scale = mod_ref[0, 1:2, :]
    h = xn * (1.0 + scale) + shift
    res = _mm(h, w_ref[...], "bf16")
    o_ref[0] = res[:, :D_Z16].astype(BF16)
    u_ref[0] = res[:, D_Z16:]


def _inproj(ctx, x, mod8, g1, w_z, tm):
    bsz, seq, d = x.shape
    ctx_blocks = ctx.shape[1] // tm
    dz = w_z.shape[1]
    ltot = ctx.shape[1] + seq
    kern = functools.partial(_inproj_kernel, ctx_blocks=ctx_blocks)
    return pl.pallas_call(
        kern,
        grid=(bsz, ctx_blocks + seq // tm),
        in_specs=[pl.BlockSpec((1, tm, d), lambda b, j: (b, jnp.minimum(j, ctx_blocks - 1), 0)),
                  pl.BlockSpec((1, tm, d), lambda b, j: (b, jnp.maximum(j - ctx_blocks, 0), 0)),
                  pl.BlockSpec((1, 8, d), lambda b, j: (jnp.where(j < ctx_blocks, bsz, b), 0, 0)),
                  pl.BlockSpec((1, d), lambda b, j: (0, 0)),
                  pl.BlockSpec((d, dz), lambda b, j: (0, 0))],
        out_specs=[pl.BlockSpec((1, tm, D_Z16), lambda b, j: (b, j, 0)),
                   pl.BlockSpec((1, tm, dz - D_Z16), lambda b, j: (b, j, 0))],
        out_shape=[jax.ShapeDtypeStruct((bsz, ltot, D_Z16), BF16),
                   jax.ShapeDtypeStruct((bsz, ltot, dz - D_Z16), F32)],
        compiler_params=_cparams(("parallel", "parallel"), 48),
    )(ctx, x, mod8, g1.reshape(1, d), w_z)


def _conv_kernel(z_ref, w_ref, o_ref, xs_ref, *, ctx_len, lat_len):
    tc = z_ref.shape[2]
    pc = CONV_PIECE
    ctx_off = CONV_PAD
    lat_off = 2 * CONV_PAD + ctx_len
    zpad = jnp.zeros((CONV_PAD, tc), F32)
    for s in range(3):
        xs_ref[s, 0:CONV_PAD, :] = zpad
        xs_ref[s, ctx_off + ctx_len:lat_off, :] = zpad
        xs_ref[s, lat_off + lat_len:lat_off + lat_len + CONV_PAD, :] = zpad
    xs_ref[1, lat_off:lat_off + 8, :] = zpad[0:8]
    xs_ref[2, lat_off + lat_len - 8:lat_off + lat_len, :] = zpad[0:8]
    col = lax.broadcasted_iota(jnp.int32, (pc, tc), 0) % GRID_W
    keep_l = (col != GRID_W - 1).astype(F32)
    keep_r = (col != 0).astype(F32)
    for p in range(ctx_len // pc):
        xs_ref[0, ctx_off + p * pc:ctx_off + (p + 1) * pc, :] = z_ref[0, p * pc:(p + 1) * pc, :].astype(F32)
    for p in range(lat_len // pc):
        x = z_ref[0, ctx_len + p * pc:ctx_len + (p + 1) * pc, :].astype(F32)
        dst = lat_off + p * pc
        xs_ref[0, dst:dst + pc, :] = x
        xs_ref[1, dst + 1:dst + 1 + pc, :] = x * keep_l
        xs_ref[2, dst - 1:dst - 1 + pc, :] = x * keep_r
    w = [w_ref[t:t + 1, :] for t in range(9)]
    for p in range(ctx_len // pc):
        base = ctx_off + p * pc
        acc = w[3] * xs_ref[0, base - 1:base - 1 + pc, :]
        acc = acc + w[4] * xs_ref[0, base:base + pc, :]
        acc = acc + w[5] * xs_ref[0, base + 1:base + 1 + pc, :]
        o_ref[0, p * pc:(p + 1) * pc, :] = acc
    for p in range(lat_len // pc):
        base = lat_off + p * pc
        acc = None
        for di in range(3):
            for dj in range(3):
                off = base + (di - 1) * GRID_W
                src = (1, 0, 2)[dj]
                term = w[3 * di + dj] * xs_ref[src, off:off + pc, :]
                acc = term if acc is None else acc + term
        o_ref[0, ctx_len + p * pc:ctx_len + (p + 1) * pc, :] = acc


def _conv(z, conv_w, ctx_len):
    bsz, ltot, _ = z.shape
    lat_len = ltot - ctx_len
    ch = 3 * D_RWKV
    rows = 3 * CONV_PAD + ltot
    kern = functools.partial(_conv_kernel, ctx_len=ctx_len, lat_len=lat_len)
    return pl.pallas_call(
        kern,
        grid=(bsz, ch // CONV_TC),
        in_specs=[pl.BlockSpec((1, ltot, CONV_TC), lambda b, j: (b, 0, j)),
                  pl.BlockSpec((9, CONV_TC), lambda b, j: (0, j))],
        out_specs=pl.BlockSpec((1, ltot, CONV_TC), lambda b, j: (b, 0, j)),
        out_shape=jax.ShapeDtypeStruct((bsz, ltot, ch), F32),
        scratch_shapes=[pltpu.VMEM((3, rows, CONV_TC), F32)],
        compiler_params=_cparams(("parallel", "parallel"), 48),
    )(z, conv_w.reshape(9, ch))


def _bd(x, lo):
    return jnp.concatenate([jnp.where(lo, x, 0.0), jnp.where(lo, 0.0, x)], axis=0)


def _tri_inv_pairs(n_mats, size, lo):
    ii = lax.broadcasted_iota(jnp.int32, (size, 2 * size), 0)
    jj = lax.broadcasted_iota(jnp.int32, (size, 2 * size), 1) & (size - 1)
    eye = (ii == jj).astype(F32)
    in2 = (ii >> 1) == (jj >> 1)
    ts = [eye + jnp.where(in2, n, 0.0) for n in n_mats]
    lg = 1
    while (1 << lg) < size:
        same = (ii >> (lg + 1)) == (jj >> (lg + 1))
        diff = (ii >> lg) != (jj >> lg)
        nls = [jnp.where(same, jnp.where(diff, n, 0.0), 0.0) for n in n_mats]
        tmp = [_mm(t, _bd(nl, lo)) for t, nl in zip(ts, nls)]
        ts = [t + _mm(x, _bd(t, lo)) for x, t in zip(tmp, ts)]
        lg += 1
    return ts


def _rwkv_kernel(r_ref, k_ref, v_ref, lo_ref, w0_ref, w2_ref, a0_ref, a2_ref, kkw_ref, kaw_ref,
                 y_ref, h_ref, *, reverse):
    step = pl.program_id(1)

    @pl.when(step == 0)
    def _():
        h_ref[...] = jnp.zeros_like(h_ref)

    r = r_ref[0]
    k = k_ref[0]
    v = v_ref[0]
    lora = lo_ref[0].astype(F32)
    n = r.shape[0]
    c = CHUNK
    nsub = n // c
    lgc = c.bit_length() - 1
    pw = 2 * HEAD
    npair = HEADS // 2

    logit_w = w0_ref[0] + _mm(jnp.tanh(lora), w2_ref[0])
    lw = -math.exp(-0.5) * _sigmoid(logit_w)
    a = _sigmoid(a0_ref[0] + _mm(lora, a2_ref[0]))
    kd = k * (1.0 + (a - 1.0) * kaw_ref[...])
    kk = k * kkw_ref[...]

    ti = lax.broadcasted_iota(jnp.int32, (n, n), 0)
    tj = lax.broadcasted_iota(jnp.int32, (n, n), 1)
    causal = (tj >= ti) if reverse else (tj <= ti)
    tri = jnp.where((ti >> lgc) == (tj >> lgc), jnp.where(causal, 1.0, 0.0), 0.0)
    g = _mm(tri, lw, "xb2")

    ci2 = lax.broadcasted_iota(jnp.int32, (c, pw), 0)
    lane2 = lax.broadcasted_iota(jnp.int32, (c, pw), 1)
    cj2 = lane2 & (c - 1)
    lo = lane2 < HEAD
    incl2 = (cj2 >= ci2) if reverse else (cj2 <= ci2)
    strict2 = (cj2 > ci2) if reverse else (cj2 < ci2)
    ri = lax.broadcasted_iota(jnp.int32, (pw, pw), 0)
    rj = lax.broadcasted_iota(jnp.int32, (pw, pw), 1)
    blk = (ri < HEAD) == (rj < HEAD)
    eye_p = (ri == rj).astype(F32)
    zeros_c = jnp.zeros((c, pw), F32)
    zeros_p = jnp.zeros((pw, pw), F32)

    def prep(j):
        rows = slice(j * c, (j + 1) * c)
        lw_j = lw[rows]
        g_j = g[rows]
        tot = jnp.sum(lw_j, axis=0, keepdims=True)
        e_prev = jnp.exp(g_j - lw_j)
        e_neg = jnp.exp(-g_j)
        e_pos = jnp.exp(g_j)
        e_tot = jnp.exp(tot)
        e_end = e_tot * e_neg
        out = []
        for p in range(npair):
            sl = slice(p * pw, (p + 1) * pw)
            kkp = kk[rows, sl]
            sq = kkp * kkp
            n2 = jnp.where(lo, jnp.sum(jnp.where(lo, sq, 0.0), axis=-1, keepdims=True),
                           jnp.sum(jnp.where(lo, 0.0, sq), axis=-1, keepdims=True))
            kkp = kkp * lax.rsqrt(jnp.maximum(n2, 1e-12))
            ka = kkp * a[rows, sl]
            out.append(dict(a_t=-kkp * e_prev[:, sl], b_t=ka * e_neg[:, sl], k_t=kd[rows, sl] * e_neg[:, sl],
                            r_t=r[rows, sl] * e_pos[:, sl], b_e=ka * e_end[:, sl],
                            k_e=kd[rows, sl] * e_end[:, sl], vh=v[rows, sl], g_end=e_tot[:, sl]))
        return out

    def local(us):
        s_bk = [_mm(jnp.concatenate([u["a_t"], u["r_t"]], axis=0),
                    jnp.concatenate([_bd(u["b_t"], lo), _bd(u["k_t"], lo)], axis=0), "bf16", NT)
                for u in us]
        nab = [jnp.where(strict2, s[:c, :pw], 0.0) for s in s_bk]
        aak = [jnp.where(strict2, s[:c, pw:], 0.0) for s in s_bk]
        arbk = [jnp.concatenate([jnp.where(incl2, s[c:, :pw], 0.0), jnp.where(incl2, s[c:, pw:], 0.0)],
                                axis=1) for s in s_bk]
        t_inv = _tri_inv_pairs(nab, c, lo)
        x1 = [_mm(m, _bd(u["vh"], lo)) for m, u in zip(aak, us)]
        uw = [_mm(t, jnp.concatenate([_bd(x, lo), _bd(u["a_t"], lo)], axis=1))
              for t, x, u in zip(t_inv, x1, us)]
        yq = [_mm(m, jnp.concatenate(
            [jnp.concatenate([_bd(w_[:, :pw], lo), _bd(w_[:, pw:], lo)], axis=1),
             jnp.concatenate([_bd(u["vh"], lo), zeros_p], axis=1)], axis=0))
            for m, w_, u in zip(arbk, uw, us)]
        gp = [_mm(jnp.concatenate([u["b_e"], u["k_e"]], axis=0),
                  jnp.concatenate([w_, jnp.concatenate([u["vh"], zeros_c], axis=1)], axis=0), "bf16", TN)
              for w_, u in zip(uw, us)]
        res = []
        for u, yq_, gp_ in zip(us, yq, gp):
            q = u["r_t"] + yq_[:, pw:]
            p_bd = jnp.where(blk, gp_[:, pw:], 0.0) + eye_p * u["g_end"]
            res.append((jnp.concatenate([q, p_bd], axis=0), yq_[:, :pw], jnp.where(blk, gp_[:, :pw], 0.0)))
        return res

    hs = [h_ref[p] for p in range(npair)]
    order = list(range(nsub - 1, -1, -1) if reverse else range(nsub))
    for grp in (order,):
        loc = local([u for j in grp for u in prep(j)])
        for gi, j in enumerate(grp):
            res = [_mm(loc[gi * npair + p][0], hs[p]) for p in range(npair)]
            for p in range(npair):
                y_ref[0, j * c:(j + 1) * c, p * pw:(p + 1) * pw] = res[p][:c] + loc[gi * npair + p][1]
                hs[p] = res[p][c:] + loc[gi * npair + p][2]
    for p in range(npair):
        h_ref[p] = hs[p]


def _rwkv(rkvc, z, w0, w2p, a0, a2p, kkw, kaw, ctx_len, reverse):
    bsz, ltot, _ = rkvc.shape
    nblk = ltot // RWKV_BLK
    nctx = ctx_len // RWKV_BLK
    lora_blk = COL_LORA // LORA_W
    d = 1 if reverse else 0

    def tok(i):
        return jnp.where(i < nctx, nctx - 1 - i, nblk + nctx - 1 - i) if reverse else i

    par3 = lambda b, i: (d, 0, 0)
    kern = functools.partial(_rwkv_kernel, reverse=reverse)
    return pl.pallas_call(
        kern,
        grid=(bsz, nblk),
        in_specs=[pl.BlockSpec((1, RWKV_BLK, D_RWKV), lambda b, i: (b, tok(i), 0)),
                  pl.BlockSpec((1, RWKV_BLK, D_RWKV), lambda b, i: (b, tok(i), 1)),
                  pl.BlockSpec((1, RWKV_BLK, D_RWKV), lambda b, i: (b, tok(i), 2)),
                  pl.BlockSpec((1, RWKV_BLK, LORA_W), lambda b, i: (b, tok(i), lora_blk)),
                  pl.BlockSpec((1, 1, D_RWKV), par3),
                  pl.BlockSpec((1, LORA_W, D_RWKV), par3),
                  pl.BlockSpec((1, 1, D_RWKV), par3),
                  pl.BlockSpec((1, LORA_W, D_RWKV), par3),
                  pl.BlockSpec((1, D_RWKV), lambda b, i: (0, 0)),
                  pl.BlockSpec((1, D_RWKV), lambda b, i: (0, 0))],
        out_specs=pl.BlockSpec((1, RWKV_BLK, D_RWKV), lambda b, i: (b, tok(i), 0)),
        out_shape=jax.ShapeDtypeStruct((bsz, ltot, D_RWKV), F32),
        scratch_shapes=[pltpu.VMEM((HEADS // 2, 2 * HEAD, 2 * HEAD), F32)],
        compiler_params=_cparams(("parallel", "arbitrary"), 48),
    )(rkvc, rkvc, rkvc, z, w0, w2p, a0, a2p, kkw, kaw)


def _s5_tables_kernel(lam_re_ref, lam_im_ref, dt_ref, bt_re_ref, bt_im_ref, c_re_ref, c_im_ref,
                      inj_ref, mt_ref, qt_ref, lre_ref, lim_ref):
    t = S5_T
    hg = S5_GROUP
    ps = S5_STATE
    row = lax.broadcasted_iota(jnp.int32, (t, 2 * ps), 0).astype(F32)
    is_f = lax.broadcasted_iota(jnp.int32, (t, 2 * ps), 1) < ps
    is_f1 = is_f[0:1]
    n_inj = jnp.where(is_f, t - 1.0 - row, row)
    n_out = jnp.where(is_f, row + 1.0, t - row)
    n_lag = jnp.where(is_f, row, t - 1.0 - row)
    for gl in range(lam_re_ref.shape[0]):
        lam_re = lam_re_ref[gl:gl + 1, :]
        lam_im = lam_im_ref[gl:gl + 1, :]
        dt = dt_ref[gl:gl + 1, :]

        def power(n):
            mag = jnp.exp(lam_re * dt * n)
            return mag * jnp.cos(lam_im * dt * n), mag * jnp.sin(lam_im * dt * n)

        lb_re, lb_im = power(1.0)
        den = lam_re * lam_re + lam_im * lam_im
        q_re = ((lb_re - 1.0) * lam_re + lb_im * lam_im) / den
        q_im = (lb_im * lam_re - (lb_re - 1.0) * lam_im) / den
        bt_re = jnp.concatenate([bt_re_ref[gl], bt_re_ref[gl]], axis=1)
        bt_im = jnp.concatenate([bt_im_ref[gl], bt_im_ref[gl]], axis=1)
        bb_re = q_re * bt_re - q_im * bt_im
        bb_im = q_re * bt_im + q_im * bt_re
        c_re = jnp.concatenate([c_re_ref[gl], c_re_ref[gl]], axis=1)
        c_im = jnp.concatenate([c_im_ref[gl], c_im_ref[gl]], axis=1)

        pi_re, pi_im = power(n_inj)
        po_re, po_im = power(n_out)
        pl_re, pl_im = power(n_lag)
        ck_re, ck_im = [], []
        for s in range(t):
            wr, wi = pi_re[s:s + 1], pi_im[s:s + 1]
            rows = slice(s * hg, (s + 1) * hg)
            inj_ref[gl, rows, 0:2 * ps] = (wr * bb_re - wi * bb_im).astype(BF16)
            inj_ref[gl, rows, 2 * ps:4 * ps] = (wr * bb_im + wi * bb_re).astype(BF16)
            wr, wi = po_re[s:s + 1], po_im[s:s + 1]
            qt_ref[gl, rows, 0:2 * ps] = (wr * c_re - wi * c_im).astype(BF16)
            qt_ref[gl, rows, 2 * ps:4 * ps] = (-(wr * c_im + wi * c_re)).astype(BF16)
            wr, wi = pl_re[s:s + 1], pl_im[s:s + 1]
            ck_re.append(wr * c_re - wi * c_im)
            ck_im.append(wr * c_im + wi * c_re)
        ck = jnp.concatenate([jnp.concatenate(ck_re, axis=0), jnp.concatenate(ck_im, axis=0)], axis=1)
        bsel = jnp.concatenate(
            [jnp.concatenate([jnp.where(is_f1, bb_re, 0.0), jnp.where(is_f1, -bb_im, 0.0)], axis=1),
             jnp.concatenate([jnp.where(is_f1, 0.0, bb_re), jnp.where(is_f1, 0.0, -bb_im)], axis=1)], axis=0)
        kt = _mm(bsel, ck, "x3", NT)
        w = t * hg
        zero = jnp.zeros((hg, w), F32)
        lag = (pltpu.roll(jnp.concatenate([kt[0:hg], zero], axis=1), w - hg, 1)
               + jnp.concatenate([kt[hg:2 * hg], zero], axis=1))
        for s in range(t):
            sh = (t - 1 - s) * hg
            blk = lag if sh == 0 else pltpu.roll(lag, 2 * w - sh, 1)
            mt_ref[gl, s * hg:(s + 1) * hg, :] = blk[:, 0:w].astype(BF16)
        l16_re, l16_im = power(float(t))
        lre_ref[gl:gl + 1, :] = l16_re
        lim_ref[gl:gl + 1, :] = l16_im


def _s5_tables(lam_re, lam_im, log_step, b_re, b_im, c_re, c_im):
    g, p, hg = b_re.shape
    w = S5_T * hg
    lanes = lambda x: jnp.concatenate([x[0], x[1]], axis=-1)
    dt = lanes(jnp.broadcast_to(jnp.exp(log_step)[..., None], lam_re.shape))
    bt_re = jnp.transpose(b_re, (0, 2, 1))
    bt_im = jnp.transpose(b_im, (0, 2, 1))
    vec = pl.BlockSpec((S5_TG, 2 * p), lambda j: (j, 0))
    mat = pl.BlockSpec((S5_TG, hg, p), lambda j: (j, 0, 0))
    tab = pl.BlockSpec((S5_TG, w, w), lambda j: (j, 0, 0))
    return pl.pallas_call(
        _s5_tables_kernel,
        grid=(g // S5_TG,),
        in_specs=[vec, vec, vec, mat, mat, mat, mat],
        out_specs=[tab, tab, tab, vec, vec],
        out_shape=[jax.ShapeDtypeStruct((g, w, w), BF16)] * 3 + [jax.ShapeDtypeStruct((g, 2 * p), F32)] * 2,
        compiler_params=_cparams(("parallel",), 32),
    )(lanes(lam_re), lanes(lam_im), dt, bt_re, bt_im, c_re, c_im)


def _block_transpose8(xs):
    lane_blk = lax.broadcasted_iota(jnp.int32, xs[0].shape, 1) >> 4
    xs = list(xs)
    for d in (4, 2, 1):
        hi = (lane_blk & d) != 0
        for i in range(8):
            if i & d:
                continue
            a, b = xs[i], xs[i + d]
            xs[i] = jnp.where(hi, pltpu.roll(b, 16 * d, 1), a)
            xs[i + d] = jnp.where(hi, b, pltpu.roll(a, 128 - 16 * d, 1))
    return xs


def _s5_kernel(u0_ref, u1_ref, inj_ref, mt_ref, qt_ref, lre_ref, lim_ref, y_ref, ug_ref, zre_ref, zim_ref,
               hre_ref, him_ref, *, nchunk, nctx):
    gh = ug_ref.shape[0]
    ps = S5_STATE
    t = S5_T
    nlat = nchunk - nctx
    for jt, u_ref in enumerate((u0_ref, u1_ref)):
        for q in range(t // 8):
            cols = [u_ref[0, pl.ds(8 * q + tt, nchunk, stride=t), :] for tt in range(8)]
            grp = _block_transpose8(cols)
            for gl in range(8):
                ug_ref[8 * jt + gl, :, 128 * q:128 * (q + 1)] = grp[gl].astype(BF16)
    for gi in range(gh):
        zg = _dot(ug_ref[gi], inj_ref[gi], NN)
        zre_ref[pl.ds(gi, nchunk, stride=gh), :] = zg[:, 0:2 * ps]
        zim_ref[pl.ds(gi, nchunk, stride=gh), :] = zg[:, 2 * ps:4 * ps]

    lane = lax.broadcasted_iota(jnp.int32, (gh, 2 * ps), 1)
    is_f = lane < ps
    lre = lre_ref[...]
    lim = lim_ref[...]

    def body(i, carry):
        hre, him = carry
        cb = jnp.where(i < nctx, nctx - 1 - i, nchunk + nctx - 1 - i)
        rf = pl.multiple_of(i * gh, gh)
        rb = pl.multiple_of(cb * gh, gh)
        hre_ref[pl.ds(rf, gh), 0:ps] = hre[:, 0:ps]
        hre_ref[pl.ds(rb, gh), ps:2 * ps] = hre[:, ps:2 * ps]
        him_ref[pl.ds(rf, gh), 0:ps] = him[:, 0:ps]
        him_ref[pl.ds(rb, gh), ps:2 * ps] = him[:, ps:2 * ps]
        zre = jnp.where(is_f, zre_ref[pl.ds(rf, gh), :], zre_ref[pl.ds(rb, gh), :])
        zim = jnp.where(is_f, zim_ref[pl.ds(rf, gh), :], zim_ref[pl.ds(rb, gh), :])
        return lre * hre - lim * him + zre, lre * him + lim * hre + zim

    zero = jnp.zeros((gh, 2 * ps), F32)
    lax.fori_loop(0, nchunk, body, (zero, zero))

    for jt in range(gh // 8):
        ys = []
        for gl in range(8):
            gi = 8 * jt + gl
            hre = hre_ref[pl.ds(gi, nchunk, stride=gh), :].astype(BF16)
            him = him_ref[pl.ds(gi, nchunk, stride=gh), :].astype(BF16)
            ys.append(_dot(ug_ref[gi], mt_ref[gi], NN)
                      + _dot(jnp.concatenate([hre, him], axis=1), qt_ref[gi], NT))
        for q in range(t // 8):
            tok = _block_transpose8([y[:, 128 * q:128 * (q + 1)] for y in ys])
            for tt in range(8):
                y_ref[0, jt, pl.ds(8 * q + tt, nlat, stride=t), :] = tok[tt][nctx:]


def _s5(u, tables, ctx_len):
    inj, mt, qt, lre, lim = tables
    bsz, ltot, _ = u.shape
    nchunk = ltot // S5_T
    nctx = ctx_len // S5_T
    gh = S5_GH
    halves = S5_GROUPS // gh
    w = S5_T * S5_GROUP
    tiles = gh * S5_GROUP // 128
    ps2 = 2 * S5_STATE
    kern = functools.partial(_s5_kernel, nchunk=nchunk, nctx=nctx)
    u_spec = lambda k: pl.BlockSpec((1, ltot, 128), lambda b, j: (b, 0, tiles * j + k))
    tab = pl.BlockSpec((gh, w, w), lambda b, j: (j, 0, 0))
    vec = pl.BlockSpec((gh, ps2), lambda b, j: (j, 0))
    return pl.pallas_call(
        kern,
        grid=(bsz, halves),
        in_specs=[u_spec(0), u_spec(1), tab, tab, tab, vec, vec],
        out_specs=pl.BlockSpec((1, tiles, ltot - ctx_len, 128), lambda b, j: (b, j, 0, 0)),
        out_shape=jax.ShapeDtypeStruct((bsz, D_S5 // 128, ltot - ctx_len, 128), F32),
        scratch_shapes=[pltpu.VMEM((gh, nchunk, w), BF16),
                        pltpu.VMEM((nchunk * gh, ps2), F32),
                        pltpu.VMEM((nchunk * gh, ps2), F32),
                        pltpu.VMEM((nchunk * gh, ps2), F32),
                        pltpu.VMEM((nchunk * gh, ps2), F32)],
        compiler_params=_cparams(("parallel", "parallel"), 48),
    )(u, u, inj, mt, qt, lre, lim)


def _mix_kernel(x_ref, yf_ref, yb_ref, r_ref, k_ref, v_ref, lo_ref, u_ref, ys_ref, mod_ref,
                seg_ref, a0_ref, a2_ref, g2_ref, kaw_ref, rk_ref, lnw_ref, lnb_ref,
                dsk_ref, gluw_ref, glub_ref, wo_ref, g2n_ref, w1_ref, w3_ref, w2_ref, gf_ref, o_ref):
    lo = lo_ref[0].astype(F32)
    seg = seg_ref[...]

    def head_sum(t):
        pw = seg.shape[0]
        return jnp.concatenate([_mm(t[:, i:i + pw], seg, "xa2") for i in range(0, t.shape[1], pw)], axis=1)

    y = yf_ref[0] + yb_ref[0]
    mu = head_sum(y) * (1.0 / HEAD)
    yc = y - mu
    var = head_sum(yc * yc) * (1.0 / HEAD)
    yn = yc * lax.rsqrt(var + RWKV_LN_EPS) * lnw_ref[...] + lnb_ref[...]
    a_f = _sigmoid(a0_ref[0] + _mm(lo, a2_ref[0]))
    a_b = _sigmoid(a0_ref[1] + _mm(lo, a2_ref[1]))
    kaw = kaw_ref[...]
    kd_sum = k_ref[0] * ((1.0 + (a_f - 1.0) * kaw) + (1.0 + (a_b - 1.0) * kaw))
    bonus = head_sum(r_ref[0] * kd_sum * rk_ref[...])
    yn = yn + bonus * v_ref[0]
    gate = _mm(_sigmoid(lo), g2_ref[...])
    rwkv_out = yn * gate

    u = u_ref[0]
    ys = jnp.concatenate([ys_ref[0, i] for i in range(ys_ref.shape[1])], axis=1) + dsk_ref[...] * u
    zg = 0.5 * ys * (1.0 + jnp.tanh(math.sqrt(2.0 / math.pi) * (ys + 0.044715 * (ys * ys * ys))))
    s5_out = zg * _sigmoid(_mm(zg, gluw_ref[...]) + glub_ref[...])

    mix = _mm(rwkv_out, wo_ref[0:D_RWKV, :]) + _mm(s5_out, wo_ref[D_RWKV:, :])
    x1 = x_ref[0] + mod_ref[0, 2:3, :] * mix

    ms = jnp.mean(x1 * x1, axis=-1, keepdims=True)
    h = x1 * lax.rsqrt(ms + NORM_EPS) * g2n_ref[...]
    h = h * (1.0 + mod_ref[0, 4:5, :]) + mod_ref[0, 3:4, :]
    hb = h.astype(BF16)
    h1 = _dot(hb, w1_ref[...], NN)
    h3 = _dot(hb, w3_ref[...], NN)
    act = (h1 * _sigmoid(h1)) * h3
    out = x1 + mod_ref[0, 5:6, :] * _dot(act.astype(BF16), w2_ref[...], NN)
    ms2 = jnp.mean(out * out, axis=-1, keepdims=True)
    o_ref[0] = out * lax.rsqrt(ms2 + NORM_EPS) * gf_ref[...]


def _mix(x, y_f, y_b, rkvc, z, u, ys5, mod8, seg, a0, a2p, g2p, kaw, rk, lnw, lnb, dsk, gluw, glub, wo,
         g2n, w1, w3, w2, gf, ctx_len, tm):
    bsz, seq, d = x.shape
    assert ctx_len % tm == 0 and seq % tm == 0
    off = ctx_len // tm
    lora_blk = COL_LORA // LORA_W
    tok = lambda c: pl.BlockSpec((1, tm, D_RWKV), lambda b, j: (b, j + off, c))
    full = lambda a: pl.BlockSpec(a.shape, lambda b, j: (0,) * a.ndim, pipeline_mode=pl.Buffered(1))
    return pl.pallas_call(
        _mix_kernel,
        grid=(bsz, seq // tm),
        in_specs=[pl.BlockSpec((1, tm, d), lambda b, j: (b, j, 0)),
                  tok(0), tok(0), tok(0), tok(1), tok(2),
                  pl.BlockSpec((1, tm, LORA_W), lambda b, j: (b, j + off, lora_blk)),
                  pl.BlockSpec((1, tm, D_S5), lambda b, j: (b, j + off, 0)),
                  pl.BlockSpec((1, D_S5 // 128, tm, 128), lambda b, j: (b, 0, j, 0)),
                  pl.BlockSpec((1, 8, d), lambda b, j: (b, 0, 0)),
                  full(seg), full(a0), full(a2p), full(g2p), full(kaw), full(rk), full(lnw), full(lnb),
                  full(dsk), full(gluw), full(glub), full(wo),
                  full(g2n), full(w1), full(w3), full(w2), full(gf)],
        out_specs=pl.BlockSpec((1, tm, d), lambda b, j: (b, j, 0)),
        out_shape=jax.ShapeDtypeStruct((bsz, seq, d), F32),
        compiler_params=_cparams(("parallel", "parallel"), 56),
    )(x, y_f, y_b, rkvc, rkvc, rkvc, z, u, ys5, mod8, seg, a0, a2p, g2p, kaw, rk, lnw, lnb,
      dsk, gluw, glub, wo, g2n, w1, w3, w2, gf)


def _pad_rows(w, row0, rows_total):
    pad = [(0, 0)] * (w.ndim - 2) + [(row0, rows_total - row0 - w.shape[-2]), (0, 0)]
    return jnp.pad(w, pad)


def kernel(x, c, ctx, c_ctx, mod_w, mod_b, norm1_g, norm2_g, w_in, w_out, rwkv_conv, rwkv_w0, rwkv_w2, rwkv_a0, rwkv_a2, rwkv_g2, rwkv_kk, rwkv_ka, rwkv_rk, rwkv_ln_w, rwkv_ln_b, s5_lam_re, s5_lam_im, s5_log_step, s5_b_re, s5_b_im, s5_c_re, s5_c_im, s5_d, s5_glu_w, s5_glu_b, ffn_w1, ffn_w3, ffn_w2, final_g):
    bsz, seq, d = x.shape
    ctx_len = ctx.shape[1]
    ltot = ctx_len + seq
    layer = 0

    cc = jnp.concatenate([c, c_ctx[None, :], jnp.zeros((8 - bsz - 1, d), F32)], axis=0)
    mod = _adaln(cc, mod_w[layer], mod_b[layer])
    mod8 = jnp.pad(mod[:bsz + 1].reshape(bsz + 1, N_MOD, d), ((0, 0), (0, 8 - N_MOD), (0, 0)))

    wi = w_in[layer]
    w_z = jnp.concatenate([wi[:, :-D_S5], jnp.zeros((d, D_Z16 + D_S5 - wi.shape[1]), F32),
                           wi[:, -D_S5:]], axis=1).astype(BF16)
    z, u = _inproj(ctx, x, mod8, norm1_g[layer], w_z, 256)

    rkvc = _conv(z, rwkv_conv[layer], ctx_len)

    w2p = _pad_rows(rwkv_w2[layer], 0, LORA_W).at[1].set(_pad_rows(rwkv_w2[layer, 1], DECAY_LORA, LORA_W))
    a2p = jnp.stack([_pad_rows(rwkv_a2[layer, 0], LORA_AD, LORA_W),
                     _pad_rows(rwkv_a2[layer, 1], LORA_AD + AAA_LORA, LORA_W)])
    w0 = rwkv_w0[layer].reshape(2, 1, D_RWKV)
    a0 = rwkv_a0[layer].reshape(2, 1, D_RWKV)
    kkw = rwkv_kk[layer].reshape(1, D_RWKV)
    kaw = rwkv_ka[layer].reshape(1, D_RWKV)
    y_f = _rwkv(rkvc, z, w0, w2p, a0, a2p, kkw, kaw, ctx_len, False)
    y_b = _rwkv(rkvc, z, w0, w2p, a0, a2p, kkw, kaw, ctx_len, True)

    tables = _s5_tables(s5_lam_re[layer], s5_lam_im[layer], s5_log_step[layer], s5_b_re[layer],
                        s5_b_im[layer], s5_c_re[layer], s5_c_im[layer])
    ys5 = _s5(u, tables, ctx_len)

    hh = jnp.arange(2 * HEAD) // HEAD
    seg = (hh[:, None] == hh[None, :]).astype(BF16)
    g2p = _pad_rows(rwkv_g2[layer], LORA_GD, LORA_W)
    return _mix(x, y_f, y_b, rkvc, z, u, ys5, mod8, seg, a0, a2p, g2p, kaw,
                rwkv_rk[layer].reshape(1, D_RWKV), rwkv_ln_w[layer].reshape(1, D_RWKV),
                rwkv_ln_b[layer].reshape(1, D_RWKV), s5_d[layer].reshape(1, D_S5),
                s5_glu_w[layer].astype(BF16), s5_glu_b[layer].reshape(1, D_S5),
                w_out[layer].astype(BF16), norm2_g[layer].reshape(1, d), ffn_w1[layer].astype(BF16),
                ffn_w3[layer].astype(BF16), ffn_w2[layer].astype(BF16), final_g.reshape(1, d), ctx_len, 256)
```

```python
import functools
import math

import jax
import jax.numpy as jnp
import numpy as np
from jax import lax
from jax.experimental import pallas as pl
from jax.experimental.pallas import tpu as pltpu

F32 = jnp.float32
BF16 = jnp.bfloat16

D_MODEL = 1024
GRID_W = 64
D_RWKV = 512
HEAD = 64
HEADS = D_RWKV // HEAD
D_S5 = 512
S5_GROUP = 16
S5_GROUPS = D_S5 // S5_GROUP
S5_STATE = 64
DECAY_LORA = 32
AAA_LORA = 32
GATE_LORA = 96
N_MOD = 6
NORM_EPS = 1e-6
RWKV_LN_EPS = 64e-5

LORA_W = 256
COL_LORA = 3 * D_RWKV
D_Z16 = COL_LORA + LORA_W
LORA_AD = 2 * DECAY_LORA
LORA_GD = LORA_AD + 2 * AAA_LORA

CHUNK = 64
RWKV_BLK = 256
S5_T = 16
S5_GH = 16
S5_TG = 8
CONV_PAD = 72
CONV_TC = 256
CONV_PIECE = 128

NN = (((1,), (0,)), ((), ()))
NT = (((1,), (1,)), ((), ()))
TN = (((0,), (0,)), ((), ()))


def _dot(a, b, dims):
    return lax.dot_general(a, b, dims, preferred_element_type=F32)


def _split2(x):
    hi = x.astype(BF16)
    lo = (x - hi.astype(F32)).astype(BF16)
    return hi, lo


def _mm(a, b, mode="bf16", dims=NN):
    if mode == "bf16":
        return _dot(a.astype(BF16), b.astype(BF16), dims)
    if mode == "x3":
        ah, al = _split2(a)
        bh, bl = _split2(b)
        return _dot(ah, bh, dims) + (_dot(ah, bl, dims) + _dot(al, bh, dims))
    if mode == "xb2":
        ab = a.astype(BF16)
        b1, b2 = _split2(b)
        return _dot(ab, b1, dims) + _dot(ab, b2, dims)
    raise ValueError(mode)


def _sigmoid(x):
    return 1.0 / (1.0 + jnp.exp(-x))


def _cparams(sem, vmem_mb):
    return pltpu.CompilerParams(dimension_semantics=sem, vmem_limit_bytes=vmem_mb * 1024 * 1024)


def _adaln_kernel(c_ref, w_ref, b_ref, o_ref):
    c = c_ref[...]
    s = c * _sigmoid(c)
    o_ref[...] = _mm(s, w_ref[...], "x3") + b_ref[...]


def _adaln(cc, mod_w, mod_b):
    rows, d = cc.shape
    n = mod_w.shape[1]
    tn = 1024
    return pl.pallas_call(
        _adaln_kernel,
        grid=(n // tn,),
        in_specs=[pl.BlockSpec((rows, d), lambda j: (0, 0)),
                  pl.BlockSpec((d, tn), lambda j: (0, j)),
                  pl.BlockSpec((1, tn), lambda j: (0, j))],
        out_specs=pl.BlockSpec((rows, tn), lambda j: (0, j)),
        out_shape=jax.ShapeDtypeStruct((rows, n), F32),
        compiler_params=_cparams(("parallel",), 40),
    )(cc, mod_w, mod_b.reshape(1, n))


def _inproj_kernel(c_ref, x_ref, mod_ref, g_ref, w_ref, o_ref, u_ref, *, ctx_blocks):
    x = jnp.where(pl.program_id(1) < ctx_blocks, c_ref[0], x_ref[0])
    ms = jnp.mean(x * x, axis=-1, keepdims=True)
    xn = x * lax.rsqrt(ms + NORM_EPS) * g_ref[...]
    shift = mod_ref[0, 0:1, :]
    scale = mod_ref[0, 1:2, :]
    h = xn * (1.0 + scale) + shift
    res = _mm(h, w_ref[...], "bf16")
    o_ref[0] = res[:, :D_Z16].astype(BF16)
    u_ref[0] = res[:, D_Z16:]


def _inproj(ctx, x, mod8, g1, w_z, tm):
    bsz, seq, d = x.shape
    ctx_blocks = ctx.shape[1] // tm
    dz = w_z.shape[1]
    ltot = ctx.shape[1] + seq
    kern = functools.partial(_inproj_kernel, ctx_blocks=ctx_blocks)
    return pl.pallas_call(
        kern,
        grid=(bsz, ctx_blocks + seq // tm),
        in_specs=[pl.BlockSpec((1, tm, d), lambda b, j: (b, jnp.minimum(j, ctx_blocks - 1), 0)),
                  pl.BlockSpec((1, tm, d), lambda b, j: (b, jnp.maximum(j - ctx_blocks, 0), 0)),
                  pl.BlockSpec((1, 8, d), lambda b, j: (jnp.where(j < ctx_blocks, bsz, b), 0, 0)),
                  pl.BlockSpec((1, d), lambda b, j: (0, 0)),
                  pl.BlockSpec((d, dz), lambda b, j: (0, 0))],
        out_specs=[pl.BlockSpec((1, tm, D_Z16), lambda b, j: (b, j, 0)),
                   pl.BlockSpec((1, tm, dz - D_Z16), lambda b, j: (b, j, 0))],
        out_shape=[jax.ShapeDtypeStruct((bsz, ltot, D_Z16), BF16),
                   jax.ShapeDtypeStruct((bsz, ltot, dz - D_Z16), F32)],
        compiler_params=_cparams(("parallel", "parallel"), 48),
    )(ctx, x, mod8, g1.reshape(1, d), w_z)


def _conv_kernel(z_ref, w_ref, o_ref, xs_ref, *, ctx_len, lat_len):
    tc = z_ref.shape[2]
    pc = CONV_PIECE
    ctx_off = CONV_PAD
    lat_off = 2 * CONV_PAD + ctx_len
    zpad = jnp.zeros((CONV_PAD, tc), F32)
    for s in range(3):
        xs_ref[s, 0:CONV_PAD, :] = zpad
        xs_ref[s, ctx_off + ctx_len:lat_off, :] = zpad
        xs_ref[s, lat_off + lat_len:lat_off + lat_len + CONV_PAD, :] = zpad
    xs_ref[1, lat_off:lat_off + 8, :] = zpad[0:8]
    xs_ref[2, lat_off + lat_len - 8:lat_off + lat_len, :] = zpad[0:8]
    col = lax.broadcasted_iota(jnp.int32, (pc, tc), 0) % GRID_W
    keep_l = (col != GRID_W - 1).astype(F32)
    keep_r = (col != 0).astype(F32)
    for p in range(ctx_len // pc):
        xs_ref[0, ctx_off + p * pc:ctx_off + (p + 1) * pc, :] = z_ref[0, p * pc:(p + 1) * pc, :].astype(F32)
    for p in range(lat_len // pc):
        x = z_ref[0, ctx_len + p * pc:ctx_len + (p + 1) * pc, :].astype(F32)
        dst = lat_off + p * pc
        xs_ref[0, dst:dst + pc, :] = x
        xs_ref[1, dst + 1:dst + 1 + pc, :] = x * keep_l
        xs_ref[2, dst - 1:dst - 1 + pc, :] = x * keep_r
    w = [w_ref[t:t + 1, :] for t in range(9)]
    for p in range(ctx_len // pc):
        base = ctx_off + p * pc
        acc = w[3] * xs_ref[0, base - 1:base - 1 + pc, :]
        acc = acc + w[4] * xs_ref[0, base:base + pc, :]
        acc = acc + w[5] * xs_ref[0, base + 1:base + 1 + pc, :]
        o_ref[0, p * pc:(p + 1) * pc, :] = acc
    for p in range(lat_len // pc):
        base = lat_off + p * pc
        acc = None
        for di in range(3):
            for dj in range(3):
                off = base + (di - 1) * GRID_W
                src = (1, 0, 2)[dj]
                term = w[3 * di + dj] * xs_ref[src, off:off + pc, :]
                acc = term if acc is None else acc + term
        o_ref[0, ctx_len + p * pc:ctx_len + (p + 1) * pc, :] = acc


def _conv(z, conv_w, ctx_len):
    bsz, ltot, _ = z.shape
    lat_len = ltot - ctx_len
    ch = 3 * D_RWKV
    rows = 3 * CONV_PAD + ltot
    kern = functools.partial(_conv_kernel, ctx_len=ctx_len, lat_len=lat_len)
    return pl.pallas_call(
        kern,
        grid=(bsz, ch // CONV_TC),
        in_specs=[pl.BlockSpec((1, ltot, CONV_TC), lambda b, j: (b, 0, j)),
                  pl.BlockSpec((9, CONV_TC), lambda b, j: (0, j))],
        out_specs=pl.BlockSpec((1, ltot, CONV_TC), lambda b, j: (b, 0, j)),
        out_shape=jax.ShapeDtypeStruct((bsz, ltot, ch), F32),
        scratch_shapes=[pltpu.VMEM((3, rows, CONV_TC), F32)],
        compiler_params=_cparams(("parallel", "parallel"), 48),
    )(z, conv_w.reshape(9, ch))


def _bd(x, lo):
    return jnp.concatenate([jnp.where(lo, x, 0.0), jnp.where(lo, 0.0, x)], axis=0)


def _rwkv_masks(reverse):
    n, c = RWKV_BLK, CHUNK
    ti, tj = np.indices((n, n))
    causal = (tj >= ti) if reverse else (tj <= ti)
    tri = ((ti // c == tj // c) & causal).astype(np.float32)
    ii, jj = np.indices((c, 2 * c))
    jj = jj % c
    tabs = [(jj > ii) if reverse else (jj < ii), (jj >= ii) if reverse else (jj <= ii), ii == jj,
            ii // 2 == jj // 2]
    m = 2
    while m < c:
        tabs.append((ii // (2 * m) == jj // (2 * m)) & (ii // m != jj // m))
        m *= 2
    return jnp.asarray(tri, BF16), jnp.asarray(np.stack(tabs).astype(np.float32))


def _tri_inv_pairs(n_mats, masks, lo):
    ts = [masks[0] + n * masks[1] for n in n_mats]
    for lvl in masks[2:]:
        nls = [n * lvl for n in n_mats]
        tmp = [_mm(t, _bd(nl, lo)) for t, nl in zip(ts, nls)]
        ts = [t + _mm(x, _bd(t, lo)) for x, t in zip(tmp, ts)]
    return ts


def _rwkv_kernel(r_ref, k_ref, v_ref, lo_ref, w0_ref, w2_ref, a0_ref, a2_ref, kkw_ref, kaw_ref,
                 tri_ref, cm_ref, y_ref, h_ref, *, reverse):
    step = pl.program_id(1)

    @pl.when(step == 0)
    def _():
        h_ref[...] = jnp.zeros_like(h_ref)

    r = r_ref[0]
    k = k_ref[0]
    v = v_ref[0]
    lora = lo_ref[0].astype(F32)
    n = r.shape[0]
    c = CHUNK
    nsub = n // c
    pw = 2 * HEAD
    npair = HEADS // 2

    logit_w = w0_ref[0] + _mm(jnp.tanh(lora), w2_ref[0])
    lw = -math.exp(-0.5) * _sigmoid(logit_w)
    a = _sigmoid(a0_ref[0] + _mm(lora, a2_ref[0]))
    kd = k * (1.0 + (a - 1.0) * kaw_ref[...])
    kk = k * kkw_ref[...]

    g = _mm(tri_ref[...], lw, "xb2")

    lo = lax.broadcasted_iota(jnp.int32, (c, pw), 1) < HEAD
    strict2 = cm_ref[0]
    incl2 = cm_ref[1]
    inv_masks = [cm_ref[i] for i in range(2, cm_ref.shape[0])]
    ri = lax.broadcasted_iota(jnp.int32, (pw, pw), 0)
    rj = lax.broadcasted_iota(jnp.int32, (pw, pw), 1)
    blk = (ri < HEAD) == (rj < HEAD)
    eye_p = (ri == rj).astype(F32)
    zeros_c = jnp.zeros((c, pw), F32)
    zeros_p = jnp.zeros((pw, pw), F32)

    def prep(j):
        rows = slice(j * c, (j + 1) * c)
        lw_j = lw[rows]
        g_j = g[rows]
        tot = jnp.sum(lw_j, axis=0, keepdims=True)
        e_prev = jnp.exp(g_j - lw_j)
        e_neg = jnp.exp(-g_j)
        e_pos = jnp.exp(g_j)
        e_tot = jnp.exp(tot)
        e_end = e_tot * e_neg
        out = []
        for p in range(npair):
            sl = slice(p * pw, (p + 1) * pw)
            kkp = kk[rows, sl]
            sq = kkp * kkp
            n2 = jnp.where(lo, jnp.sum(jnp.where(lo, sq, 0.0), axis=-1, keepdims=True),
                           jnp.sum(jnp.where(lo, 0.0, sq), axis=-1, keepdims=True))
            kkp = kkp * lax.rsqrt(jnp.maximum(n2, 1e-12))
            ka = kkp * a[rows, sl]
            out.append(dict(a_t=-kkp * e_prev[:, sl], b_t=ka * e_neg[:, sl], k_t=kd[rows, sl] * e_neg[:, sl],
                            r_t=r[rows, sl] * e_pos[:, sl], b_e=ka * e_end[:, sl],
                            k_e=kd[rows, sl] * e_end[:, sl], vh=v[rows, sl], g_end=e_tot[:, sl]))
        return out

    def local(us):
        s_bk = [_mm(jnp.concatenate([u["a_t"], u["r_t"]], axis=0),
                    jnp.concatenate([_bd(u["b_t"], lo), _bd(u["k_t"], lo)], axis=0), "bf16", NT)
                for u in us]
        nab = [s[:c, :pw] * strict2 for s in s_bk]
        aak = [s[:c, pw:] * strict2 for s in s_bk]
        arbk = [jnp.concatenate([s[c:, :pw] * incl2, s[c:, pw:] * incl2], axis=1) for s in s_bk]
        t_inv = _tri_inv_pairs(nab, inv_masks, lo)
        x1 = [_mm(m, _bd(u["vh"], lo)) for m, u in zip(aak, us)]
        uw = [_mm(t, jnp.concatenate([_bd(x, lo), _bd(u["a_t"], lo)], axis=1))
              for t, x, u in zip(t_inv, x1, us)]
        yq = [_mm(m, jnp.concatenate(
            [jnp.concatenate([_bd(w_[:, :pw], lo), _bd(w_[:, pw:], lo)], axis=1),
             jnp.concatenate([_bd(u["vh"], lo), zeros_p], axis=1)], axis=0))
            for m, w_, u in zip(arbk, uw, us)]
        gp = [_mm(jnp.concatenate([u["b_e"], u["k_e"]], axis=0),
                  jnp.concatenate([w_, jnp.concatenate([u["vh"], zeros_c], axis=1)], axis=0), "bf16", TN)
              for w_, u in zip(uw, us)]
        res = []
        for u, yq_, gp_ in zip(us, yq, gp):
            q = u["r_t"] + yq_[:, pw:]
            p_bd = jnp.where(blk, gp_[:, pw:], 0.0) + eye_p * u["g_end"]
            res.append((jnp.concatenate([q, p_bd], axis=0), yq_[:, :pw], jnp.where(blk, gp_[:, :pw], 0.0)))
        return res

    hs = [h_ref[p] for p in range(npair)]
    order = list(range(nsub - 1, -1, -1) if reverse else range(nsub))
    for grp in (order,):
        loc = local([u for j in grp for u in prep(j)])
        for gi, j in enumerate(grp):
            res = [_mm(loc[gi * npair + p][0], hs[p]) for p in range(npair)]
            for p in range(npair):
                y_ref[0, j * c:(j + 1) * c, p * pw:(p + 1) * pw] = res[p][:c] + loc[gi * npair + p][1]
                hs[p] = res[p][c:] + loc[gi * npair + p][2]
    for p in range(npair):
        h_ref[p] = hs[p]


def _rwkv(rkvc, z, w0, w2p, a0, a2p, kkw, kaw, ctx_len, reverse):
    bsz, ltot, _ = rkvc.shape
    nblk = ltot // RWKV_BLK
    nctx = ctx_len // RWKV_BLK
    lora_blk = COL_LORA // LORA_W
    d = 1 if reverse else 0

    def tok(i):
        return jnp.where(i < nctx, nctx - 1 - i, nblk + nctx - 1 - i) if reverse else i

    par3 = lambda b, i: (d, 0, 0)
    tri, cmask = _rwkv_masks(reverse)
    kern = functools.partial(_rwkv_kernel, reverse=reverse)
    return pl.pallas_call(
        kern,
        grid=(bsz, nblk),
        in_specs=[pl.BlockSpec((1, RWKV_BLK, D_RWKV), lambda b, i: (b, tok(i), 0)),
                  pl.BlockSpec((1, RWKV_BLK, D_RWKV), lambda b, i: (b, tok(i), 1)),
                  pl.BlockSpec((1, RWKV_BLK, D_RWKV), lambda b, i: (b, tok(i), 2)),
                  pl.BlockSpec((1, RWKV_BLK, LORA_W), lambda b, i: (b, tok(i), lora_blk)),
                  pl.BlockSpec((1, 1, D_RWKV), par3),
                  pl.BlockSpec((1, LORA_W, D_RWKV), par3),
                  pl.BlockSpec((1, 1, D_RWKV), par3),
                  pl.BlockSpec((1, LORA_W, D_RWKV), par3),
                  pl.BlockSpec((1, D_RWKV), lambda b, i: (0, 0)),
                  pl.BlockSpec((1, D_RWKV), lambda b, i: (0, 0)),
                  pl.BlockSpec(tri.shape, lambda b, i: (0, 0)),
                  pl.BlockSpec(cmask.shape, lambda b, i: (0, 0, 0))],
        out_specs=pl.BlockSpec((1, RWKV_BLK, D_RWKV), lambda b, i: (b, tok(i), 0)),
        out_shape=jax.ShapeDtypeStruct((bsz, ltot, D_RWKV), F32),
        scratch_shapes=[pltpu.VMEM((HEADS // 2, 2 * HEAD, 2 * HEAD), F32)],
        compiler_params=_cparams(("parallel", "arbitrary"), 48),
    )(rkvc, rkvc, rkvc, z, w0, w2p, a0, a2p, kkw, kaw, tri, cmask)


def _s5_tables_kernel(lam_re_ref, lam_im_ref, dt_ref, bt_re_ref, bt_im_ref, c_re_ref, c_im_ref,
                      inj_ref, mt_ref, qt_ref, lre_ref, lim_ref):
    t = S5_T
    hg = S5_GROUP
    ps = S5_STATE
    row = lax.broadcasted_iota(jnp.int32, (t, 2 * ps), 0).astype(F32)
    is_f = lax.broadcasted_iota(jnp.int32, (t, 2 * ps), 1) < ps
    is_f1 = is_f[0:1]
    n_inj = jnp.where(is_f, t - 1.0 - row, row)
    n_out = jnp.where(is_f, row + 1.0, t - row)
    n_lag = jnp.where(is_f, row, t - 1.0 - row)
    for gl in range(lam_re_ref.shape[0]):
        lam_re = lam_re_ref[gl:gl + 1, :]
        lam_im = lam_im_ref[gl:gl + 1, :]
        dt = dt_ref[gl:gl + 1, :]

        def power(n):
            mag = jnp.exp(lam_re * dt * n)
            return mag * jnp.cos(lam_im * dt * n), mag * jnp.sin(lam_im * dt * n)

        lb_re, lb_im = power(1.0)
        den = lam_re * lam_re + lam_im * lam_im
        q_re = ((lb_re - 1.0) * lam_re + lb_im * lam_im) / den
        q_im = (lb_im * lam_re - (lb_re - 1.0) * lam_im) / den
        bt_re = jnp.concatenate([bt_re_ref[gl], bt_re_ref[gl]], axis=1)
        bt_im = jnp.concatenate([bt_im_ref[gl], bt_im_ref[gl]], axis=1)
        bb_re = q_re * bt_re - q_im * bt_im
        bb_im = q_re * bt_im + q_im * bt_re
        c_re = jnp.concatenate([c_re_ref[gl], c_re_ref[gl]], axis=1)
        c_im = jnp.concatenate([c_im_ref[gl], c_im_ref[gl]], axis=1)

        pi_re, pi_im = power(n_inj)
        po_re, po_im = power(n_out)
        pl_re, pl_im = power(n_lag)
        ck_re, ck_im = [], []
        for s in range(t):
            wr, wi = pi_re[s:s + 1], pi_im[s:s + 1]
            rows = slice(s * hg, (s + 1) * hg)
            inj_ref[gl, rows, 0:2 * ps] = (wr * bb_re - wi * bb_im).astype(BF16)
            inj_ref[gl, rows, 2 * ps:4 * ps] = (wr * bb_im + wi * bb_re).astype(BF16)
            wr, wi = po_re[s:s + 1], po_im[s:s + 1]
            qt_ref[gl, rows, 0:2 * ps] = (wr * c_re - wi * c_im).astype(BF16)
            qt_ref[gl, rows, 2 * ps:4 * ps] = (-(wr * c_im + wi * c_re)).astype(BF16)
            wr, wi = pl_re[s:s + 1], pl_im[s:s + 1]
            ck_re.append(wr * c_re - wi * c_im)
            ck_im.append(wr * c_im + wi * c_re)
        ck = jnp.concatenate([jnp.concatenate(ck_re, axis=0), jnp.concatenate(ck_im, axis=0)], axis=1)
        bsel = jnp.concatenate(
            [jnp.concatenate([jnp.where(is_f1, bb_re, 0.0), jnp.where(is_f1, -bb_im, 0.0)], axis=1),
             jnp.concatenate([jnp.where(is_f1, 0.0, bb_re), jnp.where(is_f1, 0.0, -bb_im)], axis=1)], axis=0)
        kt = _mm(bsel, ck, "x3", NT)
        w = t * hg
        zero = jnp.zeros((hg, w), F32)
        lag = (pltpu.roll(jnp.concatenate([kt[0:hg], zero], axis=1), w - hg, 1)
               + jnp.concatenate([kt[hg:2 * hg], zero], axis=1))
        for s in range(t):
            sh = (t - 1 - s) * hg
            blk = lag if sh == 0 else pltpu.roll(lag, 2 * w - sh, 1)
            mt_ref[gl, s * hg:(s + 1) * hg, :] = blk[:, 0:w].astype(BF16)
        l16_re, l16_im = power(float(t))
        lre_ref[gl:gl + 1, :] = l16_re
        lim_ref[gl:gl + 1, :] = l16_im


def _s5_tables(lam_re, lam_im, log_step, b_re, b_im, c_re, c_im):
    g, p, hg = b_re.shape
    w = S5_T * hg
    lanes = lambda x: jnp.concatenate([x[0], x[1]], axis=-1)
    dt = lanes(jnp.broadcast_to(jnp.exp(log_step)[..., None], lam_re.shape))
    bt_re = jnp.transpose(b_re, (0, 2, 1))
    bt_im = jnp.transpose(b_im, (0, 2, 1))
    vec = pl.BlockSpec((S5_TG, 2 * p), lambda j: (j, 0))
    mat = pl.BlockSpec((S5_TG, hg, p), lambda j: (j, 0, 0))
    tab = pl.BlockSpec((S5_TG, w, w), lambda j: (j, 0, 0))
    return pl.pallas_call(
        _s5_tables_kernel,
        grid=(g // S5_TG,),
        in_specs=[vec, vec, vec, mat, mat, mat, mat],
        out_specs=[tab, tab, tab, vec, vec],
        out_shape=[jax.ShapeDtypeStruct((g, w, w), BF16)] * 3 + [jax.ShapeDtypeStruct((g, 2 * p), F32)] * 2,
        compiler_params=_cparams(("parallel",), 32),
    )(lanes(lam_re), lanes(lam_im), dt, bt_re, bt_im, c_re, c_im)


def _block_transpose8(xs):
    lane_blk = lax.broadcasted_iota(jnp.int32, xs[0].shape, 1) >> 4
    xs = list(xs)
    for d in (4, 2, 1):
        hi = (lane_blk & d) != 0
        for i in range(8):
            if i & d:
                continue
            a, b = xs[i], xs[i + d]
            xs[i] = jnp.where(hi, pltpu.roll(b, 16 * d, 1), a)
            xs[i + d] = jnp.where(hi, b, pltpu.roll(a, 128 - 16 * d, 1))
    return xs


def _s5_kernel(u0_ref, u1_ref, inj_ref, mt_ref, qt_ref, lre_ref, lim_ref, y_ref, ug_ref, zre_ref, zim_ref,
               hre_ref, him_ref, *, nchunk, nctx):
    gh = ug_ref.shape[0]
    ps = S5_STATE
    t = S5_T
    nlat = nchunk - nctx
    for jt, u_ref in enumerate((u0_ref, u1_ref)):
        for q in range(t // 8):
            cols = [u_ref[0, pl.ds(8 * q + tt, nchunk, stride=t), :] for tt in range(8)]
            grp = _block_transpose8(cols)
            for gl in range(8):
                ug_ref[8 * jt + gl, :, 128 * q:128 * (q + 1)] = grp[gl].astype(BF16)
    for gi in range(gh):
        zg = _dot(ug_ref[gi], inj_ref[gi], NN)
        zre_ref[pl.ds(gi, nchunk, stride=gh), :] = zg[:, 0:2 * ps]
        zim_ref[pl.ds(gi, nchunk, stride=gh), :] = zg[:, 2 * ps:4 * ps]

    lane = lax.broadcasted_iota(jnp.int32, (gh, 2 * ps), 1)
    is_f = lane < ps
    lre = lre_ref[...]
    lim = lim_ref[...]

    def body(i, carry):
        hre, him = carry
        cb = jnp.where(i < nctx, nctx - 1 - i, nchunk + nctx - 1 - i)
        rf = pl.multiple_of(i * gh, gh)
        rb = pl.multiple_of(cb * gh, gh)
        hre_ref[pl.ds(rf, gh), 0:ps] = hre[:, 0:ps]
        hre_ref[pl.ds(rb, gh), ps:2 * ps] = hre[:, ps:2 * ps]
        him_ref[pl.ds(rf, gh), 0:ps] = him[:, 0:ps]
        him_ref[pl.ds(rb, gh), ps:2 * ps] = him[:, ps:2 * ps]
        zre = jnp.where(is_f, zre_ref[pl.ds(rf, gh), :], zre_ref[pl.ds(rb, gh), :])
        zim = jnp.where(is_f, zim_ref[pl.ds(rf, gh), :], zim_ref[pl.ds(rb, gh), :])
        return lre * hre - lim * him + zre, lre * him + lim * hre + zim

    zero = jnp.zeros((gh, 2 * ps), F32)
    lax.fori_loop(0, nchunk, body, (zero, zero))

    for jt in range(gh // 8):
        ys = []
        for gl in range(8):
            gi = 8 * jt + gl
            hre = hre_ref[pl.ds(gi, nchunk, stride=gh), :].astype(BF16)
            him = him_ref[pl.ds(gi, nchunk, stride=gh), :].astype(BF16)
            ys.append(_dot(ug_ref[gi], mt_ref[gi], NN)
                      + _dot(jnp.concatenate([hre, him], axis=1), qt_ref[gi], NT))
        for q in range(t // 8):
            tok = _block_transpose8([y[:, 128 * q:128 * (q + 1)] for y in ys])
            for tt in range(8):
                y_ref[0, jt, pl.ds(8 * q + tt, nlat, stride=t), :] = tok[tt][nctx:]


def _s5(u, tables, ctx_len):
    inj, mt, qt, lre, lim = tables
    bsz, ltot, _ = u.shape
    nchunk = ltot // S5_T
    nctx = ctx_len // S5_T
    gh = S5_GH
    halves = S5_GROUPS // gh
    w = S5_T * S5_GROUP
    tiles = gh * S5_GROUP // 128
    ps2 = 2 * S5_STATE
    kern = functools.partial(_s5_kernel, nchunk=nchunk, nctx=nctx)
    u_spec = lambda k: pl.BlockSpec((1, ltot, 128), lambda b, j: (b, 0, tiles * j + k))
    tab = pl.BlockSpec((gh, w, w), lambda b, j: (j, 0, 0))
    vec = pl.BlockSpec((gh, ps2), lambda b, j: (j, 0))
    return pl.pallas_call(
        kern,
        grid=(bsz, halves),
        in_specs=[u_spec(0), u_spec(1), tab, tab, tab, vec, vec],
        out_specs=pl.BlockSpec((1, tiles, ltot - ctx_len, 128), lambda b, j: (b, j, 0, 0)),
        out_shape=jax.ShapeDtypeStruct((bsz, D_S5 // 128, ltot - ctx_len, 128), F32),
        scratch_shapes=[pltpu.VMEM((gh, nchunk, w), BF16),
                        pltpu.VMEM((nchunk * gh, ps2), F32),
                        pltpu.VMEM((nchunk * gh, ps2), F32),
                        pltpu.VMEM((nchunk * gh, ps2), F32),
                        pltpu.VMEM((nchunk * gh, ps2), F32)],
        compiler_params=_cparams(("parallel", "parallel"), 48),
    )(u, u, inj, mt, qt, lre, lim)


def _mix_kernel(x_ref, yf_ref, yb_ref, r_ref, k_ref, v_ref, lo_ref, u_ref, ys_ref, mod_ref,
                seg_ref, a0_ref, a2_ref, g2_ref, kaw_ref, rk_ref, lnw_ref, lnb_ref,
                dsk_ref, gluw_ref, glub_ref, wo_ref, g2n_ref, w1_ref, w3_ref, w2_ref, gf_ref, o_ref):
    lo = lo_ref[0].astype(F32)
    seg = seg_ref[...]

    def head_sum(t):
        pw = seg.shape[0]
        return jnp.concatenate([_mm(t[:, i:i + pw], seg) for i in range(0, t.shape[1], pw)], axis=1)

    y = yf_ref[0] + yb_ref[0]
    mu = head_sum(y) * (1.0 / HEAD)
    yc = y - mu
    var = head_sum(yc * yc) * (1.0 / HEAD)
    yn = yc * lax.rsqrt(var + RWKV_LN_EPS) * lnw_ref[...] + lnb_ref[...]
    a_f = _sigmoid(a0_ref[0] + _mm(lo, a2_ref[0]))
    a_b = _sigmoid(a0_ref[1] + _mm(lo, a2_ref[1]))
    kaw = kaw_ref[...]
    kd_sum = k_ref[0] * ((1.0 + (a_f - 1.0) * kaw) + (1.0 + (a_b - 1.0) * kaw))
    bonus = head_sum(r_ref[0] * kd_sum * rk_ref[...])
    yn = yn + bonus * v_ref[0]
    gate = _mm(_sigmoid(lo), g2_ref[...])
    rwkv_out = yn * gate

    u = u_ref[0]
    ys = jnp.concatenate([ys_ref[0, i] for i in range(ys_ref.shape[1])], axis=1) + dsk_ref[...] * u
    zg = 0.5 * ys * (1.0 + jnp.tanh(math.sqrt(2.0 / math.pi) * (ys + 0.044715 * (ys * ys * ys))))
    s5_out = zg * _sigmoid(_mm(zg, gluw_ref[...]) + glub_ref[...])

    mix = _mm(rwkv_out, wo_ref[0:D_RWKV, :]) + _mm(s5_out, wo_ref[D_RWKV:, :])
    x1 = x_ref[0] + mod_ref[0, 2:3, :] * mix

    ms = jnp.mean(x1 * x1, axis=-1, keepdims=True)
    h = x1 * lax.rsqrt(ms + NORM_EPS) * g2n_ref[...]
    h = h * (1.0 + mod_ref[0, 4:5, :]) + mod_ref[0, 3:4, :]
    hb = h.astype(BF16)
    h1 = _dot(hb, w1_ref[...], NN)
    h3 = _dot(hb, w3_ref[...], NN)
    act = (h1 * _sigmoid(h1)) * h3
    out = x1 + mod_ref[0, 5:6, :] * _dot(act.astype(BF16), w2_ref[...], NN)
    ms2 = jnp.mean(out * out, axis=-1, keepdims=True)
    o_ref[0] = out * lax.rsqrt(ms2 + NORM_EPS) * gf_ref[...]


def _mix(x, y_f, y_b, rkvc, z, u, ys5, mod8, seg, a0, a2p, g2p, kaw, rk, lnw, lnb, dsk, gluw, glub, wo,
         g2n, w1, w3, w2, gf, ctx_len, tm):
    bsz, seq, d = x.shape
    assert ctx_len % tm == 0 and seq % tm == 0
    off = ctx_len // tm
    lora_blk = COL_LORA // LORA_W
    tok = lambda c: pl.BlockSpec((1, tm, D_RWKV), lambda b, j: (b, j + off, c))
    full = lambda a: pl.BlockSpec(a.shape, lambda b, j: (0,) * a.ndim, pipeline_mode=pl.Buffered(1))
    return pl.pallas_call(
        _mix_kernel,
        grid=(bsz, seq // tm),
        in_specs=[pl.BlockSpec((1, tm, d), lambda b, j: (b, j, 0)),
                  tok(0), tok(0), tok(0), tok(1), tok(2),
                  pl.BlockSpec((1, tm, LORA_W), lambda b, j: (b, j + off, lora_blk)),
                  pl.BlockSpec((1, tm, D_S5), lambda b, j: (b, j + off, 0)),
                  pl.BlockSpec((1, D_S5 // 128, tm, 128), lambda b, j: (b, 0, j, 0)),
                  pl.BlockSpec((1, 8, d), lambda b, j: (b, 0, 0)),
                  full(seg), full(a0), full(a2p), full(g2p), full(kaw), full(rk), full(lnw), full(lnb),
                  full(dsk), full(gluw), full(glub), full(wo),
                  full(g2n), full(w1), full(w3), full(w2), full(gf)],
        out_specs=pl.BlockSpec((1, tm, d), lambda b, j: (b, j, 0)),
        out_shape=jax.ShapeDtypeStruct((bsz, seq, d), F32),
        compiler_params=_cparams(("parallel", "parallel"), 56),
    )(x, y_f, y_b, rkvc, rkvc, rkvc, z, u, ys5, mod8, seg, a0, a2p, g2p, kaw, rk, lnw, lnb,
      dsk, gluw, glub, wo, g2n, w1, w3, w2, gf)


def _pad_rows(w, row0, rows_total):
    pad = [(0, 0)] * (w.ndim - 2) + [(row0, rows_total - row0 - w.shape[-2]), (0, 0)]
    return jnp.pad(w, pad)


def kernel(x, c, ctx, c_ctx, mod_w, mod_b, norm1_g, norm2_g, w_in, w_out, rwkv_conv, rwkv_w0, rwkv_w2, rwkv_a0, rwkv_a2, rwkv_g2, rwkv_kk, rwkv_ka, rwkv_rk, rwkv_ln_w, rwkv_ln_b, s5_lam_re, s5_lam_im, s5_log_step, s5_b_re, s5_b_im, s5_c_re, s5_c_im, s5_d, s5_glu_w, s5_glu_b, ffn_w1, ffn_w3, ffn_w2, final_g):
    bsz, seq, d = x.shape
    ctx_len = ctx.shape[1]
    ltot = ctx_len + seq
    layer = 0

    cc = jnp.concatenate([c, c_ctx[None, :], jnp.zeros((8 - bsz - 1, d), F32)], axis=0)
    mod = _adaln(cc, mod_w[layer], mod_b[layer])
    mod8 = jnp.pad(mod[:bsz + 1].reshape(bsz + 1, N_MOD, d), ((0, 0), (0, 8 - N_MOD), (0, 0)))

    wi = w_in[layer]
    w_z = jnp.concatenate([wi[:, :-D_S5], jnp.zeros((d, D_Z16 + D_S5 - wi.shape[1]), F32),
                           wi[:, -D_S5:]], axis=1).astype(BF16)
    z, u = _inproj(ctx, x, mod8, norm1_g[layer], w_z, 256)

    rkvc = _conv(z, rwkv_conv[layer], ctx_len)

    w2p = _pad_rows(rwkv_w2[layer], 0, LORA_W).at[1].set(_pad_rows(rwkv_w2[layer, 1], DECAY_LORA, LORA_W))
    a2p = jnp.stack([_pad_rows(rwkv_a2[layer, 0], LORA_AD, LORA_W),
                     _pad_rows(rwkv_a2[layer, 1], LORA_AD + AAA_LORA, LORA_W)])
    w0 = rwkv_w0[layer].reshape(2, 1, D_RWKV)
    a0 = rwkv_a0[layer].reshape(2, 1, D_RWKV)
    kkw = rwkv_kk[layer].reshape(1, D_RWKV)
    kaw = rwkv_ka[layer].reshape(1, D_RWKV)
    y_f = _rwkv(rkvc, z, w0, w2p, a0, a2p, kkw, kaw, ctx_len, False)
    y_b = _rwkv(rkvc, z, w0, w2p, a0, a2p, kkw, kaw, ctx_len, True)

    tables = _s5_tables(s5_lam_re[layer], s5_lam_im[layer], s5_log_step[layer], s5_b_re[layer],
                        s5_b_im[layer], s5_c_re[layer], s5_c_im[layer])
    ys5 = _s5(u, tables, ctx_len)

    hh = jnp.arange(2 * HEAD) // HEAD
    seg = (hh[:, None] == hh[None, :]).astype(BF16)
    g2p = _pad_rows(rwkv_g2[layer], LORA_GD, LORA_W)
    return _mix(x, y_f, y_b, rkvc, z, u, ys5, mod8, seg, a0, a2p, g2p, kaw,
                rwkv_rk[layer].reshape(1, D_RWKV), rwkv_ln_w[layer].reshape(1, D_RWKV),
                rwkv_ln_b[layer].reshape(1, D_RWKV), s5_d[layer].reshape(1, D_S5),
                s5_glu_w[layer].astype(BF16), s5_glu_b[layer].reshape(1, D_S5),
                w_out[layer].astype(BF16), norm2_g[layer].reshape(1, d), ffn_w1[layer].astype(BF16),
                ffn_w3[layer].astype(BF16), ffn_w2[layer].astype(BF16), final_g.reshape(1, d), ctx_len, 256)
```

```python
import functools
import math

import jax
import jax.numpy as jnp
import numpy as np
from jax import lax
from jax.experimental import pallas as pl
from jax.experimental.pallas import tpu as pltpu

F32 = jnp.float32
BF16 = jnp.bfloat16

D_MODEL = 1024
GRID_W = 64
D_RWKV = 512
HEAD = 64
HEADS = D_RWKV // HEAD
D_S5 = 512
S5_GROUP = 16
S5_GROUPS = D_S5 // S5_GROUP
S5_STATE = 64
DECAY_LORA = 32
AAA_LORA = 32
GATE_LORA = 96
N_MOD = 6
NORM_EPS = 1e-6
RWKV_LN_EPS = 64e-5

LORA_W = 256
COL_LORA = 3 * D_RWKV
D_Z16 = COL_LORA + LORA_W
LORA_AD = 2 * DECAY_LORA
LORA_GD = LORA_AD + 2 * AAA_LORA

CHUNK = 64
RWKV_BLK = 256
S5_T = 16
S5_GH = 16
S5_TG = 8
CONV_PAD = 72
CONV_TC = 256
CONV_PIECE = 128

NN = (((1,), (0,)), ((), ()))
NT = (((1,), (1,)), ((), ()))
TN = (((0,), (0,)), ((), ()))


def _dot(a, b, dims):
    return lax.dot_general(a, b, dims, preferred_element_type=F32)


def _split2(x):
    hi = x.astype(BF16)
    lo = (x - hi.astype(F32)).astype(BF16)
    return hi, lo


def _mm(a, b, mode="bf16", dims=NN):
    if mode == "bf16":
        return _dot(a.astype(BF16), b.astype(BF16), dims)
    if mode == "x3":
        ah, al = _split2(a)
        bh, bl = _split2(b)
        return _dot(ah, bh, dims) + (_dot(ah, bl, dims) + _dot(al, bh, dims))
    if mode == "xb2":
        ab = a.astype(BF16)
        b1, b2 = _split2(b)
        return _dot(ab, b1, dims) + _dot(ab, b2, dims)
    raise ValueError(mode)


def _sigmoid(x):
    return 1.0 / (1.0 + jnp.exp(-x))


def _cparams(sem, vmem_mb):
    return pltpu.CompilerParams(dimension_semantics=sem, vmem_limit_bytes=vmem_mb * 1024 * 1024)


def _adaln_kernel(c_ref, w_ref, b_ref, o_ref):
    c = c_ref[...]
    s = c * _sigmoid(c)
    o_ref[...] = _mm(s, w_ref[...], "x3") + b_ref[...]


def _adaln(cc, mod_w, mod_b):
    rows, d = cc.shape
    n = mod_w.shape[1]
    tn = 1024
    return pl.pallas_call(
        _adaln_kernel,
        grid=(n // tn,),
        in_specs=[pl.BlockSpec((rows, d), lambda j: (0, 0)),
                  pl.BlockSpec((d, tn), lambda j: (0, j)),
                  pl.BlockSpec((1, tn), lambda j: (0, j))],
        out_specs=pl.BlockSpec((rows, tn), lambda j: (0, j)),
        out_shape=jax.ShapeDtypeStruct((rows, n), F32),
        compiler_params=_cparams(("parallel",), 40),
    )(cc, mod_w, mod_b.reshape(1, n))


def _inproj_kernel(c_ref, x_ref, mod_ref, g_ref, w_ref, o_ref, u_ref, *, ctx_blocks):
    x = jnp.where(pl.program_id(1) < ctx_blocks, c_ref[0], x_ref[0])
    ms = jnp.mean(x * x, axis=-1, keepdims=True)
    xn = x * lax.rsqrt(ms + NORM_EPS) * g_ref[...]
    shift = mod_ref[0, 0:1, :]
    scale = mod_ref[0, 1:2, :]
    h = xn * (1.0 + scale) + shift
    res = _mm(h, w_ref[...], "bf16")
    o_ref[0] = res[:, :D_Z16].astype(BF16)
    u_ref[0] = res[:, D_Z16:]


def _inproj(ctx, x, mod8, g1, w_z, tm):
    bsz, seq, d = x.shape
    ctx_blocks = ctx.shape[1] // tm
    dz = w_z.shape[1]
    ltot = ctx.shape[1] + seq
    kern = functools.partial(_inproj_kernel, ctx_blocks=ctx_blocks)
    return pl.pallas_call(
        kern,
        grid=(bsz, ctx_blocks + seq // tm),
        in_specs=[pl.BlockSpec((1, tm, d), lambda b, j: (b, jnp.minimum(j, ctx_blocks - 1), 0)),
                  pl.BlockSpec((1, tm, d), lambda b, j: (b, jnp.maximum(j - ctx_blocks, 0), 0)),
                  pl.BlockSpec((1, 8, d), lambda b, j: (jnp.where(j < ctx_blocks, bsz, b), 0, 0)),
                  pl.BlockSpec((1, d), lambda b, j: (0, 0)),
                  pl.BlockSpec((d, dz), lambda b, j: (0, 0))],
        out_specs=[pl.BlockSpec((1, tm, D_Z16), lambda b, j: (b, j, 0)),
                   pl.BlockSpec((1, tm, dz - D_Z16), lambda b, j: (b, j, 0))],
        out_shape=[jax.ShapeDtypeStruct((bsz, ltot, D_Z16), BF16),
                   jax.ShapeDtypeStruct((bsz, ltot, dz - D_Z16), F32)],
        compiler_params=_cparams(("parallel", "parallel"), 48),
    )(ctx, x, mod8, g1.reshape(1, d), w_z)


def _conv_kernel(z_ref, w_ref, o_ref, xs_ref, *, ctx_len, lat_len):
    tc = z_ref.shape[2]
    pc = CONV_PIECE
    ctx_off = CONV_PAD
    lat_off = 2 * CONV_PAD + ctx_len
    zpad = jnp.zeros((CONV_PAD, tc), F32)
    for s in range(3):
        xs_ref[s, 0:CONV_PAD, :] = zpad
        xs_ref[s, ctx_off + ctx_len:lat_off, :] = zpad
        xs_ref[s, lat_off + lat_len:lat_off + lat_len + CONV_PAD, :] = zpad
    xs_ref[1, lat_off:lat_off + 8, :] = zpad[0:8]
    xs_ref[2, lat_off + lat_len - 8:lat_off + lat_len, :] = zpad[0:8]
    col = lax.broadcasted_iota(jnp.int32, (pc, tc), 0) % GRID_W
    keep_l = (col != GRID_W - 1).astype(F32)
    keep_r = (col != 0).astype(F32)
    for p in range(ctx_len // pc):
        xs_ref[0, ctx_off + p * pc:ctx_off + (p + 1) * pc, :] = z_ref[0, p * pc:(p + 1) * pc, :].astype(F32)
    for p in range(lat_len // pc):
        x = z_ref[0, ctx_len + p * pc:ctx_len + (p + 1) * pc, :].astype(F32)
        dst = lat_off + p * pc
        xs_ref[0, dst:dst + pc, :] = x
        xs_ref[1, dst + 1:dst + 1 + pc, :] = x * keep_l
        xs_ref[2, dst - 1:dst - 1 + pc, :] = x * keep_r
    w = [w_ref[t:t + 1, :] for t in range(9)]
    for p in range(ctx_len // pc):
        base = ctx_off + p * pc
        acc = w[3] * xs_ref[0, base - 1:base - 1 + pc, :]
        acc = acc + w[4] * xs_ref[0, base:base + pc, :]
        acc = acc + w[5] * xs_ref[0, base + 1:base + 1 + pc, :]
        o_ref[0, p * pc:(p + 1) * pc, :] = acc
    for p in range(lat_len // pc):
        base = lat_off + p * pc
        acc = None
        for di in range(3):
            for dj in range(3):
                off = base + (di - 1) * GRID_W
                src = (1, 0, 2)[dj]
                term = w[3 * di + dj] * xs_ref[src, off:off + pc, :]
                acc = term if acc is None else acc + term
        o_ref[0, ctx_len + p * pc:ctx_len + (p + 1) * pc, :] = acc


def _conv(z, conv_w, ctx_len):
    bsz, ltot, _ = z.shape
    lat_len = ltot - ctx_len
    ch = 3 * D_RWKV
    rows = 3 * CONV_PAD + ltot
    kern = functools.partial(_conv_kernel, ctx_len=ctx_len, lat_len=lat_len)
    return pl.pallas_call(
        kern,
        grid=(bsz, ch // CONV_TC),
        in_specs=[pl.BlockSpec((1, ltot, CONV_TC), lambda b, j: (b, 0, j)),
                  pl.BlockSpec((9, CONV_TC), lambda b, j: (0, j))],
        out_specs=pl.BlockSpec((1, ltot, CONV_TC), lambda b, j: (b, 0, j)),
        out_shape=jax.ShapeDtypeStruct((bsz, ltot, ch), F32),
        scratch_shapes=[pltpu.VMEM((3, rows, CONV_TC), F32)],
        compiler_params=_cparams(("parallel", "parallel"), 48),
    )(z, conv_w.reshape(9, ch))


def _bd(x, lo):
    return jnp.concatenate([jnp.where(lo, x, 0.0), jnp.where(lo, 0.0, x)], axis=0)


def _rwkv_masks(reverse):
    n, c = RWKV_BLK, CHUNK
    ti, tj = np.indices((n, n))
    causal = (tj >= ti) if reverse else (tj <= ti)
    tri = ((ti // c == tj // c) & causal).astype(np.float32)
    ii, jj = np.indices((c, 2 * c))
    jj = jj % c
    tabs = [(jj > ii) if reverse else (jj < ii), (jj >= ii) if reverse else (jj <= ii), ii == jj,
            ii // 2 == jj // 2]
    m = 2
    while m < c:
        tabs.append((ii // (2 * m) == jj // (2 * m)) & (ii // m != jj // m))
        m *= 2
    return jnp.asarray(tri, BF16), jnp.asarray(np.stack(tabs).astype(np.float32))


def _tri_inv_pairs(n_mats, masks, lo):
    ts = [masks[0] + n * masks[1] for n in n_mats]
    for lvl in masks[2:]:
        nls = [n * lvl for n in n_mats]
        tmp = [_mm(t, _bd(nl, lo)) for t, nl in zip(ts, nls)]
        ts = [t + _mm(x, _bd(t, lo)) for x, t in zip(tmp, ts)]
    return ts


def _rwkv_kernel(r_ref, k_ref, v_ref, lo_ref, w0_ref, w2_ref, a0_ref, a2_ref, kkw_ref, kaw_ref,
                 tri_ref, cm_ref, y_ref, h_ref, *, reverse):
    step = pl.program_id(1)

    @pl.when(step == 0)
    def _():
        h_ref[...] = jnp.zeros_like(h_ref)

    r = r_ref[0]
    k = k_ref[0]
    v = v_ref[0]
    lora = lo_ref[0].astype(F32)
    n = r.shape[0]
    c = CHUNK
    nsub = n // c
    pw = 2 * HEAD
    npair = HEADS // 2

    logit_w = w0_ref[0] + _mm(jnp.tanh(lora), w2_ref[0])
    lw = -math.exp(-0.5) * _sigmoid(logit_w)
    a = _sigmoid(a0_ref[0] + _mm(lora, a2_ref[0]))
    kd = k * ((1.0 - kaw_ref[...]) + a * kaw_ref[...])
    kk = k * kkw_ref[...]

    g = _mm(tri_ref[...], lw, "xb2")

    lo = lax.broadcasted_iota(jnp.int32, (c, pw), 1) < HEAD
    strict2 = cm_ref[0]
    incl2 = cm_ref[1]
    inv_masks = [cm_ref[i] for i in range(2, cm_ref.shape[0])]
    ri = lax.broadcasted_iota(jnp.int32, (pw, pw), 0)
    rj = lax.broadcasted_iota(jnp.int32, (pw, pw), 1)
    blk = (ri < HEAD) == (rj < HEAD)
    eye_p = (ri == rj).astype(F32)
    zeros_c = jnp.zeros((c, pw), F32)
    zeros_p = jnp.zeros((pw, pw), F32)

    def prep(j):
        rows = slice(j * c, (j + 1) * c)
        lw_j = lw[rows]
        g_j = g[rows]
        tot = g_j[0:1] if reverse else g_j[c - 1:c]
        e_prev = jnp.exp(g_j - lw_j)
        e_neg = jnp.exp(-g_j)
        e_pos = jnp.exp(g_j)
        e_tot = jnp.exp(tot)
        out = []
        for p in range(npair):
            sl = slice(p * pw, (p + 1) * pw)
            kkp = kk[rows, sl]
            sq = kkp * kkp
            n2 = jnp.where(lo, jnp.sum(jnp.where(lo, sq, 0.0), axis=-1, keepdims=True),
                           jnp.sum(jnp.where(lo, 0.0, sq), axis=-1, keepdims=True))
            kkp = kkp * lax.rsqrt(jnp.maximum(n2, 1e-12))
            b_t = kkp * a[rows, sl] * e_neg[:, sl]
            k_t = kd[rows, sl] * e_neg[:, sl]
            out.append(dict(a_t=-kkp * e_prev[:, sl], b_t=b_t, k_t=k_t, r_t=r[rows, sl] * e_pos[:, sl],
                            b_e=b_t * e_tot[:, sl], k_e=k_t * e_tot[:, sl], vh=v[rows, sl],
                            g_end=e_tot[:, sl]))
        return out

    def local(us):
        s_bk = [_mm(jnp.concatenate([u["a_t"], u["r_t"]], axis=0),
                    jnp.concatenate([_bd(u["b_t"], lo), _bd(u["k_t"], lo)], axis=0), "bf16", NT)
                for u in us]
        nab = [s[:c, :pw] * strict2 for s in s_bk]
        aak = [s[:c, pw:] * strict2 for s in s_bk]
        arbk = [jnp.concatenate([s[c:, :pw] * incl2, s[c:, pw:] * incl2], axis=1) for s in s_bk]
        t_inv = _tri_inv_pairs(nab, inv_masks, lo)
        x1 = [_mm(m, _bd(u["vh"], lo)) for m, u in zip(aak, us)]
        uw = [_mm(t, jnp.concatenate([_bd(x, lo), _bd(u["a_t"], lo)], axis=1))
              for t, x, u in zip(t_inv, x1, us)]
        yq = [_mm(m, jnp.concatenate(
            [jnp.concatenate([_bd(w_[:, :pw], lo), _bd(w_[:, pw:], lo)], axis=1),
             jnp.concatenate([_bd(u["vh"], lo), zeros_p], axis=1)], axis=0))
            for m, w_, u in zip(arbk, uw, us)]
        gp = [_mm(jnp.concatenate([u["b_e"], u["k_e"]], axis=0),
                  jnp.concatenate([w_, jnp.concatenate([u["vh"], zeros_c], axis=1)], axis=0), "bf16", TN)
              for w_, u in zip(uw, us)]
        res = []
        for u, yq_, gp_ in zip(us, yq, gp):
            q = u["r_t"] + yq_[:, pw:]
            p_bd = jnp.where(blk, gp_[:, pw:], 0.0) + eye_p * u["g_end"]
            res.append((jnp.concatenate([q, p_bd], axis=0), yq_[:, :pw], jnp.where(blk, gp_[:, :pw], 0.0)))
        return res

    hs = [h_ref[p] for p in range(npair)]
    order = list(range(nsub - 1, -1, -1) if reverse else range(nsub))
    for grp in (order,):
        loc = local([u for j in grp for u in prep(j)])
        for gi, j in enumerate(grp):
            res = [_mm(loc[gi * npair + p][0], hs[p]) for p in range(npair)]
            for p in range(npair):
                y_ref[0, j * c:(j + 1) * c, p * pw:(p + 1) * pw] = res[p][:c] + loc[gi * npair + p][1]
                hs[p] = res[p][c:] + loc[gi * npair + p][2]
    for p in range(npair):
        h_ref[p] = hs[p]


def _rwkv(rkvc, z, w0, w2p, a0, a2p, kkw, kaw, ctx_len, reverse):
    bsz, ltot, _ = rkvc.shape
    nblk = ltot // RWKV_BLK
    nctx = ctx_len // RWKV_BLK
    lora_blk = COL_LORA // LORA_W
    d = 1 if reverse else 0

    def tok(i):
        return jnp.where(i < nctx, nctx - 1 - i, nblk + nctx - 1 - i) if reverse else i

    par3 = lambda b, i: (d, 0, 0)
    tri, cmask = _rwkv_masks(reverse)
    kern = functools.partial(_rwkv_kernel, reverse=reverse)
    return pl.pallas_call(
        kern,
        grid=(bsz, nblk),
        in_specs=[pl.BlockSpec((1, RWKV_BLK, D_RWKV), lambda b, i: (b, tok(i), 0)),
                  pl.BlockSpec((1, RWKV_BLK, D_RWKV), lambda b, i: (b, tok(i), 1)),
                  pl.BlockSpec((1, RWKV_BLK, D_RWKV), lambda b, i: (b, tok(i), 2)),
                  pl.BlockSpec((1, RWKV_BLK, LORA_W), lambda b, i: (b, tok(i), lora_blk)),
                  pl.BlockSpec((1, 1, D_RWKV), par3),
                  pl.BlockSpec((1, LORA_W, D_RWKV), par3),
                  pl.BlockSpec((1, 1, D_RWKV), par3),
                  pl.BlockSpec((1, LORA_W, D_RWKV), par3),
                  pl.BlockSpec((1, D_RWKV), lambda b, i: (0, 0)),
                  pl.BlockSpec((1, D_RWKV), lambda b, i: (0, 0)),
                  pl.BlockSpec(tri.shape, lambda b, i: (0, 0)),
                  pl.BlockSpec(cmask.shape, lambda b, i: (0, 0, 0))],
        out_specs=pl.BlockSpec((1, RWKV_BLK, D_RWKV), lambda b, i: (b, tok(i), 0)),
        out_shape=jax.ShapeDtypeStruct((bsz, ltot, D_RWKV), F32),
        scratch_shapes=[pltpu.VMEM((HEADS // 2, 2 * HEAD, 2 * HEAD), F32)],
        compiler_params=_cparams(("parallel", "arbitrary"), 48),
    )(rkvc, rkvc, rkvc, z, w0, w2p, a0, a2p, kkw, kaw, tri, cmask)


def _s5_tables_kernel(lam_re_ref, lam_im_ref, dt_ref, bt_re_ref, bt_im_ref, c_re_ref, c_im_ref,
                      inj_ref, mt_ref, qt_ref, lre_ref, lim_ref):
    t = S5_T
    hg = S5_GROUP
    ps = S5_STATE
    row = lax.broadcasted_iota(jnp.int32, (t, 2 * ps), 0).astype(F32)
    is_f = lax.broadcasted_iota(jnp.int32, (t, 2 * ps), 1) < ps
    is_f1 = is_f[0:1]
    n_inj = jnp.where(is_f, t - 1.0 - row, row)
    n_out = jnp.where(is_f, row + 1.0, t - row)
    n_lag = jnp.where(is_f, row, t - 1.0 - row)
    for gl in range(lam_re_ref.shape[0]):
        lam_re = lam_re_ref[gl:gl + 1, :]
        lam_im = lam_im_ref[gl:gl + 1, :]
        dt = dt_ref[gl:gl + 1, :]

        def power(n):
            mag = jnp.exp(lam_re * dt * n)
            return mag * jnp.cos(lam_im * dt * n), mag * jnp.sin(lam_im * dt * n)

        lb_re, lb_im = power(1.0)
        den = lam_re * lam_re + lam_im * lam_im
        q_re = ((lb_re - 1.0) * lam_re + lb_im * lam_im) / den
        q_im = (lb_im * lam_re - (lb_re - 1.0) * lam_im) / den
        bt_re = jnp.concatenate([bt_re_ref[gl], bt_re_ref[gl]], axis=1)
        bt_im = jnp.concatenate([bt_im_ref[gl], bt_im_ref[gl]], axis=1)
        bb_re = q_re * bt_re - q_im * bt_im
        bb_im = q_re * bt_im + q_im * bt_re
        c_re = jnp.concatenate([c_re_ref[gl], c_re_ref[gl]], axis=1)
        c_im = jnp.concatenate([c_im_ref[gl], c_im_ref[gl]], axis=1)

        pi_re, pi_im = power(n_inj)
        po_re, po_im = power(n_out)
        pl_re, pl_im = power(n_lag)
        ck_re, ck_im = [], []
        for s in range(t):
            wr, wi = pi_re[s:s + 1], pi_im[s:s + 1]
            rows = slice(s * hg, (s + 1) * hg)
            inj_ref[gl, rows, 0:2 * ps] = (wr * bb_re - wi * bb_im).astype(BF16)
            inj_ref[gl, rows, 2 * ps:4 * ps] = (wr * bb_im + wi * bb_re).astype(BF16)
            wr, wi = po_re[s:s + 1], po_im[s:s + 1]
            qt_ref[gl, rows, 0:2 * ps] = (wr * c_re - wi * c_im).astype(BF16)
            qt_ref[gl, rows, 2 * ps:4 * ps] = (-(wr * c_im + wi * c_re)).astype(BF16)
            wr, wi = pl_re[s:s + 1], pl_im[s:s + 1]
            ck_re.append(wr * c_re - wi * c_im)
            ck_im.append(wr * c_im + wi * c_re)
        ck = jnp.concatenate([jnp.concatenate(ck_re, axis=0), jnp.concatenate(ck_im, axis=0)], axis=1)
        bsel = jnp.concatenate(
            [jnp.concatenate([jnp.where(is_f1, bb_re, 0.0), jnp.where(is_f1, -bb_im, 0.0)], axis=1),
             jnp.concatenate([jnp.where(is_f1, 0.0, bb_re), jnp.where(is_f1, 0.0, -bb_im)], axis=1)], axis=0)
        kt = _mm(bsel, ck, "x3", NT)
        w = t * hg
        zero = jnp.zeros((hg, w), F32)
        lag = (pltpu.roll(jnp.concatenate([kt[0:hg], zero], axis=1), w - hg, 1)
               + jnp.concatenate([kt[hg:2 * hg], zero], axis=1))
        for s in range(t):
            sh = (t - 1 - s) * hg
            blk = lag if sh == 0 else pltpu.roll(lag, 2 * w - sh, 1)
            mt_ref[gl, s * hg:(s + 1) * hg, :] = blk[:, 0:w].astype(BF16)
        l16_re, l16_im = power(float(t))
        lre_ref[gl:gl + 1, :] = l16_re
        lim_ref[gl:gl + 1, :] = l16_im


def _s5_tables(lam_re, lam_im, log_step, b_re, b_im, c_re, c_im):
    g, p, hg = b_re.shape
    w = S5_T * hg
    lanes = lambda x: jnp.concatenate([x[0], x[1]], axis=-1)
    dt = lanes(jnp.broadcast_to(jnp.exp(log_step)[..., None], lam_re.shape))
    bt_re = jnp.transpose(b_re, (0, 2, 1))
    bt_im = jnp.transpose(b_im, (0, 2, 1))
    vec = pl.BlockSpec((S5_TG, 2 * p), lambda j: (j, 0))
    mat = pl.BlockSpec((S5_TG, hg, p), lambda j: (j, 0, 0))
    tab = pl.BlockSpec((S5_TG, w, w), lambda j: (j, 0, 0))
    return pl.pallas_call(
        _s5_tables_kernel,
        grid=(g // S5_TG,),
        in_specs=[vec, vec, vec, mat, mat, mat, mat],
        out_specs=[tab, tab, tab, vec, vec],
        out_shape=[jax.ShapeDtypeStruct((g, w, w), BF16)] * 3 + [jax.ShapeDtypeStruct((g, 2 * p), F32)] * 2,
        compiler_params=_cparams(("parallel",), 32),
    )(lanes(lam_re), lanes(lam_im), dt, bt_re, bt_im, c_re, c_im)


def _block_transpose8(xs):
    lane_blk = lax.broadcasted_iota(jnp.int32, xs[0].shape, 1) >> 4
    xs = list(xs)
    for d in (4, 2, 1):
        hi = (lane_blk & d) != 0
        for i in range(8):
            if i & d:
                continue
            a, b = xs[i], xs[i + d]
            xs[i] = jnp.where(hi, pltpu.roll(b, 16 * d, 1), a)
            xs[i + d] = jnp.where(hi, b, pltpu.roll(a, 128 - 16 * d, 1))
    return xs


def _s5_kernel(u0_ref, u1_ref, inj_ref, mt_ref, qt_ref, lre_ref, lim_ref, y_ref, ug_ref, zre_ref, zim_ref,
               hre_ref, him_ref, *, nchunk, nctx):
    gh = ug_ref.shape[0]
    ps = S5_STATE
    t = S5_T
    nlat = nchunk - nctx
    for jt, u_ref in enumerate((u0_ref, u1_ref)):
        for q in range(t // 8):
            cols = [u_ref[0, pl.ds(8 * q + tt, nchunk, stride=t), :] for tt in range(8)]
            grp = _block_transpose8(cols)
            for gl in range(8):
                ug_ref[8 * jt + gl, :, 128 * q:128 * (q + 1)] = grp[gl].astype(BF16)
    for gi in range(gh):
        zg = _dot(ug_ref[gi], inj_ref[gi], NN)
        zre_ref[pl.ds(gi, nchunk, stride=gh), :] = zg[:, 0:2 * ps]
        zim_ref[pl.ds(gi, nchunk, stride=gh), :] = zg[:, 2 * ps:4 * ps]

    lane = lax.broadcasted_iota(jnp.int32, (gh, 2 * ps), 1)
    is_f = lane < ps
    lre = lre_ref[...]
    lim = lim_ref[...]

    def body(i, carry):
        hre, him = carry
        cb = jnp.where(i < nctx, nctx - 1 - i, nchunk + nctx - 1 - i)
        rf = pl.multiple_of(i * gh, gh)
        rb = pl.multiple_of(cb * gh, gh)
        hre_ref[pl.ds(rf, gh), 0:ps] = hre[:, 0:ps]
        hre_ref[pl.ds(rb, gh), ps:2 * ps] = hre[:, ps:2 * ps]
        him_ref[pl.ds(rf, gh), 0:ps] = him[:, 0:ps]
        him_ref[pl.ds(rb, gh), ps:2 * ps] = him[:, ps:2 * ps]
        zre = jnp.where(is_f, zre_ref[pl.ds(rf, gh), :], zre_ref[pl.ds(rb, gh), :])
        zim = jnp.where(is_f, zim_ref[pl.ds(rf, gh), :], zim_ref[pl.ds(rb, gh), :])
        return lre * hre - lim * him + zre, lre * him + lim * hre + zim

    zero = jnp.zeros((gh, 2 * ps), F32)
    lax.fori_loop(0, nchunk, body, (zero, zero))

    for jt in range(gh // 8):
        ys = []
        for gl in range(8):
            gi = 8 * jt + gl
            hre = hre_ref[pl.ds(gi, nchunk, stride=gh), :].astype(BF16)
            him = him_ref[pl.ds(gi, nchunk, stride=gh), :].astype(BF16)
            ys.append(_dot(ug_ref[gi], mt_ref[gi], NN)
                      + _dot(jnp.concatenate([hre, him], axis=1), qt_ref[gi], NT))
        for q in range(t // 8):
            tok = _block_transpose8([y[:, 128 * q:128 * (q + 1)] for y in ys])
            for tt in range(8):
                y_ref[0, jt, pl.ds(8 * q + tt, nlat, stride=t), :] = tok[tt][nctx:]


def _s5(u, tables, ctx_len):
    inj, mt, qt, lre, lim = tables
    bsz, ltot, _ = u.shape
    nchunk = ltot // S5_T
    nctx = ctx_len // S5_T
    gh = S5_GH
    halves = S5_GROUPS // gh
    w = S5_T * S5_GROUP
    tiles = gh * S5_GROUP // 128
    ps2 = 2 * S5_STATE
    kern = functools.partial(_s5_kernel, nchunk=nchunk, nctx=nctx)
    u_spec = lambda k: pl.BlockSpec((1, ltot, 128), lambda b, j: (b, 0, tiles * j + k))
    tab = pl.BlockSpec((gh, w, w), lambda b, j: (j, 0, 0))
    vec = pl.BlockSpec((gh, ps2), lambda b, j: (j, 0))
    return pl.pallas_call(
        kern,
        grid=(bsz, halves),
        in_specs=[u_spec(0), u_spec(1), tab, tab, tab, vec, vec],
        out_specs=pl.BlockSpec((1, tiles, ltot - ctx_len, 128), lambda b, j: (b, j, 0, 0)),
        out_shape=jax.ShapeDtypeStruct((bsz, D_S5 // 128, ltot - ctx_len, 128), F32),
        scratch_shapes=[pltpu.VMEM((gh, nchunk, w), BF16),
                        pltpu.VMEM((nchunk * gh, ps2), F32),
                        pltpu.VMEM((nchunk * gh, ps2), F32),
                        pltpu.VMEM((nchunk * gh, ps2), F32),
                        pltpu.VMEM((nchunk * gh, ps2), F32)],
        compiler_params=_cparams(("parallel", "parallel"), 48),
    )(u, u, inj, mt, qt, lre, lim)


def _mix_kernel(x_ref, yf_ref, yb_ref, r_ref, k_ref, v_ref, lo_ref, u_ref, ys_ref, mod_ref,
                seg_ref, a0_ref, a2_ref, g2_ref, kaw_ref, rk_ref, lnw_ref, lnb_ref,
                dsk_ref, gluw_ref, glub_ref, wo_ref, g2n_ref, w1_ref, w3_ref, w2_ref, gf_ref, o_ref):
    lo = lo_ref[0].astype(F32)
    seg = seg_ref[...]

    def head_sum(t):
        pw = seg.shape[0]
        return jnp.concatenate([_mm(t[:, i:i + pw], seg) for i in range(0, t.shape[1], pw)], axis=1)

    y = yf_ref[0] + yb_ref[0]
    mu = head_sum(y) * (1.0 / HEAD)
    yc = y - mu
    var = head_sum(yc * yc) * (1.0 / HEAD)
    yn = yc * lax.rsqrt(var + RWKV_LN_EPS) * lnw_ref[...] + lnb_ref[...]
    a_f = _sigmoid(a0_ref[0] + _mm(lo, a2_ref[0]))
    a_b = _sigmoid(a0_ref[1] + _mm(lo, a2_ref[1]))
    kaw = kaw_ref[...]
    kd_sum = k_ref[0] * ((1.0 + (a_f - 1.0) * kaw) + (1.0 + (a_b - 1.0) * kaw))
    bonus = head_sum(r_ref[0] * kd_sum * rk_ref[...])
    yn = yn + bonus * v_ref[0]
    gate = _mm(_sigmoid(lo), g2_ref[...])
    rwkv_out = yn * gate

    u = u_ref[0]
    ys = jnp.concatenate([ys_ref[0, i] for i in range(ys_ref.shape[1])], axis=1) + dsk_ref[...] * u
    zg = 0.5 * ys * (1.0 + jnp.tanh(math.sqrt(2.0 / math.pi) * (ys + 0.044715 * (ys * ys * ys))))
    s5_out = zg * _sigmoid(_mm(zg, gluw_ref[...]) + glub_ref[...])

    mix = _mm(rwkv_out, wo_ref[0:D_RWKV, :]) + _mm(s5_out, wo_ref[D_RWKV:, :])
    x1 = x_ref[0] + mod_ref[0, 2:3, :] * mix

    ms = jnp.mean(x1 * x1, axis=-1, keepdims=True)
    h = x1 * lax.rsqrt(ms + NORM_EPS) * g2n_ref[...]
    h = h * (1.0 + mod_ref[0, 4:5, :]) + mod_ref[0, 3:4, :]
    hb = h.astype(BF16)
    h1 = _dot(hb, w1_ref[...], NN)
    h3 = _dot(hb, w3_ref[...], NN)
    act = (h1 * _sigmoid(h1)) * h3
    out = x1 + mod_ref[0, 5:6, :] * _dot(act.astype(BF16), w2_ref[...], NN)
    ms2 = jnp.mean(out * out, axis=-1, keepdims=True)
    o_ref[0] = out * lax.rsqrt(ms2 + NORM_EPS) * gf_ref[...]


def _mix(x, y_f, y_b, rkvc, z, u, ys5, mod8, seg, a0, a2p, g2p, kaw, rk, lnw, lnb, dsk, gluw, glub, wo,
         g2n, w1, w3, w2, gf, ctx_len, tm):
    bsz, seq, d = x.shape
    assert ctx_len % tm == 0 and seq % tm == 0
    off = ctx_len // tm
    lora_blk = COL_LORA // LORA_W
    tok = lambda c: pl.BlockSpec((1, tm, D_RWKV), lambda b, j: (b, j + off, c))
    full = lambda a: pl.BlockSpec(a.shape, lambda b, j: (0,) * a.ndim, pipeline_mode=pl.Buffered(1))
    return pl.pallas_call(
        _mix_kernel,
        grid=(bsz, seq // tm),
        in_specs=[pl.BlockSpec((1, tm, d), lambda b, j: (b, j, 0)),
                  tok(0), tok(0), tok(0), tok(1), tok(2),
                  pl.BlockSpec((1, tm, LORA_W), lambda b, j: (b, j + off, lora_blk)),
                  pl.BlockSpec((1, tm, D_S5), lambda b, j: (b, j + off, 0)),
                  pl.BlockSpec((1, D_S5 // 128, tm, 128), lambda b, j: (b, 0, j, 0)),
                  pl.BlockSpec((1, 8, d), lambda b, j: (b, 0, 0)),
                  full(seg), full(a0), full(a2p), full(g2p), full(kaw), full(rk), full(lnw), full(lnb),
                  full(dsk), full(gluw), full(glub), full(wo),
                  full(g2n), full(w1), full(w3), full(w2), full(gf)],
        out_specs=pl.BlockSpec((1, tm, d), lambda b, j: (b, j, 0)),
        out_shape=jax.ShapeDtypeStruct((bsz, seq, d), F32),
        compiler_params=_cparams(("parallel", "parallel"), 56),
    )(x, y_f, y_b, rkvc, rkvc, rkvc, z, u, ys5, mod8, seg, a0, a2p, g2p, kaw, rk, lnw, lnb,
      dsk, gluw, glub, wo, g2n, w1, w3, w2, gf)


def _pad_rows(w, row0, rows_total):
    pad = [(0, 0)] * (w.ndim - 2) + [(row0, rows_total - row0 - w.shape[-2]), (0, 0)]
    return jnp.pad(w, pad)


def kernel(x, c, ctx, c_ctx, mod_w, mod_b, norm1_g, norm2_g, w_in, w_out, rwkv_conv, rwkv_w0, rwkv_w2, rwkv_a0, rwkv_a2, rwkv_g2, rwkv_kk, rwkv_ka, rwkv_rk, rwkv_ln_w, rwkv_ln_b, s5_lam_re, s5_lam_im, s5_log_step, s5_b_re, s5_b_im, s5_c_re, s5_c_im, s5_d, s5_glu_w, s5_glu_b, ffn_w1, ffn_w3, ffn_w2, final_g):
    bsz, seq, d = x.shape
    ctx_len = ctx.shape[1]
    ltot = ctx_len + seq
    layer = 0

    cc = jnp.concatenate([c, c_ctx[None, :], jnp.zeros((8 - bsz - 1, d), F32)], axis=0)
    mod = _adaln(cc, mod_w[layer], mod_b[layer])
    mod8 = jnp.pad(mod[:bsz + 1].reshape(bsz + 1, N_MOD, d), ((0, 0), (0, 8 - N_MOD), (0, 0)))

    wi = w_in[layer]
    w_z = jnp.concatenate([wi[:, :-D_S5], jnp.zeros((d, D_Z16 + D_S5 - wi.shape[1]), F32),
                           wi[:, -D_S5:]], axis=1).astype(BF16)
    z, u = _inproj(ctx, x, mod8, norm1_g[layer], w_z, 256)

    rkvc = _conv(z, rwkv_conv[layer], ctx_len)

    w2p = _pad_rows(rwkv_w2[layer], 0, LORA_W).at[1].set(_pad_rows(rwkv_w2[layer, 1], DECAY_LORA, LORA_W))
    a2p = jnp.stack([_pad_rows(rwkv_a2[layer, 0], LORA_AD, LORA_W),
                     _pad_rows(rwkv_a2[layer, 1], LORA_AD + AAA_LORA, LORA_W)])
    w0 = rwkv_w0[layer].reshape(2, 1, D_RWKV)
    a0 = rwkv_a0[layer].reshape(2, 1, D_RWKV)
    kkw = rwkv_kk[layer].reshape(1, D_RWKV)
    kaw = rwkv_ka[layer].reshape(1, D_RWKV)
    y_f = _rwkv(rkvc, z, w0, w2p, a0, a2p, kkw, kaw, ctx_len, False)
    y_b = _rwkv(rkvc, z, w0, w2p, a0, a2p, kkw, kaw, ctx_len, True)

    tables = _s5_tables(s5_lam_re[layer], s5_lam_im[layer], s5_log_step[layer], s5_b_re[layer],
                        s5_b_im[layer], s5_c_re[layer], s5_c_im[layer])
    ys5 = _s5(u, tables, ctx_len)

    hh = jnp.arange(2 * HEAD) // HEAD
    seg = (hh[:, None] == hh[None, :]).astype(BF16)
    g2p = _pad_rows(rwkv_g2[layer], LORA_GD, LORA_W)
    return _mix(x, y_f, y_b, rkvc, z, u, ys5, mod8, seg, a0, a2p, g2p, kaw,
                rwkv_rk[layer].reshape(1, D_RWKV), rwkv_ln_w[layer].reshape(1, D_RWKV),
                rwkv_ln_b[layer].reshape(1, D_RWKV), s5_d[layer].reshape(1, D_S5),
                s5_glu_w[layer].astype(BF16), s5_glu_b[layer].reshape(1, D_S5),
                w_out[layer].astype(BF16), norm2_g[layer].reshape(1, d), ffn_w1[layer].astype(BF16),
                ffn_w3[layer].astype(BF16), ffn_w2[layer].astype(BF16), final_g.reshape(1, d), ctx_len, 256)
```

```python
import functools
import math

import jax
import jax.numpy as jnp
import numpy as np
from jax import lax
from jax.experimental import pallas as pl
from jax.experimental.pallas import tpu as pltpu

F32 = jnp.float32
BF16 = jnp.bfloat16

D_MODEL = 1024
GRID_W = 64
D_RWKV = 512
HEAD = 64
HEADS = D_RWKV // HEAD
D_S5 = 512
S5_GROUP = 16
S5_GROUPS = D_S5 // S5_GROUP
S5_STATE = 64
DECAY_LORA = 32
AAA_LORA = 32
GATE_LORA = 96
N_MOD = 6
NORM_EPS = 1e-6
RWKV_LN_EPS = 64e-5

LORA_W = 256
COL_LORA = 3 * D_RWKV
D_Z16 = COL_LORA + LORA_W
LORA_AD = 2 * DECAY_LORA
LORA_GD = LORA_AD + 2 * AAA_LORA

CHUNK = 64
RWKV_BLK = 256
RWKV_NB = 2
S5_T = 16
S5_GH = 16
S5_TG = 8
CONV_PAD = 72
CONV_TC = 256
CONV_PIECE = 128

NN = (((1,), (0,)), ((), ()))
NT = (((1,), (1,)), ((), ()))
TN = (((0,), (0,)), ((), ()))


def _dot(a, b, dims):
    return lax.dot_general(a, b, dims, preferred_element_type=F32)


def _split2(x):
    hi = x.astype(BF16)
    lo = (x - hi.astype(F32)).astype(BF16)
    return hi, lo


def _mm(a, b, mode="bf16", dims=NN):
    if mode == "bf16":
        return _dot(a.astype(BF16), b.astype(BF16), dims)
    if mode == "x3":
        ah, al = _split2(a)
        bh, bl = _split2(b)
        return _dot(ah, bh, dims) + (_dot(ah, bl, dims) + _dot(al, bh, dims))
    if mode == "xb2":
        ab = a.astype(BF16)
        b1, b2 = _split2(b)
        return _dot(ab, b1, dims) + _dot(ab, b2, dims)
    raise ValueError(mode)


def _sigmoid(x):
    return 1.0 / (1.0 + jnp.exp(-x))


def _cparams(sem, vmem_mb):
    return pltpu.CompilerParams(dimension_semantics=sem, vmem_limit_bytes=vmem_mb * 1024 * 1024)


def _adaln_kernel(c_ref, w_ref, b_ref, o_ref):
    c = c_ref[...]
    s = c * _sigmoid(c)
    o_ref[...] = _mm(s, w_ref[...], "x3") + b_ref[...]


def _adaln(cc, mod_w, mod_b):
    rows, d = cc.shape
    n = mod_w.shape[1]
    tn = 1024
    return pl.pallas_call(
        _adaln_kernel,
        grid=(n // tn,),
        in_specs=[pl.BlockSpec((rows, d), lambda j: (0, 0)),
                  pl.BlockSpec((d, tn), lambda j: (0, j)),
                  pl.BlockSpec((1, tn), lambda j: (0, j))],
        out_specs=pl.BlockSpec((rows, tn), lambda j: (0, j)),
        out_shape=jax.ShapeDtypeStruct((rows, n), F32),
        compiler_params=_cparams(("parallel",), 40),
    )(cc, mod_w, mod_b.reshape(1, n))


def _inproj_kernel(c_ref, x_ref, mod_ref, g_ref, w_ref, o_ref, u_ref, *, ctx_blocks):
    x = jnp.where(pl.program_id(1) < ctx_blocks, c_ref[0], x_ref[0])
    ms = jnp.mean(x * x, axis=-1, keepdims=True)
    xn = x * lax.rsqrt(ms + NORM_EPS) * g_ref[...]
    shift = mod_ref[0, 0:1, :]
    scale = mod_ref[0, 1:2, :]
    h = xn * (1.0 + scale) + shift
    res = _mm(h, w_ref[...], "bf16")
    o_ref[0] = res[:, :D_Z16].astype(BF16)
    u_ref[0] = res[:, D_Z16:]


def _inproj(ctx, x, mod8, g1, w_z, tm):
    bsz, seq, d = x.shape
    ctx_blocks = ctx.shape[1] // tm
    dz = w_z.shape[1]
    ltot = ctx.shape[1] + seq
    kern = functools.partial(_inproj_kernel, ctx_blocks=ctx_blocks)
    return pl.pallas_call(
        kern,
        grid=(bsz, ctx_blocks + seq // tm),
        in_specs=[pl.BlockSpec((1, tm, d), lambda b, j: (b, jnp.minimum(j, ctx_blocks - 1), 0)),
                  pl.BlockSpec((1, tm, d), lambda b, j: (b, jnp.maximum(j - ctx_blocks, 0), 0)),
                  pl.BlockSpec((1, 8, d), lambda b, j: (jnp.where(j < ctx_blocks, bsz, b), 0, 0)),
                  pl.BlockSpec((1, d), lambda b, j: (0, 0)),
                  pl.BlockSpec((d, dz), lambda b, j: (0, 0))],
        out_specs=[pl.BlockSpec((1, tm, D_Z16), lambda b, j: (b, j, 0)),
                   pl.BlockSpec((1, tm, dz - D_Z16), lambda b, j: (b, j, 0))],
        out_shape=[jax.ShapeDtypeStruct((bsz, ltot, D_Z16), BF16),
                   jax.ShapeDtypeStruct((bsz, ltot, dz - D_Z16), F32)],
        compiler_params=_cparams(("parallel", "parallel"), 48),
    )(ctx, x, mod8, g1.reshape(1, d), w_z)


def _conv_kernel(z_ref, w_ref, o_ref, xs_ref, *, ctx_len, lat_len):
    tc = z_ref.shape[2]
    pc = CONV_PIECE
    ctx_off = CONV_PAD
    lat_off = 2 * CONV_PAD + ctx_len
    zpad = jnp.zeros((CONV_PAD, tc), F32)
    for s in range(3):
        xs_ref[s, 0:CONV_PAD, :] = zpad
        xs_ref[s, ctx_off + ctx_len:lat_off, :] = zpad
        xs_ref[s, lat_off + lat_len:lat_off + lat_len + CONV_PAD, :] = zpad
    xs_ref[1, lat_off:lat_off + 8, :] = zpad[0:8]
    xs_ref[2, lat_off + lat_len - 8:lat_off + lat_len, :] = zpad[0:8]
    col = lax.broadcasted_iota(jnp.int32, (pc, tc), 0) % GRID_W
    keep_l = (col != GRID_W - 1).astype(F32)
    keep_r = (col != 0).astype(F32)
    for p in range(ctx_len // pc):
        xs_ref[0, ctx_off + p * pc:ctx_off + (p + 1) * pc, :] = z_ref[0, p * pc:(p + 1) * pc, :].astype(F32)
    for p in range(lat_len // pc):
        x = z_ref[0, ctx_len + p * pc:ctx_len + (p + 1) * pc, :].astype(F32)
        dst = lat_off + p * pc
        xs_ref[0, dst:dst + pc, :] = x
        xs_ref[1, dst + 1:dst + 1 + pc, :] = x * keep_l
        xs_ref[2, dst - 1:dst - 1 + pc, :] = x * keep_r
    w = [w_ref[t:t + 1, :] for t in range(9)]
    for p in range(ctx_len // pc):
        base = ctx_off + p * pc
        acc = w[3] * xs_ref[0, base - 1:base - 1 + pc, :]
        acc = acc + w[4] * xs_ref[0, base:base + pc, :]
        acc = acc + w[5] * xs_ref[0, base + 1:base + 1 + pc, :]
        o_ref[0, p * pc:(p + 1) * pc, :] = acc
    for p in range(lat_len // pc):
        base = lat_off + p * pc
        acc = None
        for di in range(3):
            for dj in range(3):
                off = base + (di - 1) * GRID_W
                src = (1, 0, 2)[dj]
                term = w[3 * di + dj] * xs_ref[src, off:off + pc, :]
                acc = term if acc is None else acc + term
        o_ref[0, ctx_len + p * pc:ctx_len + (p + 1) * pc, :] = acc


def _conv(z, conv_w, ctx_len):
    bsz, ltot, _ = z.shape
    lat_len = ltot - ctx_len
    ch = 3 * D_RWKV
    rows = 3 * CONV_PAD + ltot
    kern = functools.partial(_conv_kernel, ctx_len=ctx_len, lat_len=lat_len)
    return pl.pallas_call(
        kern,
        grid=(bsz, ch // CONV_TC),
        in_specs=[pl.BlockSpec((1, ltot, CONV_TC), lambda b, j: (b, 0, j)),
                  pl.BlockSpec((9, CONV_TC), lambda b, j: (0, j))],
        out_specs=pl.BlockSpec((1, ltot, CONV_TC), lambda b, j: (b, 0, j)),
        out_shape=jax.ShapeDtypeStruct((bsz, ltot, ch), F32),
        scratch_shapes=[pltpu.VMEM((3, rows, CONV_TC), F32)],
        compiler_params=_cparams(("parallel", "parallel"), 48),
    )(z, conv_w.reshape(9, ch))


def _bd(x, lo):
    return jnp.concatenate([jnp.where(lo, x, 0.0), jnp.where(lo, 0.0, x)], axis=0)


def _rwkv_masks(reverse):
    n, c = RWKV_BLK, CHUNK
    ti, tj = np.indices((n, n))
    causal = (tj >= ti) if reverse else (tj <= ti)
    tri = ((ti // c == tj // c) & causal).astype(np.float32)
    ii, jj = np.indices((c, 2 * c))
    jj = jj % c
    tabs = [(jj > ii) if reverse else (jj < ii), (jj >= ii) if reverse else (jj <= ii), ii == jj,
            ii // 2 == jj // 2]
    m = 2
    while m < c:
        tabs.append((ii // (2 * m) == jj // (2 * m)) & (ii // m != jj // m))
        m *= 2
    return jnp.asarray(tri, BF16), jnp.asarray(np.stack(tabs).astype(np.float32))


def _tri_inv_pairs(n_mats, masks, lo):
    ts = [masks[0] + n * masks[1] for n in n_mats]
    for lvl in masks[2:]:
        nls = [n * lvl for n in n_mats]
        tmp = [_mm(t, _bd(nl, lo)) for t, nl in zip(ts, nls)]
        ts = [t + _mm(x, _bd(t, lo)) for x, t in zip(tmp, ts)]
    return ts


def _rwkv_kernel(r_ref, k_ref, v_ref, lo_ref, w0_ref, w2_ref, a0_ref, a2_ref, kkw_ref, kaw_ref,
                 tri_ref, cm_ref, y_ref, h_ref, *, reverse):
    step = pl.program_id(1)

    @pl.when(step == 0)
    def _():
        h_ref[...] = jnp.zeros_like(h_ref)

    nb, n, _ = r_ref.shape
    c = CHUNK
    nsub = n // c
    pw = 2 * HEAD
    npair = HEADS // 2

    def rows_of(bb):
        r = r_ref[bb]
        k = k_ref[bb]
        v = v_ref[bb]
        lora = lo_ref[bb].astype(F32)
        logit_w = w0_ref[0] + _mm(jnp.tanh(lora), w2_ref[0])
        lw = -math.exp(-0.5) * _sigmoid(logit_w)
        a = _sigmoid(a0_ref[0] + _mm(lora, a2_ref[0]))
        kd = k * ((1.0 - kaw_ref[...]) + a * kaw_ref[...])
        kk = k * kkw_ref[...]
        g = _mm(tri_ref[...], lw, "xb2")
        return r, v, lw, a, kd, kk, g

    lo = lax.broadcasted_iota(jnp.int32, (c, pw), 1) < HEAD
    strict2 = cm_ref[0]
    incl2 = cm_ref[1]
    inv_masks = [cm_ref[i] for i in range(2, cm_ref.shape[0])]
    ri = lax.broadcasted_iota(jnp.int32, (pw, pw), 0)
    rj = lax.broadcasted_iota(jnp.int32, (pw, pw), 1)
    blk = (ri < HEAD) == (rj < HEAD)
    eye_p = (ri == rj).astype(F32)
    zeros_c = jnp.zeros((c, pw), F32)
    zeros_p = jnp.zeros((pw, pw), F32)

    def prep(tokw, j):
        r, v, lw, a, kd, kk, g = tokw
        rows = slice(j * c, (j + 1) * c)
        lw_j = lw[rows]
        g_j = g[rows]
        tot = g_j[0:1] if reverse else g_j[c - 1:c]
        e_prev = jnp.exp(g_j - lw_j)
        e_neg = jnp.exp(-g_j)
        e_pos = jnp.exp(g_j)
        e_tot = jnp.exp(tot)
        out = []
        for p in range(npair):
            sl = slice(p * pw, (p + 1) * pw)
            kkp = kk[rows, sl]
            sq = kkp * kkp
            n2 = jnp.where(lo, jnp.sum(jnp.where(lo, sq, 0.0), axis=-1, keepdims=True),
                           jnp.sum(jnp.where(lo, 0.0, sq), axis=-1, keepdims=True))
            kkp = kkp * lax.rsqrt(jnp.maximum(n2, 1e-12))
            b_t = kkp * a[rows, sl] * e_neg[:, sl]
            k_t = kd[rows, sl] * e_neg[:, sl]
            out.append(dict(a_t=-kkp * e_prev[:, sl], b_t=b_t, k_t=k_t, r_t=r[rows, sl] * e_pos[:, sl],
                            b_e=b_t * e_tot[:, sl], k_e=k_t * e_tot[:, sl], vh=v[rows, sl],
                            g_end=e_tot[:, sl]))
        return out

    def local(us):
        s_bk = [_mm(jnp.concatenate([u["a_t"], u["r_t"]], axis=0),
                    jnp.concatenate([_bd(u["b_t"], lo), _bd(u["k_t"], lo)], axis=0), "bf16", NT)
                for u in us]
        nab = [s[:c, :pw] * strict2 for s in s_bk]
        aak = [s[:c, pw:] * strict2 for s in s_bk]
        arbk = [jnp.concatenate([s[c:, :pw] * incl2, s[c:, pw:] * incl2], axis=1) for s in s_bk]
        t_inv = _tri_inv_pairs(nab, inv_masks, lo)
        x1 = [_mm(m, _bd(u["vh"], lo)) for m, u in zip(aak, us)]
        uw = [_mm(t, jnp.concatenate([_bd(x, lo), _bd(u["a_t"], lo)], axis=1))
              for t, x, u in zip(t_inv, x1, us)]
        yq = [_mm(m, jnp.concatenate(
            [jnp.concatenate([_bd(w_[:, :pw], lo), _bd(w_[:, pw:], lo)], axis=1),
             jnp.concatenate([_bd(u["vh"], lo), zeros_p], axis=1)], axis=0))
            for m, w_, u in zip(arbk, uw, us)]
        gp = [_mm(jnp.concatenate([u["b_e"], u["k_e"]], axis=0),
                  jnp.concatenate([w_, jnp.concatenate([u["vh"], zeros_c], axis=1)], axis=0), "bf16", TN)
              for w_, u in zip(uw, us)]
        res = []
        for u, yq_, gp_ in zip(us, yq, gp):
            q = u["r_t"] + yq_[:, pw:]
            p_bd = jnp.where(blk, gp_[:, pw:], 0.0) + eye_p * u["g_end"]
            res.append((jnp.concatenate([q, p_bd], axis=0), yq_[:, :pw], jnp.where(blk, gp_[:, :pw], 0.0)))
        return res

    order = list(range(nsub - 1, -1, -1) if reverse else range(nsub))
    tokws = [rows_of(bb) for bb in range(nb)]
    loc = local([u for bb in range(nb) for j in order for u in prep(tokws[bb], j)])
    hs = [h_ref[i] for i in range(nb * npair)]
    for gi, j in enumerate(order):
        idx = [(bb * nsub + gi) * npair + p for bb in range(nb) for p in range(npair)]
        res = [_mm(loc[u][0], h) for u, h in zip(idx, hs)]
        for i, u in enumerate(idx):
            bb, p = divmod(i, npair)
            y_ref[bb, j * c:(j + 1) * c, p * pw:(p + 1) * pw] = res[i][:c] + loc[u][1]
            hs[i] = res[i][c:] + loc[u][2]
    for i in range(nb * npair):
        h_ref[i] = hs[i]


def _rwkv(rkvc, z, w0, w2p, a0, a2p, kkw, kaw, ctx_len, reverse):
    bsz, ltot, _ = rkvc.shape
    nblk = ltot // RWKV_BLK
    nctx = ctx_len // RWKV_BLK
    lora_blk = COL_LORA // LORA_W
    d = 1 if reverse else 0

    def tok(i):
        return jnp.where(i < nctx, nctx - 1 - i, nblk + nctx - 1 - i) if reverse else i

    par3 = lambda b, i: (d, 0, 0)
    tri, cmask = _rwkv_masks(reverse)
    kern = functools.partial(_rwkv_kernel, reverse=reverse)
    return pl.pallas_call(
        kern,
        grid=(bsz // RWKV_NB, nblk),
        in_specs=[pl.BlockSpec((RWKV_NB, RWKV_BLK, D_RWKV), lambda b, i: (b, tok(i), 0)),
                  pl.BlockSpec((RWKV_NB, RWKV_BLK, D_RWKV), lambda b, i: (b, tok(i), 1)),
                  pl.BlockSpec((RWKV_NB, RWKV_BLK, D_RWKV), lambda b, i: (b, tok(i), 2)),
                  pl.BlockSpec((RWKV_NB, RWKV_BLK, LORA_W), lambda b, i: (b, tok(i), lora_blk)),
                  pl.BlockSpec((1, 1, D_RWKV), par3),
                  pl.BlockSpec((1, LORA_W, D_RWKV), par3),
                  pl.BlockSpec((1, 1, D_RWKV), par3),
                  pl.BlockSpec((1, LORA_W, D_RWKV), par3),
                  pl.BlockSpec((1, D_RWKV), lambda b, i: (0, 0)),
                  pl.BlockSpec((1, D_RWKV), lambda b, i: (0, 0)),
                  pl.BlockSpec(tri.shape, lambda b, i: (0, 0)),
                  pl.BlockSpec(cmask.shape, lambda b, i: (0, 0, 0))],
        out_specs=pl.BlockSpec((RWKV_NB, RWKV_BLK, D_RWKV), lambda b, i: (b, tok(i), 0)),
        out_shape=jax.ShapeDtypeStruct((bsz, ltot, D_RWKV), F32),
        scratch_shapes=[pltpu.VMEM((RWKV_NB * HEADS // 2, 2 * HEAD, 2 * HEAD), F32)],
        compiler_params=_cparams(("parallel", "arbitrary"), 48),
    )(rkvc, rkvc, rkvc, z, w0, w2p, a0, a2p, kkw, kaw, tri, cmask)


def _s5_tables_kernel(lam_re_ref, lam_im_ref, dt_ref, bt_re_ref, bt_im_ref, c_re_ref, c_im_ref,
                      inj_ref, mt_ref, qt_ref, lre_ref, lim_ref):
    t = S5_T
    hg = S5_GROUP
    ps = S5_STATE
    row = lax.broadcasted_iota(jnp.int32, (t, 2 * ps), 0).astype(F32)
    is_f = lax.broadcasted_iota(jnp.int32, (t, 2 * ps), 1) < ps
    is_f1 = is_f[0:1]
    n_inj = jnp.where(is_f, t - 1.0 - row, row)
    n_out = jnp.where(is_f, row + 1.0, t - row)
    n_lag = jnp.where(is_f, row, t - 1.0 - row)
    for gl in range(lam_re_ref.shape[0]):
        lam_re = lam_re_ref[gl:gl + 1, :]
        lam_im = lam_im_ref[gl:gl + 1, :]
        dt = dt_ref[gl:gl + 1, :]

        def power(n):
            mag = jnp.exp(lam_re * dt * n)
            return mag * jnp.cos(lam_im * dt * n), mag * jnp.sin(lam_im * dt * n)

        lb_re, lb_im = power(1.0)
        den = lam_re * lam_re + lam_im * lam_im
        q_re = ((lb_re - 1.0) * lam_re + lb_im * lam_im) / den
        q_im = (lb_im * lam_re - (lb_re - 1.0) * lam_im) / den
        bt_re = jnp.concatenate([bt_re_ref[gl], bt_re_ref[gl]], axis=1)
        bt_im = jnp.concatenate([bt_im_ref[gl], bt_im_ref[gl]], axis=1)
        bb_re = q_re * bt_re - q_im * bt_im
        bb_im = q_re * bt_im + q_im * bt_re
        c_re = jnp.concatenate([c_re_ref[gl], c_re_ref[gl]], axis=1)
        c_im = jnp.concatenate([c_im_ref[gl], c_im_ref[gl]], axis=1)

        pi_re, pi_im = power(n_inj)
        po_re, po_im = power(n_out)
        pl_re, pl_im = power(n_lag)
        ck_re, ck_im = [], []
        for s in range(t):
            wr, wi = pi_re[s:s + 1], pi_im[s:s + 1]
            rows = slice(s * hg, (s + 1) * hg)
            inj_ref[gl, rows, 0:2 * ps] = (wr * bb_re - wi * bb_im).astype(BF16)
            inj_ref[gl, rows, 2 * ps:4 * ps] = (wr * bb_im + wi * bb_re).astype(BF16)
            wr, wi = po_re[s:s + 1], po_im[s:s + 1]
            qt_ref[gl, rows, 0:2 * ps] = (wr * c_re - wi * c_im).astype(BF16)
            qt_ref[gl, rows, 2 * ps:4 * ps] = (-(wr * c_im + wi * c_re)).astype(BF16)
            wr, wi = pl_re[s:s + 1], pl_im[s:s + 1]
            ck_re.append(wr * c_re - wi * c_im)
            ck_im.append(wr * c_im + wi * c_re)
        ck = jnp.concatenate([jnp.concatenate(ck_re, axis=0), jnp.concatenate(ck_im, axis=0)], axis=1)
        bsel = jnp.concatenate(
            [jnp.concatenate([jnp.where(is_f1, bb_re, 0.0), jnp.where(is_f1, -bb_im, 0.0)], axis=1),
             jnp.concatenate([jnp.where(is_f1, 0.0, bb_re), jnp.where(is_f1, 0.0, -bb_im)], axis=1)], axis=0)
        kt = _mm(bsel, ck, "x3", NT)
        w = t * hg
        zero = jnp.zeros((hg, w), F32)
        lag = (pltpu.roll(jnp.concatenate([kt[0:hg], zero], axis=1), w - hg, 1)
               + jnp.concatenate([kt[hg:2 * hg], zero], axis=1))
        for s in range(t):
            sh = (t - 1 - s) * hg
            blk = lag if sh == 0 else pltpu.roll(lag, 2 * w - sh, 1)
            mt_ref[gl, s * hg:(s + 1) * hg, :] = blk[:, 0:w].astype(BF16)
        l16_re, l16_im = power(float(t))
        lre_ref[gl:gl + 1, :] = l16_re
        lim_ref[gl:gl + 1, :] = l16_im


def _s5_tables(lam_re, lam_im, log_step, b_re, b_im, c_re, c_im):
    g, p, hg = b_re.shape
    w = S5_T * hg
    lanes = lambda x: jnp.concatenate([x[0], x[1]], axis=-1)
    dt = lanes(jnp.broadcast_to(jnp.exp(log_step)[..., None], lam_re.shape))
    bt_re = jnp.transpose(b_re, (0, 2, 1))
    bt_im = jnp.transpose(b_im, (0, 2, 1))
    vec = pl.BlockSpec((S5_TG, 2 * p), lambda j: (j, 0))
    mat = pl.BlockSpec((S5_TG, hg, p), lambda j: (j, 0, 0))
    tab = pl.BlockSpec((S5_TG, w, w), lambda j: (j, 0, 0))
    return pl.pallas_call(
        _s5_tables_kernel,
        grid=(g // S5_TG,),
        in_specs=[vec, vec, vec, mat, mat, mat, mat],
        out_specs=[tab, tab, tab, vec, vec],
        out_shape=[jax.ShapeDtypeStruct((g, w, w), BF16)] * 3 + [jax.ShapeDtypeStruct((g, 2 * p), F32)] * 2,
        compiler_params=_cparams(("parallel",), 32),
    )(lanes(lam_re), lanes(lam_im), dt, bt_re, bt_im, c_re, c_im)


def _block_transpose8(xs):
    lane_blk = lax.broadcasted_iota(jnp.int32, xs[0].shape, 1) >> 4
    xs = list(xs)
    for d in (4, 2, 1):
        hi = (lane_blk & d) != 0
        for i in range(8):
            if i & d:
                continue
            a, b = xs[i], xs[i + d]
            xs[i] = jnp.where(hi, pltpu.roll(b, 16 * d, 1), a)
            xs[i + d] = jnp.where(hi, b, pltpu.roll(a, 128 - 16 * d, 1))
    return xs


def _s5_kernel(u0_ref, u1_ref, inj_ref, mt_ref, qt_ref, lre_ref, lim_ref, y_ref, ug_ref, zre_ref, zim_ref,
               hre_ref, him_ref, *, nchunk, nctx):
    gh = ug_ref.shape[0]
    ps = S5_STATE
    t = S5_T
    nlat = nchunk - nctx
    for jt, u_ref in enumerate((u0_ref, u1_ref)):
        for q in range(t // 8):
            cols = [u_ref[0, pl.ds(8 * q + tt, nchunk, stride=t), :] for tt in range(8)]
            grp = _block_transpose8(cols)
            for gl in range(8):
                ug_ref[8 * jt + gl, :, 128 * q:128 * (q + 1)] = grp[gl].astype(BF16)
    for gi in range(gh):
        zg = _dot(ug_ref[gi], inj_ref[gi], NN)
        zre_ref[pl.ds(gi, nchunk, stride=gh), :] = zg[:, 0:2 * ps]
        zim_ref[pl.ds(gi, nchunk, stride=gh), :] = zg[:, 2 * ps:4 * ps]

    lane = lax.broadcasted_iota(jnp.int32, (gh, 2 * ps), 1)
    is_f = lane < ps
    lre = lre_ref[...]
    lim = lim_ref[...]

    def body(i, carry):
        hre, him = carry
        cb = jnp.where(i < nctx, nctx - 1 - i, nchunk + nctx - 1 - i)
        rf = pl.multiple_of(i * gh, gh)
        rb = pl.multiple_of(cb * gh, gh)
        hre_ref[pl.ds(rf, gh), 0:ps] = hre[:, 0:ps]
        hre_ref[pl.ds(rb, gh), ps:2 * ps] = hre[:, ps:2 * ps]
        him_ref[pl.ds(rf, gh), 0:ps] = him[:, 0:ps]
        him_ref[pl.ds(rb, gh), ps:2 * ps] = him[:, ps:2 * ps]
        zre = jnp.where(is_f, zre_ref[pl.ds(rf, gh), :], zre_ref[pl.ds(rb, gh), :])
        zim = jnp.where(is_f, zim_ref[pl.ds(rf, gh), :], zim_ref[pl.ds(rb, gh), :])
        return lre * hre - lim * him + zre, lre * him + lim * hre + zim

    zero = jnp.zeros((gh, 2 * ps), F32)
    lax.fori_loop(0, nchunk, body, (zero, zero))

    for jt in range(gh // 8):
        ys = []
        for gl in range(8):
            gi = 8 * jt + gl
            hre = hre_ref[pl.ds(gi, nchunk, stride=gh), :].astype(BF16)
            him = him_ref[pl.ds(gi, nchunk, stride=gh), :].astype(BF16)
            ys.append(_dot(ug_ref[gi], mt_ref[gi], NN)
                      + _dot(jnp.concatenate([hre, him], axis=1), qt_ref[gi], NT))
        for q in range(t // 8):
            tok = _block_transpose8([y[:, 128 * q:128 * (q + 1)] for y in ys])
            for tt in range(8):
                y_ref[0, jt, pl.ds(8 * q + tt, nlat, stride=t), :] = tok[tt][nctx:]


def _s5(u, tables, ctx_len):
    inj, mt, qt, lre, lim = tables
    bsz, ltot, _ = u.shape
    nchunk = ltot // S5_T
    nctx = ctx_len // S5_T
    gh = S5_GH
    halves = S5_GROUPS // gh
    w = S5_T * S5_GROUP
    tiles = gh * S5_GROUP // 128
    ps2 = 2 * S5_STATE
    kern = functools.partial(_s5_kernel, nchunk=nchunk, nctx=nctx)
    u_spec = lambda k: pl.BlockSpec((1, ltot, 128), lambda b, j: (b, 0, tiles * j + k))
    tab = pl.BlockSpec((gh, w, w), lambda b, j: (j, 0, 0))
    vec = pl.BlockSpec((gh, ps2), lambda b, j: (j, 0))
    return pl.pallas_call(
        kern,
        grid=(bsz, halves),
        in_specs=[u_spec(0), u_spec(1), tab, tab, tab, vec, vec],
        out_specs=pl.BlockSpec((1, tiles, ltot - ctx_len, 128), lambda b, j: (b, j, 0, 0)),
        out_shape=jax.ShapeDtypeStruct((bsz, D_S5 // 128, ltot - ctx_len, 128), F32),
        scratch_shapes=[pltpu.VMEM((gh, nchunk, w), BF16),
                        pltpu.VMEM((nchunk * gh, ps2), F32),
                        pltpu.VMEM((nchunk * gh, ps2), F32),
                        pltpu.VMEM((nchunk * gh, ps2), F32),
                        pltpu.VMEM((nchunk * gh, ps2), F32)],
        compiler_params=_cparams(("parallel", "parallel"), 48),
    )(u, u, inj, mt, qt, lre, lim)


def _mix_kernel(x_ref, yf_ref, yb_ref, r_ref, k_ref, v_ref, lo_ref, u_ref, ys_ref, mod_ref,
                seg_ref, a0_ref, a2_ref, g2_ref, kaw_ref, rk_ref, lnw_ref, lnb_ref,
                dsk_ref, gluw_ref, glub_ref, wo_ref, g2n_ref, w1_ref, w3_ref, w2_ref, gf_ref, o_ref):
    lo = lo_ref[0].astype(F32)
    seg = seg_ref[...]

    def head_sum(t):
        pw = seg.shape[0]
        return jnp.concatenate([_mm(t[:, i:i + pw], seg) for i in range(0, t.shape[1], pw)], axis=1)

    y = yf_ref[0] + yb_ref[0]
    mu = head_sum(y) * (1.0 / HEAD)
    yc = y - mu
    var = head_sum(yc * yc) * (1.0 / HEAD)
    yn = yc * lax.rsqrt(var + RWKV_LN_EPS) * lnw_ref[...] + lnb_ref[...]
    a_f = _sigmoid(a0_ref[0] + _mm(lo, a2_ref[0]))
    a_b = _sigmoid(a0_ref[1] + _mm(lo, a2_ref[1]))
    kaw = kaw_ref[...]
    kd_sum = k_ref[0] * ((1.0 + (a_f - 1.0) * kaw) + (1.0 + (a_b - 1.0) * kaw))
    bonus = head_sum(r_ref[0] * kd_sum * rk_ref[...])
    yn = yn + bonus * v_ref[0]
    gate = _mm(_sigmoid(lo), g2_ref[...])
    rwkv_out = yn * gate

    u = u_ref[0]
    ys = jnp.concatenate([ys_ref[0, i] for i in range(ys_ref.shape[1])], axis=1) + dsk_ref[...] * u
    zg = 0.5 * ys * (1.0 + jnp.tanh(math.sqrt(2.0 / math.pi) * (ys + 0.044715 * (ys * ys * ys))))
    s5_out = zg * _sigmoid(_mm(zg, gluw_ref[...]) + glub_ref[...])

    mix = _mm(rwkv_out, wo_ref[0:D_RWKV, :]) + _mm(s5_out, wo_ref[D_RWKV:, :])
    x1 = x_ref[0] + mod_ref[0, 2:3, :] * mix

    ms = jnp.mean(x1 * x1, axis=-1, keepdims=True)
    h = x1 * lax.rsqrt(ms + NORM_EPS) * g2n_ref[...]
    h = h * (1.0 + mod_ref[0, 4:5, :]) + mod_ref[0, 3:4, :]
    hb = h.astype(BF16)
    h1 = _dot(hb, w1_ref[...], NN)
    h3 = _dot(hb, w3_ref[...], NN)
    act = (h1 * _sigmoid(h1)) * h3
    out = x1 + mod_ref[0, 5:6, :] * _dot(act.astype(BF16), w2_ref[...], NN)
    ms2 = jnp.mean(out * out, axis=-1, keepdims=True)
    o_ref[0] = out * lax.rsqrt(ms2 + NORM_EPS) * gf_ref[...]


def _mix(x, y_f, y_b, rkvc, z, u, ys5, mod8, seg, a0, a2p, g2p, kaw, rk, lnw, lnb, dsk, gluw, glub, wo,
         g2n, w1, w3, w2, gf, ctx_len, tm):
    bsz, seq, d = x.shape
    assert ctx_len % tm == 0 and seq % tm == 0
    off = ctx_len // tm
    lora_blk = COL_LORA // LORA_W
    tok = lambda c: pl.BlockSpec((1, tm, D_RWKV), lambda b, j: (b, j + off, c))
    full = lambda a: pl.BlockSpec(a.shape, lambda b, j: (0,) * a.ndim, pipeline_mode=pl.Buffered(1))
    return pl.pallas_call(
        _mix_kernel,
        grid=(bsz, seq // tm),
        in_specs=[pl.BlockSpec((1, tm, d), lambda b, j: (b, j, 0)),
                  tok(0), tok(0), tok(0), tok(1), tok(2),
                  pl.BlockSpec((1, tm, LORA_W), lambda b, j: (b, j + off, lora_blk)),
                  pl.BlockSpec((1, tm, D_S5), lambda b, j: (b, j + off, 0)),
                  pl.BlockSpec((1, D_S5 // 128, tm, 128), lambda b, j: (b, 0, j, 0)),
                  pl.BlockSpec((1, 8, d), lambda b, j: (b, 0, 0)),
                  full(seg), full(a0), full(a2p), full(g2p), full(kaw), full(rk), full(lnw), full(lnb),
                  full(dsk), full(gluw), full(glub), full(wo),
                  full(g2n), full(w1), full(w3), full(w2), full(gf)],
        out_specs=pl.BlockSpec((1, tm, d), lambda b, j: (b, j, 0)),
        out_shape=jax.ShapeDtypeStruct((bsz, seq, d), F32),
        compiler_params=_cparams(("parallel", "parallel"), 56),
    )(x, y_f, y_b, rkvc, rkvc, rkvc, z, u, ys5, mod8, seg, a0, a2p, g2p, kaw, rk, lnw, lnb,
      dsk, gluw, glub, wo, g2n, w1, w3, w2, gf)


def _pad_rows(w, row0, rows_total):
    pad = [(0, 0)] * (w.ndim - 2) + [(row0, rows_total - row0 - w.shape[-2]), (0, 0)]
    return jnp.pad(w, pad)


def kernel(x, c, ctx, c_ctx, mod_w, mod_b, norm1_g, norm2_g, w_in, w_out, rwkv_conv, rwkv_w0, rwkv_w2, rwkv_a0, rwkv_a2, rwkv_g2, rwkv_kk, rwkv_ka, rwkv_rk, rwkv_ln_w, rwkv_ln_b, s5_lam_re, s5_lam_im, s5_log_step, s5_b_re, s5_b_im, s5_c_re, s5_c_im, s5_d, s5_glu_w, s5_glu_b, ffn_w1, ffn_w3, ffn_w2, final_g):
    bsz, seq, d = x.shape
    ctx_len = ctx.shape[1]
    ltot = ctx_len + seq
    layer = 0

    cc = jnp.concatenate([c, c_ctx[None, :], jnp.zeros((8 - bsz - 1, d), F32)], axis=0)
    mod = _adaln(cc, mod_w[layer], mod_b[layer])
    mod8 = jnp.pad(mod[:bsz + 1].reshape(bsz + 1, N_MOD, d), ((0, 0), (0, 8 - N_MOD), (0, 0)))

    wi = w_in[layer]
    w_z = jnp.concatenate([wi[:, :-D_S5], jnp.zeros((d, D_Z16 + D_S5 - wi.shape[1]), F32),
                           wi[:, -D_S5:]], axis=1).astype(BF16)
    z, u = _inproj(ctx, x, mod8, norm1_g[layer], w_z, 256)

    rkvc = _conv(z, rwkv_conv[layer], ctx_len)

    w2p = _pad_rows(rwkv_w2[layer], 0, LORA_W).at[1].set(_pad_rows(rwkv_w2[layer, 1], DECAY_LORA, LORA_W))
    a2p = jnp.stack([_pad_rows(rwkv_a2[layer, 0], LORA_AD, LORA_W),
                     _pad_rows(rwkv_a2[layer, 1], LORA_AD + AAA_LORA, LORA_W)])
    w0 = rwkv_w0[layer].reshape(2, 1, D_RWKV)
    a0 = rwkv_a0[layer].reshape(2, 1, D_RWKV)
    kkw = rwkv_kk[layer].reshape(1, D_RWKV)
    kaw = rwkv_ka[layer].reshape(1, D_RWKV)
    y_f = _rwkv(rkvc, z, w0, w2p, a0, a2p, kkw, kaw, ctx_len, False)
    y_b = _rwkv(rkvc, z, w0, w2p, a0, a2p, kkw, kaw, ctx_len, True)

    tables = _s5_tables(s5_lam_re[layer], s5_lam_im[layer], s5_log_step[layer], s5_b_re[layer],
                        s5_b_im[layer], s5_c_re[layer], s5_c_im[layer])
    ys5 = _s5(u, tables, ctx_len)

    hh = jnp.arange(2 * HEAD) // HEAD
    seg = (hh[:, None] == hh[None, :]).astype(BF16)
    g2p = _pad_rows(rwkv_g2[layer], LORA_GD, LORA_W)
    return _mix(x, y_f, y_b, rkvc, z, u, ys5, mod8, seg, a0, a2p, g2p, kaw,
                rwkv_rk[layer].reshape(1, D_RWKV), rwkv_ln_w[layer].reshape(1, D_RWKV),
                rwkv_ln_b[layer].reshape(1, D_RWKV), s5_d[layer].reshape(1, D_S5),
                s5_glu_w[layer].astype(BF16), s5_glu_b[layer].reshape(1, D_S5),
                w_out[layer].astype(BF16), norm2_g[layer].reshape(1, d), ffn_w1[layer].astype(BF16),
                ffn_w3[layer].astype(BF16), ffn_w2[layer].astype(BF16), final_g.reshape(1, d), ctx_len, 256)
```

```python
import functools
import math

import jax
import jax.numpy as jnp
import numpy as np
from jax import lax
from jax.experimental import pallas as pl
from jax.experimental.pallas import tpu as pltpu

F32 = jnp.float32
BF16 = jnp.bfloat16

D_MODEL = 1024
GRID_W = 64
D_RWKV = 512
HEAD = 64
HEADS = D_RWKV // HEAD
D_S5 = 512
S5_GROUP = 16
S5_GROUPS = D_S5 // S5_GROUP
S5_STATE = 64
DECAY_LORA = 32
AAA_LORA = 32
GATE_LORA = 96
N_MOD = 6
NORM_EPS = 1e-6
RWKV_LN_EPS = 64e-5

LORA_W = 256
COL_LORA = 3 * D_RWKV
D_Z16 = COL_LORA + LORA_W
LORA_AD = 2 * DECAY_LORA
LORA_GD = LORA_AD + 2 * AAA_LORA

CHUNK = 64
RWKV_BLK = 256
RWKV_NB = 4
S5_T = 16
S5_GH = 16
S5_TG = 8
CONV_PAD = 72
CONV_TC = 256
CONV_PIECE = 128

NN = (((1,), (0,)), ((), ()))
NT = (((1,), (1,)), ((), ()))
TN = (((0,), (0,)), ((), ()))


def _dot(a, b, dims):
    return lax.dot_general(a, b, dims, preferred_element_type=F32)


def _split2(x):
    hi = x.astype(BF16)
    lo = (x - hi.astype(F32)).astype(BF16)
    return hi, lo


def _mm(a, b, mode="bf16", dims=NN):
    if mode == "bf16":
        return _dot(a.astype(BF16), b.astype(BF16), dims)
    if mode == "x3":
        ah, al = _split2(a)
        bh, bl = _split2(b)
        return _dot(ah, bh, dims) + (_dot(ah, bl, dims) + _dot(al, bh, dims))
    if mode == "xb2":
        ab = a.astype(BF16)
        b1, b2 = _split2(b)
        return _dot(ab, b1, dims) + _dot(ab, b2, dims)
    raise ValueError(mode)


def _sigmoid(x):
    return 1.0 / (1.0 + jnp.exp(-x))


def _cparams(sem, vmem_mb):
    return pltpu.CompilerParams(dimension_semantics=sem, vmem_limit_bytes=vmem_mb * 1024 * 1024)


def _adaln_kernel(c_ref, w_ref, b_ref, o_ref):
    c = c_ref[...]
    s = c * _sigmoid(c)
    o_ref[...] = _mm(s, w_ref[...], "x3") + b_ref[...]


def _adaln(cc, mod_w, mod_b):
    rows, d = cc.shape
    n = mod_w.shape[1]
    tn = 1024
    return pl.pallas_call(
        _adaln_kernel,
        grid=(n // tn,),
        in_specs=[pl.BlockSpec((rows, d), lambda j: (0, 0)),
                  pl.BlockSpec((d, tn), lambda j: (0, j)),
                  pl.BlockSpec((1, tn), lambda j: (0, j))],
        out_specs=pl.BlockSpec((rows, tn), lambda j: (0, j)),
        out_shape=jax.ShapeDtypeStruct((rows, n), F32),
        compiler_params=_cparams(("parallel",), 40),
    )(cc, mod_w, mod_b.reshape(1, n))


def _inproj_kernel(c_ref, x_ref, mod_ref, g_ref, w_ref, o_ref, u_ref, *, ctx_blocks):
    x = jnp.where(pl.program_id(1) < ctx_blocks, c_ref[0], x_ref[0])
    ms = jnp.mean(x * x, axis=-1, keepdims=True)
    xn = x * lax.rsqrt(ms + NORM_EPS) * g_ref[...]
    shift = mod_ref[0, 0:1, :]
    scale = mod_ref[0, 1:2, :]
    h = xn * (1.0 + scale) + shift
    res = _mm(h, w_ref[...], "bf16")
    o_ref[0] = res[:, :D_Z16].astype(BF16)
    u_ref[0] = res[:, D_Z16:]


def _inproj(ctx, x, mod8, g1, w_z, tm):
    bsz, seq, d = x.shape
    ctx_blocks = ctx.shape[1] // tm
    dz = w_z.shape[1]
    ltot = ctx.shape[1] + seq
    kern = functools.partial(_inproj_kernel, ctx_blocks=ctx_blocks)
    return pl.pallas_call(
        kern,
        grid=(bsz, ctx_blocks + seq // tm),
        in_specs=[pl.BlockSpec((1, tm, d), lambda b, j: (b, jnp.minimum(j, ctx_blocks - 1), 0)),
                  pl.BlockSpec((1, tm, d), lambda b, j: (b, jnp.maximum(j - ctx_blocks, 0), 0)),
                  pl.BlockSpec((1, 8, d), lambda b, j: (jnp.where(j < ctx_blocks, bsz, b), 0, 0)),
                  pl.BlockSpec((1, d), lambda b, j: (0, 0)),
                  pl.BlockSpec((d, dz), lambda b, j: (0, 0))],
        out_specs=[pl.BlockSpec((1, tm, D_Z16), lambda b, j: (b, j, 0)),
                   pl.BlockSpec((1, tm, dz - D_Z16), lambda b, j: (b, j, 0))],
        out_shape=[jax.ShapeDtypeStruct((bsz, ltot, D_Z16), BF16),
                   jax.ShapeDtypeStruct((bsz, ltot, dz - D_Z16), F32)],
        compiler_params=_cparams(("parallel", "parallel"), 48),
    )(ctx, x, mod8, g1.reshape(1, d), w_z)


def _conv_kernel(z_ref, w_ref, o_ref, xs_ref, *, ctx_len, lat_len):
    tc = z_ref.shape[2]
    pc = CONV_PIECE
    ctx_off = CONV_PAD
    lat_off = 2 * CONV_PAD + ctx_len
    zpad = jnp.zeros((CONV_PAD, tc), F32)
    for s in range(3):
        xs_ref[s, 0:CONV_PAD, :] = zpad
        xs_ref[s, ctx_off + ctx_len:lat_off, :] = zpad
        xs_ref[s, lat_off + lat_len:lat_off + lat_len + CONV_PAD, :] = zpad
    xs_ref[1, lat_off:lat_off + 8, :] = zpad[0:8]
    xs_ref[2, lat_off + lat_len - 8:lat_off + lat_len, :] = zpad[0:8]
    col = lax.broadcasted_iota(jnp.int32, (pc, tc), 0) % GRID_W
    keep_l = (col != GRID_W - 1).astype(F32)
    keep_r = (col != 0).astype(F32)
    for p in range(ctx_len // pc):
        xs_ref[0, ctx_off + p * pc:ctx_off + (p + 1) * pc, :] = z_ref[0, p * pc:(p + 1) * pc, :].astype(F32)
    for p in range(lat_len // pc):
        x = z_ref[0, ctx_len + p * pc:ctx_len + (p + 1) * pc, :].astype(F32)
        dst = lat_off + p * pc
        xs_ref[0, dst:dst + pc, :] = x
        xs_ref[1, dst + 1:dst + 1 + pc, :] = x * keep_l
        xs_ref[2, dst - 1:dst - 1 + pc, :] = x * keep_r
    w = [w_ref[t:t + 1, :] for t in range(9)]
    for p in range(ctx_len // pc):
        base = ctx_off + p * pc
        acc = w[3] * xs_ref[0, base - 1:base - 1 + pc, :]
        acc = acc + w[4] * xs_ref[0, base:base + pc, :]
        acc = acc + w[5] * xs_ref[0, base + 1:base + 1 + pc, :]
        o_ref[0, p * pc:(p + 1) * pc, :] = acc
    for p in range(lat_len // pc):
        base = lat_off + p * pc
        acc = None
        for di in range(3):
            for dj in range(3):
                off = base + (di - 1) * GRID_W
                src = (1, 0, 2)[dj]
                term = w[3 * di + dj] * xs_ref[src, off:off + pc, :]
                acc = term if acc is None else acc + term
        o_ref[0, ctx_len + p * pc:ctx_len + (p + 1) * pc, :] = acc


def _conv(z, conv_w, ctx_len):
    bsz, ltot, _ = z.shape
    lat_len = ltot - ctx_len
    ch = 3 * D_RWKV
    rows = 3 * CONV_PAD + ltot
    kern = functools.partial(_conv_kernel, ctx_len=ctx_len, lat_len=lat_len)
    return pl.pallas_call(
        kern,
        grid=(bsz, ch // CONV_TC),
        in_specs=[pl.BlockSpec((1, ltot, CONV_TC), lambda b, j: (b, 0, j)),
                  pl.BlockSpec((9, CONV_TC), lambda b, j: (0, j))],
        out_specs=pl.BlockSpec((1, ltot, CONV_TC), lambda b, j: (b, 0, j)),
        out_shape=jax.ShapeDtypeStruct((bsz, ltot, ch), F32),
        scratch_shapes=[pltpu.VMEM((3, rows, CONV_TC), F32)],
        compiler_params=_cparams(("parallel", "parallel"), 48),
    )(z, conv_w.reshape(9, ch))


def _bd(x, lo):
    return jnp.concatenate([jnp.where(lo, x, 0.0), jnp.where(lo, 0.0, x)], axis=0)


def _rwkv_masks(reverse):
    n, c = RWKV_BLK, CHUNK
    ti, tj = np.indices((n, n))
    causal = (tj >= ti) if reverse else (tj <= ti)
    tri = ((ti // c == tj // c) & causal).astype(np.float32)
    ii, jj = np.indices((c, 2 * c))
    jj = jj % c
    tabs = [(jj > ii) if reverse else (jj < ii), (jj >= ii) if reverse else (jj <= ii), ii == jj,
            ii // 2 == jj // 2]
    m = 2
    while m < c:
        tabs.append((ii // (2 * m) == jj // (2 * m)) & (ii // m != jj // m))
        m *= 2
    return jnp.asarray(tri, BF16), jnp.asarray(np.stack(tabs).astype(np.float32))


def _tri_inv_pairs(n_mats, masks, lo):
    ts = [masks[0] + n * masks[1] for n in n_mats]
    for lvl in masks[2:]:
        nls = [n * lvl for n in n_mats]
        tmp = [_mm(t, _bd(nl, lo)) for t, nl in zip(ts, nls)]
        ts = [t + _mm(x, _bd(t, lo)) for x, t in zip(tmp, ts)]
    return ts


def _rwkv_kernel(r_ref, k_ref, v_ref, lo_ref, w0_ref, w2_ref, a0_ref, a2_ref, kkw_ref, kaw_ref,
                 tri_ref, cm_ref, y_ref, h_ref, *, reverse):
    step = pl.program_id(1)

    @pl.when(step == 0)
    def _():
        h_ref[...] = jnp.zeros_like(h_ref)

    nb, n, _ = r_ref.shape
    c = CHUNK
    nsub = n // c
    pw = 2 * HEAD
    npair = HEADS // 2

    def rows_of(bb):
        r = r_ref[bb]
        k = k_ref[bb]
        v = v_ref[bb]
        lora = lo_ref[bb].astype(F32)
        logit_w = w0_ref[0] + _mm(jnp.tanh(lora), w2_ref[0])
        lw = -math.exp(-0.5) * _sigmoid(logit_w)
        a = _sigmoid(a0_ref[0] + _mm(lora, a2_ref[0]))
        kd = k * ((1.0 - kaw_ref[...]) + a * kaw_ref[...])
        kk = k * kkw_ref[...]
        g = _mm(tri_ref[...], lw, "xb2")
        return r, v, lw, a, kd, kk, g

    lo = lax.broadcasted_iota(jnp.int32, (c, pw), 1) < HEAD
    strict2 = cm_ref[0]
    incl2 = cm_ref[1]
    inv_masks = [cm_ref[i] for i in range(2, cm_ref.shape[0])]
    ri = lax.broadcasted_iota(jnp.int32, (pw, pw), 0)
    rj = lax.broadcasted_iota(jnp.int32, (pw, pw), 1)
    blk = (ri < HEAD) == (rj < HEAD)
    eye_p = (ri == rj).astype(F32)
    zeros_c = jnp.zeros((c, pw), F32)
    zeros_p = jnp.zeros((pw, pw), F32)

    def prep(tokw, j):
        r, v, lw, a, kd, kk, g = tokw
        rows = slice(j * c, (j + 1) * c)
        lw_j = lw[rows]
        g_j = g[rows]
        tot = g_j[0:1] if reverse else g_j[c - 1:c]
        e_prev = jnp.exp(g_j - lw_j)
        e_neg = jnp.exp(-g_j)
        e_pos = jnp.exp(g_j)
        e_tot = jnp.exp(tot)
        out = []
        for p in range(npair):
            sl = slice(p * pw, (p + 1) * pw)
            kkp = kk[rows, sl]
            sq = kkp * kkp
            n2 = jnp.where(lo, jnp.sum(jnp.where(lo, sq, 0.0), axis=-1, keepdims=True),
                           jnp.sum(jnp.where(lo, 0.0, sq), axis=-1, keepdims=True))
            kkp = kkp * lax.rsqrt(jnp.maximum(n2, 1e-12))
            b_t = kkp * a[rows, sl] * e_neg[:, sl]
            k_t = kd[rows, sl] * e_neg[:, sl]
            out.append(dict(a_t=-kkp * e_prev[:, sl], b_t=b_t, k_t=k_t, r_t=r[rows, sl] * e_pos[:, sl],
                            b_e=b_t * e_tot[:, sl], k_e=k_t * e_tot[:, sl], vh=v[rows, sl],
                            g_end=e_tot[:, sl]))
        return out

    def local(us):
        s_bk = [_mm(jnp.concatenate([u["a_t"], u["r_t"]], axis=0),
                    jnp.concatenate([_bd(u["b_t"], lo), _bd(u["k_t"], lo)], axis=0), "bf16", NT)
                for u in us]
        nab = [s[:c, :pw] * strict2 for s in s_bk]
        aak = [s[:c, pw:] * strict2 for s in s_bk]
        arbk = [jnp.concatenate([s[c:, :pw] * incl2, s[c:, pw:] * incl2], axis=1) for s in s_bk]
        t_inv = _tri_inv_pairs(nab, inv_masks, lo)
        x1 = [_mm(m, _bd(u["vh"], lo)) for m, u in zip(aak, us)]
        uw = [_mm(t, jnp.concatenate([_bd(x, lo), _bd(u["a_t"], lo)], axis=1))
              for t, x, u in zip(t_inv, x1, us)]
        yq = [_mm(m, jnp.concatenate(
            [jnp.concatenate([_bd(w_[:, :pw], lo), _bd(w_[:, pw:], lo)], axis=1),
             jnp.concatenate([_bd(u["vh"], lo), zeros_p], axis=1)], axis=0))
            for m, w_, u in zip(arbk, uw, us)]
        gp = [_mm(jnp.concatenate([u["b_e"], u["k_e"]], axis=0),
                  jnp.concatenate([w_, jnp.concatenate([u["vh"], zeros_c], axis=1)], axis=0), "bf16", TN)
              for w_, u in zip(uw, us)]
        res = []
        for u, yq_, gp_ in zip(us, yq, gp):
            q = u["r_t"] + yq_[:, pw:]
            p_bd = jnp.where(blk, gp_[:, pw:], 0.0) + eye_p * u["g_end"]
            res.append((jnp.concatenate([q, p_bd], axis=0), yq_[:, :pw], jnp.where(blk, gp_[:, :pw], 0.0)))
        return res

    order = list(range(nsub - 1, -1, -1) if reverse else range(nsub))
    tokws = [rows_of(bb) for bb in range(nb)]
    loc = local([u for bb in range(nb) for j in order for u in prep(tokws[bb], j)])
    hs = [h_ref[i] for i in range(nb * npair)]
    for gi, j in enumerate(order):
        idx = [(bb * nsub + gi) * npair + p for bb in range(nb) for p in range(npair)]
        res = [_mm(loc[u][0], h) for u, h in zip(idx, hs)]
        for i, u in enumerate(idx):
            bb, p = divmod(i, npair)
            y_ref[bb, j * c:(j + 1) * c, p * pw:(p + 1) * pw] = res[i][:c] + loc[u][1]
            hs[i] = res[i][c:] + loc[u][2]
    for i in range(nb * npair):
        h_ref[i] = hs[i]


def _rwkv(rkvc, z, w0, w2p, a0, a2p, kkw, kaw, ctx_len, reverse):
    bsz, ltot, _ = rkvc.shape
    nblk = ltot // RWKV_BLK
    nctx = ctx_len // RWKV_BLK
    lora_blk = COL_LORA // LORA_W
    d = 1 if reverse else 0

    def tok(i):
        return jnp.where(i < nctx, nctx - 1 - i, nblk + nctx - 1 - i) if reverse else i

    par3 = lambda b, i: (d, 0, 0)
    tri, cmask = _rwkv_masks(reverse)
    kern = functools.partial(_rwkv_kernel, reverse=reverse)
    return pl.pallas_call(
        kern,
        grid=(bsz // RWKV_NB, nblk),
        in_specs=[pl.BlockSpec((RWKV_NB, RWKV_BLK, D_RWKV), lambda b, i: (b, tok(i), 0)),
                  pl.BlockSpec((RWKV_NB, RWKV_BLK, D_RWKV), lambda b, i: (b, tok(i), 1)),
                  pl.BlockSpec((RWKV_NB, RWKV_BLK, D_RWKV), lambda b, i: (b, tok(i), 2)),
                  pl.BlockSpec((RWKV_NB, RWKV_BLK, LORA_W), lambda b, i: (b, tok(i), lora_blk)),
                  pl.BlockSpec((1, 1, D_RWKV), par3),
                  pl.BlockSpec((1, LORA_W, D_RWKV), par3),
                  pl.BlockSpec((1, 1, D_RWKV), par3),
                  pl.BlockSpec((1, LORA_W, D_RWKV), par3),
                  pl.BlockSpec((1, D_RWKV), lambda b, i: (0, 0)),
                  pl.BlockSpec((1, D_RWKV), lambda b, i: (0, 0)),
                  pl.BlockSpec(tri.shape, lambda b, i: (0, 0)),
                  pl.BlockSpec(cmask.shape, lambda b, i: (0, 0, 0))],
        out_specs=pl.BlockSpec((RWKV_NB, RWKV_BLK, D_RWKV), lambda b, i: (b, tok(i), 0)),
        out_shape=jax.ShapeDtypeStruct((bsz, ltot, D_RWKV), F32),
        scratch_shapes=[pltpu.VMEM((RWKV_NB * HEADS // 2, 2 * HEAD, 2 * HEAD), F32)],
        compiler_params=_cparams(("parallel", "arbitrary"), 48),
    )(rkvc, rkvc, rkvc, z, w0, w2p, a0, a2p, kkw, kaw, tri, cmask)


def _s5_tables_kernel(lam_re_ref, lam_im_ref, dt_ref, bt_re_ref, bt_im_ref, c_re_ref, c_im_ref,
                      inj_ref, mt_ref, qt_ref, lre_ref, lim_ref):
    t = S5_T
    hg = S5_GROUP
    ps = S5_STATE
    row = lax.broadcasted_iota(jnp.int32, (t, 2 * ps), 0).astype(F32)
    is_f = lax.broadcasted_iota(jnp.int32, (t, 2 * ps), 1) < ps
    is_f1 = is_f[0:1]
    n_inj = jnp.where(is_f, t - 1.0 - row, row)
    n_out = jnp.where(is_f, row + 1.0, t - row)
    n_lag = jnp.where(is_f, row, t - 1.0 - row)
    for gl in range(lam_re_ref.shape[0]):
        lam_re = lam_re_ref[gl:gl + 1, :]
        lam_im = lam_im_ref[gl:gl + 1, :]
        dt = dt_ref[gl:gl + 1, :]

        def power(n):
            mag = jnp.exp(lam_re * dt * n)
            return mag * jnp.cos(lam_im * dt * n), mag * jnp.sin(lam_im * dt * n)

        lb_re, lb_im = power(1.0)
        den = lam_re * lam_re + lam_im * lam_im
        q_re = ((lb_re - 1.0) * lam_re + lb_im * lam_im) / den
        q_im = (lb_im * lam_re - (lb_re - 1.0) * lam_im) / den
        bt_re = jnp.concatenate([bt_re_ref[gl], bt_re_ref[gl]], axis=1)
        bt_im = jnp.concatenate([bt_im_ref[gl], bt_im_ref[gl]], axis=1)
        bb_re = q_re * bt_re - q_im * bt_im
        bb_im = q_re * bt_im + q_im * bt_re
        c_re = jnp.concatenate([c_re_ref[gl], c_re_ref[gl]], axis=1)
        c_im = jnp.concatenate([c_im_ref[gl], c_im_ref[gl]], axis=1)

        pi_re, pi_im = power(n_inj)
        po_re, po_im = power(n_out)
        pl_re, pl_im = power(n_lag)
        ck_re, ck_im = [], []
        for s in range(t):
            wr, wi = pi_re[s:s + 1], pi_im[s:s + 1]
            rows = slice(s * hg, (s + 1) * hg)
            inj_ref[gl, rows, 0:2 * ps] = (wr * bb_re - wi * bb_im).astype(BF16)
            inj_ref[gl, rows, 2 * ps:4 * ps] = (wr * bb_im + wi * bb_re).astype(BF16)
            wr, wi = po_re[s:s + 1], po_im[s:s + 1]
            qt_ref[gl, rows, 0:2 * ps] = (wr * c_re - wi * c_im).astype(BF16)
            qt_ref[gl, rows, 2 * ps:4 * ps] = (-(wr * c_im + wi * c_re)).astype(BF16)
            wr, wi = pl_re[s:s + 1], pl_im[s:s + 1]
            ck_re.append(wr * c_re - wi * c_im)
            ck_im.append(wr * c_im + wi * c_re)
        ck = jnp.concatenate([jnp.concatenate(ck_re, axis=0), jnp.concatenate(ck_im, axis=0)], axis=1)
        bsel = jnp.concatenate(
            [jnp.concatenate([jnp.where(is_f1, bb_re, 0.0), jnp.where(is_f1, -bb_im, 0.0)], axis=1),
             jnp.concatenate([jnp.where(is_f1, 0.0, bb_re), jnp.where(is_f1, 0.0, -bb_im)], axis=1)], axis=0)
        kt = _mm(bsel, ck, "x3", NT)
        w = t * hg
        zero = jnp.zeros((hg, w), F32)
        lag = (pltpu.roll(jnp.concatenate([kt[0:hg], zero], axis=1), w - hg, 1)
               + jnp.concatenate([kt[hg:2 * hg], zero], axis=1))
        for s in range(t):
            sh = (t - 1 - s) * hg
            blk = lag if sh == 0 else pltpu.roll(lag, 2 * w - sh, 1)
            mt_ref[gl, s * hg:(s + 1) * hg, :] = blk[:, 0:w].astype(BF16)
        l16_re, l16_im = power(float(t))
        lre_ref[gl:gl + 1, :] = l16_re
        lim_ref[gl:gl + 1, :] = l16_im


def _s5_tables(lam_re, lam_im, log_step, b_re, b_im, c_re, c_im):
    g, p, hg = b_re.shape
    w = S5_T * hg
    lanes = lambda x: jnp.concatenate([x[0], x[1]], axis=-1)
    dt = lanes(jnp.broadcast_to(jnp.exp(log_step)[..., None], lam_re.shape))
    bt_re = jnp.transpose(b_re, (0, 2, 1))
    bt_im = jnp.transpose(b_im, (0, 2, 1))
    vec = pl.BlockSpec((S5_TG, 2 * p), lambda j: (j, 0))
    mat = pl.BlockSpec((S5_TG, hg, p), lambda j: (j, 0, 0))
    tab = pl.BlockSpec((S5_TG, w, w), lambda j: (j, 0, 0))
    return pl.pallas_call(
        _s5_tables_kernel,
        grid=(g // S5_TG,),
        in_specs=[vec, vec, vec, mat, mat, mat, mat],
        out_specs=[tab, tab, tab, vec, vec],
        out_shape=[jax.ShapeDtypeStruct((g, w, w), BF16)] * 3 + [jax.ShapeDtypeStruct((g, 2 * p), F32)] * 2,
        compiler_params=_cparams(("parallel",), 32),
    )(lanes(lam_re), lanes(lam_im), dt, bt_re, bt_im, c_re, c_im)


def _block_transpose8(xs):
    lane_blk = lax.broadcasted_iota(jnp.int32, xs[0].shape, 1) >> 4
    xs = list(xs)
    for d in (4, 2, 1):
        hi = (lane_blk & d) != 0
        for i in range(8):
            if i & d:
                continue
            a, b = xs[i], xs[i + d]
            xs[i] = jnp.where(hi, pltpu.roll(b, 16 * d, 1), a)
            xs[i + d] = jnp.where(hi, b, pltpu.roll(a, 128 - 16 * d, 1))
    return xs


def _s5_kernel(u0_ref, u1_ref, inj_ref, mt_ref, qt_ref, lre_ref, lim_ref, y_ref, ug_ref, zre_ref, zim_ref,
               hre_ref, him_ref, *, nchunk, nctx):
    gh = ug_ref.shape[0]
    ps = S5_STATE
    t = S5_T
    nlat = nchunk - nctx
    for jt, u_ref in enumerate((u0_ref, u1_ref)):
        for q in range(t // 8):
            cols = [u_ref[0, pl.ds(8 * q + tt, nchunk, stride=t), :] for tt in range(8)]
            grp = _block_transpose8(cols)
            for gl in range(8):
                ug_ref[8 * jt + gl, :, 128 * q:128 * (q + 1)] = grp[gl].astype(BF16)
    for gi in range(gh):
        zg = _dot(ug_ref[gi], inj_ref[gi], NN)
        zre_ref[pl.ds(gi, nchunk, stride=gh), :] = zg[:, 0:2 * ps]
        zim_ref[pl.ds(gi, nchunk, stride=gh), :] = zg[:, 2 * ps:4 * ps]

    lane = lax.broadcasted_iota(jnp.int32, (gh, 2 * ps), 1)
    is_f = lane < ps
    lre = lre_ref[...]
    lim = lim_ref[...]

    def body(i, carry):
        hre, him = carry
        cb = jnp.where(i < nctx, nctx - 1 - i, nchunk + nctx - 1 - i)
        rf = pl.multiple_of(i * gh, gh)
        rb = pl.multiple_of(cb * gh, gh)
        hre_ref[pl.ds(rf, gh), 0:ps] = hre[:, 0:ps]
        hre_ref[pl.ds(rb, gh), ps:2 * ps] = hre[:, ps:2 * ps]
        him_ref[pl.ds(rf, gh), 0:ps] = him[:, 0:ps]
        him_ref[pl.ds(rb, gh), ps:2 * ps] = him[:, ps:2 * ps]
        zre = jnp.where(is_f, zre_ref[pl.ds(rf, gh), :], zre_ref[pl.ds(rb, gh), :])
        zim = jnp.where(is_f, zim_ref[pl.ds(rf, gh), :], zim_ref[pl.ds(rb, gh), :])
        return lre * hre - lim * him + zre, lre * him + lim * hre + zim

    zero = jnp.zeros((gh, 2 * ps), F32)
    lax.fori_loop(0, nchunk, body, (zero, zero))

    for jt in range(gh // 8):
        ys = []
        for gl in range(8):
            gi = 8 * jt + gl
            hre = hre_ref[pl.ds(gi, nchunk, stride=gh), :].astype(BF16)
            him = him_ref[pl.ds(gi, nchunk, stride=gh), :].astype(BF16)
            ys.append(_dot(ug_ref[gi], mt_ref[gi], NN)
                      + _dot(jnp.concatenate([hre, him], axis=1), qt_ref[gi], NT))
        for q in range(t // 8):
            tok = _block_transpose8([y[:, 128 * q:128 * (q + 1)] for y in ys])
            for tt in range(8):
                y_ref[0, jt, pl.ds(8 * q + tt, nlat, stride=t), :] = tok[tt][nctx:]


def _s5(u, tables, ctx_len):
    inj, mt, qt, lre, lim = tables
    bsz, ltot, _ = u.shape
    nchunk = ltot // S5_T
    nctx = ctx_len // S5_T
    gh = S5_GH
    halves = S5_GROUPS // gh
    w = S5_T * S5_GROUP
    tiles = gh * S5_GROUP // 128
    ps2 = 2 * S5_STATE
    kern = functools.partial(_s5_kernel, nchunk=nchunk, nctx=nctx)
    u_spec = lambda k: pl.BlockSpec((1, ltot, 128), lambda b, j: (b, 0, tiles * j + k))
    tab = pl.BlockSpec((gh, w, w), lambda b, j: (j, 0, 0))
    vec = pl.BlockSpec((gh, ps2), lambda b, j: (j, 0))
    return pl.pallas_call(
        kern,
        grid=(bsz, halves),
        in_specs=[u_spec(0), u_spec(1), tab, tab, tab, vec, vec],
        out_specs=pl.BlockSpec((1, tiles, ltot - ctx_len, 128), lambda b, j: (b, j, 0, 0)),
        out_shape=jax.ShapeDtypeStruct((bsz, D_S5 // 128, ltot - ctx_len, 128), F32),
        scratch_shapes=[pltpu.VMEM((gh, nchunk, w), BF16),
                        pltpu.VMEM((nchunk * gh, ps2), F32),
                        pltpu.VMEM((nchunk * gh, ps2), F32),
                        pltpu.VMEM((nchunk * gh, ps2), F32),
                        pltpu.VMEM((nchunk * gh, ps2), F32)],
        compiler_params=_cparams(("parallel", "parallel"), 48),
    )(u, u, inj, mt, qt, lre, lim)


def _mix_kernel(x_ref, yf_ref, yb_ref, r_ref, k_ref, v_ref, lo_ref, u_ref, ys_ref, mod_ref,
                seg_ref, a0_ref, a2_ref, g2_ref, kaw_ref, rk_ref, lnw_ref, lnb_ref,
                dsk_ref, gluw_ref, glub_ref, wo_ref, g2n_ref, w1_ref, w3_ref, w2_ref, gf_ref, o_ref):
    lo = lo_ref[0].astype(F32)
    seg = seg_ref[...]

    def head_sum(t):
        pw = seg.shape[0]
        return jnp.concatenate([_mm(t[:, i:i + pw], seg) for i in range(0, t.shape[1], pw)], axis=1)

    y = yf_ref[0] + yb_ref[0]
    mu = head_sum(y) * (1.0 / HEAD)
    yc = y - mu
    var = head_sum(yc * yc) * (1.0 / HEAD)
    yn = yc * lax.rsqrt(var + RWKV_LN_EPS) * lnw_ref[...] + lnb_ref[...]
    a_f = _sigmoid(a0_ref[0] + _mm(lo, a2_ref[0]))
    a_b = _sigmoid(a0_ref[1] + _mm(lo, a2_ref[1]))
    kaw = kaw_ref[...]
    kd_sum = k_ref[0] * ((1.0 + (a_f - 1.0) * kaw) + (1.0 + (a_b - 1.0) * kaw))
    bonus = head_sum(r_ref[0] * kd_sum * rk_ref[...])
    yn = yn + bonus * v_ref[0]
    gate = _mm(_sigmoid(lo), g2_ref[...])
    rwkv_out = yn * gate

    u = u_ref[0]
    ys = jnp.concatenate([ys_ref[0, i] for i in range(ys_ref.shape[1])], axis=1) + dsk_ref[...] * u
    zg = 0.5 * ys * (1.0 + jnp.tanh(math.sqrt(2.0 / math.pi) * (ys + 0.044715 * (ys * ys * ys))))
    s5_out = zg * _sigmoid(_mm(zg, gluw_ref[...]) + glub_ref[...])

    mix = _mm(rwkv_out, wo_ref[0:D_RWKV, :]) + _mm(s5_out, wo_ref[D_RWKV:, :])
    x1 = x_ref[0] + mod_ref[0, 2:3, :] * mix

    ms = jnp.mean(x1 * x1, axis=-1, keepdims=True)
    h = x1 * lax.rsqrt(ms + NORM_EPS) * g2n_ref[...]
    h = h * (1.0 + mod_ref[0, 4:5, :]) + mod_ref[0, 3:4, :]
    hb = h.astype(BF16)
    h1 = _dot(hb, w1_ref[...], NN)
    h3 = _dot(hb, w3_ref[...], NN)
    act = (h1 * _sigmoid(h1)) * h3
    out = x1 + mod_ref[0, 5:6, :] * _dot(act.astype(BF16), w2_ref[...], NN)
    ms2 = jnp.mean(out * out, axis=-1, keepdims=True)
    o_ref[0] = out * lax.rsqrt(ms2 + NORM_EPS) * gf_ref[...]


def _mix(x, y_f, y_b, rkvc, z, u, ys5, mod8, seg, a0, a2p, g2p, kaw, rk, lnw, lnb, dsk, gluw, glub, wo,
         g2n, w1, w3, w2, gf, ctx_len, tm):
    bsz, seq, d = x.shape
    assert ctx_len % tm == 0 and seq % tm == 0
    off = ctx_len // tm
    lora_blk = COL_LORA // LORA_W
    tok = lambda c: pl.BlockSpec((1, tm, D_RWKV), lambda b, j: (b, j + off, c))
    full = lambda a: pl.BlockSpec(a.shape, lambda b, j: (0,) * a.ndim, pipeline_mode=pl.Buffered(1))
    return pl.pallas_call(
        _mix_kernel,
        grid=(bsz, seq // tm),
        in_specs=[pl.BlockSpec((1, tm, d), lambda b, j: (b, j, 0)),
                  tok(0), tok(0), tok(0), tok(1), tok(2),
                  pl.BlockSpec((1, tm, LORA_W), lambda b, j: (b, j + off, lora_blk)),
                  pl.BlockSpec((1, tm, D_S5), lambda b, j: (b, j + off, 0)),
                  pl.BlockSpec((1, D_S5 // 128, tm, 128), lambda b, j: (b, 0, j, 0)),
                  pl.BlockSpec((1, 8, d), lambda b, j: (b, 0, 0)),
                  full(seg), full(a0), full(a2p), full(g2p), full(kaw), full(rk), full(lnw), full(lnb),
                  full(dsk), full(gluw), full(glub), full(wo),
                  full(g2n), full(w1), full(w3), full(w2), full(gf)],
        out_specs=pl.BlockSpec((1, tm, d), lambda b, j: (b, j, 0)),
        out_shape=jax.ShapeDtypeStruct((bsz, seq, d), F32),
        compiler_params=_cparams(("parallel", "parallel"), 56),
    )(x, y_f, y_b, rkvc, rkvc, rkvc, z, u, ys5, mod8, seg, a0, a2p, g2p, kaw, rk, lnw, lnb,
      dsk, gluw, glub, wo, g2n, w1, w3, w2, gf)


def _pad_rows(w, row0, rows_total):
    pad = [(0, 0)] * (w.ndim - 2) + [(row0, rows_total - row0 - w.shape[-2]), (0, 0)]
    return jnp.pad(w, pad)


def kernel(x, c, ctx, c_ctx, mod_w, mod_b, norm1_g, norm2_g, w_in, w_out, rwkv_conv, rwkv_w0, rwkv_w2, rwkv_a0, rwkv_a2, rwkv_g2, rwkv_kk, rwkv_ka, rwkv_rk, rwkv_ln_w, rwkv_ln_b, s5_lam_re, s5_lam_im, s5_log_step, s5_b_re, s5_b_im, s5_c_re, s5_c_im, s5_d, s5_glu_w, s5_glu_b, ffn_w1, ffn_w3, ffn_w2, final_g):
    bsz, seq, d = x.shape
    ctx_len = ctx.shape[1]
    ltot = ctx_len + seq
    layer = 0

    cc = jnp.concatenate([c, c_ctx[None, :], jnp.zeros((8 - bsz - 1, d), F32)], axis=0)
    mod = _adaln(cc, mod_w[layer], mod_b[layer])
    mod8 = jnp.pad(mod[:bsz + 1].reshape(bsz + 1, N_MOD, d), ((0, 0), (0, 8 - N_MOD), (0, 0)))

    wi = w_in[layer]
    w_z = jnp.concatenate([wi[:, :-D_S5], jnp.zeros((d, D_Z16 + D_S5 - wi.shape[1]), F32),
                           wi[:, -D_S5:]], axis=1).astype(BF16)
    z, u = _inproj(ctx, x, mod8, norm1_g[layer], w_z, 256)

    rkvc = _conv(z, rwkv_conv[layer], ctx_len)

    w2p = _pad_rows(rwkv_w2[layer], 0, LORA_W).at[1].set(_pad_rows(rwkv_w2[layer, 1], DECAY_LORA, LORA_W))
    a2p = jnp.stack([_pad_rows(rwkv_a2[layer, 0], LORA_AD, LORA_W),
                     _pad_rows(rwkv_a2[layer, 1], LORA_AD + AAA_LORA, LORA_W)])
    w0 = rwkv_w0[layer].reshape(2, 1, D_RWKV)
    a0 = rwkv_a0[layer].reshape(2, 1, D_RWKV)
    kkw = rwkv_kk[layer].reshape(1, D_RWKV)
    kaw = rwkv_ka[layer].reshape(1, D_RWKV)
    y_f = _rwkv(rkvc, z, w0, w2p, a0, a2p, kkw, kaw, ctx_len, False)
    y_b = _rwkv(rkvc, z, w0, w2p, a0, a2p, kkw, kaw, ctx_len, True)

    tables = _s5_tables(s5_lam_re[layer], s5_lam_im[layer], s5_log_step[layer], s5_b_re[layer],
                        s5_b_im[layer], s5_c_re[layer], s5_c_im[layer])
    ys5 = _s5(u, tables, ctx_len)

    hh = jnp.arange(2 * HEAD) // HEAD
    seg = (hh[:, None] == hh[None, :]).astype(BF16)
    g2p = _pad_rows(rwkv_g2[layer], LORA_GD, LORA_W)
    return _mix(x, y_f, y_b, rkvc, z, u, ys5, mod8, seg, a0, a2p, g2p, kaw,
                rwkv_rk[layer].reshape(1, D_RWKV), rwkv_ln_w[layer].reshape(1, D_RWKV),
                rwkv_ln_b[layer].reshape(1, D_RWKV), s5_d[layer].reshape(1, D_S5),
                s5_glu_w[layer].astype(BF16), s5_glu_b[layer].reshape(1, D_S5),
                w_out[layer].astype(BF16), norm2_g[layer].reshape(1, d), ffn_w1[layer].astype(BF16),
                ffn_w3[layer].astype(BF16), ffn_w2[layer].astype(BF16), final_g.reshape(1, d), ctx_len, 256)
```

```python
import functools
import math

import jax
import jax.numpy as jnp
import numpy as np
from jax import lax
from jax.experimental import pallas as pl
from jax.experimental.pallas import tpu as pltpu

F32 = jnp.float32
BF16 = jnp.bfloat16

D_MODEL = 1024
GRID_W = 64
D_RWKV = 512
HEAD = 64
HEADS = D_RWKV // HEAD
D_S5 = 512
S5_GROUP = 16
S5_GROUPS = D_S5 // S5_GROUP
S5_STATE = 64
DECAY_LORA = 32
AAA_LORA = 32
GATE_LORA = 96
N_MOD = 6
NORM_EPS = 1e-6
RWKV_LN_EPS = 64e-5

LORA_W = 256
COL_LORA = 3 * D_RWKV
D_Z16 = COL_LORA + LORA_W
LORA_AD = 2 * DECAY_LORA
LORA_GD = LORA_AD + 2 * AAA_LORA

CHUNK = 64
RWKV_BLK = 256
RWKV_NB = 4
INPROJ_NB = 2
S5_T = 16
S5_GH = 16
S5_TG = 8
CONV_PAD = 72
CONV_TC = 256
CONV_PIECE = 128

NN = (((1,), (0,)), ((), ()))
NT = (((1,), (1,)), ((), ()))
TN = (((0,), (0,)), ((), ()))


def _dot(a, b, dims):
    return lax.dot_general(a, b, dims, preferred_element_type=F32)


def _split2(x):
    hi = x.astype(BF16)
    lo = (x - hi.astype(F32)).astype(BF16)
    return hi, lo


def _mm(a, b, mode="bf16", dims=NN):
    if mode == "bf16":
        return _dot(a.astype(BF16), b.astype(BF16), dims)
    if mode == "x3":
        ah, al = _split2(a)
        bh, bl = _split2(b)
        return _dot(ah, bh, dims) + (_dot(ah, bl, dims) + _dot(al, bh, dims))
    if mode == "xb2":
        ab = a.astype(BF16)
        b1, b2 = _split2(b)
        return _dot(ab, b1, dims) + _dot(ab, b2, dims)
    raise ValueError(mode)


def _sigmoid(x):
    return 1.0 / (1.0 + jnp.exp(-x))


def _cparams(sem, vmem_mb):
    return pltpu.CompilerParams(dimension_semantics=sem, vmem_limit_bytes=vmem_mb * 1024 * 1024)


def _adaln_kernel(c_ref, w_ref, b_ref, o_ref):
    c = c_ref[...]
    s = c * _sigmoid(c)
    o_ref[...] = _mm(s, w_ref[...], "x3") + b_ref[...]


def _adaln(cc, mod_w, mod_b):
    rows, d = cc.shape
    n = mod_w.shape[1]
    tn = 1024
    return pl.pallas_call(
        _adaln_kernel,
        grid=(n // tn,),
        in_specs=[pl.BlockSpec((rows, d), lambda j: (0, 0)),
                  pl.BlockSpec((d, tn), lambda j: (0, j)),
                  pl.BlockSpec((1, tn), lambda j: (0, j))],
        out_specs=pl.BlockSpec((rows, tn), lambda j: (0, j)),
        out_shape=jax.ShapeDtypeStruct((rows, n), F32),
        compiler_params=_cparams(("parallel",), 40),
    )(cc, mod_w, mod_b.reshape(1, n))


def _inproj_kernel(c_ref, x_ref, mod_ref, g_ref, w_ref, o_ref, u_ref, *, ctx_blocks):
    x = jnp.where(pl.program_id(1) < ctx_blocks, c_ref[...], x_ref[...])
    nb, tm, d = x.shape
    ms = jnp.mean(x * x, axis=-1, keepdims=True)
    xn = x * lax.rsqrt(ms + NORM_EPS) * g_ref[...]
    shift = mod_ref[:, 0:1, :]
    scale = mod_ref[:, 1:2, :]
    h = (xn * (1.0 + scale) + shift).reshape(nb * tm, d)
    res = _mm(h, w_ref[...], "bf16")
    o_ref[...] = res[:, :D_Z16].astype(BF16).reshape(nb, tm, D_Z16)
    u_ref[...] = res[:, D_Z16:].reshape(nb, tm, res.shape[1] - D_Z16)


def _inproj(ctx, x, mod8, g1, w_z, tm):
    bsz, seq, d = x.shape
    nb = INPROJ_NB
    ctx_blocks = ctx.shape[1] // tm
    dz = w_z.shape[1]
    ltot = ctx.shape[1] + seq
    kern = functools.partial(_inproj_kernel, ctx_blocks=ctx_blocks)
    return pl.pallas_call(
        kern,
        grid=(bsz // nb, ctx_blocks + seq // tm),
        in_specs=[pl.BlockSpec((nb, tm, d), lambda b, j: (b, jnp.minimum(j, ctx_blocks - 1), 0)),
                  pl.BlockSpec((nb, tm, d), lambda b, j: (b, jnp.maximum(j - ctx_blocks, 0), 0)),
                  pl.BlockSpec((nb, 8, d), lambda b, j: (jnp.where(j < ctx_blocks, bsz // nb, b), 0, 0)),
                  pl.BlockSpec((1, d), lambda b, j: (0, 0)),
                  pl.BlockSpec((d, dz), lambda b, j: (0, 0))],
        out_specs=[pl.BlockSpec((nb, tm, D_Z16), lambda b, j: (b, j, 0)),
                   pl.BlockSpec((nb, tm, dz - D_Z16), lambda b, j: (b, j, 0))],
        out_shape=[jax.ShapeDtypeStruct((bsz, ltot, D_Z16), BF16),
                   jax.ShapeDtypeStruct((bsz, ltot, dz - D_Z16), F32)],
        compiler_params=_cparams(("parallel", "parallel"), 48),
    )(ctx, x, mod8, g1.reshape(1, d), w_z)


def _conv_kernel(z_ref, w_ref, o_ref, xs_ref, *, ctx_len, lat_len):
    tc = z_ref.shape[2]
    pc = CONV_PIECE
    ctx_off = CONV_PAD
    lat_off = 2 * CONV_PAD + ctx_len
    zpad = jnp.zeros((CONV_PAD, tc), F32)
    for s in range(3):
        xs_ref[s, 0:CONV_PAD, :] = zpad
        xs_ref[s, ctx_off + ctx_len:lat_off, :] = zpad
        xs_ref[s, lat_off + lat_len:lat_off + lat_len + CONV_PAD, :] = zpad
    xs_ref[1, lat_off:lat_off + 8, :] = zpad[0:8]
    xs_ref[2, lat_off + lat_len - 8:lat_off + lat_len, :] = zpad[0:8]
    col = lax.broadcasted_iota(jnp.int32, (pc, tc), 0) % GRID_W
    keep_l = (col != GRID_W - 1).astype(F32)
    keep_r = (col != 0).astype(F32)
    for p in range(ctx_len // pc):
        xs_ref[0, ctx_off + p * pc:ctx_off + (p + 1) * pc, :] = z_ref[0, p * pc:(p + 1) * pc, :].astype(F32)
    for p in range(lat_len // pc):
        x = z_ref[0, ctx_len + p * pc:ctx_len + (p + 1) * pc, :].astype(F32)
        dst = lat_off + p * pc
        xs_ref[0, dst:dst + pc, :] = x
        xs_ref[1, dst + 1:dst + 1 + pc, :] = x * keep_l
        xs_ref[2, dst - 1:dst - 1 + pc, :] = x * keep_r
    w = [w_ref[t:t + 1, :] for t in range(9)]
    for p in range(ctx_len // pc):
        base = ctx_off + p * pc
        acc = w[3] * xs_ref[0, base - 1:base - 1 + pc, :]
        acc = acc + w[4] * xs_ref[0, base:base + pc, :]
        acc = acc + w[5] * xs_ref[0, base + 1:base + 1 + pc, :]
        o_ref[0, p * pc:(p + 1) * pc, :] = acc
    for p in range(lat_len // pc):
        base = lat_off + p * pc
        acc = None
        for di in range(3):
            for dj in range(3):
                off = base + (di - 1) * GRID_W
                src = (1, 0, 2)[dj]
                term = w[3 * di + dj] * xs_ref[src, off:off + pc, :]
                acc = term if acc is None else acc + term
        o_ref[0, ctx_len + p * pc:ctx_len + (p + 1) * pc, :] = acc


def _conv(z, conv_w, ctx_len):
    bsz, ltot, _ = z.shape
    lat_len = ltot - ctx_len
    ch = 3 * D_RWKV
    rows = 3 * CONV_PAD + ltot
    kern = functools.partial(_conv_kernel, ctx_len=ctx_len, lat_len=lat_len)
    return pl.pallas_call(
        kern,
        grid=(bsz, ch // CONV_TC),
        in_specs=[pl.BlockSpec((1, ltot, CONV_TC), lambda b, j: (b, 0, j)),
                  pl.BlockSpec((9, CONV_TC), lambda b, j: (0, j))],
        out_specs=pl.BlockSpec((1, ltot, CONV_TC), lambda b, j: (b, 0, j)),
        out_shape=jax.ShapeDtypeStruct((bsz, ltot, ch), F32),
        scratch_shapes=[pltpu.VMEM((3, rows, CONV_TC), F32)],
        compiler_params=_cparams(("parallel", "parallel"), 48),
    )(z, conv_w.reshape(9, ch))


def _bd(x, lo):
    return jnp.concatenate([jnp.where(lo, x, 0.0), jnp.where(lo, 0.0, x)], axis=0)


def _rwkv_masks(reverse):
    n, c = RWKV_BLK, CHUNK
    ti, tj = np.indices((n, n))
    causal = (tj >= ti) if reverse else (tj <= ti)
    tri = ((ti // c == tj // c) & causal).astype(np.float32)
    ii, jj = np.indices((c, 2 * c))
    jj = jj % c
    tabs = [(jj > ii) if reverse else (jj < ii), (jj >= ii) if reverse else (jj <= ii), ii == jj,
            ii // 2 == jj // 2]
    m = 2
    while m < c:
        tabs.append((ii // (2 * m) == jj // (2 * m)) & (ii // m != jj // m))
        m *= 2
    return jnp.asarray(tri, BF16), jnp.asarray(np.stack(tabs).astype(np.float32))


def _tri_inv_pairs(n_mats, masks, lo):
    ts = [masks[0] + n * masks[1] for n in n_mats]
    for lvl in masks[2:]:
        nls = [n * lvl for n in n_mats]
        tmp = [_mm(t, _bd(nl, lo)) for t, nl in zip(ts, nls)]
        ts = [t + _mm(x, _bd(t, lo)) for x, t in zip(tmp, ts)]
    return ts


def _rwkv_kernel(r_ref, k_ref, v_ref, lo_ref, w0_ref, w2_ref, a0_ref, a2_ref, kkw_ref, kaw_ref,
                 tri_ref, cm_ref, y_ref, h_ref, *, reverse):
    step = pl.program_id(1)

    @pl.when(step == 0)
    def _():
        h_ref[...] = jnp.zeros_like(h_ref)

    nb, n, _ = r_ref.shape
    c = CHUNK
    nsub = n // c
    pw = 2 * HEAD
    npair = HEADS // 2

    def rows_of(bb):
        r = r_ref[bb]
        k = k_ref[bb]
        v = v_ref[bb]
        lora = lo_ref[bb].astype(F32)
        logit_w = w0_ref[0] + _mm(jnp.tanh(lora), w2_ref[0])
        lw = -math.exp(-0.5) * _sigmoid(logit_w)
        a = _sigmoid(a0_ref[0] + _mm(lora, a2_ref[0]))
        kd = k * ((1.0 - kaw_ref[...]) + a * kaw_ref[...])
        kk = k * kkw_ref[...]
        g = _mm(tri_ref[...], lw, "xb2")
        return r, v, lw, a, kd, kk, g

    lo = lax.broadcasted_iota(jnp.int32, (c, pw), 1) < HEAD
    strict2 = cm_ref[0]
    incl2 = cm_ref[1]
    inv_masks = [cm_ref[i] for i in range(2, cm_ref.shape[0])]
    ri = lax.broadcasted_iota(jnp.int32, (pw, pw), 0)
    rj = lax.broadcasted_iota(jnp.int32, (pw, pw), 1)
    blk = (ri < HEAD) == (rj < HEAD)
    eye_p = (ri == rj).astype(F32)
    zeros_c = jnp.zeros((c, pw), F32)
    zeros_p = jnp.zeros((pw, pw), F32)

    def prep(tokw, j):
        r, v, lw, a, kd, kk, g = tokw
        rows = slice(j * c, (j + 1) * c)
        lw_j = lw[rows]
        g_j = g[rows]
        tot = g_j[0:1] if reverse else g_j[c - 1:c]
        e_prev = jnp.exp(g_j - lw_j)
        e_neg = jnp.exp(-g_j)
        e_pos = jnp.exp(g_j)
        e_tot = jnp.exp(tot)
        out = []
        for p in range(npair):
            sl = slice(p * pw, (p + 1) * pw)
            kkp = kk[rows, sl]
            sq = kkp * kkp
            n2 = jnp.where(lo, jnp.sum(jnp.where(lo, sq, 0.0), axis=-1, keepdims=True),
                           jnp.sum(jnp.where(lo, 0.0, sq), axis=-1, keepdims=True))
            kkp = kkp * lax.rsqrt(jnp.maximum(n2, 1e-12))
            b_t = kkp * a[rows, sl] * e_neg[:, sl]
            k_t = kd[rows, sl] * e_neg[:, sl]
            out.append(dict(a_t=-kkp * e_prev[:, sl], b_t=b_t, k_t=k_t, r_t=r[rows, sl] * e_pos[:, sl],
                            b_e=b_t * e_tot[:, sl], k_e=k_t * e_tot[:, sl], vh=v[rows, sl],
                            g_end=e_tot[:, sl]))
        return out

    def local(us):
        s_bk = [_mm(jnp.concatenate([u["a_t"], u["r_t"]], axis=0),
                    jnp.concatenate([_bd(u["b_t"], lo), _bd(u["k_t"], lo)], axis=0), "bf16", NT)
                for u in us]
        nab = [s[:c, :pw] * strict2 for s in s_bk]
        aak = [s[:c, pw:] * strict2 for s in s_bk]
        arbk = [jnp.concatenate([s[c:, :pw] * incl2, s[c:, pw:] * incl2], axis=1) for s in s_bk]
        t_inv = _tri_inv_pairs(nab, inv_masks, lo)
        x1 = [_mm(m, _bd(u["vh"], lo)) for m, u in zip(aak, us)]
        uw = [_mm(t, jnp.concatenate([_bd(x, lo), _bd(u["a_t"], lo)], axis=1))
              for t, x, u in zip(t_inv, x1, us)]
        yq = [_mm(m, jnp.concatenate(
            [jnp.concatenate([_bd(w_[:, :pw], lo), _bd(w_[:, pw:], lo)], axis=1),
             jnp.concatenate([_bd(u["vh"], lo), zeros_p], axis=1)], axis=0))
            for m, w_, u in zip(arbk, uw, us)]
        gp = [_mm(jnp.concatenate([u["b_e"], u["k_e"]], axis=0),
                  jnp.concatenate([w_, jnp.concatenate([u["vh"], zeros_c], axis=1)], axis=0), "bf16", TN)
              for w_, u in zip(uw, us)]
        res = []
        for u, yq_, gp_ in zip(us, yq, gp):
            q = u["r_t"] + yq_[:, pw:]
            p_bd = jnp.where(blk, gp_[:, pw:], 0.0) + eye_p * u["g_end"]
            res.append((jnp.concatenate([q, p_bd], axis=0), yq_[:, :pw], jnp.where(blk, gp_[:, :pw], 0.0)))
        return res

    order = list(range(nsub - 1, -1, -1) if reverse else range(nsub))
    tokws = [rows_of(bb) for bb in range(nb)]
    loc = local([u for bb in range(nb) for j in order for u in prep(tokws[bb], j)])
    hs = [h_ref[i] for i in range(nb * npair)]
    for gi, j in enumerate(order):
        idx = [(bb * nsub + gi) * npair + p for bb in range(nb) for p in range(npair)]
        res = [_mm(loc[u][0], h) for u, h in zip(idx, hs)]
        for i, u in enumerate(idx):
            bb, p = divmod(i, npair)
            y_ref[bb, j * c:(j + 1) * c, p * pw:(p + 1) * pw] = res[i][:c] + loc[u][1]
            hs[i] = res[i][c:] + loc[u][2]
    for i in range(nb * npair):
        h_ref[i] = hs[i]


def _rwkv(rkvc, z, w0, w2p, a0, a2p, kkw, kaw, ctx_len, reverse):
    bsz, ltot, _ = rkvc.shape
    nblk = ltot // RWKV_BLK
    nctx = ctx_len // RWKV_BLK
    lora_blk = COL_LORA // LORA_W
    d = 1 if reverse else 0

    def tok(i):
        return jnp.where(i < nctx, nctx - 1 - i, nblk + nctx - 1 - i) if reverse else i

    par3 = lambda b, i: (d, 0, 0)
    tri, cmask = _rwkv_masks(reverse)
    kern = functools.partial(_rwkv_kernel, reverse=reverse)
    return pl.pallas_call(
        kern,
        grid=(bsz // RWKV_NB, nblk),
        in_specs=[pl.BlockSpec((RWKV_NB, RWKV_BLK, D_RWKV), lambda b, i: (b, tok(i), 0)),
                  pl.BlockSpec((RWKV_NB, RWKV_BLK, D_RWKV), lambda b, i: (b, tok(i), 1)),
                  pl.BlockSpec((RWKV_NB, RWKV_BLK, D_RWKV), lambda b, i: (b, tok(i), 2)),
                  pl.BlockSpec((RWKV_NB, RWKV_BLK, LORA_W), lambda b, i: (b, tok(i), lora_blk)),
                  pl.BlockSpec((1, 1, D_RWKV), par3),
                  pl.BlockSpec((1, LORA_W, D_RWKV), par3),
                  pl.BlockSpec((1, 1, D_RWKV), par3),
                  pl.BlockSpec((1, LORA_W, D_RWKV), par3),
                  pl.BlockSpec((1, D_RWKV), lambda b, i: (0, 0)),
                  pl.BlockSpec((1, D_RWKV), lambda b, i: (0, 0)),
                  pl.BlockSpec(tri.shape, lambda b, i: (0, 0)),
                  pl.BlockSpec(cmask.shape, lambda b, i: (0, 0, 0))],
        out_specs=pl.BlockSpec((RWKV_NB, RWKV_BLK, D_RWKV), lambda b, i: (b, tok(i), 0)),
        out_shape=jax.ShapeDtypeStruct((bsz, ltot, D_RWKV), F32),
        scratch_shapes=[pltpu.VMEM((RWKV_NB * HEADS // 2, 2 * HEAD, 2 * HEAD), F32)],
        compiler_params=_cparams(("parallel", "arbitrary"), 48),
    )(rkvc, rkvc, rkvc, z, w0, w2p, a0, a2p, kkw, kaw, tri, cmask)


def _s5_tables_kernel(lam_re_ref, lam_im_ref, dt_ref, bt_re_ref, bt_im_ref, c_re_ref, c_im_ref,
                      inj_ref, mt_ref, qt_ref, lre_ref, lim_ref):
    t = S5_T
    hg = S5_GROUP
    ps = S5_STATE
    row = lax.broadcasted_iota(jnp.int32, (t, 2 * ps), 0).astype(F32)
    is_f = lax.broadcasted_iota(jnp.int32, (t, 2 * ps), 1) < ps
    is_f1 = is_f[0:1]
    n_inj = jnp.where(is_f, t - 1.0 - row, row)
    n_out = jnp.where(is_f, row + 1.0, t - row)
    n_lag = jnp.where(is_f, row, t - 1.0 - row)
    for gl in range(lam_re_ref.shape[0]):
        lam_re = lam_re_ref[gl:gl + 1, :]
        lam_im = lam_im_ref[gl:gl + 1, :]
        dt = dt_ref[gl:gl + 1, :]

        def power(n):
            mag = jnp.exp(lam_re * dt * n)
            return mag * jnp.cos(lam_im * dt * n), mag * jnp.sin(lam_im * dt * n)

        lb_re, lb_im = power(1.0)
        den = lam_re * lam_re + lam_im * lam_im
        q_re = ((lb_re - 1.0) * lam_re + lb_im * lam_im) / den
        q_im = (lb_im * lam_re - (lb_re - 1.0) * lam_im) / den
        bt_re = jnp.concatenate([bt_re_ref[gl], bt_re_ref[gl]], axis=1)
        bt_im = jnp.concatenate([bt_im_ref[gl], bt_im_ref[gl]], axis=1)
        bb_re = q_re * bt_re - q_im * bt_im
        bb_im = q_re * bt_im + q_im * bt_re
        c_re = jnp.concatenate([c_re_ref[gl], c_re_ref[gl]], axis=1)
        c_im = jnp.concatenate([c_im_ref[gl], c_im_ref[gl]], axis=1)

        pi_re, pi_im = power(n_inj)
        po_re, po_im = power(n_out)
        pl_re, pl_im = power(n_lag)
        ck_re, ck_im = [], []
        for s in range(t):
            wr, wi = pi_re[s:s + 1], pi_im[s:s + 1]
            rows = slice(s * hg, (s + 1) * hg)
            inj_ref[gl, rows, 0:2 * ps] = (wr * bb_re - wi * bb_im).astype(BF16)
            inj_ref[gl, rows, 2 * ps:4 * ps] = (wr * bb_im + wi * bb_re).astype(BF16)
            wr, wi = po_re[s:s + 1], po_im[s:s + 1]
            qt_ref[gl, rows, 0:2 * ps] = (wr * c_re - wi * c_im).astype(BF16)
            qt_ref[gl, rows, 2 * ps:4 * ps] = (-(wr * c_im + wi * c_re)).astype(BF16)
            wr, wi = pl_re[s:s + 1], pl_im[s:s + 1]
            ck_re.append(wr * c_re - wi * c_im)
            ck_im.append(wr * c_im + wi * c_re)
        ck = jnp.concatenate([jnp.concatenate(ck_re, axis=0), jnp.concatenate(ck_im, axis=0)], axis=1)
        bsel = jnp.concatenate(
            [jnp.concatenate([jnp.where(is_f1, bb_re, 0.0), jnp.where(is_f1, -bb_im, 0.0)], axis=1),
             jnp.concatenate([jnp.where(is_f1, 0.0, bb_re), jnp.where(is_f1, 0.0, -bb_im)], axis=1)], axis=0)
        kt = _mm(bsel, ck, "x3", NT)
        w = t * hg
        zero = jnp.zeros((hg, w), F32)
        lag = (pltpu.roll(jnp.concatenate([kt[0:hg], zero], axis=1), w - hg, 1)
               + jnp.concatenate([kt[hg:2 * hg], zero], axis=1))
        for s in range(t):
            sh = (t - 1 - s) * hg
            blk = lag if sh == 0 else pltpu.roll(lag, 2 * w - sh, 1)
            mt_ref[gl, s * hg:(s + 1) * hg, :] = blk[:, 0:w].astype(BF16)
        l16_re, l16_im = power(float(t))
        lre_ref[gl:gl + 1, :] = l16_re
        lim_ref[gl:gl + 1, :] = l16_im


def _s5_tables(lam_re, lam_im, log_step, b_re, b_im, c_re, c_im):
    g, p, hg = b_re.shape
    w = S5_T * hg
    lanes = lambda x: jnp.concatenate([x[0], x[1]], axis=-1)
    dt = lanes(jnp.broadcast_to(jnp.exp(log_step)[..., None], lam_re.shape))
    bt_re = jnp.transpose(b_re, (0, 2, 1))
    bt_im = jnp.transpose(b_im, (0, 2, 1))
    vec = pl.BlockSpec((S5_TG, 2 * p), lambda j: (j, 0))
    mat = pl.BlockSpec((S5_TG, hg, p), lambda j: (j, 0, 0))
    tab = pl.BlockSpec((S5_TG, w, w), lambda j: (j, 0, 0))
    return pl.pallas_call(
        _s5_tables_kernel,
        grid=(g // S5_TG,),
        in_specs=[vec, vec, vec, mat, mat, mat, mat],
        out_specs=[tab, tab, tab, vec, vec],
        out_shape=[jax.ShapeDtypeStruct((g, w, w), BF16)] * 3 + [jax.ShapeDtypeStruct((g, 2 * p), F32)] * 2,
        compiler_params=_cparams(("parallel",), 32),
    )(lanes(lam_re), lanes(lam_im), dt, bt_re, bt_im, c_re, c_im)


def _block_transpose8(xs):
    lane_blk = lax.broadcasted_iota(jnp.int32, xs[0].shape, 1) >> 4
    xs = list(xs)
    for d in (4, 2, 1):
        hi = (lane_blk & d) != 0
        for i in range(8):
            if i & d:
                continue
            a, b = xs[i], xs[i + d]
            xs[i] = jnp.where(hi, pltpu.roll(b, 16 * d, 1), a)
            xs[i + d] = jnp.where(hi, b, pltpu.roll(a, 128 - 16 * d, 1))
    return xs


def _s5_kernel(u0_ref, u1_ref, inj_ref, mt_ref, qt_ref, lre_ref, lim_ref, y_ref, ug_ref, zre_ref, zim_ref,
               hre_ref, him_ref, *, nchunk, nctx):
    gh = ug_ref.shape[0]
    ps = S5_STATE
    t = S5_T
    nlat = nchunk - nctx
    for jt, u_ref in enumerate((u0_ref, u1_ref)):
        for q in range(t // 8):
            cols = [u_ref[0, pl.ds(8 * q + tt, nchunk, stride=t), :] for tt in range(8)]
            grp = _block_transpose8(cols)
            for gl in range(8):
                ug_ref[8 * jt + gl, :, 128 * q:128 * (q + 1)] = grp[gl].astype(BF16)
    for gi in range(gh):
        zg = _dot(ug_ref[gi], inj_ref[gi], NN)
        zre_ref[pl.ds(gi, nchunk, stride=gh), :] = zg[:, 0:2 * ps]
        zim_ref[pl.ds(gi, nchunk, stride=gh), :] = zg[:, 2 * ps:4 * ps]

    lane = lax.broadcasted_iota(jnp.int32, (gh, 2 * ps), 1)
    is_f = lane < ps
    lre = lre_ref[...]
    lim = lim_ref[...]

    def body(i, carry):
        hre, him = carry
        cb = jnp.where(i < nctx, nctx - 1 - i, nchunk + nctx - 1 - i)
        rf = pl.multiple_of(i * gh, gh)
        rb = pl.multiple_of(cb * gh, gh)
        hre_ref[pl.ds(rf, gh), 0:ps] = hre[:, 0:ps]
        hre_ref[pl.ds(rb, gh), ps:2 * ps] = hre[:, ps:2 * ps]
        him_ref[pl.ds(rf, gh), 0:ps] = him[:, 0:ps]
        him_ref[pl.ds(rb, gh), ps:2 * ps] = him[:, ps:2 * ps]
        zre = jnp.where(is_f, zre_ref[pl.ds(rf, gh), :], zre_ref[pl.ds(rb, gh), :])
        zim = jnp.where(is_f, zim_ref[pl.ds(rf, gh), :], zim_ref[pl.ds(rb, gh), :])
        return lre * hre - lim * him + zre, lre * him + lim * hre + zim

    zero = jnp.zeros((gh, 2 * ps), F32)
    lax.fori_loop(0, nchunk, body, (zero, zero))

    for jt in range(gh // 8):
        ys = []
        for gl in range(8):
            gi = 8 * jt + gl
            hre = hre_ref[pl.ds(gi, nchunk, stride=gh), :].astype(BF16)
            him = him_ref[pl.ds(gi, nchunk, stride=gh), :].astype(BF16)
            ys.append(_dot(ug_ref[gi], mt_ref[gi], NN)
                      + _dot(jnp.concatenate([hre, him], axis=1), qt_ref[gi], NT))
        for q in range(t // 8):
            tok = _block_transpose8([y[:, 128 * q:128 * (q + 1)] for y in ys])
            for tt in range(8):
                y_ref[0, jt, pl.ds(8 * q + tt, nlat, stride=t), :] = tok[tt][nctx:]


def _s5(u, tables, ctx_len):
    inj, mt, qt, lre, lim = tables
    bsz, ltot, _ = u.shape
    nchunk = ltot // S5_T
    nctx = ctx_len // S5_T
    gh = S5_GH
    halves = S5_GROUPS // gh
    w = S5_T * S5_GROUP
    tiles = gh * S5_GROUP // 128
    ps2 = 2 * S5_STATE
    kern = functools.partial(_s5_kernel, nchunk=nchunk, nctx=nctx)
    u_spec = lambda k: pl.BlockSpec((1, ltot, 128), lambda b, j: (b, 0, tiles * j + k))
    tab = pl.BlockSpec((gh, w, w), lambda b, j: (j, 0, 0))
    vec = pl.BlockSpec((gh, ps2), lambda b, j: (j, 0))
    return pl.pallas_call(
        kern,
        grid=(bsz, halves),
        in_specs=[u_spec(0), u_spec(1), tab, tab, tab, vec, vec],
        out_specs=pl.BlockSpec((1, tiles, ltot - ctx_len, 128), lambda b, j: (b, j, 0, 0)),
        out_shape=jax.ShapeDtypeStruct((bsz, D_S5 // 128, ltot - ctx_len, 128), F32),
        scratch_shapes=[pltpu.VMEM((gh, nchunk, w), BF16),
                        pltpu.VMEM((nchunk * gh, ps2), F32),
                        pltpu.VMEM((nchunk * gh, ps2), F32),
                        pltpu.VMEM((nchunk * gh, ps2), F32),
                        pltpu.VMEM((nchunk * gh, ps2), F32)],
        compiler_params=_cparams(("parallel", "parallel"), 48),
    )(u, u, inj, mt, qt, lre, lim)


def _mix_kernel(x_ref, yf_ref, yb_ref, r_ref, k_ref, v_ref, lo_ref, u_ref, ys_ref, mod_ref,
                seg_ref, a0_ref, a2_ref, g2_ref, kaw_ref, rk_ref, lnw_ref, lnb_ref,
                dsk_ref, gluw_ref, glub_ref, wo_ref, g2n_ref, w1_ref, w3_ref, w2_ref, gf_ref, o_ref):
    lo = lo_ref[0].astype(F32)
    seg = seg_ref[...]

    def head_sum(t):
        pw = seg.shape[0]
        return jnp.concatenate([_mm(t[:, i:i + pw], seg) for i in range(0, t.shape[1], pw)], axis=1)

    y = yf_ref[0] + yb_ref[0]
    mu = head_sum(y) * (1.0 / HEAD)
    yc = y - mu
    var = head_sum(yc * yc) * (1.0 / HEAD)
    yn = yc * lax.rsqrt(var + RWKV_LN_EPS) * lnw_ref[...] + lnb_ref[...]
    a_f = _sigmoid(a0_ref[0] + _mm(lo, a2_ref[0]))
    a_b = _sigmoid(a0_ref[1] + _mm(lo, a2_ref[1]))
    kaw = kaw_ref[...]
    kd_sum = k_ref[0] * ((1.0 + (a_f - 1.0) * kaw) + (1.0 + (a_b - 1.0) * kaw))
    bonus = head_sum(r_ref[0] * kd_sum * rk_ref[...])
    yn = yn + bonus * v_ref[0]
    gate = _mm(_sigmoid(lo), g2_ref[...])
    rwkv_out = yn * gate

    u = u_ref[0]
    ys = jnp.concatenate([ys_ref[0, i] for i in range(ys_ref.shape[1])], axis=1) + dsk_ref[...] * u
    zg = 0.5 * ys * (1.0 + jnp.tanh(math.sqrt(2.0 / math.pi) * (ys + 0.044715 * (ys * ys * ys))))
    s5_out = zg * _sigmoid(_mm(zg, gluw_ref[...]) + glub_ref[...])

    mix = _mm(rwkv_out, wo_ref[0:D_RWKV, :]) + _mm(s5_out, wo_ref[D_RWKV:, :])
    x1 = x_ref[0] + mod_ref[0, 2:3, :] * mix

    ms = jnp.mean(x1 * x1, axis=-1, keepdims=True)
    h = x1 * lax.rsqrt(ms + NORM_EPS) * g2n_ref[...]
    h = h * (1.0 + mod_ref[0, 4:5, :]) + mod_ref[0, 3:4, :]
    hb = h.astype(BF16)
    h1 = _dot(hb, w1_ref[...], NN)
    h3 = _dot(hb, w3_ref[...], NN)
    act = (h1 * _sigmoid(h1)) * h3
    out = x1 + mod_ref[0, 5:6, :] * _dot(act.astype(BF16), w2_ref[...], NN)
    ms2 = jnp.mean(out * out, axis=-1, keepdims=True)
    o_ref[0] = out * lax.rsqrt(ms2 + NORM_EPS) * gf_ref[...]


def _mix(x, y_f, y_b, rkvc, z, u, ys5, mod8, seg, a0, a2p, g2p, kaw, rk, lnw, lnb, dsk, gluw, glub, wo,
         g2n, w1, w3, w2, gf, ctx_len, tm):
    bsz, seq, d = x.shape
    assert ctx_len % tm == 0 and seq % tm == 0
    off = ctx_len // tm
    lora_blk = COL_LORA // LORA_W
    tok = lambda c: pl.BlockSpec((1, tm, D_RWKV), lambda b, j: (b, j + off, c))
    full = lambda a: pl.BlockSpec(a.shape, lambda b, j: (0,) * a.ndim, pipeline_mode=pl.Buffered(1))
    return pl.pallas_call(
        _mix_kernel,
        grid=(bsz, seq // tm),
        in_specs=[pl.BlockSpec((1, tm, d), lambda b, j: (b, j, 0)),
                  tok(0), tok(0), tok(0), tok(1), tok(2),
                  pl.BlockSpec((1, tm, LORA_W), lambda b, j: (b, j + off, lora_blk)),
                  pl.BlockSpec((1, tm, D_S5), lambda b, j: (b, j + off, 0)),
                  pl.BlockSpec((1, D_S5 // 128, tm, 128), lambda b, j: (b, 0, j, 0)),
                  pl.BlockSpec((1, 8, d), lambda b, j: (b, 0, 0)),
                  full(seg), full(a0), full(a2p), full(g2p), full(kaw), full(rk), full(lnw), full(lnb),
                  full(dsk), full(gluw), full(glub), full(wo),
                  full(g2n), full(w1), full(w3), full(w2), full(gf)],
        out_specs=pl.BlockSpec((1, tm, d), lambda b, j: (b, j, 0)),
        out_shape=jax.ShapeDtypeStruct((bsz, seq, d), F32),
        compiler_params=_cparams(("parallel", "parallel"), 56),
    )(x, y_f, y_b, rkvc, rkvc, rkvc, z, u, ys5, mod8, seg, a0, a2p, g2p, kaw, rk, lnw, lnb,
      dsk, gluw, glub, wo, g2n, w1, w3, w2, gf)


def _pad_rows(w, row0, rows_total):
    pad = [(0, 0)] * (w.ndim - 2) + [(row0, rows_total - row0 - w.shape[-2]), (0, 0)]
    return jnp.pad(w, pad)


def kernel(x, c, ctx, c_ctx, mod_w, mod_b, norm1_g, norm2_g, w_in, w_out, rwkv_conv, rwkv_w0, rwkv_w2, rwkv_a0, rwkv_a2, rwkv_g2, rwkv_kk, rwkv_ka, rwkv_rk, rwkv_ln_w, rwkv_ln_b, s5_lam_re, s5_lam_im, s5_log_step, s5_b_re, s5_b_im, s5_c_re, s5_c_im, s5_d, s5_glu_w, s5_glu_b, ffn_w1, ffn_w3, ffn_w2, final_g):
    bsz, seq, d = x.shape
    ctx_len = ctx.shape[1]
    ltot = ctx_len + seq
    layer = 0

    nmod = bsz + INPROJ_NB
    cc = jnp.concatenate([c, jnp.broadcast_to(c_ctx[None, :], (INPROJ_NB, d)),
                          jnp.zeros((8 - nmod, d), F32)], axis=0)
    mod = _adaln(cc, mod_w[layer], mod_b[layer])
    mod8 = jnp.pad(mod[:nmod].reshape(nmod, N_MOD, d), ((0, 0), (0, 8 - N_MOD), (0, 0)))

    wi = w_in[layer]
    w_z = jnp.concatenate([wi[:, :-D_S5], jnp.zeros((d, D_Z16 + D_S5 - wi.shape[1]), F32),
                           wi[:, -D_S5:]], axis=1).astype(BF16)
    z, u = _inproj(ctx, x, mod8, norm1_g[layer], w_z, 256)

    rkvc = _conv(z, rwkv_conv[layer], ctx_len)

    w2p = _pad_rows(rwkv_w2[layer], 0, LORA_W).at[1].set(_pad_rows(rwkv_w2[layer, 1], DECAY_LORA, LORA_W))
    a2p = jnp.stack([_pad_rows(rwkv_a2[layer, 0], LORA_AD, LORA_W),
                     _pad_rows(rwkv_a2[layer, 1], LORA_AD + AAA_LORA, LORA_W)])
    w0 = rwkv_w0[layer].reshape(2, 1, D_RWKV)
    a0 = rwkv_a0[layer].reshape(2, 1, D_RWKV)
    kkw = rwkv_kk[layer].reshape(1, D_RWKV)
    kaw = rwkv_ka[layer].reshape(1, D_RWKV)
    y_f = _rwkv(rkvc, z, w0, w2p, a0, a2p, kkw, kaw, ctx_len, False)
    y_b = _rwkv(rkvc, z, w0, w2p, a0, a2p, kkw, kaw, ctx_len, True)

    tables = _s5_tables(s5_lam_re[layer], s5_lam_im[layer], s5_log_step[layer], s5_b_re[layer],
                        s5_b_im[layer], s5_c_re[layer], s5_c_im[layer])
    ys5 = _s5(u, tables, ctx_len)

    hh = jnp.arange(2 * HEAD) // HEAD
    seg = (hh[:, None] == hh[None, :]).astype(BF16)
    g2p = _pad_rows(rwkv_g2[layer], LORA_GD, LORA_W)
    return _mix(x, y_f, y_b, rkvc, z, u, ys5, mod8, seg, a0, a2p, g2p, kaw,
                rwkv_rk[layer].reshape(1, D_RWKV), rwkv_ln_w[layer].reshape(1, D_RWKV),
                rwkv_ln_b[layer].reshape(1, D_RWKV), s5_d[layer].reshape(1, D_S5),
                s5_glu_w[layer].astype(BF16), s5_glu_b[layer].reshape(1, D_S5),
                w_out[layer].astype(BF16), norm2_g[layer].reshape(1, d), ffn_w1[layer].astype(BF16),
                ffn_w3[layer].astype(BF16), ffn_w2[layer].astype(BF16), final_g.reshape(1, d), ctx_len, 256)
```

```python
import functools
import math

import jax
import jax.numpy as jnp
import numpy as np
from jax import lax
from jax.experimental import pallas as pl
from jax.experimental.pallas import tpu as pltpu

F32 = jnp.float32
BF16 = jnp.bfloat16

D_MODEL = 1024
GRID_W = 64
D_RWKV = 512
HEAD = 64
HEADS = D_RWKV // HEAD
D_S5 = 512
S5_GROUP = 16
S5_GROUPS = D_S5 // S5_GROUP
S5_STATE = 64
DECAY_LORA = 32
AAA_LORA = 32
GATE_LORA = 96
N_MOD = 6
NORM_EPS = 1e-6
RWKV_LN_EPS = 64e-5

LORA_W = 256
COL_LORA = 3 * D_RWKV
D_Z16 = COL_LORA + LORA_W
LORA_AD = 2 * DECAY_LORA
LORA_GD = LORA_AD + 2 * AAA_LORA

CHUNK = 64
RWKV_BLK = 256
RWKV_NB = 4
INPROJ_NB = 4
S5_T = 16
S5_GH = 16
S5_TG = 8
CONV_PAD = 72
CONV_TC = 256
CONV_PIECE = 128

NN = (((1,), (0,)), ((), ()))
NT = (((1,), (1,)), ((), ()))
TN = (((0,), (0,)), ((), ()))


def _dot(a, b, dims):
    return lax.dot_general(a, b, dims, preferred_element_type=F32)


def _split2(x):
    hi = x.astype(BF16)
    lo = (x - hi.astype(F32)).astype(BF16)
    return hi, lo


def _mm(a, b, mode="bf16", dims=NN):
    if mode == "bf16":
        return _dot(a.astype(BF16), b.astype(BF16), dims)
    if mode == "x3":
        ah, al = _split2(a)
        bh, bl = _split2(b)
        return _dot(ah, bh, dims) + (_dot(ah, bl, dims) + _dot(al, bh, dims))
    if mode == "xb2":
        ab = a.astype(BF16)
        b1, b2 = _split2(b)
        return _dot(ab, b1, dims) + _dot(ab, b2, dims)
    raise ValueError(mode)


def _sigmoid(x):
    return 1.0 / (1.0 + jnp.exp(-x))


def _cparams(sem, vmem_mb):
    return pltpu.CompilerParams(dimension_semantics=sem, vmem_limit_bytes=vmem_mb * 1024 * 1024)


def _adaln_kernel(c_ref, w_ref, b_ref, o_ref):
    c = c_ref[...]
    s = c * _sigmoid(c)
    o_ref[...] = _mm(s, w_ref[...], "x3") + b_ref[...]


def _adaln(cc, mod_w, mod_b):
    rows, d = cc.shape
    n = mod_w.shape[1]
    tn = 1024
    return pl.pallas_call(
        _adaln_kernel,
        grid=(n // tn,),
        in_specs=[pl.BlockSpec((rows, d), lambda j: (0, 0)),
                  pl.BlockSpec((d, tn), lambda j: (0, j)),
                  pl.BlockSpec((1, tn), lambda j: (0, j))],
        out_specs=pl.BlockSpec((rows, tn), lambda j: (0, j)),
        out_shape=jax.ShapeDtypeStruct((rows, n), F32),
        compiler_params=_cparams(("parallel",), 40),
    )(cc, mod_w, mod_b.reshape(1, n))


def _inproj_kernel(c_ref, x_ref, mod_ref, g_ref, w_ref, o_ref, u_ref, *, ctx_blocks):
    x = jnp.where(pl.program_id(1) < ctx_blocks, c_ref[...], x_ref[...])
    nb, tm, d = x.shape
    ms = jnp.mean(x * x, axis=-1, keepdims=True)
    xn = x * lax.rsqrt(ms + NORM_EPS) * g_ref[...]
    shift = mod_ref[:, 0:1, :]
    scale = mod_ref[:, 1:2, :]
    h = (xn * (1.0 + scale) + shift).reshape(nb * tm, d)
    res = _mm(h, w_ref[...], "bf16")
    o_ref[...] = res[:, :D_Z16].astype(BF16).reshape(nb, tm, D_Z16)
    u_ref[...] = res[:, D_Z16:].reshape(nb, tm, res.shape[1] - D_Z16)


def _inproj(ctx, x, mod8, g1, w_z, tm):
    bsz, seq, d = x.shape
    nb = INPROJ_NB
    ctx_blocks = ctx.shape[1] // tm
    dz = w_z.shape[1]
    ltot = ctx.shape[1] + seq
    kern = functools.partial(_inproj_kernel, ctx_blocks=ctx_blocks)
    return pl.pallas_call(
        kern,
        grid=(bsz // nb, ctx_blocks + seq // tm),
        in_specs=[pl.BlockSpec((nb, tm, d), lambda b, j: (b, jnp.minimum(j, ctx_blocks - 1), 0)),
                  pl.BlockSpec((nb, tm, d), lambda b, j: (b, jnp.maximum(j - ctx_blocks, 0), 0)),
                  pl.BlockSpec((nb, 8, d), lambda b, j: (jnp.where(j < ctx_blocks, bsz // nb, b), 0, 0)),
                  pl.BlockSpec((1, d), lambda b, j: (0, 0)),
                  pl.BlockSpec((d, dz), lambda b, j: (0, 0))],
        out_specs=[pl.BlockSpec((nb, tm, D_Z16), lambda b, j: (b, j, 0)),
                   pl.BlockSpec((nb, tm, dz - D_Z16), lambda b, j: (b, j, 0))],
        out_shape=[jax.ShapeDtypeStruct((bsz, ltot, D_Z16), BF16),
                   jax.ShapeDtypeStruct((bsz, ltot, dz - D_Z16), F32)],
        compiler_params=_cparams(("parallel", "parallel"), 48),
    )(ctx, x, mod8, g1.reshape(1, d), w_z)


def _conv_kernel(z_ref, w_ref, o_ref, xs_ref, *, ctx_len, lat_len):
    tc = z_ref.shape[2]
    pc = CONV_PIECE
    ctx_off = CONV_PAD
    lat_off = 2 * CONV_PAD + ctx_len
    zpad = jnp.zeros((CONV_PAD, tc), F32)
    for s in range(3):
        xs_ref[s, 0:CONV_PAD, :] = zpad
        xs_ref[s, ctx_off + ctx_len:lat_off, :] = zpad
        xs_ref[s, lat_off + lat_len:lat_off + lat_len + CONV_PAD, :] = zpad
    xs_ref[1, lat_off:lat_off + 8, :] = zpad[0:8]
    xs_ref[2, lat_off + lat_len - 8:lat_off + lat_len, :] = zpad[0:8]
    col = lax.broadcasted_iota(jnp.int32, (pc, tc), 0) % GRID_W
    keep_l = (col != GRID_W - 1).astype(F32)
    keep_r = (col != 0).astype(F32)
    for p in range(ctx_len // pc):
        xs_ref[0, ctx_off + p * pc:ctx_off + (p + 1) * pc, :] = z_ref[0, p * pc:(p + 1) * pc, :].astype(F32)
    for p in range(lat_len // pc):
        x = z_ref[0, ctx_len + p * pc:ctx_len + (p + 1) * pc, :].astype(F32)
        dst = lat_off + p * pc
        xs_ref[0, dst:dst + pc, :] = x
        xs_ref[1, dst + 1:dst + 1 + pc, :] = x * keep_l
        xs_ref[2, dst - 1:dst - 1 + pc, :] = x * keep_r
    w = [w_ref[t:t + 1, :] for t in range(9)]
    for p in range(ctx_len // pc):
        base = ctx_off + p * pc
        acc = w[3] * xs_ref[0, base - 1:base - 1 + pc, :]
        acc = acc + w[4] * xs_ref[0, base:base + pc, :]
        acc = acc + w[5] * xs_ref[0, base + 1:base + 1 + pc, :]
        o_ref[0, p * pc:(p + 1) * pc, :] = acc
    for p in range(lat_len // pc):
        base = lat_off + p * pc
        acc = None
        for di in range(3):
            for dj in range(3):
                off = base + (di - 1) * GRID_W
                src = (1, 0, 2)[dj]
                term = w[3 * di + dj] * xs_ref[src, off:off + pc, :]
                acc = term if acc is None else acc + term
        o_ref[0, ctx_len + p * pc:ctx_len + (p + 1) * pc, :] = acc


def _conv(z, conv_w, ctx_len):
    bsz, ltot, _ = z.shape
    lat_len = ltot - ctx_len
    ch = 3 * D_RWKV
    rows = 3 * CONV_PAD + ltot
    kern = functools.partial(_conv_kernel, ctx_len=ctx_len, lat_len=lat_len)
    return pl.pallas_call(
        kern,
        grid=(bsz, ch // CONV_TC),
        in_specs=[pl.BlockSpec((1, ltot, CONV_TC), lambda b, j: (b, 0, j)),
                  pl.BlockSpec((9, CONV_TC), lambda b, j: (0, j))],
        out_specs=pl.BlockSpec((1, ltot, CONV_TC), lambda b, j: (b, 0, j)),
        out_shape=jax.ShapeDtypeStruct((bsz, ltot, ch), F32),
        scratch_shapes=[pltpu.VMEM((3, rows, CONV_TC), F32)],
        compiler_params=_cparams(("parallel", "parallel"), 48),
    )(z, conv_w.reshape(9, ch))


def _bd(x, lo):
    return jnp.concatenate([jnp.where(lo, x, 0.0), jnp.where(lo, 0.0, x)], axis=0)


def _rwkv_masks(reverse):
    n, c = RWKV_BLK, CHUNK
    ti, tj = np.indices((n, n))
    causal = (tj >= ti) if reverse else (tj <= ti)
    tri = ((ti // c == tj // c) & causal).astype(np.float32)
    ii, jj = np.indices((c, 2 * c))
    jj = jj % c
    tabs = [(jj > ii) if reverse else (jj < ii), (jj >= ii) if reverse else (jj <= ii), ii == jj,
            ii // 2 == jj // 2]
    m = 2
    while m < c:
        tabs.append((ii // (2 * m) == jj // (2 * m)) & (ii // m != jj // m))
        m *= 2
    return jnp.asarray(tri, BF16), jnp.asarray(np.stack(tabs).astype(np.float32))


def _tri_inv_pairs(n_mats, masks, lo):
    ts = [masks[0] + n * masks[1] for n in n_mats]
    for lvl in masks[2:]:
        nls = [n * lvl for n in n_mats]
        tmp = [_mm(t, _bd(nl, lo)) for t, nl in zip(ts, nls)]
        ts = [t + _mm(x, _bd(t, lo)) for x, t in zip(tmp, ts)]
    return ts


def _rwkv_kernel(r_ref, k_ref, v_ref, lo_ref, w0_ref, w2_ref, a0_ref, a2_ref, kkw_ref, kaw_ref,
                 tri_ref, cm_ref, y_ref, h_ref, *, reverse):
    step = pl.program_id(1)

    @pl.when(step == 0)
    def _():
        h_ref[...] = jnp.zeros_like(h_ref)

    nb, n, _ = r_ref.shape
    c = CHUNK
    nsub = n // c
    pw = 2 * HEAD
    npair = HEADS // 2

    def rows_of(bb):
        r = r_ref[bb]
        k = k_ref[bb]
        v = v_ref[bb]
        lora = lo_ref[bb].astype(F32)
        logit_w = w0_ref[0] + _mm(jnp.tanh(lora), w2_ref[0])
        lw = -math.exp(-0.5) * _sigmoid(logit_w)
        a = _sigmoid(a0_ref[0] + _mm(lora, a2_ref[0]))
        kd = k * ((1.0 - kaw_ref[...]) + a * kaw_ref[...])
        kk = k * kkw_ref[...]
        g = _mm(tri_ref[...], lw, "xb2")
        return r, v, lw, a, kd, kk, g

    lo = lax.broadcasted_iota(jnp.int32, (c, pw), 1) < HEAD
    strict2 = cm_ref[0]
    incl2 = cm_ref[1]
    inv_masks = [cm_ref[i] for i in range(2, cm_ref.shape[0])]
    ri = lax.broadcasted_iota(jnp.int32, (pw, pw), 0)
    rj = lax.broadcasted_iota(jnp.int32, (pw, pw), 1)
    blk = (ri < HEAD) == (rj < HEAD)
    eye_p = (ri == rj).astype(F32)
    zeros_c = jnp.zeros((c, pw), F32)
    zeros_p = jnp.zeros((pw, pw), F32)

    def prep(tokw, j):
        r, v, lw, a, kd, kk, g = tokw
        rows = slice(j * c, (j + 1) * c)
        lw_j = lw[rows]
        g_j = g[rows]
        tot = g_j[0:1] if reverse else g_j[c - 1:c]
        e_prev = jnp.exp(g_j - lw_j)
        e_neg = jnp.exp(-g_j)
        e_pos = jnp.exp(g_j)
        e_tot = jnp.exp(tot)
        out = []
        for p in range(npair):
            sl = slice(p * pw, (p + 1) * pw)
            kkp = kk[rows, sl]
            sq = kkp * kkp
            n2 = jnp.where(lo, jnp.sum(jnp.where(lo, sq, 0.0), axis=-1, keepdims=True),
                           jnp.sum(jnp.where(lo, 0.0, sq), axis=-1, keepdims=True))
            kkp = kkp * lax.rsqrt(jnp.maximum(n2, 1e-12))
            b_t = kkp * a[rows, sl] * e_neg[:, sl]
            k_t = kd[rows, sl] * e_neg[:, sl]
            out.append(dict(a_t=-kkp * e_prev[:, sl], b_t=b_t, k_t=k_t, r_t=r[rows, sl] * e_pos[:, sl],
                            b_e=b_t * e_tot[:, sl], k_e=k_t * e_tot[:, sl], vh=v[rows, sl],
                            g_end=e_tot[:, sl]))
        return out

    def local(us):
        s_bk = [_mm(jnp.concatenate([u["a_t"], u["r_t"]], axis=0),
                    jnp.concatenate([_bd(u["b_t"], lo), _bd(u["k_t"], lo)], axis=0), "bf16", NT)
                for u in us]
        nab = [s[:c, :pw] * strict2 for s in s_bk]
        aak = [s[:c, pw:] * strict2 for s in s_bk]
        arbk = [jnp.concatenate([s[c:, :pw] * incl2, s[c:, pw:] * incl2], axis=1) for s in s_bk]
        t_inv = _tri_inv_pairs(nab, inv_masks, lo)
        x1 = [_mm(m, _bd(u["vh"], lo)) for m, u in zip(aak, us)]
        uw = [_mm(t, jnp.concatenate([_bd(x, lo), _bd(u["a_t"], lo)], axis=1))
              for t, x, u in zip(t_inv, x1, us)]
        yq = [_mm(m, jnp.concatenate(
            [jnp.concatenate([_bd(w_[:, :pw], lo), _bd(w_[:, pw:], lo)], axis=1),
             jnp.concatenate([_bd(u["vh"], lo), zeros_p], axis=1)], axis=0))
            for m, w_, u in zip(arbk, uw, us)]
        gp = [_mm(jnp.concatenate([u["b_e"], u["k_e"]], axis=0),
                  jnp.concatenate([w_, jnp.concatenate([u["vh"], zeros_c], axis=1)], axis=0), "bf16", TN)
              for w_, u in zip(uw, us)]
        res = []
        for u, yq_, gp_ in zip(us, yq, gp):
            q = u["r_t"] + yq_[:, pw:]
            p_bd = jnp.where(blk, gp_[:, pw:], 0.0) + eye_p * u["g_end"]
            res.append((jnp.concatenate([q, p_bd], axis=0), yq_[:, :pw], jnp.where(blk, gp_[:, :pw], 0.0)))
        return res

    order = list(range(nsub - 1, -1, -1) if reverse else range(nsub))
    tokws = [rows_of(bb) for bb in range(nb)]
    loc = local([u for bb in range(nb) for j in order for u in prep(tokws[bb], j)])
    hs = [h_ref[i] for i in range(nb * npair)]
    for gi, j in enumerate(order):
        idx = [(bb * nsub + gi) * npair + p for bb in range(nb) for p in range(npair)]
        res = [_mm(loc[u][0], h) for u, h in zip(idx, hs)]
        for i, u in enumerate(idx):
            bb, p = divmod(i, npair)
            y_ref[bb, j * c:(j + 1) * c, p * pw:(p + 1) * pw] = res[i][:c] + loc[u][1]
            hs[i] = res[i][c:] + loc[u][2]
    for i in range(nb * npair):
        h_ref[i] = hs[i]


def _rwkv(rkvc, z, w0, w2p, a0, a2p, kkw, kaw, ctx_len, reverse):
    bsz, ltot, _ = rkvc.shape
    nblk = ltot // RWKV_BLK
    nctx = ctx_len // RWKV_BLK
    lora_blk = COL_LORA // LORA_W
    d = 1 if reverse else 0

    def tok(i):
        return jnp.where(i < nctx, nctx - 1 - i, nblk + nctx - 1 - i) if reverse else i

    par3 = lambda b, i: (d, 0, 0)
    tri, cmask = _rwkv_masks(reverse)
    kern = functools.partial(_rwkv_kernel, reverse=reverse)
    return pl.pallas_call(
        kern,
        grid=(bsz // RWKV_NB, nblk),
        in_specs=[pl.BlockSpec((RWKV_NB, RWKV_BLK, D_RWKV), lambda b, i: (b, tok(i), 0)),
                  pl.BlockSpec((RWKV_NB, RWKV_BLK, D_RWKV), lambda b, i: (b, tok(i), 1)),
                  pl.BlockSpec((RWKV_NB, RWKV_BLK, D_RWKV), lambda b, i: (b, tok(i), 2)),
                  pl.BlockSpec((RWKV_NB, RWKV_BLK, LORA_W), lambda b, i: (b, tok(i), lora_blk)),
                  pl.BlockSpec((1, 1, D_RWKV), par3),
                  pl.BlockSpec((1, LORA_W, D_RWKV), par3),
                  pl.BlockSpec((1, 1, D_RWKV), par3),
                  pl.BlockSpec((1, LORA_W, D_RWKV), par3),
                  pl.BlockSpec((1, D_RWKV), lambda b, i: (0, 0)),
                  pl.BlockSpec((1, D_RWKV), lambda b, i: (0, 0)),
                  pl.BlockSpec(tri.shape, lambda b, i: (0, 0)),
                  pl.BlockSpec(cmask.shape, lambda b, i: (0, 0, 0))],
        out_specs=pl.BlockSpec((RWKV_NB, RWKV_BLK, D_RWKV), lambda b, i: (b, tok(i), 0)),
        out_shape=jax.ShapeDtypeStruct((bsz, ltot, D_RWKV), F32),
        scratch_shapes=[pltpu.VMEM((RWKV_NB * HEADS // 2, 2 * HEAD, 2 * HEAD), F32)],
        compiler_params=_cparams(("parallel", "arbitrary"), 48),
    )(rkvc, rkvc, rkvc, z, w0, w2p, a0, a2p, kkw, kaw, tri, cmask)


def _s5_tables_kernel(lam_re_ref, lam_im_ref, dt_ref, bt_re_ref, bt_im_ref, c_re_ref, c_im_ref,
                      inj_ref, mt_ref, qt_ref, lre_ref, lim_ref):
    t = S5_T
    hg = S5_GROUP
    ps = S5_STATE
    row = lax.broadcasted_iota(jnp.int32, (t, 2 * ps), 0).astype(F32)
    is_f = lax.broadcasted_iota(jnp.int32, (t, 2 * ps), 1) < ps
    is_f1 = is_f[0:1]
    n_inj = jnp.where(is_f, t - 1.0 - row, row)
    n_out = jnp.where(is_f, row + 1.0, t - row)
    n_lag = jnp.where(is_f, row, t - 1.0 - row)
    for gl in range(lam_re_ref.shape[0]):
        lam_re = lam_re_ref[gl:gl + 1, :]
        lam_im = lam_im_ref[gl:gl + 1, :]
        dt = dt_ref[gl:gl + 1, :]

        def power(n):
            mag = jnp.exp(lam_re * dt * n)
            return mag * jnp.cos(lam_im * dt * n), mag * jnp.sin(lam_im * dt * n)

        lb_re, lb_im = power(1.0)
        den = lam_re * lam_re + lam_im * lam_im
        q_re = ((lb_re - 1.0) * lam_re + lb_im * lam_im) / den
        q_im = (lb_im * lam_re - (lb_re - 1.0) * lam_im) / den
        bt_re = jnp.concatenate([bt_re_ref[gl], bt_re_ref[gl]], axis=1)
        bt_im = jnp.concatenate([bt_im_ref[gl], bt_im_ref[gl]], axis=1)
        bb_re = q_re * bt_re - q_im * bt_im
        bb_im = q_re * bt_im + q_im * bt_re
        c_re = jnp.concatenate([c_re_ref[gl], c_re_ref[gl]], axis=1)
        c_im = jnp.concatenate([c_im_ref[gl], c_im_ref[gl]], axis=1)

        pi_re, pi_im = power(n_inj)
        po_re, po_im = power(n_out)
        pl_re, pl_im = power(n_lag)
        ck_re, ck_im = [], []
        for s in range(t):
            wr, wi = pi_re[s:s + 1], pi_im[s:s + 1]
            rows = slice(s * hg, (s + 1) * hg)
            inj_ref[gl, rows, 0:2 * ps] = (wr * bb_re - wi * bb_im).astype(BF16)
            inj_ref[gl, rows, 2 * ps:4 * ps] = (wr * bb_im + wi * bb_re).astype(BF16)
            wr, wi = po_re[s:s + 1], po_im[s:s + 1]
            qt_ref[gl, rows, 0:2 * ps] = (wr * c_re - wi * c_im).astype(BF16)
            qt_ref[gl, rows, 2 * ps:4 * ps] = (-(wr * c_im + wi * c_re)).astype(BF16)
            wr, wi = pl_re[s:s + 1], pl_im[s:s + 1]
            ck_re.append(wr * c_re - wi * c_im)
            ck_im.append(wr * c_im + wi * c_re)
        ck = jnp.concatenate([jnp.concatenate(ck_re, axis=0), jnp.concatenate(ck_im, axis=0)], axis=1)
        bsel = jnp.concatenate(
            [jnp.concatenate([jnp.where(is_f1, bb_re, 0.0), jnp.where(is_f1, -bb_im, 0.0)], axis=1),
             jnp.concatenate([jnp.where(is_f1, 0.0, bb_re), jnp.where(is_f1, 0.0, -bb_im)], axis=1)], axis=0)
        kt = _mm(bsel, ck, "x3", NT)
        w = t * hg
        zero = jnp.zeros((hg, w), F32)
        lag = (pltpu.roll(jnp.concatenate([kt[0:hg], zero], axis=1), w - hg, 1)
               + jnp.concatenate([kt[hg:2 * hg], zero], axis=1))
        for s in range(t):
            sh = (t - 1 - s) * hg
            blk = lag if sh == 0 else pltpu.roll(lag, 2 * w - sh, 1)
            mt_ref[gl, s * hg:(s + 1) * hg, :] = blk[:, 0:w].astype(BF16)
        l16_re, l16_im = power(float(t))
        lre_ref[gl:gl + 1, :] = l16_re
        lim_ref[gl:gl + 1, :] = l16_im


def _s5_tables(lam_re, lam_im, log_step, b_re, b_im, c_re, c_im):
    g, p, hg = b_re.shape
    w = S5_T * hg
    lanes = lambda x: jnp.concatenate([x[0], x[1]], axis=-1)
    dt = lanes(jnp.broadcast_to(jnp.exp(log_step)[..., None], lam_re.shape))
    bt_re = jnp.transpose(b_re, (0, 2, 1))
    bt_im = jnp.transpose(b_im, (0, 2, 1))
    vec = pl.BlockSpec((S5_TG, 2 * p), lambda j: (j, 0))
    mat = pl.BlockSpec((S5_TG, hg, p), lambda j: (j, 0, 0))
    tab = pl.BlockSpec((S5_TG, w, w), lambda j: (j, 0, 0))
    return pl.pallas_call(
        _s5_tables_kernel,
        grid=(g // S5_TG,),
        in_specs=[vec, vec, vec, mat, mat, mat, mat],
        out_specs=[tab, tab, tab, vec, vec],
        out_shape=[jax.ShapeDtypeStruct((g, w, w), BF16)] * 3 + [jax.ShapeDtypeStruct((g, 2 * p), F32)] * 2,
        compiler_params=_cparams(("parallel",), 32),
    )(lanes(lam_re), lanes(lam_im), dt, bt_re, bt_im, c_re, c_im)


def _block_transpose8(xs):
    lane_blk = lax.broadcasted_iota(jnp.int32, xs[0].shape, 1) >> 4
    xs = list(xs)
    for d in (4, 2, 1):
        hi = (lane_blk & d) != 0
        for i in range(8):
            if i & d:
                continue
            a, b = xs[i], xs[i + d]
            xs[i] = jnp.where(hi, pltpu.roll(b, 16 * d, 1), a)
            xs[i + d] = jnp.where(hi, b, pltpu.roll(a, 128 - 16 * d, 1))
    return xs


def _s5_kernel(u0_ref, u1_ref, inj_ref, mt_ref, qt_ref, lre_ref, lim_ref, y_ref, ug_ref, zre_ref, zim_ref,
               hre_ref, him_ref, *, nchunk, nctx):
    gh = ug_ref.shape[0]
    ps = S5_STATE
    t = S5_T
    nlat = nchunk - nctx
    for jt, u_ref in enumerate((u0_ref, u1_ref)):
        for q in range(t // 8):
            cols = [u_ref[0, pl.ds(8 * q + tt, nchunk, stride=t), :] for tt in range(8)]
            grp = _block_transpose8(cols)
            for gl in range(8):
                ug_ref[8 * jt + gl, :, 128 * q:128 * (q + 1)] = grp[gl].astype(BF16)
    for gi in range(gh):
        zg = _dot(ug_ref[gi], inj_ref[gi], NN)
        zre_ref[pl.ds(gi, nchunk, stride=gh), :] = zg[:, 0:2 * ps]
        zim_ref[pl.ds(gi, nchunk, stride=gh), :] = zg[:, 2 * ps:4 * ps]

    lane = lax.broadcasted_iota(jnp.int32, (gh, 2 * ps), 1)
    is_f = lane < ps
    lre = lre_ref[...]
    lim = lim_ref[...]

    def body(i, carry):
        hre, him = carry
        cb = jnp.where(i < nctx, nctx - 1 - i, nchunk + nctx - 1 - i)
        rf = pl.multiple_of(i * gh, gh)
        rb = pl.multiple_of(cb * gh, gh)
        hre_ref[pl.ds(rf, gh), 0:ps] = hre[:, 0:ps]
        hre_ref[pl.ds(rb, gh), ps:2 * ps] = hre[:, ps:2 * ps]
        him_ref[pl.ds(rf, gh), 0:ps] = him[:, 0:ps]
        him_ref[pl.ds(rb, gh), ps:2 * ps] = him[:, ps:2 * ps]
        zre = jnp.where(is_f, zre_ref[pl.ds(rf, gh), :], zre_ref[pl.ds(rb, gh), :])
        zim = jnp.where(is_f, zim_ref[pl.ds(rf, gh), :], zim_ref[pl.ds(rb, gh), :])
        return lre * hre - lim * him + zre, lre * him + lim * hre + zim

    zero = jnp.zeros((gh, 2 * ps), F32)
    lax.fori_loop(0, nchunk, body, (zero, zero))

    for jt in range(gh // 8):
        ys = []
        for gl in range(8):
            gi = 8 * jt + gl
            hre = hre_ref[pl.ds(gi, nchunk, stride=gh), :].astype(BF16)
            him = him_ref[pl.ds(gi, nchunk, stride=gh), :].astype(BF16)
            ys.append(_dot(ug_ref[gi], mt_ref[gi], NN)
                      + _dot(jnp.concatenate([hre, him], axis=1), qt_ref[gi], NT))
        for q in range(t // 8):
            tok = _block_transpose8([y[:, 128 * q:128 * (q + 1)] for y in ys])
            for tt in range(8):
                y_ref[0, jt, pl.ds(8 * q + tt, nlat, stride=t), :] = tok[tt][nctx:]


def _s5(u, tables, ctx_len):
    inj, mt, qt, lre, lim = tables
    bsz, ltot, _ = u.shape
    nchunk = ltot // S5_T
    nctx = ctx_len // S5_T
    gh = S5_GH
    halves = S5_GROUPS // gh
    w = S5_T * S5_GROUP
    tiles = gh * S5_GROUP // 128
    ps2 = 2 * S5_STATE
    kern = functools.partial(_s5_kernel, nchunk=nchunk, nctx=nctx)
    u_spec = lambda k: pl.BlockSpec((1, ltot, 128), lambda b, j: (b, 0, tiles * j + k))
    tab = pl.BlockSpec((gh, w, w), lambda b, j: (j, 0, 0))
    vec = pl.BlockSpec((gh, ps2), lambda b, j: (j, 0))
    return pl.pallas_call(
        kern,
        grid=(bsz, halves),
        in_specs=[u_spec(0), u_spec(1), tab, tab, tab, vec, vec],
        out_specs=pl.BlockSpec((1, tiles, ltot - ctx_len, 128), lambda b, j: (b, j, 0, 0)),
        out_shape=jax.ShapeDtypeStruct((bsz, D_S5 // 128, ltot - ctx_len, 128), F32),
        scratch_shapes=[pltpu.VMEM((gh, nchunk, w), BF16),
                        pltpu.VMEM((nchunk * gh, ps2), F32),
                        pltpu.VMEM((nchunk * gh, ps2), F32),
                        pltpu.VMEM((nchunk * gh, ps2), F32),
                        pltpu.VMEM((nchunk * gh, ps2), F32)],
        compiler_params=_cparams(("parallel", "parallel"), 48),
    )(u, u, inj, mt, qt, lre, lim)


def _mix_kernel(x_ref, yf_ref, yb_ref, r_ref, k_ref, v_ref, lo_ref, u_ref, ys_ref, mod_ref,
                seg_ref, a0_ref, a2_ref, g2_ref, kaw_ref, rk_ref, lnw_ref, lnb_ref,
                dsk_ref, gluw_ref, glub_ref, wo_ref, g2n_ref, w1_ref, w3_ref, w2_ref, gf_ref, o_ref):
    lo = lo_ref[0].astype(F32)
    seg = seg_ref[...]

    def head_sum(t):
        pw = seg.shape[0]
        return jnp.concatenate([_mm(t[:, i:i + pw], seg) for i in range(0, t.shape[1], pw)], axis=1)

    y = yf_ref[0] + yb_ref[0]
    mu = head_sum(y) * (1.0 / HEAD)
    yc = y - mu
    var = head_sum(yc * yc) * (1.0 / HEAD)
    yn = yc * lax.rsqrt(var + RWKV_LN_EPS) * lnw_ref[...] + lnb_ref[...]
    a_f = _sigmoid(a0_ref[0] + _mm(lo, a2_ref[0]))
    a_b = _sigmoid(a0_ref[1] + _mm(lo, a2_ref[1]))
    kaw = kaw_ref[...]
    kd_sum = k_ref[0] * ((1.0 + (a_f - 1.0) * kaw) + (1.0 + (a_b - 1.0) * kaw))
    bonus = head_sum(r_ref[0] * kd_sum * rk_ref[...])
    yn = yn + bonus * v_ref[0]
    gate = _mm(_sigmoid(lo), g2_ref[...])
    rwkv_out = yn * gate

    u = u_ref[0]
    ys = jnp.concatenate([ys_ref[0, i] for i in range(ys_ref.shape[1])], axis=1) + dsk_ref[...] * u
    zg = 0.5 * ys * (1.0 + jnp.tanh(math.sqrt(2.0 / math.pi) * (ys + 0.044715 * (ys * ys * ys))))
    s5_out = zg * _sigmoid(_mm(zg, gluw_ref[...]) + glub_ref[...])

    mix = _mm(rwkv_out, wo_ref[0:D_RWKV, :]) + _mm(s5_out, wo_ref[D_RWKV:, :])
    x1 = x_ref[0] + mod_ref[0, 2:3, :] * mix

    ms = jnp.mean(x1 * x1, axis=-1, keepdims=True)
    h = x1 * lax.rsqrt(ms + NORM_EPS) * g2n_ref[...]
    h = h * (1.0 + mod_ref[0, 4:5, :]) + mod_ref[0, 3:4, :]
    hb = h.astype(BF16)
    h1 = _dot(hb, w1_ref[...], NN)
    h3 = _dot(hb, w3_ref[...], NN)
    act = (h1 * _sigmoid(h1)) * h3
    out = x1 + mod_ref[0, 5:6, :] * _dot(act.astype(BF16), w2_ref[...], NN)
    ms2 = jnp.mean(out * out, axis=-1, keepdims=True)
    o_ref[0] = out * lax.rsqrt(ms2 + NORM_EPS) * gf_ref[...]


def _mix(x, y_f, y_b, rkvc, z, u, ys5, mod8, seg, a0, a2p, g2p, kaw, rk, lnw, lnb, dsk, gluw, glub, wo,
         g2n, w1, w3, w2, gf, ctx_len, tm):
    bsz, seq, d = x.shape
    assert ctx_len % tm == 0 and seq % tm == 0
    off = ctx_len // tm
    lora_blk = COL_LORA // LORA_W
    tok = lambda c: pl.BlockSpec((1, tm, D_RWKV), lambda b, j: (b, j + off, c))
    full = lambda a: pl.BlockSpec(a.shape, lambda b, j: (0,) * a.ndim, pipeline_mode=pl.Buffered(1))
    return pl.pallas_call(
        _mix_kernel,
        grid=(bsz, seq // tm),
        in_specs=[pl.BlockSpec((1, tm, d), lambda b, j: (b, j, 0)),
                  tok(0), tok(0), tok(0), tok(1), tok(2),
                  pl.BlockSpec((1, tm, LORA_W), lambda b, j: (b, j + off, lora_blk)),
                  pl.BlockSpec((1, tm, D_S5), lambda b, j: (b, j + off, 0)),
                  pl.BlockSpec((1, D_S5 // 128, tm, 128), lambda b, j: (b, 0, j, 0)),
                  pl.BlockSpec((1, 8, d), lambda b, j: (b, 0, 0)),
                  full(seg), full(a0), full(a2p), full(g2p), full(kaw), full(rk), full(lnw), full(lnb),
                  full(dsk), full(gluw), full(glub), full(wo),
                  full(g2n), full(w1), full(w3), full(w2), full(gf)],
        out_specs=pl.BlockSpec((1, tm, d), lambda b, j: (b, j, 0)),
        out_shape=jax.ShapeDtypeStruct((bsz, seq, d), F32),
        compiler_params=_cparams(("parallel", "parallel"), 56),
    )(x, y_f, y_b, rkvc, rkvc, rkvc, z, u, ys5, mod8, seg, a0, a2p, g2p, kaw, rk, lnw, lnb,
      dsk, gluw, glub, wo, g2n, w1, w3, w2, gf)


def _pad_rows(w, row0, rows_total):
    pad = [(0, 0)] * (w.ndim - 2) + [(row0, rows_total - row0 - w.shape[-2]), (0, 0)]
    return jnp.pad(w, pad)


def kernel(x, c, ctx, c_ctx, mod_w, mod_b, norm1_g, norm2_g, w_in, w_out, rwkv_conv, rwkv_w0, rwkv_w2, rwkv_a0, rwkv_a2, rwkv_g2, rwkv_kk, rwkv_ka, rwkv_rk, rwkv_ln_w, rwkv_ln_b, s5_lam_re, s5_lam_im, s5_log_step, s5_b_re, s5_b_im, s5_c_re, s5_c_im, s5_d, s5_glu_w, s5_glu_b, ffn_w1, ffn_w3, ffn_w2, final_g):
    bsz, seq, d = x.shape
    ctx_len = ctx.shape[1]
    ltot = ctx_len + seq
    layer = 0

    nmod = bsz + INPROJ_NB
    cc = jnp.concatenate([c, jnp.broadcast_to(c_ctx[None, :], (INPROJ_NB, d)),
                          jnp.zeros((8 - nmod, d), F32)], axis=0)
    mod = _adaln(cc, mod_w[layer], mod_b[layer])
    mod8 = jnp.pad(mod[:nmod].reshape(nmod, N_MOD, d), ((0, 0), (0, 8 - N_MOD), (0, 0)))

    wi = w_in[layer]
    w_z = jnp.concatenate([wi[:, :-D_S5], jnp.zeros((d, D_Z16 + D_S5 - wi.shape[1]), F32),
                           wi[:, -D_S5:]], axis=1).astype(BF16)
    z, u = _inproj(ctx, x, mod8, norm1_g[layer], w_z, 256)

    rkvc = _conv(z, rwkv_conv[layer], ctx_len)

    w2p = _pad_rows(rwkv_w2[layer], 0, LORA_W).at[1].set(_pad_rows(rwkv_w2[layer, 1], DECAY_LORA, LORA_W))
    a2p = jnp.stack([_pad_rows(rwkv_a2[layer, 0], LORA_AD, LORA_W),
                     _pad_rows(rwkv_a2[layer, 1], LORA_AD + AAA_LORA, LORA_W)])
    w0 = rwkv_w0[layer].reshape(2, 1, D_RWKV)
    a0 = rwkv_a0[layer].reshape(2, 1, D_RWKV)
    kkw = rwkv_kk[layer].reshape(1, D_RWKV)
    kaw = rwkv_ka[layer].reshape(1, D_RWKV)
    y_f = _rwkv(rkvc, z, w0, w2p, a0, a2p, kkw, kaw, ctx_len, False)
    y_b = _rwkv(rkvc, z, w0, w2p, a0, a2p, kkw, kaw, ctx_len, True)

    tables = _s5_tables(s5_lam_re[layer], s5_lam_im[layer], s5_log_step[layer], s5_b_re[layer],
                        s5_b_im[layer], s5_c_re[layer], s5_c_im[layer])
    ys5 = _s5(u, tables, ctx_len)

    hh = jnp.arange(2 * HEAD) // HEAD
    seg = (hh[:, None] == hh[None, :]).astype(BF16)
    g2p = _pad_rows(rwkv_g2[layer], LORA_GD, LORA_W)
    return _mix(x, y_f, y_b, rkvc, z, u, ys5, mod8, seg, a0, a2p, g2p, kaw,
                rwkv_rk[layer].reshape(1, D_RWKV), rwkv_ln_w[layer].reshape(1, D_RWKV),
                rwkv_ln_b[layer].reshape(1, D_RWKV), s5_d[layer].reshape(1, D_S5),
                s5_glu_w[layer].astype(BF16), s5_glu_b[layer].reshape(1, D_S5),
                w_out[layer].astype(BF16), norm2_g[layer].reshape(1, d), ffn_w1[layer].astype(BF16),
                ffn_w3[layer].astype(BF16), ffn_w2[layer].astype(BF16), final_g.reshape(1, d), ctx_len, 256)
```

```python
import functools
import math

import jax
import jax.numpy as jnp
import numpy as np
from jax import lax
from jax.experimental import pallas as pl
from jax.experimental.pallas import tpu as pltpu

F32 = jnp.float32
BF16 = jnp.bfloat16

D_MODEL = 1024
GRID_W = 64
D_RWKV = 512
HEAD = 64
HEADS = D_RWKV // HEAD
D_S5 = 512
S5_GROUP = 16
S5_GROUPS = D_S5 // S5_GROUP
S5_STATE = 64
DECAY_LORA = 32
AAA_LORA = 32
GATE_LORA = 96
N_MOD = 6
NORM_EPS = 1e-6
RWKV_LN_EPS = 64e-5

LORA_W = 256
COL_LORA = 3 * D_RWKV
D_Z16 = COL_LORA + LORA_W
LORA_AD = 2 * DECAY_LORA
LORA_GD = LORA_AD + 2 * AAA_LORA

CHUNK = 64
RWKV_BLK = 256
RWKV_NB = 4
INPROJ_NB = 4
S5_T = 16
S5_GH = 16
S5_TG = 8
CONV_PAD = 72
CONV_TC = 256
CONV_PIECE = 128

NN = (((1,), (0,)), ((), ()))
NT = (((1,), (1,)), ((), ()))
TN = (((0,), (0,)), ((), ()))


def _dot(a, b, dims):
    return lax.dot_general(a, b, dims, preferred_element_type=F32)


def _split2(x):
    hi = x.astype(BF16)
    lo = (x - hi.astype(F32)).astype(BF16)
    return hi, lo


def _mm(a, b, mode="bf16", dims=NN):
    if mode == "bf16":
        return _dot(a.astype(BF16), b.astype(BF16), dims)
    if mode == "x3":
        ah, al = _split2(a)
        bh, bl = _split2(b)
        return _dot(ah, bh, dims) + (_dot(ah, bl, dims) + _dot(al, bh, dims))
    if mode == "xb2":
        ab = a.astype(BF16)
        b1, b2 = _split2(b)
        return _dot(ab, b1, dims) + _dot(ab, b2, dims)
    raise ValueError(mode)


def _sigmoid(x):
    return 1.0 / (1.0 + jnp.exp(-x))


def _cparams(sem, vmem_mb):
    return pltpu.CompilerParams(dimension_semantics=sem, vmem_limit_bytes=vmem_mb * 1024 * 1024)


def _adaln_kernel(c_ref, w_ref, b_ref, o_ref):
    c = c_ref[...]
    s = c * _sigmoid(c)
    o_ref[...] = _mm(s, w_ref[...], "x3") + b_ref[...]


def _adaln(cc, mod_w, mod_b):
    rows, d = cc.shape
    n = mod_w.shape[1]
    tn = 1024
    return pl.pallas_call(
        _adaln_kernel,
        grid=(n // tn,),
        in_specs=[pl.BlockSpec((rows, d), lambda j: (0, 0)),
                  pl.BlockSpec((d, tn), lambda j: (0, j)),
                  pl.BlockSpec((1, tn), lambda j: (0, j))],
        out_specs=pl.BlockSpec((rows, tn), lambda j: (0, j)),
        out_shape=jax.ShapeDtypeStruct((rows, n), F32),
        compiler_params=_cparams(("parallel",), 40),
    )(cc, mod_w, mod_b.reshape(1, n))


def _inproj_kernel(c_ref, x_ref, mod_ref, g_ref, w_ref, wu_ref, o_ref, u_ref, *, ctx_blocks):
    x = jnp.where(pl.program_id(1) < ctx_blocks, c_ref[...], x_ref[...])
    nb, tm, d = x.shape
    ms = jnp.mean(x * x, axis=-1, keepdims=True)
    xn = x * lax.rsqrt(ms + NORM_EPS) * g_ref[...]
    shift = mod_ref[:, 0:1, :]
    scale = mod_ref[:, 1:2, :]
    h = (xn * (1.0 + scale) + shift).reshape(nb * tm, d)
    hb = h.astype(BF16)
    o_ref[...] = _dot(hb, w_ref[...], NN).astype(BF16).reshape(o_ref.shape)
    u_ref[...] = _dot(hb, wu_ref[...], NN).reshape(u_ref.shape)


def _inproj(ctx, x, mod8, g1, w_z, w_u, tm):
    bsz, seq, d = x.shape
    nb = INPROJ_NB
    ctx_blocks = ctx.shape[1] // tm
    assert bsz % nb == 0 and w_z.shape[1] == D_Z16
    du = w_u.shape[1]
    ltot = ctx.shape[1] + seq
    kern = functools.partial(_inproj_kernel, ctx_blocks=ctx_blocks)
    return pl.pallas_call(
        kern,
        grid=(bsz // nb, ctx_blocks + seq // tm),
        in_specs=[pl.BlockSpec((nb, tm, d), lambda b, j: (b, jnp.minimum(j, ctx_blocks - 1), 0)),
                  pl.BlockSpec((nb, tm, d), lambda b, j: (b, jnp.maximum(j - ctx_blocks, 0), 0)),
                  pl.BlockSpec((nb, 8, d), lambda b, j: (jnp.where(j < ctx_blocks, bsz // nb, b), 0, 0)),
                  pl.BlockSpec((1, d), lambda b, j: (0, 0)),
                  pl.BlockSpec((d, D_Z16), lambda b, j: (0, 0)),
                  pl.BlockSpec((d, du), lambda b, j: (0, 0))],
        out_specs=[pl.BlockSpec((nb, tm, D_Z16), lambda b, j: (b, j, 0)),
                   pl.BlockSpec((nb, tm, du), lambda b, j: (b, j, 0))],
        out_shape=[jax.ShapeDtypeStruct((bsz, ltot, D_Z16), BF16),
                   jax.ShapeDtypeStruct((bsz, ltot, du), F32)],
        compiler_params=_cparams(("parallel", "parallel"), 48),
    )(ctx, x, mod8, g1.reshape(1, d), w_z, w_u)


def _conv_kernel(z_ref, w_ref, o_ref, xs_ref, *, ctx_len, lat_len):
    tc = z_ref.shape[2]
    pc = CONV_PIECE
    ctx_off = CONV_PAD
    lat_off = 2 * CONV_PAD + ctx_len
    zpad = jnp.zeros((CONV_PAD, tc), F32)
    for s in range(3):
        xs_ref[s, 0:CONV_PAD, :] = zpad
        xs_ref[s, ctx_off + ctx_len:lat_off, :] = zpad
        xs_ref[s, lat_off + lat_len:lat_off + lat_len + CONV_PAD, :] = zpad
    xs_ref[1, lat_off:lat_off + 8, :] = zpad[0:8]
    xs_ref[2, lat_off + lat_len - 8:lat_off + lat_len, :] = zpad[0:8]
    col = lax.broadcasted_iota(jnp.int32, (pc, tc), 0) % GRID_W
    keep_l = (col != GRID_W - 1).astype(F32)
    keep_r = (col != 0).astype(F32)
    for p in range(ctx_len // pc):
        xs_ref[0, ctx_off + p * pc:ctx_off + (p + 1) * pc, :] = z_ref[0, p * pc:(p + 1) * pc, :].astype(F32)
    for p in range(lat_len // pc):
        x = z_ref[0, ctx_len + p * pc:ctx_len + (p + 1) * pc, :].astype(F32)
        dst = lat_off + p * pc
        xs_ref[0, dst:dst + pc, :] = x
        xs_ref[1, dst + 1:dst + 1 + pc, :] = x * keep_l
        xs_ref[2, dst - 1:dst - 1 + pc, :] = x * keep_r
    w = [w_ref[t:t + 1, :] for t in range(9)]
    for p in range(ctx_len // pc):
        base = ctx_off + p * pc
        acc = w[3] * xs_ref[0, base - 1:base - 1 + pc, :]
        acc = acc + w[4] * xs_ref[0, base:base + pc, :]
        acc = acc + w[5] * xs_ref[0, base + 1:base + 1 + pc, :]
        o_ref[0, p * pc:(p + 1) * pc, :] = acc
    for p in range(lat_len // pc):
        base = lat_off + p * pc
        acc = None
        for di in range(3):
            for dj in range(3):
                off = base + (di - 1) * GRID_W
                src = (1, 0, 2)[dj]
                term = w[3 * di + dj] * xs_ref[src, off:off + pc, :]
                acc = term if acc is None else acc + term
        o_ref[0, ctx_len + p * pc:ctx_len + (p + 1) * pc, :] = acc


def _conv(z, conv_w, ctx_len):
    bsz, ltot, _ = z.shape
    lat_len = ltot - ctx_len
    ch = 3 * D_RWKV
    rows = 3 * CONV_PAD + ltot
    kern = functools.partial(_conv_kernel, ctx_len=ctx_len, lat_len=lat_len)
    return pl.pallas_call(
        kern,
        grid=(bsz, ch // CONV_TC),
        in_specs=[pl.BlockSpec((1, ltot, CONV_TC), lambda b, j: (b, 0, j)),
                  pl.BlockSpec((9, CONV_TC), lambda b, j: (0, j))],
        out_specs=pl.BlockSpec((1, ltot, CONV_TC), lambda b, j: (b, 0, j)),
        out_shape=jax.ShapeDtypeStruct((bsz, ltot, ch), F32),
        scratch_shapes=[pltpu.VMEM((3, rows, CONV_TC), F32)],
        compiler_params=_cparams(("parallel", "parallel"), 48),
    )(z, conv_w.reshape(9, ch))


def _bd(x, lo):
    return jnp.concatenate([jnp.where(lo, x, 0.0), jnp.where(lo, 0.0, x)], axis=0)


def _rwkv_masks(reverse):
    n, c = RWKV_BLK, CHUNK
    ti, tj = np.indices((n, n))
    causal = (tj >= ti) if reverse else (tj <= ti)
    tri = ((ti // c == tj // c) & causal).astype(np.float32)
    ii, jj = np.indices((c, 2 * c))
    jj = jj % c
    tabs = [(jj > ii) if reverse else (jj < ii), (jj >= ii) if reverse else (jj <= ii), ii == jj,
            ii // 2 == jj // 2]
    m = 2
    while m < c:
        tabs.append((ii // (2 * m) == jj // (2 * m)) & (ii // m != jj // m))
        m *= 2
    return jnp.asarray(tri, BF16), jnp.asarray(np.stack(tabs).astype(np.float32))


def _tri_inv_pairs(n_mats, masks, lo):
    ts = [masks[0] + n * masks[1] for n in n_mats]
    for lvl in masks[2:]:
        nls = [n * lvl for n in n_mats]
        tmp = [_mm(t, _bd(nl, lo)) for t, nl in zip(ts, nls)]
        ts = [t + _mm(x, _bd(t, lo)) for x, t in zip(tmp, ts)]
    return ts


def _rwkv_kernel(r_ref, k_ref, v_ref, lo_ref, w0_ref, w2_ref, a0_ref, a2_ref, kkw_ref, kaw_ref,
                 tri_ref, cm_ref, y_ref, h_ref, *, reverse):
    step = pl.program_id(1)

    @pl.when(step == 0)
    def _():
        h_ref[...] = jnp.zeros_like(h_ref)

    nb, n, _ = r_ref.shape
    c = CHUNK
    nsub = n // c
    pw = 2 * HEAD
    npair = HEADS // 2

    def rows_of(bb):
        r = r_ref[bb]
        k = k_ref[bb]
        v = v_ref[bb]
        lora = lo_ref[bb].astype(F32)
        logit_w = w0_ref[0] + _mm(jnp.tanh(lora), w2_ref[0])
        lw = -math.exp(-0.5) * _sigmoid(logit_w)
        a = _sigmoid(a0_ref[0] + _mm(lora, a2_ref[0]))
        kd = k * ((1.0 - kaw_ref[...]) + a * kaw_ref[...])
        kk = k * kkw_ref[...]
        g = _mm(tri_ref[...], lw, "xb2")
        return r, v, lw, a, kd, kk, g

    lo = lax.broadcasted_iota(jnp.int32, (c, pw), 1) < HEAD
    strict2 = cm_ref[0]
    incl2 = cm_ref[1]
    inv_masks = [cm_ref[i] for i in range(2, cm_ref.shape[0])]
    ri = lax.broadcasted_iota(jnp.int32, (pw, pw), 0)
    rj = lax.broadcasted_iota(jnp.int32, (pw, pw), 1)
    blk = (ri < HEAD) == (rj < HEAD)
    eye_p = (ri == rj).astype(F32)
    zeros_c = jnp.zeros((c, pw), F32)
    zeros_p = jnp.zeros((pw, pw), F32)

    def prep(tokw, j):
        r, v, lw, a, kd, kk, g = tokw
        rows = slice(j * c, (j + 1) * c)
        lw_j = lw[rows]
        g_j = g[rows]
        tot = g_j[0:1] if reverse else g_j[c - 1:c]
        e_prev = jnp.exp(g_j - lw_j)
        e_neg = jnp.exp(-g_j)
        e_pos = jnp.exp(g_j)
        e_tot = jnp.exp(tot)
        out = []
        for p in range(npair):
            sl = slice(p * pw, (p + 1) * pw)
            kkp = kk[rows, sl]
            sq = kkp * kkp
            n2 = jnp.where(lo, jnp.sum(jnp.where(lo, sq, 0.0), axis=-1, keepdims=True),
                           jnp.sum(jnp.where(lo, 0.0, sq), axis=-1, keepdims=True))
            kkp = kkp * lax.rsqrt(jnp.maximum(n2, 1e-12))
            b_t = kkp * a[rows, sl] * e_neg[:, sl]
            k_t = kd[rows, sl] * e_neg[:, sl]
            out.append(dict(a_t=-kkp * e_prev[:, sl], b_t=b_t, k_t=k_t, r_t=r[rows, sl] * e_pos[:, sl],
                            b_e=b_t * e_tot[:, sl], k_e=k_t * e_tot[:, sl], vh=v[rows, sl],
                            g_end=e_tot[:, sl]))
        return out

    def local(us):
        s_bk = [_mm(jnp.concatenate([u["a_t"], u["r_t"]], axis=0),
                    jnp.concatenate([_bd(u["b_t"], lo), _bd(u["k_t"], lo)], axis=0), "bf16", NT)
                for u in us]
        nab = [s[:c, :pw] * strict2 for s in s_bk]
        aak = [s[:c, pw:] * strict2 for s in s_bk]
        arbk = [jnp.concatenate([s[c:, :pw] * incl2, s[c:, pw:] * incl2], axis=1) for s in s_bk]
        t_inv = _tri_inv_pairs(nab, inv_masks, lo)
        x1 = [_mm(m, _bd(u["vh"], lo)) for m, u in zip(aak, us)]
        uw = [_mm(t, jnp.concatenate([_bd(x, lo), _bd(u["a_t"], lo)], axis=1))
              for t, x, u in zip(t_inv, x1, us)]
        yq = [_mm(m, jnp.concatenate(
            [jnp.concatenate([_bd(w_[:, :pw], lo), _bd(w_[:, pw:], lo)], axis=1),
             jnp.concatenate([_bd(u["vh"], lo), zeros_p], axis=1)], axis=0))
            for m, w_, u in zip(arbk, uw, us)]
        gp = [_mm(jnp.concatenate([u["b_e"], u["k_e"]], axis=0),
                  jnp.concatenate([w_, jnp.concatenate([u["vh"], zeros_c], axis=1)], axis=0), "bf16", TN)
              for w_, u in zip(uw, us)]
        res = []
        for u, yq_, gp_ in zip(us, yq, gp):
            q = u["r_t"] + yq_[:, pw:]
            p_bd = jnp.where(blk, gp_[:, pw:], 0.0) + eye_p * u["g_end"]
            res.append((jnp.concatenate([q, p_bd], axis=0), yq_[:, :pw], jnp.where(blk, gp_[:, :pw], 0.0)))
        return res

    order = list(range(nsub - 1, -1, -1) if reverse else range(nsub))
    tokws = [rows_of(bb) for bb in range(nb)]
    loc = local([u for bb in range(nb) for j in order for u in prep(tokws[bb], j)])
    hs = [h_ref[i] for i in range(nb * npair)]
    for gi, j in enumerate(order):
        idx = [(bb * nsub + gi) * npair + p for bb in range(nb) for p in range(npair)]
        res = [_mm(loc[u][0], h) for u, h in zip(idx, hs)]
        for i, u in enumerate(idx):
            bb, p = divmod(i, npair)
            y_ref[bb, j * c:(j + 1) * c, p * pw:(p + 1) * pw] = res[i][:c] + loc[u][1]
            hs[i] = res[i][c:] + loc[u][2]
    for i in range(nb * npair):
        h_ref[i] = hs[i]


def _rwkv(rkvc, z, w0, w2p, a0, a2p, kkw, kaw, ctx_len, reverse):
    bsz, ltot, _ = rkvc.shape
    assert bsz % RWKV_NB == 0 and ltot % RWKV_BLK == 0 and ctx_len % RWKV_BLK == 0
    nblk = ltot // RWKV_BLK
    nctx = ctx_len // RWKV_BLK
    lora_blk = COL_LORA // LORA_W
    d = 1 if reverse else 0

    def tok(i):
        return jnp.where(i < nctx, nctx - 1 - i, nblk + nctx - 1 - i) if reverse else i

    par3 = lambda b, i: (d, 0, 0)
    tri, cmask = _rwkv_masks(reverse)
    kern = functools.partial(_rwkv_kernel, reverse=reverse)
    return pl.pallas_call(
        kern,
        grid=(bsz // RWKV_NB, nblk),
        in_specs=[pl.BlockSpec((RWKV_NB, RWKV_BLK, D_RWKV), lambda b, i: (b, tok(i), 0)),
                  pl.BlockSpec((RWKV_NB, RWKV_BLK, D_RWKV), lambda b, i: (b, tok(i), 1)),
                  pl.BlockSpec((RWKV_NB, RWKV_BLK, D_RWKV), lambda b, i: (b, tok(i), 2)),
                  pl.BlockSpec((RWKV_NB, RWKV_BLK, LORA_W), lambda b, i: (b, tok(i), lora_blk)),
                  pl.BlockSpec((1, 1, D_RWKV), par3),
                  pl.BlockSpec((1, LORA_W, D_RWKV), par3),
                  pl.BlockSpec((1, 1, D_RWKV), par3),
                  pl.BlockSpec((1, LORA_W, D_RWKV), par3),
                  pl.BlockSpec((1, D_RWKV), lambda b, i: (0, 0)),
                  pl.BlockSpec((1, D_RWKV), lambda b, i: (0, 0)),
                  pl.BlockSpec(tri.shape, lambda b, i: (0, 0)),
                  pl.BlockSpec(cmask.shape, lambda b, i: (0, 0, 0))],
        out_specs=pl.BlockSpec((RWKV_NB, RWKV_BLK, D_RWKV), lambda b, i: (b, tok(i), 0)),
        out_shape=jax.ShapeDtypeStruct((bsz, ltot, D_RWKV), F32),
        scratch_shapes=[pltpu.VMEM((RWKV_NB * HEADS // 2, 2 * HEAD, 2 * HEAD), F32)],
        compiler_params=_cparams(("parallel", "arbitrary"), 48),
    )(rkvc, rkvc, rkvc, z, w0, w2p, a0, a2p, kkw, kaw, tri, cmask)


def _s5_tables_kernel(lam_re_ref, lam_im_ref, dt_ref, bt_re_ref, bt_im_ref, c_re_ref, c_im_ref,
                      inj_ref, mt_ref, qt_ref, lre_ref, lim_ref):
    t = S5_T
    hg = S5_GROUP
    ps = S5_STATE
    row = lax.broadcasted_iota(jnp.int32, (t, 2 * ps), 0).astype(F32)
    is_f = lax.broadcasted_iota(jnp.int32, (t, 2 * ps), 1) < ps
    is_f1 = is_f[0:1]
    n_inj = jnp.where(is_f, t - 1.0 - row, row)
    n_out = jnp.where(is_f, row + 1.0, t - row)
    n_lag = jnp.where(is_f, row, t - 1.0 - row)
    for gl in range(lam_re_ref.shape[0]):
        lam_re = lam_re_ref[gl:gl + 1, :]
        lam_im = lam_im_ref[gl:gl + 1, :]
        dt = dt_ref[gl:gl + 1, :]

        def power(n):
            mag = jnp.exp(lam_re * dt * n)
            return mag * jnp.cos(lam_im * dt * n), mag * jnp.sin(lam_im * dt * n)

        lb_re, lb_im = power(1.0)
        den = lam_re * lam_re + lam_im * lam_im
        q_re = ((lb_re - 1.0) * lam_re + lb_im * lam_im) / den
        q_im = (lb_im * lam_re - (lb_re - 1.0) * lam_im) / den
        bt_re = jnp.concatenate([bt_re_ref[gl], bt_re_ref[gl]], axis=1)
        bt_im = jnp.concatenate([bt_im_ref[gl], bt_im_ref[gl]], axis=1)
        bb_re = q_re * bt_re - q_im * bt_im
        bb_im = q_re * bt_im + q_im * bt_re
        c_re = jnp.concatenate([c_re_ref[gl], c_re_ref[gl]], axis=1)
        c_im = jnp.concatenate([c_im_ref[gl], c_im_ref[gl]], axis=1)

        pi_re, pi_im = power(n_inj)
        po_re, po_im = power(n_out)
        pl_re, pl_im = power(n_lag)
        ck_re, ck_im = [], []
        for s in range(t):
            wr, wi = pi_re[s:s + 1], pi_im[s:s + 1]
            rows = slice(s * hg, (s + 1) * hg)
            inj_ref[gl, rows, 0:2 * ps] = (wr * bb_re - wi * bb_im).astype(BF16)
            inj_ref[gl, rows, 2 * ps:4 * ps] = (wr * bb_im + wi * bb_re).astype(BF16)
            wr, wi = po_re[s:s + 1], po_im[s:s + 1]
            qt_ref[gl, rows, 0:2 * ps] = (wr * c_re - wi * c_im).astype(BF16)
            qt_ref[gl, rows, 2 * ps:4 * ps] = (-(wr * c_im + wi * c_re)).astype(BF16)
            wr, wi = pl_re[s:s + 1], pl_im[s:s + 1]
            ck_re.append(wr * c_re - wi * c_im)
            ck_im.append(wr * c_im + wi * c_re)
        ck = jnp.concatenate([jnp.concatenate(ck_re, axis=0), jnp.concatenate(ck_im, axis=0)], axis=1)
        bsel = jnp.concatenate(
            [jnp.concatenate([jnp.where(is_f1, bb_re, 0.0), jnp.where(is_f1, -bb_im, 0.0)], axis=1),
             jnp.concatenate([jnp.where(is_f1, 0.0, bb_re), jnp.where(is_f1, 0.0, -bb_im)], axis=1)], axis=0)
        kt = _mm(bsel, ck, "x3", NT)
        w = t * hg
        zero = jnp.zeros((hg, w), F32)
        lag = (pltpu.roll(jnp.concatenate([kt[0:hg], zero], axis=1), w - hg, 1)
               + jnp.concatenate([kt[hg:2 * hg], zero], axis=1))
        for s in range(t):
            sh = (t - 1 - s) * hg
            blk = lag if sh == 0 else pltpu.roll(lag, 2 * w - sh, 1)
            mt_ref[gl, s * hg:(s + 1) * hg, :] = blk[:, 0:w].astype(BF16)
        l16_re, l16_im = power(float(t))
        lre_ref[gl:gl + 1, :] = l16_re
        lim_ref[gl:gl + 1, :] = l16_im


def _s5_tables(lam_re, lam_im, log_step, b_re, b_im, c_re, c_im):
    g, p, hg = b_re.shape
    w = S5_T * hg
    lanes = lambda x: jnp.concatenate([x[0], x[1]], axis=-1)
    dt = lanes(jnp.broadcast_to(jnp.exp(log_step)[..., None], lam_re.shape))
    bt_re = jnp.transpose(b_re, (0, 2, 1))
    bt_im = jnp.transpose(b_im, (0, 2, 1))
    vec = pl.BlockSpec((S5_TG, 2 * p), lambda j: (j, 0))
    mat = pl.BlockSpec((S5_TG, hg, p), lambda j: (j, 0, 0))
    tab = pl.BlockSpec((S5_TG, w, w), lambda j: (j, 0, 0))
    return pl.pallas_call(
        _s5_tables_kernel,
        grid=(g // S5_TG,),
        in_specs=[vec, vec, vec, mat, mat, mat, mat],
        out_specs=[tab, tab, tab, vec, vec],
        out_shape=[jax.ShapeDtypeStruct((g, w, w), BF16)] * 3 + [jax.ShapeDtypeStruct((g, 2 * p), F32)] * 2,
        compiler_params=_cparams(("parallel",), 32),
    )(lanes(lam_re), lanes(lam_im), dt, bt_re, bt_im, c_re, c_im)


def _block_transpose8(xs):
    lane_blk = lax.broadcasted_iota(jnp.int32, xs[0].shape, 1) >> 4
    xs = list(xs)
    for d in (4, 2, 1):
        hi = (lane_blk & d) != 0
        for i in range(8):
            if i & d:
                continue
            a, b = xs[i], xs[i + d]
            xs[i] = jnp.where(hi, pltpu.roll(b, 16 * d, 1), a)
            xs[i + d] = jnp.where(hi, b, pltpu.roll(a, 128 - 16 * d, 1))
    return xs


def _s5_kernel(u0_ref, u1_ref, inj_ref, mt_ref, qt_ref, lre_ref, lim_ref, y_ref, ug_ref, zre_ref, zim_ref,
               hre_ref, him_ref, *, nchunk, nctx):
    gh = ug_ref.shape[0]
    ps = S5_STATE
    t = S5_T
    nlat = nchunk - nctx
    for jt, u_ref in enumerate((u0_ref, u1_ref)):
        for q in range(t // 8):
            cols = [u_ref[0, pl.ds(8 * q + tt, nchunk, stride=t), :] for tt in range(8)]
            grp = _block_transpose8(cols)
            for gl in range(8):
                ug_ref[8 * jt + gl, :, 128 * q:128 * (q + 1)] = grp[gl].astype(BF16)
    for gi in range(gh):
        zg = _dot(ug_ref[gi], inj_ref[gi], NN)
        zre_ref[pl.ds(gi, nchunk, stride=gh), :] = zg[:, 0:2 * ps]
        zim_ref[pl.ds(gi, nchunk, stride=gh), :] = zg[:, 2 * ps:4 * ps]

    lane = lax.broadcasted_iota(jnp.int32, (gh, 2 * ps), 1)
    is_f = lane < ps
    lre = lre_ref[...]
    lim = lim_ref[...]

    def body(i, carry):
        hre, him = carry
        cb = jnp.where(i < nctx, nctx - 1 - i, nchunk + nctx - 1 - i)
        rf = pl.multiple_of(i * gh, gh)
        rb = pl.multiple_of(cb * gh, gh)
        hre_ref[pl.ds(rf, gh), 0:ps] = hre[:, 0:ps]
        hre_ref[pl.ds(rb, gh), ps:2 * ps] = hre[:, ps:2 * ps]
        him_ref[pl.ds(rf, gh), 0:ps] = him[:, 0:ps]
        him_ref[pl.ds(rb, gh), ps:2 * ps] = him[:, ps:2 * ps]
        zre = jnp.where(is_f, zre_ref[pl.ds(rf, gh), :], zre_ref[pl.ds(rb, gh), :])
        zim = jnp.where(is_f, zim_ref[pl.ds(rf, gh), :], zim_ref[pl.ds(rb, gh), :])
        return lre * hre - lim * him + zre, lre * him + lim * hre + zim

    zero = jnp.zeros((gh, 2 * ps), F32)
    lax.fori_loop(0, nchunk, body, (zero, zero))

    for jt in range(gh // 8):
        ys = []
        for gl in range(8):
            gi = 8 * jt + gl
            hre = hre_ref[pl.ds(gi, nchunk, stride=gh), :].astype(BF16)
            him = him_ref[pl.ds(gi, nchunk, stride=gh), :].astype(BF16)
            ys.append(_dot(ug_ref[gi], mt_ref[gi], NN)
                      + _dot(jnp.concatenate([hre, him], axis=1), qt_ref[gi], NT))
        for q in range(t // 8):
            tok = _block_transpose8([y[:, 128 * q:128 * (q + 1)] for y in ys])
            for tt in range(8):
                y_ref[0, jt, pl.ds(8 * q + tt, nlat, stride=t), :] = tok[tt][nctx:]


def _s5(u, tables, ctx_len):
    inj, mt, qt, lre, lim = tables
    bsz, ltot, _ = u.shape
    nchunk = ltot // S5_T
    nctx = ctx_len // S5_T
    gh = S5_GH
    halves = S5_GROUPS // gh
    w = S5_T * S5_GROUP
    tiles = gh * S5_GROUP // 128
    ps2 = 2 * S5_STATE
    kern = functools.partial(_s5_kernel, nchunk=nchunk, nctx=nctx)
    u_spec = lambda k: pl.BlockSpec((1, ltot, 128), lambda b, j: (b, 0, tiles * j + k))
    tab = pl.BlockSpec((gh, w, w), lambda b, j: (j, 0, 0))
    vec = pl.BlockSpec((gh, ps2), lambda b, j: (j, 0))
    return pl.pallas_call(
        kern,
        grid=(bsz, halves),
        in_specs=[u_spec(0), u_spec(1), tab, tab, tab, vec, vec],
        out_specs=pl.BlockSpec((1, tiles, ltot - ctx_len, 128), lambda b, j: (b, j, 0, 0)),
        out_shape=jax.ShapeDtypeStruct((bsz, D_S5 // 128, ltot - ctx_len, 128), F32),
        scratch_shapes=[pltpu.VMEM((gh, nchunk, w), BF16),
                        pltpu.VMEM((nchunk * gh, ps2), F32),
                        pltpu.VMEM((nchunk * gh, ps2), F32),
                        pltpu.VMEM((nchunk * gh, ps2), F32),
                        pltpu.VMEM((nchunk * gh, ps2), F32)],
        compiler_params=_cparams(("parallel", "parallel"), 48),
    )(u, u, inj, mt, qt, lre, lim)


def _mix_kernel(x_ref, yf_ref, yb_ref, r_ref, k_ref, v_ref, lo_ref, u_ref, ys_ref, mod_ref,
                seg_ref, a0_ref, a2_ref, g2_ref, kaw_ref, rk_ref, lnw_ref, lnb_ref,
                dsk_ref, gluw_ref, glub_ref, wo_ref, g2n_ref, w1_ref, w3_ref, w2_ref, gf_ref, o_ref):
    lo = lo_ref[0].astype(F32)
    seg = seg_ref[...]

    def head_sum(t):
        pw = seg.shape[0]
        return jnp.concatenate([_mm(t[:, i:i + pw], seg) for i in range(0, t.shape[1], pw)], axis=1)

    y = yf_ref[0] + yb_ref[0]
    mu = head_sum(y) * (1.0 / HEAD)
    yc = y - mu
    var = head_sum(yc * yc) * (1.0 / HEAD)
    yn = yc * lax.rsqrt(var + RWKV_LN_EPS) * lnw_ref[...] + lnb_ref[...]
    a_f = _sigmoid(a0_ref[0] + _mm(lo, a2_ref[0]))
    a_b = _sigmoid(a0_ref[1] + _mm(lo, a2_ref[1]))
    kaw = kaw_ref[...]
    kd_sum = k_ref[0] * ((1.0 + (a_f - 1.0) * kaw) + (1.0 + (a_b - 1.0) * kaw))
    bonus = head_sum(r_ref[0] * kd_sum * rk_ref[...])
    yn = yn + bonus * v_ref[0]
    gate = _mm(_sigmoid(lo), g2_ref[...])
    rwkv_out = yn * gate

    u = u_ref[0]
    ys = jnp.concatenate([ys_ref[0, i] for i in range(ys_ref.shape[1])], axis=1) + dsk_ref[...] * u
    zg = 0.5 * ys * (1.0 + jnp.tanh(math.sqrt(2.0 / math.pi) * (ys + 0.044715 * (ys * ys * ys))))
    s5_out = zg * _sigmoid(_mm(zg, gluw_ref[...]) + glub_ref[...])

    mix = _mm(rwkv_out, wo_ref[0:D_RWKV, :]) + _mm(s5_out, wo_ref[D_RWKV:, :])
    x1 = x_ref[0] + mod_ref[0, 2:3, :] * mix

    ms = jnp.mean(x1 * x1, axis=-1, keepdims=True)
    h = x1 * lax.rsqrt(ms + NORM_EPS) * g2n_ref[...]
    h = h * (1.0 + mod_ref[0, 4:5, :]) + mod_ref[0, 3:4, :]
    hb = h.astype(BF16)
    h1 = _dot(hb, w1_ref[...], NN)
    h3 = _dot(hb, w3_ref[...], NN)
    act = (h1 * _sigmoid(h1)) * h3
    out = x1 + mod_ref[0, 5:6, :] * _dot(act.astype(BF16), w2_ref[...], NN)
    ms2 = jnp.mean(out * out, axis=-1, keepdims=True)
    o_ref[0] = out * lax.rsqrt(ms2 + NORM_EPS) * gf_ref[...]


def _mix(x, y_f, y_b, rkvc, z, u, ys5, mod8, seg, a0, a2p, g2p, kaw, rk, lnw, lnb, dsk, gluw, glub, wo,
         g2n, w1, w3, w2, gf, ctx_len, tm):
    bsz, seq, d = x.shape
    assert ctx_len % tm == 0 and seq % tm == 0
    off = ctx_len // tm
    lora_blk = COL_LORA // LORA_W
    tok = lambda c: pl.BlockSpec((1, tm, D_RWKV), lambda b, j: (b, j + off, c))
    full = lambda a: pl.BlockSpec(a.shape, lambda b, j: (0,) * a.ndim, pipeline_mode=pl.Buffered(1))
    return pl.pallas_call(
        _mix_kernel,
        grid=(bsz, seq // tm),
        in_specs=[pl.BlockSpec((1, tm, d), lambda b, j: (b, j, 0)),
                  tok(0), tok(0), tok(0), tok(1), tok(2),
                  pl.BlockSpec((1, tm, LORA_W), lambda b, j: (b, j + off, lora_blk)),
                  pl.BlockSpec((1, tm, D_S5), lambda b, j: (b, j + off, 0)),
                  pl.BlockSpec((1, D_S5 // 128, tm, 128), lambda b, j: (b, 0, j, 0)),
                  pl.BlockSpec((1, 8, d), lambda b, j: (b, 0, 0)),
                  full(seg), full(a0), full(a2p), full(g2p), full(kaw), full(rk), full(lnw), full(lnb),
                  full(dsk), full(gluw), full(glub), full(wo),
                  full(g2n), full(w1), full(w3), full(w2), full(gf)],
        out_specs=pl.BlockSpec((1, tm, d), lambda b, j: (b, j, 0)),
        out_shape=jax.ShapeDtypeStruct((bsz, seq, d), F32),
        compiler_params=_cparams(("parallel", "parallel"), 56),
    )(x, y_f, y_b, rkvc, rkvc, rkvc, z, u, ys5, mod8, seg, a0, a2p, g2p, kaw, rk, lnw, lnb,
      dsk, gluw, glub, wo, g2n, w1, w3, w2, gf)


def _pad_rows(w, row0, rows_total):
    pad = [(0, 0)] * (w.ndim - 2) + [(row0, rows_total - row0 - w.shape[-2]), (0, 0)]
    return jnp.pad(w, pad)


def kernel(x, c, ctx, c_ctx, mod_w, mod_b, norm1_g, norm2_g, w_in, w_out, rwkv_conv, rwkv_w0, rwkv_w2, rwkv_a0, rwkv_a2, rwkv_g2, rwkv_kk, rwkv_ka, rwkv_rk, rwkv_ln_w, rwkv_ln_b, s5_lam_re, s5_lam_im, s5_log_step, s5_b_re, s5_b_im, s5_c_re, s5_c_im, s5_d, s5_glu_w, s5_glu_b, ffn_w1, ffn_w3, ffn_w2, final_g):
    bsz, seq, d = x.shape
    ctx_len = ctx.shape[1]
    ltot = ctx_len + seq
    layer = 0

    nmod = bsz + INPROJ_NB
    cc = jnp.concatenate([c, jnp.broadcast_to(c_ctx[None, :], (INPROJ_NB, d)),
                          jnp.zeros((8 - nmod, d), F32)], axis=0)
    mod = _adaln(cc, mod_w[layer], mod_b[layer])
    mod8 = jnp.pad(mod[:nmod].reshape(nmod, N_MOD, d), ((0, 0), (0, 8 - N_MOD), (0, 0)))

    wi = w_in[layer]
    z, u = _inproj(ctx, x, mod8, norm1_g[layer], wi[:, :D_Z16].astype(BF16), wi[:, -D_S5:].astype(BF16), 256)

    rkvc = _conv(z, rwkv_conv[layer], ctx_len)

    w2p = _pad_rows(rwkv_w2[layer], 0, LORA_W).at[1].set(_pad_rows(rwkv_w2[layer, 1], DECAY_LORA, LORA_W))
    a2p = jnp.stack([_pad_rows(rwkv_a2[layer, 0], LORA_AD, LORA_W),
                     _pad_rows(rwkv_a2[layer, 1], LORA_AD + AAA_LORA, LORA_W)])
    w0 = rwkv_w0[layer].reshape(2, 1, D_RWKV)
    a0 = rwkv_a0[layer].reshape(2, 1, D_RWKV)
    kkw = rwkv_kk[layer].reshape(1, D_RWKV)
    kaw = rwkv_ka[layer].reshape(1, D_RWKV)
    y_f = _rwkv(rkvc, z, w0, w2p, a0, a2p, kkw, kaw, ctx_len, False)
    y_b = _rwkv(rkvc, z, w0, w2p, a0, a2p, kkw, kaw, ctx_len, True)

    tables = _s5_tables(s5_lam_re[layer], s5_lam_im[layer], s5_log_step[layer], s5_b_re[layer],
                        s5_b_im[layer], s5_c_re[layer], s5_c_im[layer])
    ys5 = _s5(u, tables, ctx_len)

    hh = jnp.arange(2 * HEAD) // HEAD
    seg = (hh[:, None] == hh[None, :]).astype(BF16)
    g2p = _pad_rows(rwkv_g2[layer], LORA_GD, LORA_W)
    return _mix(x, y_f, y_b, rkvc, z, u, ys5, mod8, seg, a0, a2p, g2p, kaw,
                rwkv_rk[layer].reshape(1, D_RWKV), rwkv_ln_w[layer].reshape(1, D_RWKV),
                rwkv_ln_b[layer].reshape(1, D_RWKV), s5_d[layer].reshape(1, D_S5),
                s5_glu_w[layer].astype(BF16), s5_glu_b[layer].reshape(1, D_S5),
                w_out[layer].astype(BF16), norm2_g[layer].reshape(1, d), ffn_w1[layer].astype(BF16),
                ffn_w3[layer].astype(BF16), ffn_w2[layer].astype(BF16), final_g.reshape(1, d), ctx_len, 256)
```

```python
import functools
import math

import jax
import jax.numpy as jnp
import numpy as np
from jax import lax
from jax.experimental import pallas as pl
from jax.experimental.pallas import tpu as pltpu

F32 = jnp.float32
BF16 = jnp.bfloat16

D_MODEL = 1024
GRID_W = 64
D_RWKV = 512
HEAD = 64
HEADS = D_RWKV // HEAD
D_S5 = 512
S5_GROUP = 16
S5_GROUPS = D_S5 // S5_GROUP
S5_STATE = 64
DECAY_LORA = 32
AAA_LORA = 32
GATE_LORA = 96
N_MOD = 6
NORM_EPS = 1e-6
RWKV_LN_EPS = 64e-5

LORA_W = 256
COL_LORA = 3 * D_RWKV
D_Z16 = COL_LORA + LORA_W
LORA_AD = 2 * DECAY_LORA
LORA_GD = LORA_AD + 2 * AAA_LORA

CHUNK = 64
RWKV_BLK = 256
RWKV_NB = 4
INPROJ_NB = 4
S5_T = 16
S5_GH = 16
S5_TG = 8
CONV_PAD = 72
CONV_TC = 256
CONV_PIECE = 128

NN = (((1,), (0,)), ((), ()))
NT = (((1,), (1,)), ((), ()))
TN = (((0,), (0,)), ((), ()))


def _dot(a, b, dims):
    return lax.dot_general(a, b, dims, preferred_element_type=F32)


def _split2(x):
    hi = x.astype(BF16)
    lo = (x - hi.astype(F32)).astype(BF16)
    return hi, lo


def _mm(a, b, mode="bf16", dims=NN):
    if mode == "bf16":
        return _dot(a.astype(BF16), b.astype(BF16), dims)
    if mode == "x3":
        ah, al = _split2(a)
        bh, bl = _split2(b)
        return _dot(ah, bh, dims) + (_dot(ah, bl, dims) + _dot(al, bh, dims))
    if mode == "xb2":
        ab = a.astype(BF16)
        b1, b2 = _split2(b)
        return _dot(ab, b1, dims) + _dot(ab, b2, dims)
    raise ValueError(mode)


def _sigmoid(x):
    return 1.0 / (1.0 + jnp.exp(-x))


def _cparams(sem, vmem_mb):
    return pltpu.CompilerParams(dimension_semantics=sem, vmem_limit_bytes=vmem_mb * 1024 * 1024)


def _adaln_kernel(c_ref, w_ref, b_ref, o_ref):
    c = c_ref[...]
    s = c * _sigmoid(c)
    o_ref[...] = _mm(s, w_ref[...], "x3") + b_ref[...]


def _adaln(cc, mod_w, mod_b):
    rows, d = cc.shape
    n = mod_w.shape[1]
    tn = 1024
    return pl.pallas_call(
        _adaln_kernel,
        grid=(n // tn,),
        in_specs=[pl.BlockSpec((rows, d), lambda j: (0, 0)),
                  pl.BlockSpec((d, tn), lambda j: (0, j)),
                  pl.BlockSpec((1, tn), lambda j: (0, j))],
        out_specs=pl.BlockSpec((rows, tn), lambda j: (0, j)),
        out_shape=jax.ShapeDtypeStruct((rows, n), F32),
        compiler_params=_cparams(("parallel",), 40),
    )(cc, mod_w, mod_b.reshape(1, n))


def _inproj_kernel(c_ref, x_ref, mod_ref, g_ref, w_ref, wu_ref, o_ref, u_ref, *, ctx_blocks):
    x = jnp.where(pl.program_id(1) < ctx_blocks, c_ref[...], x_ref[...])
    nb, tm, d = x.shape
    ms = jnp.mean(x * x, axis=-1, keepdims=True)
    xn = x * lax.rsqrt(ms + NORM_EPS) * g_ref[...]
    shift = mod_ref[:, 0:1, :]
    scale = mod_ref[:, 1:2, :]
    h = (xn * (1.0 + scale) + shift).reshape(nb * tm, d)
    hb = h.astype(BF16)
    o_ref[...] = _dot(hb, w_ref[...], NN).astype(BF16).reshape(o_ref.shape)
    u_ref[...] = _dot(hb, wu_ref[...], NN).reshape(u_ref.shape)


def _inproj(ctx, x, mod8, g1, w_z, w_u, tm):
    bsz, seq, d = x.shape
    nb = INPROJ_NB
    ctx_blocks = ctx.shape[1] // tm
    assert bsz % nb == 0 and w_z.shape[1] == D_Z16
    du = w_u.shape[1]
    ltot = ctx.shape[1] + seq
    kern = functools.partial(_inproj_kernel, ctx_blocks=ctx_blocks)
    return pl.pallas_call(
        kern,
        grid=(bsz // nb, ctx_blocks + seq // tm),
        in_specs=[pl.BlockSpec((nb, tm, d), lambda b, j: (b, jnp.minimum(j, ctx_blocks - 1), 0)),
                  pl.BlockSpec((nb, tm, d), lambda b, j: (b, jnp.maximum(j - ctx_blocks, 0), 0)),
                  pl.BlockSpec((nb, 8, d), lambda b, j: (jnp.where(j < ctx_blocks, bsz // nb, b), 0, 0)),
                  pl.BlockSpec((1, d), lambda b, j: (0, 0)),
                  pl.BlockSpec((d, D_Z16), lambda b, j: (0, 0)),
                  pl.BlockSpec((d, du), lambda b, j: (0, 0))],
        out_specs=[pl.BlockSpec((nb, tm, D_Z16), lambda b, j: (b, j, 0)),
                   pl.BlockSpec((nb, tm, du), lambda b, j: (b, j, 0))],
        out_shape=[jax.ShapeDtypeStruct((bsz, ltot, D_Z16), BF16),
                   jax.ShapeDtypeStruct((bsz, ltot, du), F32)],
        compiler_params=_cparams(("parallel", "parallel"), 48),
    )(ctx, x, mod8, g1.reshape(1, d), w_z, w_u)


def _conv_kernel(z_ref, w_ref, o_ref, xs_ref, *, ctx_len, lat_len):
    tc = z_ref.shape[2]
    pc = CONV_PIECE
    ctx_off = CONV_PAD
    lat_off = 2 * CONV_PAD + ctx_len
    zpad = jnp.zeros((CONV_PAD, tc), F32)
    for s in range(3):
        xs_ref[s, 0:CONV_PAD, :] = zpad
        xs_ref[s, ctx_off + ctx_len:lat_off, :] = zpad
        xs_ref[s, lat_off + lat_len:lat_off + lat_len + CONV_PAD, :] = zpad
    xs_ref[1, lat_off:lat_off + 8, :] = zpad[0:8]
    xs_ref[2, lat_off + lat_len - 8:lat_off + lat_len, :] = zpad[0:8]
    col = lax.broadcasted_iota(jnp.int32, (pc, tc), 0) % GRID_W
    keep_l = (col != GRID_W - 1).astype(F32)
    keep_r = (col != 0).astype(F32)
    for p in range(ctx_len // pc):
        xs_ref[0, ctx_off + p * pc:ctx_off + (p + 1) * pc, :] = z_ref[0, p * pc:(p + 1) * pc, :].astype(F32)
    for p in range(lat_len // pc):
        x = z_ref[0, ctx_len + p * pc:ctx_len + (p + 1) * pc, :].astype(F32)
        dst = lat_off + p * pc
        xs_ref[0, dst:dst + pc, :] = x
        xs_ref[1, dst + 1:dst + 1 + pc, :] = x * keep_l
        xs_ref[2, dst - 1:dst - 1 + pc, :] = x * keep_r
    w = [w_ref[t:t + 1, :] for t in range(9)]
    for p in range(ctx_len // pc):
        base = ctx_off + p * pc
        acc = w[3] * xs_ref[0, base - 1:base - 1 + pc, :]
        acc = acc + w[4] * xs_ref[0, base:base + pc, :]
        acc = acc + w[5] * xs_ref[0, base + 1:base + 1 + pc, :]
        o_ref[0, p * pc:(p + 1) * pc, :] = acc
    for p in range(lat_len // pc):
        base = lat_off + p * pc
        acc = None
        for di in range(3):
            for dj in range(3):
                off = base + (di - 1) * GRID_W
                src = (1, 0, 2)[dj]
                term = w[3 * di + dj] * xs_ref[src, off:off + pc, :]
                acc = term if acc is None else acc + term
        o_ref[0, ctx_len + p * pc:ctx_len + (p + 1) * pc, :] = acc


def _conv(z, conv_w, ctx_len):
    bsz, ltot, _ = z.shape
    lat_len = ltot - ctx_len
    ch = 3 * D_RWKV
    rows = 3 * CONV_PAD + ltot
    kern = functools.partial(_conv_kernel, ctx_len=ctx_len, lat_len=lat_len)
    return pl.pallas_call(
        kern,
        grid=(bsz, ch // CONV_TC),
        in_specs=[pl.BlockSpec((1, ltot, CONV_TC), lambda b, j: (b, 0, j)),
                  pl.BlockSpec((9, CONV_TC), lambda b, j: (0, j))],
        out_specs=pl.BlockSpec((1, ltot, CONV_TC), lambda b, j: (b, 0, j)),
        out_shape=jax.ShapeDtypeStruct((bsz, ltot, ch), F32),
        scratch_shapes=[pltpu.VMEM((3, rows, CONV_TC), F32)],
        compiler_params=_cparams(("parallel", "parallel"), 48),
    )(z, conv_w.reshape(9, ch))


def _bd(x, lo):
    return jnp.concatenate([jnp.where(lo, x, 0.0), jnp.where(lo, 0.0, x)], axis=0)


def _rwkv_masks(reverse):
    n, c = RWKV_BLK, CHUNK
    ti, tj = np.indices((n, n))
    causal = (tj >= ti) if reverse else (tj <= ti)
    tri = ((ti // c == tj // c) & causal).astype(np.float32)
    ii, jj = np.indices((c, 2 * c))
    jj = jj % c
    tabs = [(jj > ii) if reverse else (jj < ii), (jj >= ii) if reverse else (jj <= ii), ii == jj,
            ii // 2 == jj // 2]
    m = 2
    while m < c:
        tabs.append((ii // (2 * m) == jj // (2 * m)) & (ii // m != jj // m))
        m *= 2
    return jnp.asarray(tri, BF16), jnp.asarray(np.stack(tabs).astype(np.float32))


def _tri_inv_pairs(n_mats, masks, lo):
    ts = [masks[0] + n * masks[1] for n in n_mats]
    for lvl in masks[2:]:
        nls = [n * lvl for n in n_mats]
        tmp = [_mm(t, _bd(nl, lo)) for t, nl in zip(ts, nls)]
        ts = [t + _mm(x, _bd(t, lo)) for x, t in zip(tmp, ts)]
    return ts


def _rwkv_kernel(r_ref, k_ref, v_ref, lo_ref, w0_ref, w2_ref, a0_ref, a2_ref, kkw_ref, kaw_ref,
                 tri_ref, cm_ref, y_ref, h_ref, *, reverse):
    step = pl.program_id(1)

    @pl.when(step == 0)
    def _():
        h_ref[...] = jnp.zeros_like(h_ref)

    nb, n, _ = r_ref.shape
    c = CHUNK
    nsub = n // c
    pw = 2 * HEAD
    npair = HEADS // 2

    def rows_of(bb):
        r = r_ref[bb]
        k = k_ref[bb]
        v = v_ref[bb]
        lora = lo_ref[bb].astype(F32)
        logit_w = w0_ref[0] + _mm(jnp.tanh(lora), w2_ref[0])
        lw = -math.exp(-0.5) * _sigmoid(logit_w)
        a = _sigmoid(a0_ref[0] + _mm(lora, a2_ref[0]))
        kd = k * ((1.0 - kaw_ref[...]) + a * kaw_ref[...])
        kk = k * kkw_ref[...]
        g = _mm(tri_ref[...], lw, "xb2")
        return r, v, lw, a, kd, kk, g

    lo = lax.broadcasted_iota(jnp.int32, (c, pw), 1) < HEAD
    strict2 = cm_ref[0]
    incl2 = cm_ref[1]
    inv_masks = [cm_ref[i] for i in range(2, cm_ref.shape[0])]
    ri = lax.broadcasted_iota(jnp.int32, (pw, pw), 0)
    rj = lax.broadcasted_iota(jnp.int32, (pw, pw), 1)
    blk = (ri < HEAD) == (rj < HEAD)
    eye_p = (ri == rj).astype(F32)
    zeros_c = jnp.zeros((c, pw), F32)
    zeros_p = jnp.zeros((pw, pw), F32)

    def prep(tokw, j):
        r, v, lw, a, kd, kk, g = tokw
        rows = slice(j * c, (j + 1) * c)
        lw_j = lw[rows]
        g_j = g[rows]
        tot = g_j[0:1] if reverse else g_j[c - 1:c]
        e_prev = jnp.exp(g_j - lw_j)
        e_neg = jnp.exp(-g_j)
        e_pos = jnp.exp(g_j)
        e_tot = jnp.exp(tot)
        out = []
        for p in range(npair):
            sl = slice(p * pw, (p + 1) * pw)
            kkp = kk[rows, sl]
            sq = kkp * kkp
            n2 = jnp.where(lo, jnp.sum(jnp.where(lo, sq, 0.0), axis=-1, keepdims=True),
                           jnp.sum(jnp.where(lo, 0.0, sq), axis=-1, keepdims=True))
            kkp = kkp * lax.rsqrt(jnp.maximum(n2, 1e-12))
            b_t = kkp * a[rows, sl] * e_neg[:, sl]
            k_t = kd[rows, sl] * e_neg[:, sl]
            out.append(dict(a_t=-kkp * e_prev[:, sl], b_t=b_t, k_t=k_t, r_t=r[rows, sl] * e_pos[:, sl],
                            b_e=b_t * e_tot[:, sl], k_e=k_t * e_tot[:, sl], vh=v[rows, sl],
                            g_end=e_tot[:, sl]))
        return out

    def local(us):
        s_bk = [_mm(jnp.concatenate([u["a_t"], u["r_t"]], axis=0),
                    jnp.concatenate([_bd(u["b_t"], lo), _bd(u["k_t"], lo)], axis=0), "bf16", NT)
                for u in us]
        nab = [s[:c, :pw] * strict2 for s in s_bk]
        aak = [s[:c, pw:] * strict2 for s in s_bk]
        arbk = [jnp.concatenate([s[c:, :pw] * incl2, s[c:, pw:] * incl2], axis=1) for s in s_bk]
        t_inv = _tri_inv_pairs(nab, inv_masks, lo)
        x1 = [_mm(m, _bd(u["vh"], lo)) for m, u in zip(aak, us)]
        uw = [_mm(t, jnp.concatenate([_bd(x, lo), _bd(u["a_t"], lo)], axis=1))
              for t, x, u in zip(t_inv, x1, us)]
        yq = [_mm(m, jnp.concatenate(
            [jnp.concatenate([_bd(w_[:, :pw], lo), _bd(w_[:, pw:], lo)], axis=1),
             jnp.concatenate([_bd(u["vh"], lo), zeros_p], axis=1)], axis=0))
            for m, w_, u in zip(arbk, uw, us)]
        gp = [_mm(jnp.concatenate([u["b_e"], u["k_e"]], axis=0),
                  jnp.concatenate([w_, jnp.concatenate([u["vh"], zeros_c], axis=1)], axis=0), "bf16", TN)
              for w_, u in zip(uw, us)]
        res = []
        for u, yq_, gp_ in zip(us, yq, gp):
            q = u["r_t"] + yq_[:, pw:]
            p_bd = jnp.where(blk, gp_[:, pw:], 0.0) + eye_p * u["g_end"]
            res.append((jnp.concatenate([q, p_bd], axis=0), yq_[:, :pw], jnp.where(blk, gp_[:, :pw], 0.0)))
        return res

    order = list(range(nsub - 1, -1, -1) if reverse else range(nsub))
    tokws = [rows_of(bb) for bb in range(nb)]
    loc = local([u for bb in range(nb) for j in order for u in prep(tokws[bb], j)])
    hs = [h_ref[i] for i in range(nb * npair)]
    for gi, j in enumerate(order):
        idx = [(bb * nsub + gi) * npair + p for bb in range(nb) for p in range(npair)]
        res = [_mm(loc[u][0], h) for u, h in zip(idx, hs)]
        for i, u in enumerate(idx):
            bb, p = divmod(i, npair)
            y_ref[bb, j * c:(j + 1) * c, p * pw:(p + 1) * pw] = res[i][:c] + loc[u][1]
            hs[i] = res[i][c:] + loc[u][2]
    for i in range(nb * npair):
        h_ref[i] = hs[i]


def _rwkv(rkvc, z, w0, w2p, a0, a2p, kkw, kaw, ctx_len, reverse):
    bsz, ltot, _ = rkvc.shape
    assert bsz % RWKV_NB == 0 and ltot % RWKV_BLK == 0 and ctx_len % RWKV_BLK == 0
    nblk = ltot // RWKV_BLK
    nctx = ctx_len // RWKV_BLK
    lora_blk = COL_LORA // LORA_W
    d = 1 if reverse else 0

    def tok(i):
        return jnp.where(i < nctx, nctx - 1 - i, nblk + nctx - 1 - i) if reverse else i

    par3 = lambda b, i: (d, 0, 0)
    tri, cmask = _rwkv_masks(reverse)
    kern = functools.partial(_rwkv_kernel, reverse=reverse)
    return pl.pallas_call(
        kern,
        grid=(bsz // RWKV_NB, nblk),
        in_specs=[pl.BlockSpec((RWKV_NB, RWKV_BLK, D_RWKV), lambda b, i: (b, tok(i), 0)),
                  pl.BlockSpec((RWKV_NB, RWKV_BLK, D_RWKV), lambda b, i: (b, tok(i), 1)),
                  pl.BlockSpec((RWKV_NB, RWKV_BLK, D_RWKV), lambda b, i: (b, tok(i), 2)),
                  pl.BlockSpec((RWKV_NB, RWKV_BLK, LORA_W), lambda b, i: (b, tok(i), lora_blk)),
                  pl.BlockSpec((1, 1, D_RWKV), par3),
                  pl.BlockSpec((1, LORA_W, D_RWKV), par3),
                  pl.BlockSpec((1, 1, D_RWKV), par3),
                  pl.BlockSpec((1, LORA_W, D_RWKV), par3),
                  pl.BlockSpec((1, D_RWKV), lambda b, i: (0, 0)),
                  pl.BlockSpec((1, D_RWKV), lambda b, i: (0, 0)),
                  pl.BlockSpec(tri.shape, lambda b, i: (0, 0)),
                  pl.BlockSpec(cmask.shape, lambda b, i: (0, 0, 0))],
        out_specs=pl.BlockSpec((RWKV_NB, RWKV_BLK, D_RWKV), lambda b, i: (b, tok(i), 0)),
        out_shape=jax.ShapeDtypeStruct((bsz, ltot, D_RWKV), F32),
        scratch_shapes=[pltpu.VMEM((RWKV_NB * HEADS // 2, 2 * HEAD, 2 * HEAD), F32)],
        compiler_params=_cparams(("parallel", "arbitrary"), 48),
    )(rkvc, rkvc, rkvc, z, w0, w2p, a0, a2p, kkw, kaw, tri, cmask)


def _s5_tables_kernel(lam_re_ref, lam_im_ref, dt_ref, bt_re_ref, bt_im_ref, c_re_ref, c_im_ref,
                      inj_ref, mt_ref, qt_ref, lre_ref, lim_ref):
    t = S5_T
    hg = S5_GROUP
    ps = S5_STATE
    row = lax.broadcasted_iota(jnp.int32, (t, 2 * ps), 0).astype(F32)
    is_f = lax.broadcasted_iota(jnp.int32, (t, 2 * ps), 1) < ps
    is_f1 = is_f[0:1]
    n_inj = jnp.where(is_f, t - 1.0 - row, row)
    n_out = jnp.where(is_f, row + 1.0, t - row)
    n_lag = jnp.where(is_f, row, t - 1.0 - row)
    for gl in range(lam_re_ref.shape[0]):
        lam_re = lam_re_ref[gl:gl + 1, :]
        lam_im = lam_im_ref[gl:gl + 1, :]
        dt = dt_ref[gl:gl + 1, :]

        def power(n):
            mag = jnp.exp(lam_re * dt * n)
            return mag * jnp.cos(lam_im * dt * n), mag * jnp.sin(lam_im * dt * n)

        lb_re, lb_im = power(1.0)
        den = lam_re * lam_re + lam_im * lam_im
        q_re = ((lb_re - 1.0) * lam_re + lb_im * lam_im) / den
        q_im = (lb_im * lam_re - (lb_re - 1.0) * lam_im) / den
        bt_re = jnp.concatenate([bt_re_ref[gl], bt_re_ref[gl]], axis=1)
        bt_im = jnp.concatenate([bt_im_ref[gl], bt_im_ref[gl]], axis=1)
        bb_re = q_re * bt_re - q_im * bt_im
        bb_im = q_re * bt_im + q_im * bt_re
        c_re = jnp.concatenate([c_re_ref[gl], c_re_ref[gl]], axis=1)
        c_im = jnp.concatenate([c_im_ref[gl], c_im_ref[gl]], axis=1)

        pi_re, pi_im = power(n_inj)
        po_re, po_im = power(n_out)
        pl_re, pl_im = power(n_lag)
        ck_re, ck_im = [], []
        for s in range(t):
            wr, wi = pi_re[s:s + 1], pi_im[s:s + 1]
            rows = slice(s * hg, (s + 1) * hg)
            inj_ref[gl, rows, 0:2 * ps] = (wr * bb_re - wi * bb_im).astype(BF16)
            inj_ref[gl, rows, 2 * ps:4 * ps] = (wr * bb_im + wi * bb_re).astype(BF16)
            wr, wi = po_re[s:s + 1], po_im[s:s + 1]
            qt_ref[gl, rows, 0:2 * ps] = (wr * c_re - wi * c_im).astype(BF16)
            qt_ref[gl, rows, 2 * ps:4 * ps] = (-(wr * c_im + wi * c_re)).astype(BF16)
            wr, wi = pl_re[s:s + 1], pl_im[s:s + 1]
            ck_re.append(wr * c_re - wi * c_im)
            ck_im.append(wr * c_im + wi * c_re)
        ck = jnp.concatenate([jnp.concatenate(ck_re, axis=0), jnp.concatenate(ck_im, axis=0)], axis=1)
        bsel = jnp.concatenate(
            [jnp.concatenate([jnp.where(is_f1, bb_re, 0.0), jnp.where(is_f1, -bb_im, 0.0)], axis=1),
             jnp.concatenate([jnp.where(is_f1, 0.0, bb_re), jnp.where(is_f1, 0.0, -bb_im)], axis=1)], axis=0)
        kt = _mm(bsel, ck, "x3", NT)
        w = t * hg
        zero = jnp.zeros((hg, w), F32)
        lag = (pltpu.roll(jnp.concatenate([kt[0:hg], zero], axis=1), w - hg, 1)
               + jnp.concatenate([kt[hg:2 * hg], zero], axis=1))
        for s in range(t):
            sh = (t - 1 - s) * hg
            blk = lag if sh == 0 else pltpu.roll(lag, 2 * w - sh, 1)
            mt_ref[gl, s * hg:(s + 1) * hg, :] = blk[:, 0:w].astype(BF16)
        l16_re, l16_im = power(float(t))
        lre_ref[gl:gl + 1, :] = l16_re
        lim_ref[gl:gl + 1, :] = l16_im


def _s5_tables(lam_re, lam_im, log_step, b_re, b_im, c_re, c_im):
    g, p, hg = b_re.shape
    w = S5_T * hg
    lanes = lambda x: jnp.concatenate([x[0], x[1]], axis=-1)
    dt = lanes(jnp.broadcast_to(jnp.exp(log_step)[..., None], lam_re.shape))
    bt_re = jnp.transpose(b_re, (0, 2, 1))
    bt_im = jnp.transpose(b_im, (0, 2, 1))
    vec = pl.BlockSpec((S5_TG, 2 * p), lambda j: (j, 0))
    mat = pl.BlockSpec((S5_TG, hg, p), lambda j: (j, 0, 0))
    tab = pl.BlockSpec((S5_TG, w, w), lambda j: (j, 0, 0))
    return pl.pallas_call(
        _s5_tables_kernel,
        grid=(g // S5_TG,),
        in_specs=[vec, vec, vec, mat, mat, mat, mat],
        out_specs=[tab, tab, tab, vec, vec],
        out_shape=[jax.ShapeDtypeStruct((g, w, w), BF16)] * 3 + [jax.ShapeDtypeStruct((g, 2 * p), F32)] * 2,
        compiler_params=_cparams(("parallel",), 32),
    )(lanes(lam_re), lanes(lam_im), dt, bt_re, bt_im, c_re, c_im)


def _block_transpose8(xs):
    lane_blk = lax.broadcasted_iota(jnp.int32, xs[0].shape, 1) >> 4
    xs = list(xs)
    for d in (4, 2, 1):
        hi = (lane_blk & d) != 0
        for i in range(8):
            if i & d:
                continue
            a, b = xs[i], xs[i + d]
            xs[i] = jnp.where(hi, pltpu.roll(b, 16 * d, 1), a)
            xs[i + d] = jnp.where(hi, b, pltpu.roll(a, 128 - 16 * d, 1))
    return xs


def _s5_kernel(u0_ref, u1_ref, inj_ref, mt_ref, qt_ref, lre_ref, lim_ref, y_ref, ug_ref, zre_ref, zim_ref,
               hre_ref, him_ref, *, nchunk, nctx):
    gh = ug_ref.shape[0]
    ps = S5_STATE
    t = S5_T
    nlat = nchunk - nctx
    for jt, u_ref in enumerate((u0_ref, u1_ref)):
        for q in range(t // 8):
            cols = [u_ref[0, pl.ds(8 * q + tt, nchunk, stride=t), :] for tt in range(8)]
            grp = _block_transpose8(cols)
            for gl in range(8):
                ug_ref[8 * jt + gl, :, 128 * q:128 * (q + 1)] = grp[gl].astype(BF16)
    for gi in range(gh):
        zg = _dot(ug_ref[gi], inj_ref[gi], NN)
        zre_ref[pl.ds(gi, nchunk, stride=gh), :] = zg[:, 0:2 * ps]
        zim_ref[pl.ds(gi, nchunk, stride=gh), :] = zg[:, 2 * ps:4 * ps]

    lane = lax.broadcasted_iota(jnp.int32, (gh, 2 * ps), 1)
    is_f = lane < ps
    lre = lre_ref[...]
    lim = lim_ref[...]

    def body(i, carry):
        hre, him = carry
        cb = jnp.where(i < nctx, nctx - 1 - i, nchunk + nctx - 1 - i)
        rf = pl.multiple_of(i * gh, gh)
        rb = pl.multiple_of(cb * gh, gh)
        hre_ref[pl.ds(rf, gh), 0:ps] = hre[:, 0:ps]
        hre_ref[pl.ds(rb, gh), ps:2 * ps] = hre[:, ps:2 * ps]
        him_ref[pl.ds(rf, gh), 0:ps] = him[:, 0:ps]
        him_ref[pl.ds(rb, gh), ps:2 * ps] = him[:, ps:2 * ps]
        zre = jnp.where(is_f, zre_ref[pl.ds(rf, gh), :], zre_ref[pl.ds(rb, gh), :])
        zim = jnp.where(is_f, zim_ref[pl.ds(rf, gh), :], zim_ref[pl.ds(rb, gh), :])
        return lre * hre - lim * him + zre, lre * him + lim * hre + zim

    zero = jnp.zeros((gh, 2 * ps), F32)
    lax.fori_loop(0, nchunk, body, (zero, zero))

    for jt in range(gh // 8):
        ys = []
        for gl in range(8):
            gi = 8 * jt + gl
            hre = hre_ref[pl.ds(gi, nchunk, stride=gh), :].astype(BF16)
            him = him_ref[pl.ds(gi, nchunk, stride=gh), :].astype(BF16)
            ys.append(_dot(ug_ref[gi], mt_ref[gi], NN)
                      + _dot(jnp.concatenate([hre, him], axis=1), qt_ref[gi], NT))
        for q in range(t // 8):
            tok = _block_transpose8([y[:, 128 * q:128 * (q + 1)] for y in ys])
            for tt in range(8):
                y_ref[0, jt, pl.ds(8 * q + tt, nlat, stride=t), :] = tok[tt][nctx:]


def _s5(u, tables, ctx_len):
    inj, mt, qt, lre, lim = tables
    bsz, ltot, _ = u.shape
    nchunk = ltot // S5_T
    nctx = ctx_len // S5_T
    gh = S5_GH
    halves = S5_GROUPS // gh
    w = S5_T * S5_GROUP
    tiles = gh * S5_GROUP // 128
    ps2 = 2 * S5_STATE
    kern = functools.partial(_s5_kernel, nchunk=nchunk, nctx=nctx)
    u_spec = lambda k: pl.BlockSpec((1, ltot, 128), lambda j, b: (b, 0, tiles * j + k))
    tab = pl.BlockSpec((gh, w, w), lambda j, b: (j, 0, 0))
    vec = pl.BlockSpec((gh, ps2), lambda j, b: (j, 0))
    return pl.pallas_call(
        kern,
        grid=(halves, bsz),
        in_specs=[u_spec(0), u_spec(1), tab, tab, tab, vec, vec],
        out_specs=pl.BlockSpec((1, tiles, ltot - ctx_len, 128), lambda j, b: (b, j, 0, 0)),
        out_shape=jax.ShapeDtypeStruct((bsz, D_S5 // 128, ltot - ctx_len, 128), F32),
        scratch_shapes=[pltpu.VMEM((gh, nchunk, w), BF16),
                        pltpu.VMEM((nchunk * gh, ps2), F32),
                        pltpu.VMEM((nchunk * gh, ps2), F32),
                        pltpu.VMEM((nchunk * gh, ps2), F32),
                        pltpu.VMEM((nchunk * gh, ps2), F32)],
        compiler_params=_cparams(("parallel", "parallel"), 48),
    )(u, u, inj, mt, qt, lre, lim)


def _mix_kernel(x_ref, yf_ref, yb_ref, r_ref, k_ref, v_ref, lo_ref, u_ref, ys_ref, mod_ref,
                seg_ref, a0_ref, a2_ref, g2_ref, kaw_ref, rk_ref, lnw_ref, lnb_ref,
                dsk_ref, gluw_ref, glub_ref, wo_ref, g2n_ref, w1_ref, w3_ref, w2_ref, gf_ref, o_ref):
    lo = lo_ref[0].astype(F32)
    seg = seg_ref[...]

    def head_sum(t):
        pw = seg.shape[0]
        return jnp.concatenate([_mm(t[:, i:i + pw], seg) for i in range(0, t.shape[1], pw)], axis=1)

    y = yf_ref[0] + yb_ref[0]
    mu = head_sum(y) * (1.0 / HEAD)
    yc = y - mu
    var = head_sum(yc * yc) * (1.0 / HEAD)
    yn = yc * lax.rsqrt(var + RWKV_LN_EPS) * lnw_ref[...] + lnb_ref[...]
    a_f = _sigmoid(a0_ref[0] + _mm(lo, a2_ref[0]))
    a_b = _sigmoid(a0_ref[1] + _mm(lo, a2_ref[1]))
    kaw = kaw_ref[...]
    kd_sum = k_ref[0] * ((1.0 + (a_f - 1.0) * kaw) + (1.0 + (a_b - 1.0) * kaw))
    bonus = head_sum(r_ref[0] * kd_sum * rk_ref[...])
    yn = yn + bonus * v_ref[0]
    gate = _mm(_sigmoid(lo), g2_ref[...])
    rwkv_out = yn * gate

    u = u_ref[0]
    ys = jnp.concatenate([ys_ref[0, i] for i in range(ys_ref.shape[1])], axis=1) + dsk_ref[...] * u
    zg = 0.5 * ys * (1.0 + jnp.tanh(math.sqrt(2.0 / math.pi) * (ys + 0.044715 * (ys * ys * ys))))
    s5_out = zg * _sigmoid(_mm(zg, gluw_ref[...]) + glub_ref[...])

    mix = _mm(rwkv_out, wo_ref[0:D_RWKV, :]) + _mm(s5_out, wo_ref[D_RWKV:, :])
    x1 = x_ref[0] + mod_ref[0, 2:3, :] * mix

    ms = jnp.mean(x1 * x1, axis=-1, keepdims=True)
    h = x1 * lax.rsqrt(ms + NORM_EPS) * g2n_ref[...]
    h = h * (1.0 + mod_ref[0, 4:5, :]) + mod_ref[0, 3:4, :]
    hb = h.astype(BF16)
    h1 = _dot(hb, w1_ref[...], NN)
    h3 = _dot(hb, w3_ref[...], NN)
    act = (h1 * _sigmoid(h1)) * h3
    out = x1 + mod_ref[0, 5:6, :] * _dot(act.astype(BF16), w2_ref[...], NN)
    ms2 = jnp.mean(out * out, axis=-1, keepdims=True)
    o_ref[0] = out * lax.rsqrt(ms2 + NORM_EPS) * gf_ref[...]


def _mix(x, y_f, y_b, rkvc, z, u, ys5, mod8, seg, a0, a2p, g2p, kaw, rk, lnw, lnb, dsk, gluw, glub, wo,
         g2n, w1, w3, w2, gf, ctx_len, tm):
    bsz, seq, d = x.shape
    assert ctx_len % tm == 0 and seq % tm == 0
    off = ctx_len // tm
    lora_blk = COL_LORA // LORA_W
    tok = lambda c: pl.BlockSpec((1, tm, D_RWKV), lambda b, j: (b, j + off, c))
    full = lambda a: pl.BlockSpec(a.shape, lambda b, j: (0,) * a.ndim, pipeline_mode=pl.Buffered(1))
    return pl.pallas_call(
        _mix_kernel,
        grid=(bsz, seq // tm),
        in_specs=[pl.BlockSpec((1, tm, d), lambda b, j: (b, j, 0)),
                  tok(0), tok(0), tok(0), tok(1), tok(2),
                  pl.BlockSpec((1, tm, LORA_W), lambda b, j: (b, j + off, lora_blk)),
                  pl.BlockSpec((1, tm, D_S5), lambda b, j: (b, j + off, 0)),
                  pl.BlockSpec((1, D_S5 // 128, tm, 128), lambda b, j: (b, 0, j, 0)),
                  pl.BlockSpec((1, 8, d), lambda b, j: (b, 0, 0)),
                  full(seg), full(a0), full(a2p), full(g2p), full(kaw), full(rk), full(lnw), full(lnb),
                  full(dsk), full(gluw), full(glub), full(wo),
                  full(g2n), full(w1), full(w3), full(w2), full(gf)],
        out_specs=pl.BlockSpec((1, tm, d), lambda b, j: (b, j, 0)),
        out_shape=jax.ShapeDtypeStruct((bsz, seq, d), F32),
        compiler_params=_cparams(("parallel", "parallel"), 56),
    )(x, y_f, y_b, rkvc, rkvc, rkvc, z, u, ys5, mod8, seg, a0, a2p, g2p, kaw, rk, lnw, lnb,
      dsk, gluw, glub, wo, g2n, w1, w3, w2, gf)


def _pad_rows(w, row0, rows_total):
    pad = [(0, 0)] * (w.ndim - 2) + [(row0, rows_total - row0 - w.shape[-2]), (0, 0)]
    return jnp.pad(w, pad)


def kernel(x, c, ctx, c_ctx, mod_w, mod_b, norm1_g, norm2_g, w_in, w_out, rwkv_conv, rwkv_w0, rwkv_w2, rwkv_a0, rwkv_a2, rwkv_g2, rwkv_kk, rwkv_ka, rwkv_rk, rwkv_ln_w, rwkv_ln_b, s5_lam_re, s5_lam_im, s5_log_step, s5_b_re, s5_b_im, s5_c_re, s5_c_im, s5_d, s5_glu_w, s5_glu_b, ffn_w1, ffn_w3, ffn_w2, final_g):
    bsz, seq, d = x.shape
    ctx_len = ctx.shape[1]
    ltot = ctx_len + seq
    layer = 0

    nmod = bsz + INPROJ_NB
    cc = jnp.concatenate([c, jnp.broadcast_to(c_ctx[None, :], (INPROJ_NB, d)),
                          jnp.zeros((8 - nmod, d), F32)], axis=0)
    mod = _adaln(cc, mod_w[layer], mod_b[layer])
    mod8 = jnp.pad(mod[:nmod].reshape(nmod, N_MOD, d), ((0, 0), (0, 8 - N_MOD), (0, 0)))

    wi = w_in[layer]
    z, u = _inproj(ctx, x, mod8, norm1_g[layer], wi[:, :D_Z16].astype(BF16), wi[:, -D_S5:].astype(BF16), 256)

    rkvc = _conv(z, rwkv_conv[layer], ctx_len)

    w2p = _pad_rows(rwkv_w2[layer], 0, LORA_W).at[1].set(_pad_rows(rwkv_w2[layer, 1], DECAY_LORA, LORA_W))
    a2p = jnp.stack([_pad_rows(rwkv_a2[layer, 0], LORA_AD, LORA_W),
                     _pad_rows(rwkv_a2[layer, 1], LORA_AD + AAA_LORA, LORA_W)])
    w0 = rwkv_w0[layer].reshape(2, 1, D_RWKV)
    a0 = rwkv_a0[layer].reshape(2, 1, D_RWKV)
    kkw = rwkv_kk[layer].reshape(1, D_RWKV)
    kaw = rwkv_ka[layer].reshape(1, D_RWKV)
    y_f = _rwkv(rkvc, z, w0, w2p, a0, a2p, kkw, kaw, ctx_len, False)
    y_b = _rwkv(rkvc, z, w0, w2p, a0, a2p, kkw, kaw, ctx_len, True)

    tables = _s5_tables(s5_lam_re[layer], s5_lam_im[layer], s5_log_step[layer], s5_b_re[layer],
                        s5_b_im[layer], s5_c_re[layer], s5_c_im[layer])
    ys5 = _s5(u, tables, ctx_len)

    hh = jnp.arange(2 * HEAD) // HEAD
    seg = (hh[:, None] == hh[None, :]).astype(BF16)
    g2p = _pad_rows(rwkv_g2[layer], LORA_GD, LORA_W)
    return _mix(x, y_f, y_b, rkvc, z, u, ys5, mod8, seg, a0, a2p, g2p, kaw,
                rwkv_rk[layer].reshape(1, D_RWKV), rwkv_ln_w[layer].reshape(1, D_RWKV),
                rwkv_ln_b[layer].reshape(1, D_RWKV), s5_d[layer].reshape(1, D_S5),
                s5_glu_w[layer].astype(BF16), s5_glu_b[layer].reshape(1, D_S5),
                w_out[layer].astype(BF16), norm2_g[layer].reshape(1, d), ffn_w1[layer].astype(BF16),
                ffn_w3[layer].astype(BF16), ffn_w2[layer].astype(BF16), final_g.reshape(1, d), ctx_len, 256)
```
